```python
import jax, jax.numpy as jnp
from jax import lax
import numpy as np

D_MODEL = 1024
BATCH = 2
SEQ = 16384
DEPTH = 4

CHUNK = 64
N_META = 16
N_EVEN = (DEPTH + 1) // 2
N_ODD = DEPTH // 2
EPS = 1e-6
D_FF = 4 * D_MODEL

HG_HEADS = 4
HG_K = 128
HG_V = (D_MODEL // 2) // HG_HEADS
HG_KW = HG_HEADS * HG_K
HG_VW = HG_HEADS * HG_V

MLA_HEADS = 4
NOPE = 128
ROPE = 64
V_DIM = (D_MODEL // 2) // MLA_HEADS
QK_DIM = NOPE + ROPE
Q_RANK = 256
KV_RANK = 256
ROPE_THETA = 10000.0
Q_BLOCK = 128

IN_SPLITS = (HG_KW, 2 * HG_KW, 2 * HG_KW + HG_VW, 2 * HG_KW + 2 * HG_VW,
             2 * HG_KW + 2 * HG_VW + Q_RANK, 2 * HG_KW + 2 * HG_VW + Q_RANK + KV_RANK)
IN_COLS = 2 * HG_KW + 2 * HG_VW + Q_RANK + KV_RANK + ROPE

POOL_WINDOWS = (2, 4, 8, 16)
POOL_GROUPS = len(POOL_WINDOWS)
POOL_G = D_MODEL // POOL_GROUPS

kernel_name = "hybrid_hgrn2_mla_pool_trunk"


def rmsnorm(x, g):
    xf = x.astype(jnp.float32)
    y = xf * lax.rsqrt(jnp.mean(xf * xf, axis=-1, keepdims=True) + EPS)
    return (y * g.astype(jnp.float32)).astype(x.dtype)


def chunk_ids(L):
    p = jnp.arange(L)
    return jnp.where(p < N_META, 0, 1 + (p - N_META) // CHUNK)


def rope_tables(L):
    half = ROPE // 2
    inv = ROPE_THETA ** (-jnp.arange(half, dtype=jnp.float32) / half)
    ang = jnp.arange(L, dtype=jnp.float32)[:, None] * inv[None, :]
    return jnp.cos(ang), jnp.sin(ang)


def rope_tail(x, cos, sin):
    xn, xr = x[..., :NOPE], x[..., NOPE:]
    x1, x2 = xr[..., :ROPE // 2], xr[..., ROPE // 2:]
    c = cos[None, :, None, :].astype(x.dtype)
    s = sin[None, :, None, :].astype(x.dtype)
    return jnp.concatenate([xn, x1 * c - x2 * s, x2 * c + x1 * s], axis=-1)


def hgrn2_chunk_scan(q, k, v, log_f):
    B, L, H, K = q.shape
    lead = CHUNK - N_META
    tail = (-(L + lead)) % CHUNK
    nc = (L + lead + tail) // CHUNK

    def to_chunks(a):
        a = jnp.pad(a.astype(jnp.float32), ((0, 0), (lead, tail), (0, 0), (0, 0)))
        return a.reshape(B, nc, CHUNK, H, a.shape[-1]).transpose(1, 0, 3, 2, 4)

    qc, kc, vc, gc = to_chunks(q), to_chunks(k), to_chunks(v), to_chunks(log_f)
    tri = jnp.tril(jnp.ones((CHUNK, CHUNK), dtype=bool))[:, :, None]

    def step(S, inp):
        qi, ki, vi, gi = inp
        b = jnp.cumsum(gi, axis=2)
        o_inter = jnp.einsum('bhtk,bhkv->bhtv', qi * jnp.exp(b), S)
        diff = b[:, :, :, None, :] - b[:, :, None, :, :]
        dec = jnp.exp(jnp.where(tri, diff, -jnp.inf))
        att = jnp.einsum('bhtk,bhtsk,bhsk->bhts', qi, dec, ki)
        o_intra = jnp.einsum('bhts,bhsv->bhtv', att, vi)
        b_last = b[:, :, -1:, :]
        S_new = jnp.exp(b_last[:, :, 0, :])[..., None] * S + jnp.einsum(
            'bhsk,bhsv->bhkv', ki * jnp.exp(b_last - b), vi)
        return S_new, o_inter + o_intra

    S0 = jnp.zeros((B, H, K, v.shape[-1]), jnp.float32)
    _, o = lax.scan(step, S0, (qc, kc, vc, gc))
    o = o.transpose(1, 0, 3, 2, 4).reshape(B, nc * CHUNK, H, -1)
    return o[:, lead:lead + L]


def hgrn2_mixer(hq, hf, hi, hg, lb, out_gain):
    B, L, _ = hq.shape
    lb = lb.astype(jnp.float32)
    log_f = jnp.logaddexp(jnp.log(lb), jnp.log1p(-lb) + jax.nn.log_sigmoid(hf.astype(jnp.float32)))
    k = -jnp.expm1(log_f)
    q = jax.nn.silu(hq.astype(jnp.float32))
    o = hgrn2_chunk_scan(q.reshape(B, L, HG_HEADS, HG_K), k.reshape(B, L, HG_HEADS, HG_K),
                         hi.reshape(B, L, HG_HEADS, HG_V), log_f.reshape(B, L, HG_HEADS, HG_K))
    o = rmsnorm(o, out_gain) * jax.nn.silu(hg.reshape(B, L, HG_HEADS, HG_V).astype(jnp.float32))
    return o.reshape(B, L, HG_VW).astype(hq.dtype)


def block_causal_attention(q, k, v, cid):
    B, L, H, Dq = q.shape
    nq = -(-L // Q_BLOCK)
    pad = nq * Q_BLOCK - L
    qp = jnp.pad(q, ((0, 0), (0, pad), (0, 0), (0, 0)))
    qb = qp.reshape(B, nq, Q_BLOCK, H, Dq).transpose(1, 0, 2, 3, 4)
    qcid = jnp.pad(cid, (0, pad), constant_values=L).reshape(nq, Q_BLOCK)
    scale = Dq ** -0.5

    def one_block(args):
        qblk, qc = args
        s = jnp.einsum('bqhd,bkhd->bhqk', qblk, k).astype(jnp.float32) * scale
        mask = cid[None, :] <= qc[:, None]
        p = jax.nn.softmax(jnp.where(mask, s, -jnp.inf), axis=-1).astype(v.dtype)
        return jnp.einsum('bhqk,bkhd->bqhd', p, v)

    out = lax.map(one_block, (qb, qcid))
    out = out.transpose(1, 0, 2, 3, 4).reshape(B, nq * Q_BLOCK, H, v.shape[-1])
    return out[:, :L]


def mla_mixer(cq, ckv, kr, qa_g, kva_g, w_q_up, w_kv_up, qn_g, kn_g, cos, sin, cid):
    B, L, _ = cq.shape
    q = jnp.einsum('blr,re->ble', rmsnorm(cq, qa_g), w_q_up).reshape(B, L, MLA_HEADS, QK_DIM)
    kv = jnp.einsum('blr,re->ble', rmsnorm(ckv, kva_g), w_kv_up).reshape(B, L, MLA_HEADS, NOPE + V_DIM)
    k_nope, v = kv[..., :NOPE], kv[..., NOPE:]
    k = jnp.concatenate([k_nope, jnp.broadcast_to(kr[:, :, None, :], (B, L, MLA_HEADS, ROPE))], axis=-1)
    q = rope_tail(rmsnorm(q, qn_g), cos, sin)
    k = rope_tail(rmsnorm(k, kn_g), cos, sin)
    o = block_causal_attention(q, k, v, cid)
    return o.reshape(B, L, MLA_HEADS * V_DIM)


def multiscale_pool(h, w_groups, scale):
    B, L, D = h.shape
    hf = h.astype(jnp.float32).reshape(B, L, POOL_GROUPS, POOL_G)
    t = jnp.arange(L, dtype=jnp.float32)
    diffs = []
    for gi, w in enumerate(POOL_WINDOWS):
        c = jnp.pad(jnp.cumsum(hf[:, :, gi], axis=1), ((0, 0), (w, 0), (0, 0)))
        win = c[:, w:] - c[:, :L]
        cnt = jnp.minimum(t + 1.0, float(w))[None, :, None]
        diffs.append(win / cnt - hf[:, :, gi])
    d = jnp.stack(diffs, axis=2).astype(h.dtype)
    y = jnp.einsum('blgc,gce->blge', d, w_groups).reshape(B, L, D)
    return y * scale


def squared_relu_mlp(h, w_up, w_down):
    a = jax.nn.relu(jnp.einsum('bld,df->blf', h, w_up))
    return jnp.einsum('blf,fd->bld', a * a, w_down)


def setup_inputs(seed: int = 0) -> dict:
    key = jax.random.key(seed)
    ks = jax.random.split(key, 20)
    n = jax.random.normal
    f32 = jnp.float32
    return {
        'x': n(ks[0], (BATCH, SEQ, D_MODEL), f32),
        'meta_tokens': n(ks[1], (N_META, D_MODEL), f32),
        'mix_norm': 1.0 + 0.02 * n(ks[2], (DEPTH, D_MODEL), f32),
        'mlp_norm': 1.0 + 0.02 * n(ks[3], (DEPTH, D_MODEL), f32),
        'w_mlp_up': n(ks[4], (DEPTH, D_MODEL, D_FF), f32) * D_MODEL ** -0.5,
        'w_mlp_down': n(ks[5], (DEPTH, D_FF, D_MODEL), f32) * (0.5 * D_FF ** -0.5),
        'w_in': n(ks[6], (N_EVEN, D_MODEL, IN_COLS), f32) * D_MODEL ** -0.5,
        'hgrn_lb': 0.5 * n(ks[7], (N_EVEN, HG_KW), f32),
        'hgrn_out_norm': 1.0 + 0.02 * n(ks[8], (N_EVEN, HG_V), f32),
        'mla_q_a_norm': 1.0 + 0.02 * n(ks[9], (N_EVEN, Q_RANK), f32),
        'mla_kv_a_norm': 1.0 + 0.02 * n(ks[10], (N_EVEN, KV_RANK), f32),
        'w_q_up': n(ks[11], (N_EVEN, Q_RANK, MLA_HEADS * QK_DIM), f32) * Q_RANK ** -0.5,
        'w_kv_up': n(ks[12], (N_EVEN, KV_RANK, MLA_HEADS * (NOPE + V_DIM)), f32) * KV_RANK ** -0.5,
        'q_norm': 1.0 + 0.02 * n(ks[13], (N_EVEN, QK_DIM), f32),
        'k_norm': 1.0 + 0.02 * n(ks[14], (N_EVEN, QK_DIM), f32),
        'w_out': n(ks[15], (N_EVEN, HG_VW + MLA_HEADS * V_DIM, D_MODEL), f32) * D_MODEL ** -0.5,
        'pool_w': n(ks[16], (N_ODD, POOL_GROUPS, POOL_G, POOL_G), f32) * POOL_G ** -0.5,
        'pool_scale': 1.0 + 0.1 * n(ks[17], (N_ODD, D_MODEL), f32),
    }


def reference(x, meta_tokens, mix_norm, mlp_norm, w_mlp_up, w_mlp_down, w_in, hgrn_lb,
              hgrn_out_norm, mla_q_a_norm, mla_kv_a_norm, w_q_up, w_kv_up, q_norm, k_norm,
              w_out, pool_w, pool_scale):
    B = x.shape[0]
    meta = jnp.broadcast_to(meta_tokens[None].astype(x.dtype), (B, N_META, D_MODEL))
    h = jnp.concatenate([meta, x], axis=1)
    L = h.shape[1]
    cid = chunk_ids(L)
    cos, sin = rope_tables(L)
    lb_cum = jnp.cumsum(jax.nn.softmax(hgrn_lb.astype(jnp.float32), axis=0), axis=0)
    lower_bounds = lb_cum - lb_cum[0:1]

    for layer in range(DEPTH):
        u = rmsnorm(h, mix_norm[layer])
        if layer % 2 == 0:
            e = layer // 2
            z = jnp.einsum('bld,de->ble', u, w_in[e])
            hq, hf, hi, hg, cq, ckv, kr = jnp.split(z, IN_SPLITS, axis=-1)
            o_a = hgrn2_mixer(hq, hf, hi, hg, lower_bounds[e], hgrn_out_norm[e])
            o_b = mla_mixer(cq, ckv, kr, mla_q_a_norm[e], mla_kv_a_norm[e], w_q_up[e], w_kv_up[e],
                            q_norm[e], k_norm[e], cos, sin, cid)
            mix = jnp.einsum('ble,ed->bld', jnp.concatenate([o_a, o_b], axis=-1), w_out[e])
        else:
            o = layer // 2
            mix = multiscale_pool(u, pool_w[o], pool_scale[o])
        h = h + mix
        h = h + squared_relu_mlp(rmsnorm(h, mlp_norm[layer]), w_mlp_up[layer], w_mlp_down[layer])

    return h[:, N_META:]
```

```python
import functools

import numpy as np
import jax
import jax.numpy as jnp
from jax import lax
from jax.experimental import pallas as pl
from jax.experimental.pallas import tpu as pltpu

F32 = jnp.float32
BF16 = jnp.bfloat16

D_MODEL = 1024
D_FF = 4 * D_MODEL
EPS = 1e-6
N_META = 16
CHUNK = 64
HEADS = 4
HD = 128
HW = HEADS * HD
ROPE = 64
QK_DIM = HD + ROPE
QK_PAD = 256
Q_RANK = 256
KV_RANK = 256
ROPE_THETA = 10000.0
POOL_WINDOWS = (2, 4, 8, 16)
POOL_G = D_MODEL // len(POOL_WINDOWS)

LEAD = 512
PAD = LEAD - N_META
TM = 512
TQ = 512
SUB = 16
HALO = 32
IN_COLS = 4 * HW + Q_RANK + KV_RANK + HD
MASK_VALUE = -1e30
EXP_CLAMP = 80.0
VMEM_LIMIT = 56 * 1024 * 1024


def _rms(x, g):
    return x * lax.rsqrt(jnp.mean(x * x, axis=-1, keepdims=True) + EPS) * g


def _const_spec(shape):
    nd = len(shape)
    return pl.BlockSpec(shape, lambda *_: (0,) * nd, pipeline_mode=pl.Buffered(1))


def _params(sem):
    return pltpu.CompilerParams(dimension_semantics=sem, vmem_limit_bytes=VMEM_LIMIT)


def _mlp(h, g_ref, wu_ref, wd_ref):
    hn = _rms(h, g_ref[...]).astype(BF16)
    acc = jnp.zeros_like(h)
    fc = 1024
    for c in range(D_FF // fc):
        a = jnp.dot(hn, wu_ref[:, c * fc:(c + 1) * fc], preferred_element_type=F32)
        a = jnp.maximum(a, 0.0)
        a = (a * a).astype(BF16)
        acc = acc + jnp.dot(a, wd_ref[c * fc:(c + 1) * fc, :], preferred_element_type=F32)
    return h + acc


def _rope(x, c, s):
    return x * c + pltpu.roll(x, 64, axis=1) * s


def _inproj_body(h_ref, g_ref, win_ref, qag_ref, kvag_ref, wq_ref, wkv_ref,
                 qn_ref, kn_ref, cos_ref, sin_ref,
                 zh_ref, q_ref, k_ref, v_ref):
    u = _rms(h_ref[...], g_ref[...]).astype(BF16)
    z = jnp.dot(u, win_ref[...], preferred_element_type=F32)
    zh_ref[...] = z[:, :4 * HW]
    cq = z[:, 4 * HW:4 * HW + Q_RANK]
    ckv = z[:, 4 * HW + Q_RANK:4 * HW + Q_RANK + KV_RANK]
    kr = z[:, 4 * HW + Q_RANK + KV_RANK:]
    q = jnp.dot(_rms(cq, qag_ref[...]).astype(BF16), wq_ref[...], preferred_element_type=F32)
    kv = jnp.dot(_rms(ckv, kvag_ref[...]).astype(BF16), wkv_ref[...], preferred_element_type=F32)
    v_ref[...] = kv[:, HW:].astype(BF16)
    c = cos_ref[...]
    s = sin_ref[...]
    qg = qn_ref[...]
    kg = kn_ref[...]
    kr_ss = jnp.sum(kr * kr, axis=-1, keepdims=True)
    q_scale = QK_DIM ** -0.5
    for hd in range(HEADS):
        qa = q[:, hd * QK_PAD:hd * QK_PAD + HD]
        qb = q[:, hd * QK_PAD + HD:(hd + 1) * QK_PAD]
        ss = (jnp.sum(qa * qa, axis=-1, keepdims=True) + jnp.sum(qb * qb, axis=-1, keepdims=True))
        inv = lax.rsqrt(ss * (1.0 / QK_DIM) + EPS)
        q_ref[:, hd * QK_PAD:hd * QK_PAD + HD] = (qa * inv * qg[:, :HD] * q_scale).astype(BF16)
        q_ref[:, hd * QK_PAD + HD:(hd + 1) * QK_PAD] = (
            _rope(qb * inv * qg[:, HD:], c, s) * q_scale).astype(BF16)
        ka = kv[:, hd * HD:(hd + 1) * HD]
        ss = jnp.sum(ka * ka, axis=-1, keepdims=True) + kr_ss
        inv = lax.rsqrt(ss * (1.0 / QK_DIM) + EPS)
        k_ref[:, hd * QK_PAD:hd * QK_PAD + HD] = (ka * inv * kg[:, :HD]).astype(BF16)
        k_ref[:, hd * QK_PAD + HD:(hd + 1) * QK_PAD] = _rope(kr * inv * kg[:, HD:], c, s).astype(BF16)


def _inproj(h, g, win, qag, kvag, wq, wkv, qn, kn, cos_t, sin_t, lp):
    r = h.shape[0]
    tpb = lp // TM
    row = lambda i: (i, 0)
    tab = lambda i: (i % tpb, 0)
    return pl.pallas_call(
        _inproj_body,
        grid=(r // TM,),
        in_specs=[
            pl.BlockSpec((TM, D_MODEL), row),
            _const_spec((1, D_MODEL)),
            _const_spec((D_MODEL, IN_COLS)),
            _const_spec((1, Q_RANK)),
            _const_spec((1, KV_RANK)),
            _const_spec((Q_RANK, HEADS * QK_PAD)),
            _const_spec((KV_RANK, 2 * HW)),
            _const_spec((1, QK_PAD)),
            _const_spec((1, QK_PAD)),
            pl.BlockSpec((TM, HD), tab),
            pl.BlockSpec((TM, HD), tab),
        ],
        out_specs=[
            pl.BlockSpec((TM, 4 * HW), row),
            pl.BlockSpec((TM, HEADS * QK_PAD), row),
            pl.BlockSpec((TM, HEADS * QK_PAD), row),
            pl.BlockSpec((TM, HW), row),
        ],
        out_shape=[
            jax.ShapeDtypeStruct((r, 4 * HW), F32),
            jax.ShapeDtypeStruct((r, HEADS * QK_PAD), BF16),
            jax.ShapeDtypeStruct((r, HEADS * QK_PAD), BF16),
            jax.ShapeDtypeStruct((r, HW), BF16),
        ],
        compiler_params=_params(("parallel",)),
        name="inproj",
    )(h, g, win, qag, kvag, wq, wkv, qn, kn, cos_t, sin_t)


def _hgrn_body(zh_ref, loglb_ref, log1m_ref, og_ref, tri_ref, o_ref, st_ref):
    @pl.when(pl.program_id(1) == 0)
    def _():
        st_ref[...] = jnp.zeros_like(st_ref)

    tri3 = tri_ref[...]
    loglb = loglb_ref[...]
    log1m = log1m_ref[...]
    og = og_ref[...]
    rows = lax.broadcasted_iota(jnp.int32, (CHUNK, HD), 0)
    sub_of_row = rows // SUB
    tt = lax.broadcasted_iota(jnp.int32, (CHUNK, CHUNK), 0)
    ss_ = lax.broadcasted_iota(jnp.int32, (CHUNK, CHUNK), 1)
    causal = ss_ <= tt

    def chunk(ci, carry):
        r0 = pl.multiple_of(ci * CHUNK, CHUNK)
        hq = zh_ref[pl.ds(r0, CHUNK), 0:HW]
        hf = zh_ref[pl.ds(r0, CHUNK), HW:2 * HW]
        hi = zh_ref[pl.ds(r0, CHUNK), 2 * HW:3 * HW]
        hg = zh_ref[pl.ds(r0, CHUNK), 3 * HW:4 * HW]
        q = hq * jax.nn.sigmoid(hq)
        lsig = jnp.minimum(hf, 0.0) - jnp.log1p(jnp.exp(-jnp.abs(hf)))
        cterm = log1m + lsig
        mx = jnp.maximum(loglb, cterm)
        logf = mx + jnp.log(jnp.exp(loglb - mx) + jnp.exp(cterm - mx))
        k = jnp.exp(cterm - hf)
        g1 = logf.astype(BF16)
        r1 = logf - g1.astype(F32)
        g2 = r1.astype(BF16)
        g3 = (r1 - g2.astype(F32)).astype(BF16)
        b = jnp.dot(tri3, jnp.concatenate([g1, g2, g3], axis=0), preferred_element_type=F32)
        gate = hg * jax.nn.sigmoid(hg)
        for hd in range(HEADS):
            sl = slice(hd * HD, (hd + 1) * HD)
            bh = b[:, sl]
            qh = q[:, sl]
            kh = k[:, sl]
            vh = hi[:, sl].astype(BF16)
            b_last = bh[CHUNK - 1:CHUNK, :]
            qcat = []
            kcat = []
            for j in range(CHUNK // SUB):
                rj = jnp.zeros((1, HD), F32) if j == 0 else bh[j * SUB - 1:j * SUB, :]
                qcat.append((qh * jnp.exp(jnp.minimum(bh - rj, 0.0))).astype(BF16))
                kj = kh * jnp.exp(jnp.minimum(rj - bh, EXP_CLAMP))
                kcat.append(jnp.where(sub_of_row == j, kj, 0.0).astype(BF16))
            qcat = jnp.concatenate(qcat, axis=1)
            kcat = jnp.concatenate(kcat, axis=1)
            att = lax.dot_general(qcat, kcat, (((1,), (1,)), ((), ())), preferred_element_type=F32)
            att = jnp.where(causal, att, 0.0).astype(BF16)
            st = st_ref[hd]
            q_in = (qh * jnp.exp(bh)).astype(BF16)
            o = lax.dot_general(q_in, st.astype(BF16), (((1,), (1,)), ((), ())),
                                preferred_element_type=F32)
            o = o + jnp.dot(att, vh, preferred_element_type=F32)
            k_out = (kh * jnp.exp(b_last - bh)).astype(BF16)
            upd = lax.dot_general(vh, k_out, (((0,), (0,)), ((), ())), preferred_element_type=F32)
            st_ref[hd] = jnp.exp(b_last) * st + upd
            on = _rms(o, og) * gate[:, sl]
            o_ref[pl.ds(r0, CHUNK), sl] = on.astype(BF16)
        return carry

    lax.fori_loop(0, TM // CHUNK, chunk, 0)


def _hgrn(zh, loglb, log1m, og, tri3, nbatch, lp):
    r = zh.shape[0]
    tpb = lp // TM
    return pl.pallas_call(
        _hgrn_body,
        grid=(nbatch, tpb),
        in_specs=[
            pl.BlockSpec((TM, 4 * HW), lambda b, t: (b * tpb + t, 0)),
            _const_spec((1, HW)),
            _const_spec((1, HW)),
            _const_spec((1, HD)),
            _const_spec((CHUNK, 3 * CHUNK)),
        ],
        out_specs=pl.BlockSpec((TM, HW), lambda b, t: (b * tpb + t, 0)),
        out_shape=jax.ShapeDtypeStruct((r, HW), BF16),
        scratch_shapes=[pltpu.VMEM((HEADS, HD, HD), F32)],
        compiler_params=_params(("parallel", "arbitrary")),
        name="hgrn",
    )(zh, loglb, log1m, og, tri3)


def _attn_body(qi_ref, kj_ref, q_ref, k_ref, v_ref, o_ref, m_ref, l_ref, acc_ref):
    step = pl.program_id(1)
    qi = qi_ref[step]
    kj = kj_ref[step]

    @pl.when(kj == 0)
    def _():
        m_ref[...] = jnp.full_like(m_ref, MASK_VALUE)
        l_ref[...] = jnp.zeros_like(l_ref)
        acc_ref[...] = jnp.zeros_like(acc_ref)

    def update(mask):
        for hd in range(HEADS):
            qh = q_ref[:, hd * QK_PAD:(hd + 1) * QK_PAD]
            kh = k_ref[:, hd * QK_PAD:(hd + 1) * QK_PAD]
            s = lax.dot_general(qh, kh, (((1,), (1,)), ((), ())), preferred_element_type=F32)
            if mask is not None:
                s = jnp.where(mask, s, MASK_VALUE)
            m_prev = m_ref[hd]
            m_new = jnp.maximum(m_prev, jnp.max(s, axis=-1, keepdims=True))
            alpha = jnp.exp(m_prev - m_new)
            p = jnp.exp(s - jnp.tile(m_new, (1, TQ // HD)))
            l_ref[hd] = alpha * l_ref[hd] + jnp.sum(p, axis=-1, keepdims=True)
            m_ref[hd] = m_new
            pv = jnp.dot(p.astype(BF16), v_ref[:, hd * HD:(hd + 1) * HD], preferred_element_type=F32)
            acc_ref[:, hd * HD:(hd + 1) * HD] = alpha * acc_ref[:, hd * HD:(hd + 1) * HD] + pv

    edge = jnp.logical_or(kj == 0, kj == qi)

    @pl.when(edge)
    def _():
        row = lax.broadcasted_iota(jnp.int32, (TQ, TQ), 0) + qi * TQ
        col = lax.broadcasted_iota(jnp.int32, (TQ, TQ), 1) + kj * TQ
        col_chunk = jnp.where(col >= PAD, col // CHUNK, jnp.int32(2 ** 30))
        update(col_chunk <= row // CHUNK)

    @pl.when(jnp.logical_not(edge))
    def _():
        update(None)

    @pl.when(kj == qi)
    def _():
        row = lax.broadcasted_iota(jnp.int32, (TQ, HD), 0) + qi * TQ
        valid = row >= PAD
        for hd in range(HEADS):
            o = acc_ref[:, hd * HD:(hd + 1) * HD] / l_ref[hd]
            o_ref[:, hd * HD:(hd + 1) * HD] = jnp.where(valid, o, 0.0).astype(BF16)


def _attn(q, k, v, nbatch, lp):
    r = q.shape[0]
    nb = lp // TQ
    qi = np.concatenate([np.full(i + 1, i) for i in range(nb)]).astype(np.int32)
    kj = np.concatenate([np.arange(i + 1) for i in range(nb)]).astype(np.int32)
    grid_spec = pltpu.PrefetchScalarGridSpec(
        num_scalar_prefetch=2,
        grid=(nbatch, len(qi)),
        in_specs=[
            pl.BlockSpec((TQ, HEADS * QK_PAD), lambda b, s, qi, kj: (b * nb + qi[s], 0)),
            pl.BlockSpec((TQ, HEADS * QK_PAD), lambda b, s, qi, kj: (b * nb + kj[s], 0)),
            pl.BlockSpec((TQ, HW), lambda b, s, qi, kj: (b * nb + kj[s], 0)),
        ],
        out_specs=pl.BlockSpec((TQ, HW), lambda b, s, qi, kj: (b * nb + qi[s], 0)),
        scratch_shapes=[
            pltpu.VMEM((HEADS, TQ, HD), F32),
            pltpu.VMEM((HEADS, TQ, HD), F32),
            pltpu.VMEM((TQ, HW), F32),
        ],
    )
    return pl.pallas_call(
        _attn_body,
        grid_spec=grid_spec,
        out_shape=jax.ShapeDtypeStruct((r, HW), BF16),
        compiler_params=_params(("parallel", "arbitrary")),
        name="attn",
    )(jnp.asarray(qi), jnp.asarray(kj), q, k, v)


def _outproj_mlp_body(h_ref, oa_ref, ob_ref, wo_ref, g_ref, wu_ref, wd_ref, out_ref):
    mix = jnp.dot(oa_ref[...], wo_ref[:HW, :], preferred_element_type=F32)
    mix = mix + jnp.dot(ob_ref[...], wo_ref[HW:, :], preferred_element_type=F32)
    out_ref[...] = _mlp(h_ref[...] + mix, g_ref, wu_ref, wd_ref)


def _outproj_mlp(h, oa, ob, wo, g, wu, wd):
    r = h.shape[0]
    row = lambda i: (i, 0)
    return pl.pallas_call(
        _outproj_mlp_body,
        grid=(r // TM,),
        in_specs=[
            pl.BlockSpec((TM, D_MODEL), row),
            pl.BlockSpec((TM, HW), row),
            pl.BlockSpec((TM, HW), row),
            _const_spec((2 * HW, D_MODEL)),
            _const_spec((1, D_MODEL)),
            _const_spec((D_MODEL, D_FF)),
            _const_spec((D_FF, D_MODEL)),
        ],
        out_specs=pl.BlockSpec((TM, D_MODEL), row),
        out_shape=jax.ShapeDtypeStruct((r, D_MODEL), F32),
        compiler_params=_params(("parallel",)),
        name="outproj_mlp",
    )(h, oa, ob, wo, g, wu, wd)


def _pool_mlp_body(tpb, h_ref, halo_ref, gm_ref, pw_ref, ps_ref, g_ref, wu_ref, wd_ref,
                   out_ref, u_ref, a_ref, b_ref):
    i = pl.program_id(0)
    h = h_ref[...]
    gm = gm_ref[...]
    keep = (i % tpb != 0).astype(F32)
    u_ref[0:HALO, :] = _rms(halo_ref[...], gm) * keep
    u = _rms(h, gm)
    u_ref[HALO:, :] = u
    n = TM + HALO
    g = POOL_G
    a_ref[8:n, :] = u_ref[8:n, :] + u_ref[7:n - 1, :]
    b_ref[16:n, g:] = a_ref[16:n, g:] + a_ref[14:n - 2, g:]
    a_ref[24:n, 2 * g:] = b_ref[24:n, 2 * g:] + b_ref[20:n - 4, 2 * g:]
    b_ref[32:n, 3 * g:] = a_ref[32:n, 3 * g:] + a_ref[24:n - 8, 3 * g:]
    wins = (a_ref[HALO:, 0:g], b_ref[HALO:, g:2 * g], a_ref[HALO:, 2 * g:3 * g], b_ref[HALO:, 3 * g:])
    pos = lax.broadcasted_iota(jnp.int32, (TM, g), 0) + ((i % tpb) * TM - PAD)
    cnt = jnp.maximum(pos + 1, 1).astype(F32)
    ps = ps_ref[...]
    ys = []
    for gi, w in enumerate(POOL_WINDOWS):
        d = wins[gi] / jnp.minimum(cnt, float(w)) - u[:, gi * g:(gi + 1) * g]
        y = jnp.dot(d.astype(BF16), pw_ref[gi], preferred_element_type=F32)
        ys.append(y * ps[:, gi * g:(gi + 1) * g])
    h = h + jnp.concatenate(ys, axis=1)
    out_ref[...] = _mlp(h, g_ref, wu_ref, wd_ref)


def _pool_mlp(h, gm, pw, ps, g, wu, wd, lp):
    r = h.shape[0]
    tpb = lp // TM
    row = lambda i: (i, 0)
    return pl.pallas_call(
        functools.partial(_pool_mlp_body, tpb),
        grid=(r // TM,),
        in_specs=[
            pl.BlockSpec((TM, D_MODEL), row),
            pl.BlockSpec((HALO, D_MODEL), lambda i: (jnp.maximum(i * (TM // HALO) - 1, 0), 0)),
            _const_spec((1, D_MODEL)),
            _const_spec((len(POOL_WINDOWS), POOL_G, POOL_G)),
            _const_spec((1, D_MODEL)),
            _const_spec((1, D_MODEL)),
            _const_spec((D_MODEL, D_FF)),
            _const_spec((D_FF, D_MODEL)),
        ],
        out_specs=pl.BlockSpec((TM, D_MODEL), row),
        out_shape=jax.ShapeDtypeStruct((r, D_MODEL), F32),
        scratch_shapes=[pltpu.VMEM((TM + HALO, D_MODEL), F32)] * 3,
        compiler_params=_params(("parallel",)),
        name="pool_mlp",
    )(h, h, gm, pw, ps, g, wu, wd)


def _rope_cols(w):
    z = jnp.zeros(w.shape[:-1] + (32,), w.dtype)
    return jnp.concatenate([w[..., :32], z, w[..., 32:], z], axis=-1)


def _qk_cols(w):
    w = w.reshape(w.shape[:-1] + (HEADS, QK_DIM))
    w = jnp.concatenate([w[..., :HD], _rope_cols(w[..., HD:])], axis=-1)
    return w.reshape(w.shape[:-2] + (HEADS * QK_PAD,))


def _rope_tables(lp):
    half = ROPE // 2
    inv = ROPE_THETA ** (-jnp.arange(half, dtype=F32) / half)
    pos = jnp.maximum(jnp.arange(lp, dtype=F32) - PAD, 0.0)
    ang = pos[:, None] * inv[None, :]
    c, s = jnp.cos(ang), jnp.sin(ang)
    z = jnp.zeros_like(c)
    return (jnp.concatenate([c, z, c, z], axis=1), jnp.concatenate([-s, z, s, z], axis=1))


def kernel(x, meta_tokens, mix_norm, mlp_norm, w_mlp_up, w_mlp_down, w_in, hgrn_lb, hgrn_out_norm, mla_q_a_norm, mla_kv_a_norm, w_q_up, w_kv_up, q_norm, k_norm, w_out, pool_w, pool_scale):
    nbatch, seq, _ = x.shape
    assert seq % TM == 0
    lp = seq + LEAD
    depth = mix_norm.shape[0]

    meta = jnp.broadcast_to(meta_tokens[None].astype(x.dtype), (nbatch, N_META, D_MODEL))
    h = jnp.concatenate([jnp.zeros((nbatch, PAD, D_MODEL), x.dtype), meta, x], axis=1)
    h = h.reshape(nbatch * lp, D_MODEL)

    cos_t, sin_t = _rope_tables(lp)
    lb_cum = jnp.cumsum(jax.nn.softmax(hgrn_lb.astype(F32), axis=0), axis=0)
    lower = lb_cum - lb_cum[0:1]
    log_lb = jnp.log(lower)
    log_1m = jnp.log1p(-lower)
    tri = jnp.tril(jnp.ones((CHUNK, CHUNK), F32)).astype(BF16)
    tri3 = jnp.concatenate([tri, tri, tri], axis=1)

    w_in_l = jnp.concatenate(
        [w_in[..., :4 * HW + Q_RANK + KV_RANK], _rope_cols(w_in[..., 4 * HW + Q_RANK + KV_RANK:])],
        axis=-1).astype(BF16)
    wq_l = _qk_cols(w_q_up).astype(BF16)
    wkv = w_kv_up.reshape(w_kv_up.shape[0], KV_RANK, HEADS, 2 * HD)
    wkv_l = jnp.concatenate([wkv[..., :HD].reshape(-1, KV_RANK, HW),
                             wkv[..., HD:].reshape(-1, KV_RANK, HW)], axis=-1).astype(BF16)
    qn_l = jnp.concatenate([q_norm[:, :HD], _rope_cols(q_norm[:, HD:])], axis=-1)
    kn_l = jnp.concatenate([k_norm[:, :HD], _rope_cols(k_norm[:, HD:])], axis=-1)
    wo_l = w_out.astype(BF16)
    wu_l = w_mlp_up.astype(BF16)
    wd_l = w_mlp_down.astype(BF16)
    pw_l = pool_w.astype(BF16)

    for layer in range(depth):
        if layer % 2 == 0:
            e = layer // 2
            zh, q, k, v = _inproj(h, mix_norm[layer][None], w_in_l[e], mla_q_a_norm[e][None],
                                  mla_kv_a_norm[e][None], wq_l[e], wkv_l[e], qn_l[e][None],
                                  kn_l[e][None], cos_t, sin_t, lp)
            oa = _hgrn(zh, log_lb[e][None], log_1m[e][None], hgrn_out_norm[e][None], tri3, nbatch, lp)
            ob = _attn(q, k, v, nbatch, lp)
            h = _outproj_mlp(h, oa, ob, wo_l[e], mlp_norm[layer][None], wu_l[layer], wd_l[layer])
        else:
            o = layer // 2
            h = _pool_mlp(h, mix_norm[layer][None], pw_l[o], pool_scale[o][None],
                          mlp_norm[layer][None], wu_l[layer], wd_l[layer], lp)

    return h.reshape(nbatch, lp, D_MODEL)[:, LEAD:]
```

```python
import functools

import numpy as np
import jax
import jax.numpy as jnp
from jax import lax
from jax.experimental import pallas as pl
from jax.experimental.pallas import tpu as pltpu

F32 = jnp.float32
BF16 = jnp.bfloat16

D_MODEL = 1024
D_FF = 4 * D_MODEL
EPS = 1e-6
N_META = 16
CHUNK = 64
HEADS = 4
HD = 128
HW = HEADS * HD
ROPE = 64
QK_DIM = HD + ROPE
QK_PAD = 256
Q_RANK = 256
KV_RANK = 256
ROPE_THETA = 10000.0
POOL_WINDOWS = (2, 4, 8, 16)
POOL_G = D_MODEL // len(POOL_WINDOWS)

LEAD = 512
PAD = LEAD - N_META
TM = 512
TQ = 512
SUB = 16
HALO = 32
IN_COLS = 4 * HW + Q_RANK + KV_RANK + HD
Q_SCALE = QK_DIM ** -0.5 * float(np.log2(np.e))
MAX_FIXED_SHIFT = 56.0
MASK_VALUE = -1e30
EXP_CLAMP = 80.0
VMEM_LIMIT = 56 * 1024 * 1024


def _rms(x, g):
    return x * lax.rsqrt(jnp.mean(x * x, axis=-1, keepdims=True) + EPS) * g


def _const_spec(shape):
    nd = len(shape)
    return pl.BlockSpec(shape, lambda *_: (0,) * nd, pipeline_mode=pl.Buffered(1))


def _params(sem):
    return pltpu.CompilerParams(dimension_semantics=sem, vmem_limit_bytes=VMEM_LIMIT)


def _mlp(h, g_ref, wu_ref, wd_ref):
    hn = _rms(h, g_ref[...]).astype(BF16)
    acc = jnp.zeros_like(h)
    fc = 1024
    for c in range(D_FF // fc):
        a = jnp.dot(hn, wu_ref[:, c * fc:(c + 1) * fc], preferred_element_type=F32)
        a = jnp.maximum(a, 0.0)
        a = (a * a).astype(BF16)
        acc = acc + jnp.dot(a, wd_ref[c * fc:(c + 1) * fc, :], preferred_element_type=F32)
    return h + acc


def _rope(x, c, s):
    return x * c + pltpu.roll(x, 64, axis=1) * s


def _inproj_body(h_ref, g_ref, win_ref, qag_ref, kvag_ref, wq_ref, wkv_ref,
                 qn_ref, kn_ref, cos_ref, sin_ref,
                 zh_ref, q_ref, k_ref, v_ref):
    u = _rms(h_ref[...], g_ref[...]).astype(BF16)
    z = jnp.dot(u, win_ref[...], preferred_element_type=F32)
    zh_ref[...] = z[:, :4 * HW]
    cq = z[:, 4 * HW:4 * HW + Q_RANK]
    ckv = z[:, 4 * HW + Q_RANK:4 * HW + Q_RANK + KV_RANK]
    kr = z[:, 4 * HW + Q_RANK + KV_RANK:]
    q = jnp.dot(_rms(cq, qag_ref[...]).astype(BF16), wq_ref[...], preferred_element_type=F32)
    kv = jnp.dot(_rms(ckv, kvag_ref[...]).astype(BF16), wkv_ref[...], preferred_element_type=F32)
    v_ref[...] = kv[:, HW:].astype(BF16)
    c = cos_ref[...]
    s = sin_ref[...]
    qg = qn_ref[...]
    kg = kn_ref[...]
    kr_ss = jnp.sum(kr * kr, axis=-1, keepdims=True)
    q_scale = Q_SCALE
    for hd in range(HEADS):
        qa = q[:, hd * QK_PAD:hd * QK_PAD + HD]
        qb = q[:, hd * QK_PAD + HD:(hd + 1) * QK_PAD]
        ss = (jnp.sum(qa * qa, axis=-1, keepdims=True) + jnp.sum(qb * qb, axis=-1, keepdims=True))
        inv = lax.rsqrt(ss * (1.0 / QK_DIM) + EPS)
        q_ref[:, hd * QK_PAD:hd * QK_PAD + HD] = (qa * inv * qg[:, :HD] * q_scale).astype(BF16)
        q_ref[:, hd * QK_PAD + HD:(hd + 1) * QK_PAD] = (
            _rope(qb * inv * qg[:, HD:], c, s) * q_scale).astype(BF16)
        ka = kv[:, hd * HD:(hd + 1) * HD]
        ss = jnp.sum(ka * ka, axis=-1, keepdims=True) + kr_ss
        inv = lax.rsqrt(ss * (1.0 / QK_DIM) + EPS)
        k_ref[:, hd * QK_PAD:hd * QK_PAD + HD] = (ka * inv * kg[:, :HD]).astype(BF16)
        k_ref[:, hd * QK_PAD + HD:(hd + 1) * QK_PAD] = _rope(kr * inv * kg[:, HD:], c, s).astype(BF16)


def _inproj(h, g, win, qag, kvag, wq, wkv, qn, kn, cos_t, sin_t, lp):
    r = h.shape[0]
    tpb = lp // TM
    row = lambda i: (i, 0)
    tab = lambda i: (i % tpb, 0)
    return pl.pallas_call(
        _inproj_body,
        grid=(r // TM,),
        in_specs=[
            pl.BlockSpec((TM, D_MODEL), row),
            _const_spec((1, D_MODEL)),
            _const_spec((D_MODEL, IN_COLS)),
            _const_spec((1, Q_RANK)),
            _const_spec((1, KV_RANK)),
            _const_spec((Q_RANK, HEADS * QK_PAD)),
            _const_spec((KV_RANK, 2 * HW)),
            _const_spec((1, QK_PAD)),
            _const_spec((1, QK_PAD)),
            pl.BlockSpec((TM, HD), tab),
            pl.BlockSpec((TM, HD), tab),
        ],
        out_specs=[
            pl.BlockSpec((TM, 4 * HW), row),
            pl.BlockSpec((TM, HEADS * QK_PAD), row),
            pl.BlockSpec((TM, HEADS * QK_PAD), row),
            pl.BlockSpec((TM, HW), row),
        ],
        out_shape=[
            jax.ShapeDtypeStruct((r, 4 * HW), F32),
            jax.ShapeDtypeStruct((r, HEADS * QK_PAD), BF16),
            jax.ShapeDtypeStruct((r, HEADS * QK_PAD), BF16),
            jax.ShapeDtypeStruct((r, HW), BF16),
        ],
        compiler_params=_params(("parallel",)),
        name="inproj",
    )(h, g, win, qag, kvag, wq, wkv, qn, kn, cos_t, sin_t)


def _hgrn_body(zh_ref, loglb_ref, log1m_ref, og_ref, tri_ref, o_ref, st_ref):
    @pl.when(pl.program_id(1) == 0)
    def _():
        st_ref[...] = jnp.zeros_like(st_ref)

    tri3 = tri_ref[...]
    loglb = loglb_ref[...]
    log1m = log1m_ref[...]
    og = og_ref[...]
    rows = lax.broadcasted_iota(jnp.int32, (CHUNK, HD), 0)
    sub_of_row = rows // SUB
    tt = lax.broadcasted_iota(jnp.int32, (CHUNK, CHUNK), 0)
    ss_ = lax.broadcasted_iota(jnp.int32, (CHUNK, CHUNK), 1)
    causal = ss_ <= tt

    def chunk(ci, carry):
        r0 = pl.multiple_of(ci * CHUNK, CHUNK)
        hq = zh_ref[pl.ds(r0, CHUNK), 0:HW]
        hf = zh_ref[pl.ds(r0, CHUNK), HW:2 * HW]
        hi = zh_ref[pl.ds(r0, CHUNK), 2 * HW:3 * HW]
        hg = zh_ref[pl.ds(r0, CHUNK), 3 * HW:4 * HW]
        q = hq * jax.nn.sigmoid(hq)
        lsig = jnp.minimum(hf, 0.0) - jnp.log1p(jnp.exp(-jnp.abs(hf)))
        cterm = log1m + lsig
        mx = jnp.maximum(loglb, cterm)
        logf = mx + jnp.log(jnp.exp(loglb - mx) + jnp.exp(cterm - mx))
        k = jnp.exp(cterm - hf)
        g1 = logf.astype(BF16)
        r1 = logf - g1.astype(F32)
        g2 = r1.astype(BF16)
        g3 = (r1 - g2.astype(F32)).astype(BF16)
        b = jnp.dot(tri3, jnp.concatenate([g1, g2, g3], axis=0), preferred_element_type=F32)
        gate = hg * jax.nn.sigmoid(hg)
        for hd in range(HEADS):
            sl = slice(hd * HD, (hd + 1) * HD)
            bh = b[:, sl]
            qh = q[:, sl]
            kh = k[:, sl]
            vh = hi[:, sl].astype(BF16)
            b_last = bh[CHUNK - 1:CHUNK, :]
            qcat = []
            kcat = []
            for j in range(CHUNK // SUB):
                rj = jnp.zeros((1, HD), F32) if j == 0 else bh[j * SUB - 1:j * SUB, :]
                qcat.append((qh * jnp.exp(jnp.minimum(bh - rj, 0.0))).astype(BF16))
                kj = kh * jnp.exp(jnp.minimum(rj - bh, EXP_CLAMP))
                kcat.append(jnp.where(sub_of_row == j, kj, 0.0).astype(BF16))
            qcat = jnp.concatenate(qcat, axis=1)
            kcat = jnp.concatenate(kcat, axis=1)
            att = lax.dot_general(qcat, kcat, (((1,), (1,)), ((), ())), preferred_element_type=F32)
            att = jnp.where(causal, att, 0.0).astype(BF16)
            st = st_ref[hd]
            q_in = (qh * jnp.exp(bh)).astype(BF16)
            o = lax.dot_general(q_in, st.astype(BF16), (((1,), (1,)), ((), ())),
                                preferred_element_type=F32)
            o = o + jnp.dot(att, vh, preferred_element_type=F32)
            k_out = (kh * jnp.exp(b_last - bh)).astype(BF16)
            upd = lax.dot_general(vh, k_out, (((0,), (0,)), ((), ())), preferred_element_type=F32)
            st_ref[hd] = jnp.exp(b_last) * st + upd
            on = _rms(o, og) * gate[:, sl]
            o_ref[pl.ds(r0, CHUNK), sl] = on.astype(BF16)
        return carry

    lax.fori_loop(0, TM // CHUNK, chunk, 0)


def _hgrn(zh, loglb, log1m, og, tri3, nbatch, lp):
    r = zh.shape[0]
    tpb = lp // TM
    return pl.pallas_call(
        _hgrn_body,
        grid=(nbatch, tpb),
        in_specs=[
            pl.BlockSpec((TM, 4 * HW), lambda b, t: (b * tpb + t, 0)),
            _const_spec((1, HW)),
            _const_spec((1, HW)),
            _const_spec((1, HD)),
            _const_spec((CHUNK, 3 * CHUNK)),
        ],
        out_specs=pl.BlockSpec((TM, HW), lambda b, t: (b * tpb + t, 0)),
        out_shape=jax.ShapeDtypeStruct((r, HW), BF16),
        scratch_shapes=[pltpu.VMEM((HEADS, HD, HD), F32)],
        compiler_params=_params(("parallel", "arbitrary")),
        name="hgrn",
    )(zh, loglb, log1m, og, tri3)


def _qk_scores(q_ref, k_ref, hd, k_rows=slice(None)):
    qh = q_ref[:, hd * QK_PAD:(hd + 1) * QK_PAD]
    kh = k_ref[k_rows, hd * QK_PAD:(hd + 1) * QK_PAD]
    return lax.dot_general(qh, kh, (((1,), (1,)), ((), ())), preferred_element_type=F32)


def _attn_fast_body(qi_ref, kj_ref, bound_ref, q_ref, k_ref, v_ref, o_ref, l_ref, acc_ref):
    step = pl.program_id(1)
    qi = qi_ref[step]
    kj = kj_ref[step]
    bound = bound_ref[0]
    tail = slice(TQ - HD, TQ)

    @pl.when(kj == 0)
    def _():
        col = lax.broadcasted_iota(jnp.int32, (TQ, HD), 1) + (TQ - HD)
        is_key = col >= PAD
        for hd in range(HEADS):
            p = jnp.where(is_key, jnp.exp2(_qk_scores(q_ref, k_ref, hd, tail) - bound), 0.0)
            l_ref[hd] = p
            acc_ref[:, hd * HD:(hd + 1) * HD] = jnp.dot(
                p.astype(BF16), v_ref[tail, hd * HD:(hd + 1) * HD], preferred_element_type=F32)

    def update(mask):
        for hd in range(HEADS):
            p = jnp.exp2(_qk_scores(q_ref, k_ref, hd) - bound)
            if mask is not None:
                p = jnp.where(mask, p, 0.0)
            part = p[:, 0:HD]
            for c in range(1, TQ // HD):
                part = part + p[:, c * HD:(c + 1) * HD]
            l_ref[hd] += part
            acc_ref[:, hd * HD:(hd + 1) * HD] += jnp.dot(
                p.astype(BF16), v_ref[:, hd * HD:(hd + 1) * HD], preferred_element_type=F32)

    @pl.when(jnp.logical_and(kj > 0, kj < qi))
    def _():
        update(None)

    @pl.when(jnp.logical_and(kj > 0, kj == qi))
    def _():
        row = lax.broadcasted_iota(jnp.int32, (TQ, TQ), 0)
        col = lax.broadcasted_iota(jnp.int32, (TQ, TQ), 1)
        update(col // CHUNK <= row // CHUNK)

    @pl.when(kj == qi)
    def _():
        row = lax.broadcasted_iota(jnp.int32, (TQ, HD), 0) + qi * TQ
        valid = row >= PAD
        for hd in range(HEADS):
            o = acc_ref[:, hd * HD:(hd + 1) * HD] / jnp.sum(l_ref[hd], axis=-1, keepdims=True)
            o_ref[:, hd * HD:(hd + 1) * HD] = jnp.where(valid, o, 0.0).astype(BF16)


def _attn_body(qi_ref, kj_ref, bound_ref, q_ref, k_ref, v_ref, o_ref, m_ref, l_ref, acc_ref):
    step = pl.program_id(1)
    qi = qi_ref[step]
    kj = kj_ref[step]

    @pl.when(kj == 0)
    def _():
        m_ref[...] = jnp.full_like(m_ref, MASK_VALUE)
        l_ref[...] = jnp.zeros_like(l_ref)
        acc_ref[...] = jnp.zeros_like(acc_ref)

    def update(mask):
        for hd in range(HEADS):
            s = _qk_scores(q_ref, k_ref, hd)
            if mask is not None:
                s = jnp.where(mask, s, MASK_VALUE)
            m_prev = m_ref[hd]
            m_new = jnp.maximum(m_prev, jnp.max(s, axis=-1, keepdims=True))
            alpha = jnp.exp2(m_prev - m_new)
            p = jnp.exp2(s - jnp.tile(m_new, (1, TQ // HD)))
            l_ref[hd] = alpha * l_ref[hd] + jnp.sum(p, axis=-1, keepdims=True)
            m_ref[hd] = m_new
            pv = jnp.dot(p.astype(BF16), v_ref[:, hd * HD:(hd + 1) * HD], preferred_element_type=F32)
            acc_ref[:, hd * HD:(hd + 1) * HD] = alpha * acc_ref[:, hd * HD:(hd + 1) * HD] + pv

    edge = jnp.logical_or(kj == 0, kj == qi)

    @pl.when(edge)
    def _():
        row = lax.broadcasted_iota(jnp.int32, (TQ, TQ), 0) + qi * TQ
        col = lax.broadcasted_iota(jnp.int32, (TQ, TQ), 1) + kj * TQ
        col_chunk = jnp.where(col >= PAD, col // CHUNK, jnp.int32(2 ** 30))
        update(col_chunk <= row // CHUNK)

    @pl.when(jnp.logical_not(edge))
    def _():
        update(None)

    @pl.when(kj == qi)
    def _():
        row = lax.broadcasted_iota(jnp.int32, (TQ, HD), 0) + qi * TQ
        valid = row >= PAD
        for hd in range(HEADS):
            o = acc_ref[:, hd * HD:(hd + 1) * HD] / l_ref[hd]
            o_ref[:, hd * HD:(hd + 1) * HD] = jnp.where(valid, o, 0.0).astype(BF16)


def _attn(q, k, v, bound, nbatch, lp, fixed_shift):
    r = q.shape[0]
    nb = lp // TQ
    qi = np.concatenate([np.full(i + 1, i) for i in range(nb)]).astype(np.int32)
    kj = np.concatenate([np.arange(i + 1) for i in range(nb)]).astype(np.int32)
    stats = [pltpu.VMEM((HEADS, TQ, HD), F32)] * (1 if fixed_shift else 2)
    grid_spec = pltpu.PrefetchScalarGridSpec(
        num_scalar_prefetch=3,
        grid=(nbatch, len(qi)),
        in_specs=[
            pl.BlockSpec((TQ, HEADS * QK_PAD), lambda b, s, qi, kj, bd: (b * nb + qi[s], 0)),
            pl.BlockSpec((TQ, HEADS * QK_PAD), lambda b, s, qi, kj, bd: (b * nb + kj[s], 0)),
            pl.BlockSpec((TQ, HW), lambda b, s, qi, kj, bd: (b * nb + kj[s], 0)),
        ],
        out_specs=pl.BlockSpec((TQ, HW), lambda b, s, qi, kj, bd: (b * nb + qi[s], 0)),
        scratch_shapes=stats + [pltpu.VMEM((TQ, HW), F32)],
    )
    return pl.pallas_call(
        _attn_fast_body if fixed_shift else _attn_body,
        grid_spec=grid_spec,
        out_shape=jax.ShapeDtypeStruct((r, HW), BF16),
        compiler_params=_params(("parallel", "arbitrary")),
        name="attn_fixed_shift" if fixed_shift else "attn_online",
    )(jnp.asarray(qi), jnp.asarray(kj), bound, q, k, v)


def _outproj_mlp_body(h_ref, oa_ref, ob_ref, wo_ref, g_ref, wu_ref, wd_ref, out_ref):
    mix = jnp.dot(oa_ref[...], wo_ref[:HW, :], preferred_element_type=F32)
    mix = mix + jnp.dot(ob_ref[...], wo_ref[HW:, :], preferred_element_type=F32)
    out_ref[...] = _mlp(h_ref[...] + mix, g_ref, wu_ref, wd_ref)


def _outproj_mlp(h, oa, ob, wo, g, wu, wd):
    r = h.shape[0]
    row = lambda i: (i, 0)
    return pl.pallas_call(
        _outproj_mlp_body,
        grid=(r // TM,),
        in_specs=[
            pl.BlockSpec((TM, D_MODEL), row),
            pl.BlockSpec((TM, HW), row),
            pl.BlockSpec((TM, HW), row),
            _const_spec((2 * HW, D_MODEL)),
            _const_spec((1, D_MODEL)),
            _const_spec((D_MODEL, D_FF)),
            _const_spec((D_FF, D_MODEL)),
        ],
        out_specs=pl.BlockSpec((TM, D_MODEL), row),
        out_shape=jax.ShapeDtypeStruct((r, D_MODEL), F32),
        compiler_params=_params(("parallel",)),
        name="outproj_mlp",
    )(h, oa, ob, wo, g, wu, wd)


def _pool_mlp_body(tpb, h_ref, halo_ref, gm_ref, pw_ref, ps_ref, g_ref, wu_ref, wd_ref,
                   out_ref, u_ref, a_ref, b_ref):
    i = pl.program_id(0)
    h = h_ref[...]
    gm = gm_ref[...]
    keep = (i % tpb != 0).astype(F32)
    u_ref[0:HALO, :] = _rms(halo_ref[...], gm) * keep
    u = _rms(h, gm)
    u_ref[HALO:, :] = u
    n = TM + HALO
    g = POOL_G
    a_ref[8:n, :] = u_ref[8:n, :] + u_ref[7:n - 1, :]
    b_ref[16:n, g:] = a_ref[16:n, g:] + a_ref[14:n - 2, g:]
    a_ref[24:n, 2 * g:] = b_ref[24:n, 2 * g:] + b_ref[20:n - 4, 2 * g:]
    b_ref[32:n, 3 * g:] = a_ref[32:n, 3 * g:] + a_ref[24:n - 8, 3 * g:]
    wins = (a_ref[HALO:, 0:g], b_ref[HALO:, g:2 * g], a_ref[HALO:, 2 * g:3 * g], b_ref[HALO:, 3 * g:])
    pos = lax.broadcasted_iota(jnp.int32, (TM, g), 0) + ((i % tpb) * TM - PAD)
    cnt = jnp.maximum(pos + 1, 1).astype(F32)
    ps = ps_ref[...]
    ys = []
    for gi, w in enumerate(POOL_WINDOWS):
        d = wins[gi] / jnp.minimum(cnt, float(w)) - u[:, gi * g:(gi + 1) * g]
        y = jnp.dot(d.astype(BF16), pw_ref[gi], preferred_element_type=F32)
        ys.append(y * ps[:, gi * g:(gi + 1) * g])
    h = h + jnp.concatenate(ys, axis=1)
    out_ref[...] = _mlp(h, g_ref, wu_ref, wd_ref)


def _pool_mlp(h, gm, pw, ps, g, wu, wd, lp):
    r = h.shape[0]
    tpb = lp // TM
    row = lambda i: (i, 0)
    return pl.pallas_call(
        functools.partial(_pool_mlp_body, tpb),
        grid=(r // TM,),
        in_specs=[
            pl.BlockSpec((TM, D_MODEL), row),
            pl.BlockSpec((HALO, D_MODEL), lambda i: (jnp.maximum(i * (TM // HALO) - 1, 0), 0)),
            _const_spec((1, D_MODEL)),
            _const_spec((len(POOL_WINDOWS), POOL_G, POOL_G)),
            _const_spec((1, D_MODEL)),
            _const_spec((1, D_MODEL)),
            _const_spec((D_MODEL, D_FF)),
            _const_spec((D_FF, D_MODEL)),
        ],
        out_specs=pl.BlockSpec((TM, D_MODEL), row),
        out_shape=jax.ShapeDtypeStruct((r, D_MODEL), F32),
        scratch_shapes=[pltpu.VMEM((TM + HALO, D_MODEL), F32)] * 3,
        compiler_params=_params(("parallel",)),
        name="pool_mlp",
    )(h, h, gm, pw, ps, g, wu, wd)


def _rope_cols(w):
    z = jnp.zeros(w.shape[:-1] + (32,), w.dtype)
    return jnp.concatenate([w[..., :32], z, w[..., 32:], z], axis=-1)


def _qk_cols(w):
    w = w.reshape(w.shape[:-1] + (HEADS, QK_DIM))
    w = jnp.concatenate([w[..., :HD], _rope_cols(w[..., HD:])], axis=-1)
    return w.reshape(w.shape[:-2] + (HEADS * QK_PAD,))


def _rope_tables(lp):
    half = ROPE // 2
    inv = ROPE_THETA ** (-jnp.arange(half, dtype=F32) / half)
    pos = jnp.maximum(jnp.arange(lp, dtype=F32) - PAD, 0.0)
    ang = pos[:, None] * inv[None, :]
    c, s = jnp.cos(ang), jnp.sin(ang)
    z = jnp.zeros_like(c)
    return (jnp.concatenate([c, z, c, z], axis=1), jnp.concatenate([-s, z, s, z], axis=1))


def kernel(x, meta_tokens, mix_norm, mlp_norm, w_mlp_up, w_mlp_down, w_in, hgrn_lb, hgrn_out_norm, mla_q_a_norm, mla_kv_a_norm, w_q_up, w_kv_up, q_norm, k_norm, w_out, pool_w, pool_scale):
    nbatch, seq, _ = x.shape
    assert seq % TM == 0
    lp = seq + LEAD
    depth = mix_norm.shape[0]

    meta = jnp.broadcast_to(meta_tokens[None].astype(x.dtype), (nbatch, N_META, D_MODEL))
    h = jnp.concatenate([jnp.zeros((nbatch, PAD, D_MODEL), x.dtype), meta, x], axis=1)
    h = h.reshape(nbatch * lp, D_MODEL)

    cos_t, sin_t = _rope_tables(lp)
    lb_cum = jnp.cumsum(jax.nn.softmax(hgrn_lb.astype(F32), axis=0), axis=0)
    lower = lb_cum - lb_cum[0:1]
    log_lb = jnp.log(lower)
    log_1m = jnp.log1p(-lower)
    tri = jnp.tril(jnp.ones((CHUNK, CHUNK), F32)).astype(BF16)
    tri3 = jnp.concatenate([tri, tri, tri], axis=1)

    w_in_l = jnp.concatenate(
        [w_in[..., :4 * HW + Q_RANK + KV_RANK], _rope_cols(w_in[..., 4 * HW + Q_RANK + KV_RANK:])],
        axis=-1).astype(BF16)
    wq_l = _qk_cols(w_q_up).astype(BF16)
    wkv = w_kv_up.reshape(w_kv_up.shape[0], KV_RANK, HEADS, 2 * HD)
    wkv_l = jnp.concatenate([wkv[..., :HD].reshape(-1, KV_RANK, HW),
                             wkv[..., HD:].reshape(-1, KV_RANK, HW)], axis=-1).astype(BF16)
    qn_l = jnp.concatenate([q_norm[:, :HD], _rope_cols(q_norm[:, HD:])], axis=-1)
    kn_l = jnp.concatenate([k_norm[:, :HD], _rope_cols(k_norm[:, HD:])], axis=-1)
    wo_l = w_out.astype(BF16)
    wu_l = w_mlp_up.astype(BF16)
    wd_l = w_mlp_down.astype(BF16)
    pw_l = pool_w.astype(BF16)

    for layer in range(depth):
        if layer % 2 == 0:
            e = layer // 2
            zh, q, k, v = _inproj(h, mix_norm[layer][None], w_in_l[e], mla_q_a_norm[e][None],
                                  mla_kv_a_norm[e][None], wq_l[e], wkv_l[e], qn_l[e][None],
                                  kn_l[e][None], cos_t, sin_t, lp)
            oa = _hgrn(zh, log_lb[e][None], log_1m[e][None], hgrn_out_norm[e][None], tri3, nbatch, lp)
            bound = (Q_SCALE * QK_DIM * 1.01) * jnp.max(jnp.abs(q_norm[e])) * jnp.max(jnp.abs(k_norm[e]))
            bound = bound.reshape(1).astype(F32)
            ob = lax.cond(bound[0] <= MAX_FIXED_SHIFT,
                          functools.partial(_attn, nbatch=nbatch, lp=lp, fixed_shift=True),
                          functools.partial(_attn, nbatch=nbatch, lp=lp, fixed_shift=False),
                          q, k, v, bound)
            h = _outproj_mlp(h, oa, ob, wo_l[e], mlp_norm[layer][None], wu_l[layer], wd_l[layer])
        else:
            o = layer // 2
            h = _pool_mlp(h, mix_norm[layer][None], pw_l[o], pool_scale[o][None],
                          mlp_norm[layer][None], wu_l[layer], wd_l[layer], lp)

    return h.reshape(nbatch, lp, D_MODEL)[:, LEAD:]
```

```python
import functools

import numpy as np
import jax
import jax.numpy as jnp
from jax import lax
from jax.experimental import pallas as pl
from jax.experimental.pallas import tpu as pltpu

F32 = jnp.float32
BF16 = jnp.bfloat16

D_MODEL = 1024
D_FF = 4 * D_MODEL
EPS = 1e-6
N_META = 16
CHUNK = 64
HEADS = 4
HD = 128
HW = HEADS * HD
ROPE = 64
QK_DIM = HD + ROPE
QK_PAD = 256
Q_RANK = 256
KV_RANK = 256
ROPE_THETA = 10000.0
POOL_WINDOWS = (2, 4, 8, 16)
POOL_G = D_MODEL // len(POOL_WINDOWS)

LEAD = 512
PAD = LEAD - N_META
TM = 512
TQ = 512
SUB = 16
HALO = 32
IN_COLS = 4 * HW + Q_RANK + KV_RANK + HD
Q_SCALE = QK_DIM ** -0.5 * float(np.log2(np.e))
MAX_FIXED_SHIFT = 56.0
MASK_VALUE = -1e30
EXP_CLAMP = 80.0
TINY = 1e-37
VMEM_LIMIT = 56 * 1024 * 1024


def _rms(x, g):
    return x * lax.rsqrt(jnp.mean(x * x, axis=-1, keepdims=True) + EPS) * g


def _const_spec(shape):
    nd = len(shape)
    return pl.BlockSpec(shape, lambda *_: (0,) * nd, pipeline_mode=pl.Buffered(1))


def _params(sem):
    return pltpu.CompilerParams(dimension_semantics=sem, vmem_limit_bytes=VMEM_LIMIT)


def _mlp(h, g_ref, wu_ref, wd_ref):
    hn = _rms(h, g_ref[...]).astype(BF16)
    acc = jnp.zeros_like(h)
    fc = 1024
    for c in range(D_FF // fc):
        a = jnp.dot(hn, wu_ref[:, c * fc:(c + 1) * fc], preferred_element_type=F32)
        a = jnp.maximum(a, 0.0)
        a = (a * a).astype(BF16)
        acc = acc + jnp.dot(a, wd_ref[c * fc:(c + 1) * fc, :], preferred_element_type=F32)
    return h + acc


def _rope(x, c, s):
    return x * c + pltpu.roll(x, 64, axis=1) * s


def _inproj_body(h_ref, g_ref, win_ref, qag_ref, kvag_ref, wq_ref, wkv_ref,
                 qn_ref, kn_ref, cos_ref, sin_ref,
                 zh_ref, q_ref, k_ref, v_ref):
    halves = [slice(i * (TM // 2), (i + 1) * (TM // 2)) for i in range(2)]
    g = g_ref[...]
    u = [_rms(h_ref[r, :], g).astype(BF16) for r in halves]
    z = [jnp.dot(ui, win_ref[...], preferred_element_type=F32) for ui in u]
    qa_in, kva_in = [], []
    for r, zi in zip(halves, z):
        zh_ref[r, :] = zi[:, :4 * HW]
        qa_in.append(_rms(zi[:, 4 * HW:4 * HW + Q_RANK], qag_ref[...]).astype(BF16))
        kva_in.append(_rms(zi[:, 4 * HW + Q_RANK:4 * HW + Q_RANK + KV_RANK], kvag_ref[...]).astype(BF16))
    qs = [jnp.dot(a, wq_ref[...], preferred_element_type=F32) for a in qa_in]
    kvs = [jnp.dot(a, wkv_ref[...], preferred_element_type=F32) for a in kva_in]
    qg = qn_ref[...]
    kg = kn_ref[...]
    for r, zi, q, kv in zip(halves, z, qs, kvs):
        kr = zi[:, 4 * HW + Q_RANK + KV_RANK:]
        v_ref[r, :] = kv[:, HW:].astype(BF16)
        c = cos_ref[r, :]
        s = sin_ref[r, :]
        kr_ss = jnp.sum(kr * kr, axis=-1, keepdims=True)
        for hd in range(HEADS):
            qa = q[:, hd * QK_PAD:hd * QK_PAD + HD]
            qb = q[:, hd * QK_PAD + HD:(hd + 1) * QK_PAD]
            ss = (jnp.sum(qa * qa, axis=-1, keepdims=True) + jnp.sum(qb * qb, axis=-1, keepdims=True))
            inv = lax.rsqrt(ss * (1.0 / QK_DIM) + EPS)
            q_ref[r, hd * QK_PAD:hd * QK_PAD + HD] = (qa * inv * qg[:, :HD]).astype(BF16)
            q_ref[r, hd * QK_PAD + HD:(hd + 1) * QK_PAD] = _rope(qb * inv * qg[:, HD:], c, s).astype(BF16)
            ka = kv[:, hd * HD:(hd + 1) * HD]
            ss = jnp.sum(ka * ka, axis=-1, keepdims=True) + kr_ss
            inv = lax.rsqrt(ss * (1.0 / QK_DIM) + EPS)
            k_ref[r, hd * QK_PAD:hd * QK_PAD + HD] = (ka * inv * kg[:, :HD]).astype(BF16)
            k_ref[r, hd * QK_PAD + HD:(hd + 1) * QK_PAD] = _rope(kr * inv * kg[:, HD:], c, s).astype(BF16)


def _inproj(h, g, win, qag, kvag, wq, wkv, qn, kn, cos_t, sin_t, lp):
    r = h.shape[0]
    tpb = lp // TM
    row = lambda i: (i, 0)
    tab = lambda i: (i % tpb, 0)
    return pl.pallas_call(
        _inproj_body,
        grid=(r // TM,),
        in_specs=[
            pl.BlockSpec((TM, D_MODEL), row),
            _const_spec((1, D_MODEL)),
            _const_spec((D_MODEL, IN_COLS)),
            _const_spec((1, Q_RANK)),
            _const_spec((1, KV_RANK)),
            _const_spec((Q_RANK, HEADS * QK_PAD)),
            _const_spec((KV_RANK, 2 * HW)),
            _const_spec((1, QK_PAD)),
            _const_spec((1, QK_PAD)),
            pl.BlockSpec((TM, HD), tab),
            pl.BlockSpec((TM, HD), tab),
        ],
        out_specs=[
            pl.BlockSpec((TM, 4 * HW), row),
            pl.BlockSpec((TM, HEADS * QK_PAD), row),
            pl.BlockSpec((TM, HEADS * QK_PAD), row),
            pl.BlockSpec((TM, HW), row),
        ],
        out_shape=[
            jax.ShapeDtypeStruct((r, 4 * HW), F32),
            jax.ShapeDtypeStruct((r, HEADS * QK_PAD), BF16),
            jax.ShapeDtypeStruct((r, HEADS * QK_PAD), BF16),
            jax.ShapeDtypeStruct((r, HW), BF16),
        ],
        compiler_params=_params(("parallel",)),
        name="inproj",
    )(h, g, win, qag, kvag, wq, wkv, qn, kn, cos_t, sin_t)


def _silu(x):
    return x * (0.5 * jnp.tanh(0.5 * x) + 0.5)


def _group_rows(rows):
    return jnp.concatenate([jnp.broadcast_to(r, (SUB, HD)) for r in rows], axis=0)


def _hgrn_body(zh_ref, lb_ref, one_m_lb_ref, og_ref, tri_ref, o_ref, st_ref):
    @pl.when(pl.program_id(1) == 0)
    def _():
        st_ref[...] = jnp.zeros_like(st_ref)

    tri3 = tri_ref[...]
    lb = lb_ref[...]
    one_m_lb = one_m_lb_ref[...]
    og = og_ref[...]
    tt = lax.broadcasted_iota(jnp.int32, (CHUNK, CHUNK), 0)
    ss_ = lax.broadcasted_iota(jnp.int32, (CHUNK, CHUNK), 1)
    causal = ss_ <= tt
    nsub = CHUNK // SUB
    zero_row = jnp.zeros((1, HD), F32)
    one_row = jnp.ones((1, HD), F32)
    zero_sub = jnp.zeros((SUB, HD), BF16)

    def chunk(ci, carry):
        r0 = pl.multiple_of(ci * CHUNK, CHUNK)
        hq = zh_ref[pl.ds(r0, CHUNK), 0:HW]
        hf = zh_ref[pl.ds(r0, CHUNK), HW:2 * HW]
        hi = zh_ref[pl.ds(r0, CHUNK), 2 * HW:3 * HW]
        hg = zh_ref[pl.ds(r0, CHUNK), 3 * HW:4 * HW]
        q = _silu(hq)
        gate = _silu(hg)
        t = jnp.exp(-jnp.abs(hf))
        r = 1.0 / (1.0 + t)
        tr = t * r
        pos = hf >= 0.0
        logf = jnp.log(jnp.maximum(lb + one_m_lb * jnp.where(pos, r, tr), TINY))
        k = one_m_lb * jnp.where(pos, tr, r)
        g1 = logf.astype(BF16)
        r1 = logf - g1.astype(F32)
        g2 = r1.astype(BF16)
        g3 = (r1 - g2.astype(F32)).astype(BF16)
        b = jnp.dot(tri3, jnp.concatenate([g1, g2, g3], axis=0), preferred_element_type=F32)
        heads = range(HEADS)
        sls = [slice(hd * HD, (hd + 1) * HD) for hd in heads]
        vb = hi.astype(BF16)
        att, q_in, k_out, decay = [], [], [], []
        for hd in heads:
            bh = b[:, sls[hd]]
            b_last = bh[CHUNK - 1:CHUNK, :]
            refs = [zero_row] + [bh[i * SUB - 1:i * SUB, :] for i in range(1, nsub)]
            dq = bh - _group_rows(refs)
            qe = q[:, sls[hd]] * jnp.exp(dq)
            ke = k[:, sls[hd]] * jnp.exp(jnp.minimum(-dq, EXP_CLAMP))
            keb = ke.astype(BF16)
            qcat = []
            kcat = []
            for j in range(nsub):
                scale = [zero_row if i < j else one_row if i == j else jnp.exp(refs[i] - refs[j])
                         for i in range(nsub)]
                qcat.append((qe * _group_rows(scale)).astype(BF16))
                kcat.append(jnp.concatenate(
                    [keb[j * SUB:(j + 1) * SUB] if i == j else zero_sub for i in range(nsub)], axis=0))
            q_in.append(qcat[0])
            att.append(lax.dot_general(jnp.concatenate(qcat, axis=1), jnp.concatenate(kcat, axis=1),
                                       (((1,), (1,)), ((), ())), preferred_element_type=F32))
            to_end = [jnp.exp(b_last - refs[i]) for i in range(nsub)]
            k_out.append((ke * _group_rows(to_end)).astype(BF16))
            decay.append(to_end[0])
        st = [st_ref[hd] for hd in heads]
        o = [lax.dot_general(q_in[hd], st[hd].astype(BF16), (((1,), (1,)), ((), ())),
                             preferred_element_type=F32) for hd in heads]
        upd = [lax.dot_general(vb[:, sls[hd]], k_out[hd], (((0,), (0,)), ((), ())),
                               preferred_element_type=F32) for hd in heads]
        for hd in heads:
            st_ref[hd] = decay[hd] * st[hd] + upd[hd]
        for hd in heads:
            a = jnp.where(causal, att[hd], 0.0).astype(BF16)
            oh = o[hd] + jnp.dot(a, vb[:, sls[hd]], preferred_element_type=F32)
            on = _rms(oh, og) * gate[:, sls[hd]]
            o_ref[pl.ds(r0, CHUNK), sls[hd]] = on.astype(BF16)
        return carry

    lax.fori_loop(0, TM // CHUNK, chunk, 0, unroll=4)


def _hgrn(zh, lb, one_m_lb, og, tri3, nbatch, lp):
    r = zh.shape[0]
    tpb = lp // TM
    return pl.pallas_call(
        _hgrn_body,
        grid=(nbatch, tpb),
        in_specs=[
            pl.BlockSpec((TM, 4 * HW), lambda b, t: (b * tpb + t, 0)),
            _const_spec((1, HW)),
            _const_spec((1, HW)),
            _const_spec((1, HD)),
            _const_spec((CHUNK, 3 * CHUNK)),
        ],
        out_specs=pl.BlockSpec((TM, HW), lambda b, t: (b * tpb + t, 0)),
        out_shape=jax.ShapeDtypeStruct((r, HW), BF16),
        scratch_shapes=[pltpu.VMEM((HEADS, HD, HD), F32)],
        compiler_params=_params(("parallel", "arbitrary")),
        name="hgrn",
    )(zh, lb, one_m_lb, og, tri3)


def _qk_scores(q_ref, k_ref, hd, k_rows=slice(None)):
    qh = q_ref[:, hd * QK_PAD:(hd + 1) * QK_PAD]
    kh = k_ref[k_rows, hd * QK_PAD:(hd + 1) * QK_PAD]
    return lax.dot_general(qh, kh, (((1,), (1,)), ((), ())), preferred_element_type=F32)


def _attn_fast_body(qi_ref, kj_ref, bound_ref, q_ref, k_ref, v_ref, o_ref, l_ref, acc_ref):
    step = pl.program_id(1)
    qi = qi_ref[step]
    kj = kj_ref[step]
    bound = bound_ref[0]
    tail = slice(TQ - HD, TQ)

    @pl.when(kj == 0)
    def _():
        col = lax.broadcasted_iota(jnp.int32, (TQ, HD), 1) + (TQ - HD)
        is_key = col >= PAD
        for hd in range(HEADS):
            p = jnp.where(is_key, jnp.exp2(_qk_scores(q_ref, k_ref, hd, tail) - bound), 0.0)
            l_ref[hd] = p
            acc_ref[:, hd * HD:(hd + 1) * HD] = jnp.dot(
                p.astype(BF16), v_ref[tail, hd * HD:(hd + 1) * HD], preferred_element_type=F32)

    def update(mask):
        for hd in range(HEADS):
            p = jnp.exp2(_qk_scores(q_ref, k_ref, hd) - bound)
            if mask is not None:
                p = jnp.where(mask, p, 0.0)
            part = p[:, 0:HD]
            for c in range(1, TQ // HD):
                part = part + p[:, c * HD:(c + 1) * HD]
            l_ref[hd] += part
            acc_ref[:, hd * HD:(hd + 1) * HD] += jnp.dot(
                p.astype(BF16), v_ref[:, hd * HD:(hd + 1) * HD], preferred_element_type=F32)

    @pl.when(jnp.logical_and(kj > 0, kj < qi))
    def _():
        update(None)

    @pl.when(jnp.logical_and(kj > 0, kj == qi))
    def _():
        row = lax.broadcasted_iota(jnp.int32, (TQ, TQ), 0)
        col = lax.broadcasted_iota(jnp.int32, (TQ, TQ), 1)
        update(col // CHUNK <= row // CHUNK)

    @pl.when(kj == qi)
    def _():
        row = lax.broadcasted_iota(jnp.int32, (TQ, HD), 0) + qi * TQ
        valid = row >= PAD
        for hd in range(HEADS):
            o = acc_ref[:, hd * HD:(hd + 1) * HD] / jnp.sum(l_ref[hd], axis=-1, keepdims=True)
            o_ref[:, hd * HD:(hd + 1) * HD] = jnp.where(valid, o, 0.0).astype(BF16)


def _attn_body(qi_ref, kj_ref, bound_ref, q_ref, k_ref, v_ref, o_ref, m_ref, l_ref, acc_ref):
    step = pl.program_id(1)
    qi = qi_ref[step]
    kj = kj_ref[step]

    @pl.when(kj == 0)
    def _():
        m_ref[...] = jnp.full_like(m_ref, MASK_VALUE)
        l_ref[...] = jnp.zeros_like(l_ref)
        acc_ref[...] = jnp.zeros_like(acc_ref)

    def update(mask):
        for hd in range(HEADS):
            s = _qk_scores(q_ref, k_ref, hd)
            if mask is not None:
                s = jnp.where(mask, s, MASK_VALUE)
            m_prev = m_ref[hd]
            m_new = jnp.maximum(m_prev, jnp.max(s, axis=-1, keepdims=True))
            alpha = jnp.exp2(m_prev - m_new)
            p = jnp.exp2(s - jnp.tile(m_new, (1, TQ // HD)))
            l_ref[hd] = alpha * l_ref[hd] + jnp.sum(p, axis=-1, keepdims=True)
            m_ref[hd] = m_new
            pv = jnp.dot(p.astype(BF16), v_ref[:, hd * HD:(hd + 1) * HD], preferred_element_type=F32)
            acc_ref[:, hd * HD:(hd + 1) * HD] = alpha * acc_ref[:, hd * HD:(hd + 1) * HD] + pv

    edge = jnp.logical_or(kj == 0, kj == qi)

    @pl.when(edge)
    def _():
        row = lax.broadcasted_iota(jnp.int32, (TQ, TQ), 0) + qi * TQ
        col = lax.broadcasted_iota(jnp.int32, (TQ, TQ), 1) + kj * TQ
        col_chunk = jnp.where(col >= PAD, col // CHUNK, jnp.int32(2 ** 30))
        update(col_chunk <= row // CHUNK)

    @pl.when(jnp.logical_not(edge))
    def _():
        update(None)

    @pl.when(kj == qi)
    def _():
        row = lax.broadcasted_iota(jnp.int32, (TQ, HD), 0) + qi * TQ
        valid = row >= PAD
        for hd in range(HEADS):
            o = acc_ref[:, hd * HD:(hd + 1) * HD] / l_ref[hd]
            o_ref[:, hd * HD:(hd + 1) * HD] = jnp.where(valid, o, 0.0).astype(BF16)


def _attn(q, k, v, bound, nbatch, lp, fixed_shift):
    r = q.shape[0]
    nb = lp // TQ
    qi = np.concatenate([np.full(i + 1, i) for i in range(nb)]).astype(np.int32)
    kj = np.concatenate([np.arange(i + 1) for i in range(nb)]).astype(np.int32)
    stats = [pltpu.VMEM((HEADS, TQ, HD), F32)] * (1 if fixed_shift else 2)
    grid_spec = pltpu.PrefetchScalarGridSpec(
        num_scalar_prefetch=3,
        grid=(nbatch, len(qi)),
        in_specs=[
            pl.BlockSpec((TQ, HEADS * QK_PAD), lambda b, s, qi, kj, bd: (b * nb + qi[s], 0)),
            pl.BlockSpec((TQ, HEADS * QK_PAD), lambda b, s, qi, kj, bd: (b * nb + kj[s], 0)),
            pl.BlockSpec((TQ, HW), lambda b, s, qi, kj, bd: (b * nb + kj[s], 0)),
        ],
        out_specs=pl.BlockSpec((TQ, HW), lambda b, s, qi, kj, bd: (b * nb + qi[s], 0)),
        scratch_shapes=stats + [pltpu.VMEM((TQ, HW), F32)],
    )
    return pl.pallas_call(
        _attn_fast_body if fixed_shift else _attn_body,
        grid_spec=grid_spec,
        out_shape=jax.ShapeDtypeStruct((r, HW), BF16),
        compiler_params=_params(("parallel", "arbitrary")),
        name="attn_fixed_shift" if fixed_shift else "attn_online",
    )(jnp.asarray(qi), jnp.asarray(kj), bound, q, k, v)


def _outproj_mlp_body(h_ref, oa_ref, ob_ref, wo_ref, g_ref, wu_ref, wd_ref, out_ref):
    mix = jnp.dot(oa_ref[...], wo_ref[:HW, :], preferred_element_type=F32)
    mix = mix + jnp.dot(ob_ref[...], wo_ref[HW:, :], preferred_element_type=F32)
    out_ref[...] = _mlp(h_ref[...] + mix, g_ref, wu_ref, wd_ref)


def _outproj_mlp(h, oa, ob, wo, g, wu, wd):
    r = h.shape[0]
    row = lambda i: (i, 0)
    return pl.pallas_call(
        _outproj_mlp_body,
        grid=(r // TM,),
        in_specs=[
            pl.BlockSpec((TM, D_MODEL), row),
            pl.BlockSpec((TM, HW), row),
            pl.BlockSpec((TM, HW), row),
            _const_spec((2 * HW, D_MODEL)),
            _const_spec((1, D_MODEL)),
            _const_spec((D_MODEL, D_FF)),
            _const_spec((D_FF, D_MODEL)),
        ],
        out_specs=pl.BlockSpec((TM, D_MODEL), row),
        out_shape=jax.ShapeDtypeStruct((r, D_MODEL), F32),
        compiler_params=_params(("parallel",)),
        name="outproj_mlp",
    )(h, oa, ob, wo, g, wu, wd)


def _pool_mlp_body(tpb, h_ref, halo_ref, gm_ref, pw_ref, ps_ref, g_ref, wu_ref, wd_ref,
                   out_ref, u_ref, a_ref, b_ref):
    i = pl.program_id(0)
    h = h_ref[...]
    gm = gm_ref[...]
    keep = (i % tpb != 0).astype(F32)
    u_ref[0:HALO, :] = _rms(halo_ref[...], gm) * keep
    u = _rms(h, gm)
    u_ref[HALO:, :] = u
    n = TM + HALO
    g = POOL_G
    a_ref[8:n, :] = u_ref[8:n, :] + u_ref[7:n - 1, :]
    b_ref[16:n, g:] = a_ref[16:n, g:] + a_ref[14:n - 2, g:]
    a_ref[24:n, 2 * g:] = b_ref[24:n, 2 * g:] + b_ref[20:n - 4, 2 * g:]
    b_ref[32:n, 3 * g:] = a_ref[32:n, 3 * g:] + a_ref[24:n - 8, 3 * g:]
    wins = (a_ref[HALO:, 0:g], b_ref[HALO:, g:2 * g], a_ref[HALO:, 2 * g:3 * g], b_ref[HALO:, 3 * g:])
    pos = lax.broadcasted_iota(jnp.int32, (TM, g), 0) + ((i % tpb) * TM - PAD)
    cnt = jnp.maximum(pos + 1, 1).astype(F32)
    ps = ps_ref[...]
    ys = []
    for gi, w in enumerate(POOL_WINDOWS):
        d = wins[gi] / jnp.minimum(cnt, float(w)) - u[:, gi * g:(gi + 1) * g]
        y = jnp.dot(d.astype(BF16), pw_ref[gi], preferred_element_type=F32)
        ys.append(y * ps[:, gi * g:(gi + 1) * g])
    h = h + jnp.concatenate(ys, axis=1)
    out_ref[...] = _mlp(h, g_ref, wu_ref, wd_ref)


def _pool_mlp(h, gm, pw, ps, g, wu, wd, lp):
    r = h.shape[0]
    tpb = lp // TM
    row = lambda i: (i, 0)
    return pl.pallas_call(
        functools.partial(_pool_mlp_body, tpb),
        grid=(r // TM,),
        in_specs=[
            pl.BlockSpec((TM, D_MODEL), row),
            pl.BlockSpec((HALO, D_MODEL), lambda i: (jnp.maximum(i * (TM // HALO) - 1, 0), 0)),
            _const_spec((1, D_MODEL)),
            _const_spec((len(POOL_WINDOWS), POOL_G, POOL_G)),
            _const_spec((1, D_MODEL)),
            _const_spec((1, D_MODEL)),
            _const_spec((D_MODEL, D_FF)),
            _const_spec((D_FF, D_MODEL)),
        ],
        out_specs=pl.BlockSpec((TM, D_MODEL), row),
        out_shape=jax.ShapeDtypeStruct((r, D_MODEL), F32),
        scratch_shapes=[pltpu.VMEM((TM + HALO, D_MODEL), F32)] * 3,
        compiler_params=_params(("parallel",)),
        name="pool_mlp",
    )(h, h, gm, pw, ps, g, wu, wd)


def _rope_cols(w):
    z = jnp.zeros(w.shape[:-1] + (32,), w.dtype)
    return jnp.concatenate([w[..., :32], z, w[..., 32:], z], axis=-1)


def _qk_cols(w):
    w = w.reshape(w.shape[:-1] + (HEADS, QK_DIM))
    w = jnp.concatenate([w[..., :HD], _rope_cols(w[..., HD:])], axis=-1)
    return w.reshape(w.shape[:-2] + (HEADS * QK_PAD,))


def _rope_tables(lp):
    half = ROPE // 2
    inv = ROPE_THETA ** (-jnp.arange(half, dtype=F32) / half)
    pos = jnp.maximum(jnp.arange(lp, dtype=F32) - PAD, 0.0)
    ang = pos[:, None] * inv[None, :]
    c, s = jnp.cos(ang), jnp.sin(ang)
    z = jnp.zeros_like(c)
    return (jnp.concatenate([c, z, c, z], axis=1), jnp.concatenate([-s, z, s, z], axis=1))


def kernel(x, meta_tokens, mix_norm, mlp_norm, w_mlp_up, w_mlp_down, w_in, hgrn_lb, hgrn_out_norm, mla_q_a_norm, mla_kv_a_norm, w_q_up, w_kv_up, q_norm, k_norm, w_out, pool_w, pool_scale):
    nbatch, seq, _ = x.shape
    assert seq % TM == 0
    lp = seq + LEAD
    depth = mix_norm.shape[0]

    meta = jnp.broadcast_to(meta_tokens[None].astype(x.dtype), (nbatch, N_META, D_MODEL))
    h = jnp.concatenate([jnp.zeros((nbatch, PAD, D_MODEL), x.dtype), meta, x], axis=1)
    h = h.reshape(nbatch * lp, D_MODEL)

    cos_t, sin_t = _rope_tables(lp)
    lb_cum = jnp.cumsum(jax.nn.softmax(hgrn_lb.astype(F32), axis=0), axis=0)
    lower = lb_cum - lb_cum[0:1]
    tri =jnp.tril(jnp.ones((CHUNK, CHUNK), F32)).astype(BF16)
    tri3 = jnp.concatenate([tri, tri, tri], axis=1)

    w_in_l = jnp.concatenate(
        [w_in[..., :4 * HW + Q_RANK + KV_RANK], _rope_cols(w_in[..., 4 * HW + Q_RANK + KV_RANK:])],
        axis=-1).astype(BF16)
    wq_l = _qk_cols(w_q_up).astype(BF16)
    wkv = w_kv_up.reshape(w_kv_up.shape[0], KV_RANK, HEADS, 2 * HD)
    wkv_l = jnp.concatenate([wkv[..., :HD].reshape(-1, KV_RANK, HW),
                             wkv[..., HD:].reshape(-1, KV_RANK, HW)], axis=-1).astype(BF16)
    qn_l = jnp.concatenate([q_norm[:, :HD], _rope_cols(q_norm[:, HD:])], axis=-1) * Q_SCALE
    kn_l = jnp.concatenate([k_norm[:, :HD], _rope_cols(k_norm[:, HD:])], axis=-1)
    wo_l = w_out.astype(BF16)
    wu_l = w_mlp_up.astype(BF16)
    wd_l = w_mlp_down.astype(BF16)
    pw_l = pool_w.astype(BF16)

    for layer in range(depth):
        if layer % 2 == 0:
            e = layer // 2
            zh, q, k, v = _inproj(h, mix_norm[layer][None], w_in_l[e], mla_q_a_norm[e][None],
                                  mla_kv_a_norm[e][None], wq_l[e], wkv_l[e], qn_l[e][None],
                                  kn_l[e][None], cos_t, sin_t, lp)
            oa = _hgrn(zh, lower[e][None], 1.0 - lower[e][None], hgrn_out_norm[e][None], tri3, nbatch, lp)
            bound = (Q_SCALE * QK_DIM * 1.01) * jnp.max(jnp.abs(q_norm[e])) * jnp.max(jnp.abs(k_norm[e]))
            bound = bound.reshape(1).astype(F32)
            ob = lax.cond(bound[0] <= MAX_FIXED_SHIFT,
                          functools.partial(_attn, nbatch=nbatch, lp=lp, fixed_shift=True),
                          functools.partial(_attn, nbatch=nbatch, lp=lp, fixed_shift=False),
                          q, k, v, bound)
            h = _outproj_mlp(h, oa, ob, wo_l[e], mlp_norm[layer][None], wu_l[layer], wd_l[layer])
        else:
            o = layer // 2
            h = _pool_mlp(h, mix_norm[layer][None], pw_l[o], pool_scale[o][None],
                          mlp_norm[layer][None], wu_l[layer], wd_l[layer], lp)

    return h.reshape(nbatch, lp, D_MODEL)[:, LEAD:]
```

```python
import functools

import numpy as np
import jax
import jax.numpy as jnp
from jax import lax
from jax.experimental import pallas as pl
from jax.experimental.pallas import tpu as pltpu

F32 = jnp.float32
BF16 = jnp.bfloat16

D_MODEL = 1024
D_FF = 4 * D_MODEL
EPS = 1e-6
N_META = 16
CHUNK = 64
HEADS = 4
HD = 128
HW = HEADS * HD
ROPE = 64
QK_DIM = HD + ROPE
QK_PAD = 256
Q_RANK = 256
KV_RANK = 256
ROPE_THETA = 10000.0
POOL_WINDOWS = (2, 4, 8, 16)
POOL_G = D_MODEL // len(POOL_WINDOWS)

LEAD = 1024
PAD = LEAD - N_META
TM = 512
LEAD_TILES = LEAD // TM
TQ = 1024
TK = 512
SUB = 16
HALO = 32
IN_COLS = 4 * HW + Q_RANK + KV_RANK + HD
Q_SCALE = QK_DIM ** -0.5 * float(np.log2(np.e))
MAX_FIXED_SHIFT = 56.0
MASK_VALUE = -1e30
EXP_CLAMP = 80.0
TINY = 1e-37
VMEM_LIMIT = 56 * 1024 * 1024


def _rms(x, g):
    return x * lax.rsqrt(jnp.mean(x * x, axis=-1, keepdims=True) + EPS) * g


def _silu(x):
    return x * (0.5 * jnp.tanh(0.5 * x) + 0.5)


def _const_spec(shape):
    nd = len(shape)
    return pl.BlockSpec(shape, lambda *_: (0,) * nd, pipeline_mode=pl.Buffered(1))


def _params(sem):
    return pltpu.CompilerParams(dimension_semantics=sem, vmem_limit_bytes=VMEM_LIMIT)


def _frame_tile(i, tpb):
    return (i // tpb) * (tpb - LEAD_TILES) + jnp.maximum(i % tpb - LEAD_TILES, 0)


def _stream_tile(h_ref, meta_ref, tt):
    lead = jnp.concatenate([jnp.zeros((TM - N_META, D_MODEL), F32), meta_ref[...]], axis=0)
    return jnp.where(tt == LEAD_TILES - 1, lead, h_ref[...])


def _mlp(h, g_ref, wu_ref, wd_ref):
    hn = _rms(h, g_ref[...]).astype(BF16)
    acc = jnp.zeros_like(h)
    fc = 1024
    for c in range(D_FF // fc):
        a = jnp.dot(hn, wu_ref[:, c * fc:(c + 1) * fc], preferred_element_type=F32)
        a = jnp.maximum(a, 0.0)
        a = (a * a).astype(BF16)
        acc = acc + jnp.dot(a, wd_ref[c * fc:(c + 1) * fc, :], preferred_element_type=F32)
    return h + acc


def _rope(x, c, s):
    return x * c + pltpu.roll(x, 64, axis=1) * s


def _inproj_body(tpb, first_layer, *refs):
    if first_layer:
        meta_ref, refs = refs[0], refs[1:]
    (h_ref, g_ref, win_ref, qag_ref, kvag_ref, wq_ref, wkv_ref, qn_ref, kn_ref, cos_ref, sin_ref,
     zh_ref, q_ref, k_ref, v_ref) = refs
    tt = pl.program_id(0) % tpb

    @pl.when(tt == 0)
    def _():
        for o_ref in (zh_ref, q_ref, k_ref, v_ref):
            o_ref[...] = jnp.zeros_like(o_ref)

    @pl.when(tt != 0)
    def _():
        h = _stream_tile(h_ref, meta_ref, tt) if first_layer else h_ref[...]
        halves = [slice(i * (TM // 2), (i + 1) * (TM // 2)) for i in range(2)]
        g = g_ref[...]
        u = [_rms(h[r, :], g).astype(BF16) for r in halves]
        z = [jnp.dot(ui, win_ref[...], preferred_element_type=F32) for ui in u]
        qa_in, kva_in = [], []
        for r, zi in zip(halves, z):
            zh_ref[r, :] = zi[:, :4 * HW]
            qa_in.append(_rms(zi[:, 4 * HW:4 * HW + Q_RANK], qag_ref[...]).astype(BF16))
            kva_in.append(_rms(zi[:, 4 * HW + Q_RANK:4 * HW + Q_RANK + KV_RANK], kvag_ref[...]).astype(BF16))
        qs = [jnp.dot(a, wq_ref[...], preferred_element_type=F32) for a in qa_in]
        kvs = [jnp.dot(a, wkv_ref[...], preferred_element_type=F32) for a in kva_in]
        qg = qn_ref[...]
        kg = kn_ref[...]
        for r, zi, q, kv in zip(halves, z, qs, kvs):
            kr = zi[:, 4 * HW + Q_RANK + KV_RANK:]
            v_ref[r, :] = kv[:, HW:].astype(BF16)
            c = cos_ref[r, :]
            s = sin_ref[r, :]
            kr_ss = jnp.sum(kr * kr, axis=-1, keepdims=True)
            for hd in range(HEADS):
                qa = q[:, hd * QK_PAD:hd * QK_PAD + HD]
                qb = q[:, hd * QK_PAD + HD:(hd + 1) * QK_PAD]
                ss = (jnp.sum(qa * qa, axis=-1, keepdims=True) + jnp.sum(qb * qb, axis=-1, keepdims=True))
                inv = lax.rsqrt(ss * (1.0 / QK_DIM) + EPS)
                q_ref[r, hd * QK_PAD:hd * QK_PAD + HD] = (qa * inv * qg[:, :HD]).astype(BF16)
                q_ref[r, hd * QK_PAD + HD:(hd + 1) * QK_PAD] = _rope(qb * inv * qg[:, HD:], c, s).astype(BF16)
                ka = kv[:, hd * HD:(hd + 1) * HD]
                ss = jnp.sum(ka * ka, axis=-1, keepdims=True) + kr_ss
                inv = lax.rsqrt(ss * (1.0 / QK_DIM) + EPS)
                k_ref[r, hd * QK_PAD:hd * QK_PAD + HD] = (ka * inv * kg[:, :HD]).astype(BF16)
                k_ref[r, hd * QK_PAD + HD:(hd + 1) * QK_PAD] = _rope(kr * inv * kg[:, HD:], c, s).astype(BF16)


def _inproj(h, meta, g, win, qag, kvag, wq, wkv, qn, kn, cos_t, sin_t, nbatch, lp):
    r = nbatch * lp
    tpb = lp // TM
    first_layer = meta is not None
    row = lambda i: (i, 0)
    tab = lambda i: (i % tpb, 0)
    h_spec = pl.BlockSpec((TM, D_MODEL), (lambda i: (_frame_tile(i, tpb), 0)) if first_layer else row)
    lead_specs = [_const_spec((N_META, D_MODEL))] if first_layer else []
    lead_args = [meta] if first_layer else []
    return pl.pallas_call(
        functools.partial(_inproj_body, tpb, first_layer),
        grid=(r // TM,),
        in_specs=lead_specs + [
            h_spec,
            _const_spec((1, D_MODEL)),
            _const_spec((D_MODEL, IN_COLS)),
            _const_spec((1, Q_RANK)),
            _const_spec((1, KV_RANK)),
            _const_spec((Q_RANK, HEADS * QK_PAD)),
            _const_spec((KV_RANK, 2 * HW)),
            _const_spec((1, QK_PAD)),
            _const_spec((1, QK_PAD)),
            pl.BlockSpec((TM, HD), tab),
            pl.BlockSpec((TM, HD), tab),
        ],
        out_specs=[
            pl.BlockSpec((TM, 4 * HW), row),
            pl.BlockSpec((TM, HEADS * QK_PAD), row),
            pl.BlockSpec((TM, HEADS * QK_PAD), row),
            pl.BlockSpec((TM, HW), row),
        ],
        out_shape=[
            jax.ShapeDtypeStruct((r, 4 * HW), F32),
            jax.ShapeDtypeStruct((r, HEADS * QK_PAD), BF16),
            jax.ShapeDtypeStruct((r, HEADS * QK_PAD), BF16),
            jax.ShapeDtypeStruct((r, HW), BF16),
        ],
        compiler_params=_params(("parallel",)),
        name="inproj",
    )(*lead_args, h, g, win, qag, kvag, wq, wkv, qn, kn, cos_t, sin_t)


def _group_rows(rows):
    return jnp.concatenate([jnp.broadcast_to(r, (SUB, HD)) for r in rows], axis=0)


def _hgrn_body(zh_ref, lb_ref, one_m_lb_ref, og_ref, tri_ref, o_ref, st_ref):
    @pl.when(pl.program_id(1) == 0)
    def _():
        st_ref[...] = jnp.zeros_like(st_ref)
        o_ref[...] = jnp.zeros_like(o_ref)

    tri3 = tri_ref[...]
    lb = lb_ref[...]
    one_m_lb = one_m_lb_ref[...]
    og = og_ref[...]
    tt = lax.broadcasted_iota(jnp.int32, (CHUNK, CHUNK), 0)
    ss_ = lax.broadcasted_iota(jnp.int32, (CHUNK, CHUNK), 1)
    causal = ss_ <= tt
    nsub = CHUNK // SUB
    zero_row = jnp.zeros((1, HD), F32)
    one_row = jnp.ones((1, HD), F32)
    zero_sub = jnp.zeros((SUB, HD), BF16)

    def chunk(ci, carry):
        r0 = pl.multiple_of(ci * CHUNK, CHUNK)
        hq = zh_ref[pl.ds(r0, CHUNK), 0:HW]
        hf = zh_ref[pl.ds(r0, CHUNK), HW:2 * HW]
        hi = zh_ref[pl.ds(r0, CHUNK), 2 * HW:3 * HW]
        hg = zh_ref[pl.ds(r0, CHUNK), 3 * HW:4 * HW]
        q = _silu(hq)
        gate = _silu(hg)
        t = jnp.exp(-jnp.abs(hf))
        r = 1.0 / (1.0 + t)
        tr = t * r
        pos = hf >= 0.0
        logf = jnp.log(jnp.maximum(lb + one_m_lb * jnp.where(pos, r, tr), TINY))
        k = one_m_lb * jnp.where(pos, tr, r)
        g1 = logf.astype(BF16)
        r1 = logf - g1.astype(F32)
        g2 = r1.astype(BF16)
        g3 = (r1 - g2.astype(F32)).astype(BF16)
        b = jnp.dot(tri3, jnp.concatenate([g1, g2, g3], axis=0), preferred_element_type=F32)
        heads = range(HEADS)
        sls = [slice(hd * HD, (hd + 1) * HD) for hd in heads]
        vb = hi.astype(BF16)
        att, q_in, k_out, decay = [], [], [], []
        for hd in heads:
            bh = b[:, sls[hd]]
            b_last = bh[CHUNK - 1:CHUNK, :]
            refs = [zero_row] + [bh[i * SUB - 1:i * SUB, :] for i in range(1, nsub)]
            dq = bh - _group_rows(refs)
            qe = q[:, sls[hd]] * jnp.exp(dq)
            ke = k[:, sls[hd]] * jnp.exp(jnp.minimum(-dq, EXP_CLAMP))
            keb = ke.astype(BF16)
            qcat = []
            kcat = []
            for j in range(nsub):
                scale = [zero_row if i < j else one_row if i == j else jnp.exp(refs[i] - refs[j])
                         for i in range(nsub)]
                qcat.append((qe * _group_rows(scale)).astype(BF16))
                kcat.append(jnp.concatenate(
                    [keb[j * SUB:(j + 1) * SUB] if i == j else zero_sub for i in range(nsub)], axis=0))
            q_in.append(qcat[0])
            att.append(lax.dot_general(jnp.concatenate(qcat, axis=1), jnp.concatenate(kcat, axis=1),
                                       (((1,), (1,)), ((), ())), preferred_element_type=F32))
            to_end = [jnp.exp(b_last - refs[i]) for i in range(nsub)]
            k_out.append((ke * _group_rows(to_end)).astype(BF16))
            decay.append(to_end[0])
        st = [st_ref[hd] for hd in heads]
        o = [lax.dot_general(q_in[hd], st[hd].astype(BF16), (((1,), (1,)), ((), ())),
                             preferred_element_type=F32) for hd in heads]
        upd = [lax.dot_general(vb[:, sls[hd]], k_out[hd], (((0,), (0,)), ((), ())),
                               preferred_element_type=F32) for hd in heads]
        for hd in heads:
            st_ref[hd] = decay[hd] * st[hd] + upd[hd]
        for hd in heads:
            a = jnp.where(causal, att[hd], 0.0).astype(BF16)
            oh = o[hd] + jnp.dot(a, vb[:, sls[hd]], preferred_element_type=F32)
            on = _rms(oh, og) * gate[:, sls[hd]]
            o_ref[pl.ds(r0, CHUNK), sls[hd]] = on.astype(BF16)
        return carry

    @pl.when(pl.program_id(1) != 0)
    def _():
        lax.fori_loop(0, TM // CHUNK, chunk, 0, unroll=4)


def _hgrn(zh, lb, one_m_lb, og, tri3, nbatch, lp):
    r = zh.shape[0]
    tpb = lp // TM
    return pl.pallas_call(
        _hgrn_body,
        grid=(nbatch, tpb),
        in_specs=[
            pl.BlockSpec((TM, 4 * HW), lambda b, t: (b * tpb + t, 0)),
            _const_spec((1, HW)),
            _const_spec((1, HW)),
            _const_spec((1, HD)),
            _const_spec((CHUNK, 3 * CHUNK)),
        ],
        out_specs=pl.BlockSpec((TM, HW), lambda b, t: (b * tpb + t, 0)),
        out_shape=jax.ShapeDtypeStruct((r, HW), BF16),
        scratch_shapes=[pltpu.VMEM((HEADS, HD, HD), F32)],
        compiler_params=_params(("parallel", "arbitrary")),
        name="hgrn",
    )(zh, lb, one_m_lb, og, tri3)


META_STEP, FULL_STEP, DIAG_LO_STEP, DIAG_HI_STEP = range(4)


def _attn_steps(lp):
    kind, qi, kj, fin = [], [], [], []
    for i in range(lp // TQ):
        tiles = [1] + list(range(2, 2 * i + 2))
        for j in tiles:
            kind.append(META_STEP if j == 1 else DIAG_LO_STEP if j == 2 * i
                        else DIAG_HI_STEP if j == 2 * i + 1 else FULL_STEP)
            qi.append(i)
            kj.append(j)
            fin.append(int(j == tiles[-1]))
    return [np.asarray(a, np.int32) for a in (kind, qi, kj, fin)]


def _attn_body(fixed_shift, kind_ref, qi_ref, kj_ref, fin_ref, bound_ref, q_ref, k_ref, v_ref, o_ref,
               *scratch):
    if fixed_shift:
        l_ref, acc_ref = scratch
    else:
        m_ref, l_ref, acc_ref = scratch
    step = pl.program_id(1)
    kind = kind_ref[step]
    qi = qi_ref[step]
    kj = kj_ref[step]
    bound = bound_ref[0]

    def update(rows, keys, mask, first):
        for hd in range(HEADS):
            hq = slice(hd * QK_PAD, (hd + 1) * QK_PAD)
            hv = slice(hd * HD, (hd + 1) * HD)
            s = lax.dot_general(q_ref[rows, hq], k_ref[keys, hq], (((1,), (1,)), ((), ())),
                                preferred_element_type=F32)
            if fixed_shift:
                p = jnp.exp2(s - bound)
                if mask is not None:
                    p = jnp.where(mask, p, 0.0)
            else:
                if mask is not None:
                    s = jnp.where(mask, s, MASK_VALUE)
                m_prev = m_ref[hd, rows, :]
                m_new = jnp.maximum(m_prev, jnp.max(s, axis=-1, keepdims=True))
                alpha = jnp.exp2(m_prev - m_new)
                m_ref[hd, rows, :] = m_new
                p = jnp.exp2(s - jnp.tile(m_new, (1, s.shape[1] // HD)))
            part = p[:, 0:HD]
            for c in range(1, p.shape[1] // HD):
                part = part + p[:, c * HD:(c + 1) * HD]
            pv = jnp.dot(p.astype(BF16), v_ref[keys, hv], preferred_element_type=F32)
            if first and fixed_shift:
                l_ref[hd, rows, :] = part
                acc_ref[rows, hv] = pv
            elif fixed_shift:
                l_ref[hd, rows, :] += part
                acc_ref[rows, hv] += pv
            else:
                l_ref[hd, rows, :] = alpha * l_ref[hd, rows, :] + part
                acc_ref[rows, hv] = alpha * acc_ref[rows, hv] + pv

    all_rows = slice(0, TQ)
    all_keys = slice(0, TK)

    @pl.when(kind == META_STEP)
    def _():
        if not fixed_shift:
            m_ref[...] = jnp.full_like(m_ref, MASK_VALUE)
            l_ref[...] = jnp.zeros_like(l_ref)
            acc_ref[...] = jnp.zeros_like(acc_ref)
        col = lax.broadcasted_iota(jnp.int32, (TQ, HD), 1) + (kj * TK + TK - HD)
        update(all_rows, slice(TK - HD, TK), col >= PAD, True)

    @pl.when(kind == FULL_STEP)
    def _():
        update(all_rows, all_keys, None, False)

    @pl.when(kind == DIAG_LO_STEP)
    def _():
        row = lax.broadcasted_iota(jnp.int32, (TQ, TK), 0)
        col = lax.broadcasted_iota(jnp.int32, (TQ, TK), 1)
        update(all_rows, all_keys, col // CHUNK <= row // CHUNK, False)

    @pl.when(kind == DIAG_HI_STEP)
    def _():
        row = lax.broadcasted_iota(jnp.int32, (TQ - TK, TK), 0)
        col = lax.broadcasted_iota(jnp.int32, (TQ - TK, TK), 1)
        update(slice(TK, TQ), all_keys, col // CHUNK <= row // CHUNK, False)

    @pl.when(fin_ref[step] == 1)
    def _():
        row = lax.broadcasted_iota(jnp.int32, (TQ, HD), 0) + qi * TQ
        valid = row >= PAD
        for hd in range(HEADS):
            hv = slice(hd * HD, (hd + 1) * HD)
            o = acc_ref[:, hv] / jnp.sum(l_ref[hd], axis=-1, keepdims=True)
            o_ref[:, hv] = jnp.where(valid, o, 0.0).astype(BF16)


def _attn(q, k, v, bound, nbatch, lp, fixed_shift):
    r = q.shape[0]
    nq, nk = lp // TQ, lp // TK
    tables = _attn_steps(lp)
    stats = [pltpu.VMEM((HEADS, TQ, HD), F32)] * (1 if fixed_shift else 2)
    qmap = lambda b, s, kind, qi, kj, fin, bd: (b * nq + qi[s], 0)
    kmap = lambda b, s, kind, qi, kj, fin, bd: (b * nk + kj[s], 0)
    grid_spec = pltpu.PrefetchScalarGridSpec(
        num_scalar_prefetch=5,
        grid=(nbatch, len(tables[0])),
        in_specs=[
            pl.BlockSpec((TQ, HEADS * QK_PAD), qmap),
            pl.BlockSpec((TK, HEADS * QK_PAD), kmap),
            pl.BlockSpec((TK, HW), kmap),
        ],
        out_specs=pl.BlockSpec((TQ, HW), qmap),
        scratch_shapes=stats + [pltpu.VMEM((TQ, HW), F32)],
    )
    return pl.pallas_call(
        functools.partial(_attn_body, fixed_shift),
        grid_spec=grid_spec,
        out_shape=jax.ShapeDtypeStruct((r, HW), BF16),
        compiler_params=_params(("parallel", "arbitrary")),
        name="attn_fixed_shift" if fixed_shift else "attn_online",
    )(*[jnp.asarray(t) for t in tables], bound, q, k, v)


def _outproj_mlp_body(tpb, first_layer, *refs):
    if first_layer:
        meta_ref, refs = refs[0], refs[1:]
    h_ref, oa_ref, ob_ref, wo_ref, g_ref, wu_ref, wd_ref, out_ref = refs
    tt = pl.program_id(0) % tpb

    @pl.when(tt == 0)
    def _():
        out_ref[...] = jnp.zeros_like(out_ref)

    @pl.when(tt != 0)
    def _():
        h = _stream_tile(h_ref, meta_ref, tt) if first_layer else h_ref[...]
        mix = jnp.dot(oa_ref[...], wo_ref[:HW, :], preferred_element_type=F32)
        mix = mix + jnp.dot(ob_ref[...], wo_ref[HW:, :], preferred_element_type=F32)
        out_ref[...] = _mlp(h + mix, g_ref, wu_ref, wd_ref)


def _outproj_mlp(h, meta, oa, ob, wo, g, wu, wd, nbatch, lp):
    r = nbatch * lp
    tpb = lp // TM
    first_layer = meta is not None
    row = lambda i: (i, 0)
    h_spec = pl.BlockSpec((TM, D_MODEL), (lambda i: (_frame_tile(i, tpb), 0)) if first_layer else row)
    lead_specs = [_const_spec((N_META, D_MODEL))] if first_layer else []
    lead_args = [meta] if first_layer else []
    return pl.pallas_call(
        functools.partial(_outproj_mlp_body, tpb, first_layer),
        grid=(r // TM,),
        in_specs=lead_specs + [
            h_spec,
            pl.BlockSpec((TM, HW), row),
            pl.BlockSpec((TM, HW), row),
            _const_spec((2 * HW, D_MODEL)),
            _const_spec((1, D_MODEL)),
            _const_spec((D_MODEL, D_FF)),
            _const_spec((D_FF, D_MODEL)),
        ],
        out_specs=pl.BlockSpec((TM, D_MODEL), row),
        out_shape=jax.ShapeDtypeStruct((r, D_MODEL), F32),
        compiler_params=_params(("parallel",)),
        name="outproj_mlp",
    )(*lead_args, h, oa, ob, wo, g, wu, wd)


def _pool_mlp_body(tpb, h_ref, halo_ref, gm_ref, pw_ref, ps_ref, g_ref, wu_ref, wd_ref,
                   out_ref, u_ref, a_ref, b_ref):
    tt = pl.program_id(0) % tpb

    @pl.when(tt == 0)
    def _():
        out_ref[...] = jnp.zeros_like(out_ref)

    @pl.when(tt != 0)
    def _():
        h = h_ref[...]
        gm = gm_ref[...]
        u_ref[0:HALO, :] = _rms(halo_ref[...], gm)
        u = _rms(h, gm)
        u_ref[HALO:, :] = u
        n = TM + HALO
        g = POOL_G
        a_ref[8:n, :] = u_ref[8:n, :] + u_ref[7:n - 1, :]
        b_ref[16:n, g:] = a_ref[16:n, g:] + a_ref[14:n - 2, g:]
        a_ref[24:n, 2 * g:] = b_ref[24:n, 2 * g:] + b_ref[20:n - 4, 2 * g:]
        b_ref[32:n, 3 * g:] = a_ref[32:n, 3 * g:] + a_ref[24:n - 8, 3 * g:]
        wins = (a_ref[HALO:, 0:g], b_ref[HALO:, g:2 * g], a_ref[HALO:, 2 * g:3 * g], b_ref[HALO:, 3 * g:])
        pos = lax.broadcasted_iota(jnp.int32, (TM, g), 0) + (tt * TM - PAD)
        cnt = jnp.maximum(pos + 1, 1).astype(F32)
        ps = ps_ref[...]
        ys = []
        for gi, w in enumerate(POOL_WINDOWS):
            d = wins[gi] / jnp.minimum(cnt, float(w)) - u[:, gi * g:(gi + 1) * g]
            y = jnp.dot(d.astype(BF16), pw_ref[gi], preferred_element_type=F32)
            ys.append(y * ps[:, gi * g:(gi + 1) * g])
        out_ref[...] = _mlp(h + jnp.concatenate(ys, axis=1), g_ref, wu_ref, wd_ref)


def _pool_mlp(h, gm, pw, ps, g, wu, wd, nbatch, lp, to_frames):
    r = nbatch * lp
    tpb = lp // TM
    row = lambda i: (i, 0)
    out_rows = nbatch * (lp - LEAD) if to_frames else r
    return pl.pallas_call(
        functools.partial(_pool_mlp_body, tpb),
        grid=(r // TM,),
        in_specs=[
            pl.BlockSpec((TM, D_MODEL), row),
            pl.BlockSpec((HALO, D_MODEL), lambda i: (jnp.maximum(i * (TM // HALO) - 1, 0), 0)),
            _const_spec((1, D_MODEL)),
            _const_spec((len(POOL_WINDOWS), POOL_G, POOL_G)),
            _const_spec((1, D_MODEL)),
            _const_spec((1, D_MODEL)),
            _const_spec((D_MODEL, D_FF)),
            _const_spec((D_FF, D_MODEL)),
        ],
        out_specs=pl.BlockSpec((TM, D_MODEL), (lambda i: (_frame_tile(i, tpb), 0)) if to_frames else row),
        out_shape=jax.ShapeDtypeStruct((out_rows, D_MODEL), F32),
        scratch_shapes=[pltpu.VMEM((TM + HALO, D_MODEL), F32)] * 3,
        compiler_params=_params(("arbitrary",) if to_frames else ("parallel",)),
        name="pool_mlp",
    )(h, h, gm, pw, ps, g, wu, wd)


def _rope_cols(w):
    z = jnp.zeros(w.shape[:-1] + (32,), w.dtype)
    return jnp.concatenate([w[..., :32], z, w[..., 32:], z], axis=-1)


def _qk_cols(w):
    w = w.reshape(w.shape[:-1] + (HEADS, QK_DIM))
    w = jnp.concatenate([w[..., :HD], _rope_cols(w[..., HD:])], axis=-1)
    return w.reshape(w.shape[:-2] + (HEADS * QK_PAD,))


def _rope_tables(lp):
    half = ROPE // 2
    inv = ROPE_THETA ** (-jnp.arange(half, dtype=F32) / half)
    pos = jnp.maximum(jnp.arange(lp, dtype=F32) - PAD, 0.0)
    ang = pos[:, None] * inv[None, :]
    c, s = jnp.cos(ang), jnp.sin(ang)
    z = jnp.zeros_like(c)
    return (jnp.concatenate([c, z, c, z], axis=1), jnp.concatenate([-s, z, s, z], axis=1))


def kernel(x, meta_tokens, mix_norm, mlp_norm, w_mlp_up, w_mlp_down, w_in, hgrn_lb, hgrn_out_norm, mla_q_a_norm, mla_kv_a_norm, w_q_up, w_kv_up, q_norm, k_norm, w_out, pool_w, pool_scale):
    nbatch, seq, _ = x.shape
    depth = mix_norm.shape[0]
    assert seq % TQ == 0 and depth % 2 == 0
    lp = seq + LEAD

    cos_t, sin_t = _rope_tables(lp)
    lb_cum = jnp.cumsum(jax.nn.softmax(hgrn_lb.astype(F32), axis=0), axis=0)
    lower = lb_cum - lb_cum[0:1]
    tri = jnp.tril(jnp.ones((CHUNK, CHUNK), F32)).astype(BF16)
    tri3 = jnp.concatenate([tri, tri, tri], axis=1)

    w_in_l = jnp.concatenate(
        [w_in[..., :4 * HW + Q_RANK + KV_RANK], _rope_cols(w_in[..., 4 * HW + Q_RANK + KV_RANK:])],
        axis=-1).astype(BF16)
    wq_l = _qk_cols(w_q_up).astype(BF16)
    wkv = w_kv_up.reshape(w_kv_up.shape[0], KV_RANK, HEADS, 2 * HD)
    wkv_l = jnp.concatenate([wkv[..., :HD].reshape(-1, KV_RANK, HW),
                             wkv[..., HD:].reshape(-1, KV_RANK, HW)], axis=-1).astype(BF16)
    qn_l = jnp.concatenate([q_norm[:, :HD], _rope_cols(q_norm[:, HD:])], axis=-1) * Q_SCALE
    kn_l = jnp.concatenate([k_norm[:, :HD], _rope_cols(k_norm[:, HD:])], axis=-1)
    wo_l = w_out.astype(BF16)
    wu_l = w_mlp_up.astype(BF16)
    wd_l = w_mlp_down.astype(BF16)
    pw_l = pool_w.astype(BF16)

    h = x.reshape(nbatch * seq, D_MODEL)
    meta = meta_tokens.astype(F32)
    for layer in range(depth):
        if layer % 2 == 0:
            e = layer // 2
            zh, q, k, v = _inproj(h, meta, mix_norm[layer][None], w_in_l[e], mla_q_a_norm[e][None],
                                  mla_kv_a_norm[e][None], wq_l[e], wkv_l[e], qn_l[e][None],
                                  kn_l[e][None], cos_t, sin_t, nbatch, lp)
            oa = _hgrn(zh, lower[e][None], 1.0 - lower[e][None], hgrn_out_norm[e][None], tri3, nbatch, lp)
            bound = (Q_SCALE * QK_DIM * 1.01) * jnp.max(jnp.abs(q_norm[e])) * jnp.max(jnp.abs(k_norm[e]))
            bound = bound.reshape(1).astype(F32)
            ob = lax.cond(bound[0] <= MAX_FIXED_SHIFT,
                          functools.partial(_attn, nbatch=nbatch, lp=lp, fixed_shift=True),
                          functools.partial(_attn, nbatch=nbatch, lp=lp, fixed_shift=False),
                          q, k, v, bound)
            h = _outproj_mlp(h, meta, oa, ob, wo_l[e], mlp_norm[layer][None], wu_l[layer], wd_l[layer],
                             nbatch, lp)
            meta = None
        else:
            o = layer // 2
            h = _pool_mlp(h, mix_norm[layer][None], pw_l[o], pool_scale[o][None],
                          mlp_norm[layer][None], wu_l[layer], wd_l[layer], nbatch, lp,
                          to_frames=layer == depth - 1)

    return h.reshape(nbatch, seq, D_MODEL)
```

```python
import functools

import numpy as np
import jax
import jax.numpy as jnp
from jax import lax
from jax.experimental import pallas as pl
from jax.experimental.pallas import tpu as pltpu

F32 = jnp.float32
BF16 = jnp.bfloat16

D_MODEL = 1024
D_FF = 4 * D_MODEL
EPS = 1e-6
N_META = 16
CHUNK = 64
HEADS = 4
HD = 128
HW = HEADS * HD
ROPE = 64
QK_DIM = HD + ROPE
QK_PAD = 256
Q_RANK = 256
KV_RANK = 256
ROPE_THETA = 10000.0
POOL_WINDOWS = (2, 4, 8, 16)
POOL_G = D_MODEL // len(POOL_WINDOWS)

LEAD = 1024
PAD = LEAD - N_META
TM = 512
LEAD_TILES = LEAD // TM
TQ = 1024
TK = 512
SUB = 16
HGRN_GROUP = 4
HGRN_UNROLL = 1
HALO = 32
IN_COLS = 4 * HW + Q_RANK + KV_RANK + HD
Q_SCALE = QK_DIM ** -0.5 * float(np.log2(np.e))
MAX_FIXED_SHIFT = 56.0
MASK_VALUE = -1e30
EXP2_CLAMP = 115.0
TINY = 1e-37
VMEM_LIMIT = 56 * 1024 * 1024


def _rms(x, g):
    return x * lax.rsqrt(jnp.mean(x * x, axis=-1, keepdims=True) + EPS) * g


def _silu(x):
    hx = 0.5 * x
    return hx + hx * jnp.tanh(hx)


def _const_spec(shape):
    nd = len(shape)
    return pl.BlockSpec(shape, lambda *_: (0,) * nd, pipeline_mode=pl.Buffered(1))


def _params(sem):
    return pltpu.CompilerParams(dimension_semantics=sem, vmem_limit_bytes=VMEM_LIMIT)


def _frame_tile(i, tpb):
    return (i // tpb) * (tpb - LEAD_TILES) + jnp.maximum(i % tpb - LEAD_TILES, 0)


def _stream_tile(h_ref, meta_ref, tt):
    lead = jnp.concatenate([jnp.zeros((TM - N_META, D_MODEL), F32), meta_ref[...]], axis=0)
    lead = jnp.where(tt == LEAD_TILES - 1, lead, 0.0)
    return jnp.where(tt >= LEAD_TILES, h_ref[...], lead)


def _mlp(h, g_ref, wu_ref, wd_ref):
    hn = _rms(h, g_ref[...]).astype(BF16)
    acc = jnp.zeros_like(h)
    fc = 1024
    for c in range(D_FF // fc):
        a = jnp.dot(hn, wu_ref[:, c * fc:(c + 1) * fc], preferred_element_type=F32)
        a = jnp.maximum(a, 0.0)
        a = (a * a).astype(BF16)
        acc = acc + jnp.dot(a, wd_ref[c * fc:(c + 1) * fc, :], preferred_element_type=F32)
    return h + acc


def _rope(x, c, s):
    return x * c + pltpu.roll(x, 64, axis=1) * s


def _inproj_body(tpb, ntiles, first_layer, *refs):
    if first_layer:
        meta_ref, refs = refs[0], refs[1:]
    (h_ref, g_ref, win_ref, qag_ref, kvag_ref, wq_ref, wkv_ref, qn_ref, kn_ref, cos_ref, sin_ref,
     zh_ref, q_ref, k_ref, v_ref, lat_ref) = refs
    i = pl.program_id(0)

    @pl.when(i == 0)
    def _():
        lat_ref[1] = jnp.zeros(lat_ref.shape[1:], F32)

    lat = lat_ref[(i + 1) % 2]
    q = jnp.dot(_rms(lat[:, :Q_RANK], qag_ref[...]).astype(BF16), wq_ref[...], preferred_element_type=F32)
    kv = jnp.dot(_rms(lat[:, Q_RANK:Q_RANK + KV_RANK], kvag_ref[...]).astype(BF16), wkv_ref[...],
                 preferred_element_type=F32)
    kr = lat[:, Q_RANK + KV_RANK:]

    tt = jnp.minimum(i, ntiles - 1) % tpb
    h = _stream_tile(h_ref, meta_ref, tt) if first_layer else h_ref[...]
    z = jnp.dot(_rms(h, g_ref[...]).astype(BF16), win_ref[...], preferred_element_type=F32)
    zh_ref[...] = z[:, :4 * HW]
    lat_ref[i % 2] = z[:, 4 * HW:]

    v_ref[...] = kv[:, HW:].astype(BF16)
    c = cos_ref[...]
    s = sin_ref[...]
    qg = qn_ref[...]
    kg = kn_ref[...]
    kr_ss = jnp.sum(kr * kr, axis=-1, keepdims=True)
    for hd in range(HEADS):
        qa = q[:, hd * QK_PAD:hd * QK_PAD + HD]
        qb = q[:, hd * QK_PAD + HD:(hd + 1) * QK_PAD]
        ss = jnp.sum(qa * qa + qb * qb, axis=-1, keepdims=True)
        inv = lax.rsqrt(ss * (1.0 / QK_DIM) + EPS)
        q_ref[:, hd * QK_PAD:hd * QK_PAD + HD] = (qa * inv * qg[:, :HD]).astype(BF16)
        q_ref[:, hd * QK_PAD + HD:(hd + 1) * QK_PAD] = _rope(qb * inv * qg[:, HD:], c, s).astype(BF16)
        ka = kv[:, hd * HD:(hd + 1) * HD]
        ss = jnp.sum(ka * ka, axis=-1, keepdims=True) + kr_ss
        inv = lax.rsqrt(ss * (1.0 / QK_DIM) + EPS)
        k_ref[:, hd * QK_PAD:hd * QK_PAD + HD] = (ka * inv * kg[:, :HD]).astype(BF16)
        k_ref[:, hd * QK_PAD + HD:(hd + 1) * QK_PAD] = _rope(kr * inv * kg[:, HD:], c, s).astype(BF16)


def _inproj(h, meta, g, win, qag, kvag, wq, wkv, qn, kn, cos_t, sin_t, nbatch, lp):
    r = nbatch * lp
    tpb = lp // TM
    ntiles = r // TM
    first_layer = meta is not None
    cur = lambda i: jnp.minimum(i, ntiles - 1)
    prev = lambda i: jnp.maximum(i - 1, 0)
    row = lambda i: (prev(i), 0)
    tab = lambda i: (prev(i) % tpb, 0)
    h_spec = pl.BlockSpec((TM, D_MODEL),
                          (lambda i: (_frame_tile(cur(i), tpb), 0)) if first_layer else (lambda i: (cur(i), 0)))
    lead_specs = [_const_spec((N_META, D_MODEL))] if first_layer else []
    lead_args = [meta] if first_layer else []
    return pl.pallas_call(
        functools.partial(_inproj_body, tpb, ntiles, first_layer),
        grid=(ntiles + 1,),
        in_specs=lead_specs + [
            h_spec,
            _const_spec((1, D_MODEL)),
            _const_spec((D_MODEL, IN_COLS)),
            _const_spec((1, Q_RANK)),
            _const_spec((1, KV_RANK)),
            _const_spec((Q_RANK, HEADS * QK_PAD)),
            _const_spec((KV_RANK, 2 * HW)),
            _const_spec((1, QK_PAD)),
            _const_spec((1, QK_PAD)),
            pl.BlockSpec((TM, HD), tab),
            pl.BlockSpec((TM, HD), tab),
        ],
        out_specs=[
            pl.BlockSpec((TM, 4 * HW), lambda i: (cur(i), 0)),
            pl.BlockSpec((TM, HEADS * QK_PAD), row),
            pl.BlockSpec((TM, HEADS * QK_PAD), row),
            pl.BlockSpec((TM, HW), row),
        ],
        out_shape=[
            jax.ShapeDtypeStruct((r, 4 * HW), F32),
            jax.ShapeDtypeStruct((r, HEADS * QK_PAD), BF16),
            jax.ShapeDtypeStruct((r, HEADS * QK_PAD), BF16),
            jax.ShapeDtypeStruct((r, HW), BF16),
        ],
        scratch_shapes=[pltpu.VMEM((2, TM, IN_COLS - 4 * HW), F32)],
        compiler_params=_params(("arbitrary",)),
        name="inproj",
    )(*lead_args, h, g, win, qag, kvag, wq, wkv, qn, kn, cos_t, sin_t)


def _group_rows(rows):
    return jnp.concatenate([jnp.broadcast_to(r, (SUB, HD)) for r in rows], axis=0)


def _hgrn_body(zh_ref, lb_ref, one_m_lb_ref, og_ref, tri_ref, o_ref, st_ref):
    @pl.when(pl.program_id(1) == 0)
    def _():
        st_ref[...] = jnp.zeros_like(st_ref)
        o_ref[...] = jnp.zeros_like(o_ref)

    tri2 = tri_ref[...]
    lb = lb_ref[...]
    one_m_lb = one_m_lb_ref[...]
    og = og_ref[...]
    tt = lax.broadcasted_iota(jnp.int32, (CHUNK, CHUNK), 0)
    ss_ = lax.broadcasted_iota(jnp.int32, (CHUNK, CHUNK), 1)
    causal = ss_ <= tt
    nsub = CHUNK // SUB
    zero_row = jnp.zeros((1, HD), F32)
    zero_sub = jnp.zeros((SUB, HD), BF16)

    heads = range(HEADS)
    sls = [slice(hd * HD, (hd + 1) * HD) for hd in heads]

    def prepare(r0):
        hq = zh_ref[pl.ds(r0, CHUNK), 0:HW]
        hf = zh_ref[pl.ds(r0, CHUNK), HW:2 * HW]
        hi = zh_ref[pl.ds(r0, CHUNK), 2 * HW:3 * HW]
        hg = zh_ref[pl.ds(r0, CHUNK), 3 * HW:4 * HW]
        q = _silu(hq)
        gate = _silu(hg)
        t = jnp.exp(-jnp.abs(hf))
        r = 1.0 / (1.0 + t)
        tr = t * r
        pos = hf >= 0.0
        log2f = jnp.log2(jnp.maximum(lb + one_m_lb * jnp.where(pos, r, tr), TINY))
        k = one_m_lb * jnp.where(pos, tr, r)
        g1 = log2f.astype(BF16)
        g2 = (log2f - g1.astype(F32)).astype(BF16)
        b = jnp.dot(tri2, jnp.concatenate([g1, g2], axis=0), preferred_element_type=F32)
        vb = hi.astype(BF16)
        att, q_in, k_out, decay = [], [], [], []
        for hd in heads:
            bh = b[:, sls[hd]]
            b_last = bh[CHUNK - 1:CHUNK, :]
            refs = [zero_row] + [bh[i * SUB - 1:i * SUB, :] for i in range(1, nsub)]
            dq = bh - _group_rows(refs)
            qe = q[:, sls[hd]] * jnp.exp2(dq)
            ke = k[:, sls[hd]] * jnp.exp2(jnp.minimum(-dq, EXP2_CLAMP))
            keb = ke.astype(BF16)
            qcat = []
            kcat = []
            for j in range(nsub):
                qcat.append(jnp.concatenate(
                    [zero_sub if i < j else
                     qe[i * SUB:(i + 1) * SUB].astype(BF16) if i == j else
                     (qe[i * SUB:(i + 1) * SUB] * jnp.exp2(refs[i] - refs[j])).astype(BF16)
                     for i in range(nsub)], axis=0))
                kcat.append(jnp.concatenate(
                    [keb[j * SUB:(j + 1) * SUB] if i == j else zero_sub for i in range(nsub)], axis=0))
            q_in.append(qcat[0])
            att.append(lax.dot_general(jnp.concatenate(qcat, axis=1), jnp.concatenate(kcat, axis=1),
                                       (((1,), (1,)), ((), ())), preferred_element_type=F32))
            to_end = [jnp.exp2(b_last - refs[i]) for i in range(nsub)]
            k_out.append((ke * _group_rows(to_end)).astype(BF16))
            decay.append(to_end[0])
        return att, q_in, k_out, decay, vb, gate

    def group(gi, carry):
        rows = [pl.multiple_of((gi * HGRN_GROUP + j) * CHUNK, CHUNK) for j in range(HGRN_GROUP)]
        prep = [prepare(r0) for r0 in rows]
        st = [st_ref[hd] for hd in heads]
        o = []
        for att, q_in, k_out, decay, vb, gate in prep:
            o.append([lax.dot_general(q_in[hd], st[hd].astype(BF16), (((1,), (1,)), ((), ())),
                                      preferred_element_type=F32) for hd in heads])
            upd = [lax.dot_general(vb[:, sls[hd]], k_out[hd], (((0,), (0,)), ((), ())),
                                   preferred_element_type=F32) for hd in heads]
            st = [decay[hd] * st[hd] + upd[hd] for hd in heads]
        for hd in heads:
            st_ref[hd] = st[hd]
        for r0, oc, (att, q_in, k_out, decay, vb, gate) in zip(rows, o, prep):
            for hd in heads:
                a = jnp.where(causal, att[hd], 0.0).astype(BF16)
                oh = oc[hd] + jnp.dot(a, vb[:, sls[hd]], preferred_element_type=F32)
                on = _rms(oh, og) * gate[:, sls[hd]]
                o_ref[pl.ds(r0, CHUNK), sls[hd]] = on.astype(BF16)
        return carry

    @pl.when(pl.program_id(1) != 0)
    def _():
        lax.fori_loop(0, TM // (CHUNK * HGRN_GROUP), group, 0, unroll=HGRN_UNROLL)


def _hgrn(zh, lb, one_m_lb, og, tri2, nbatch, lp):
    r = zh.shape[0]
    tpb = lp // TM
    return pl.pallas_call(
        _hgrn_body,
        grid=(nbatch, tpb),
        in_specs=[
            pl.BlockSpec((TM, 4 * HW), lambda b, t: (b * tpb + t, 0)),
            _const_spec((1, HW)),
            _const_spec((1, HW)),
            _const_spec((1, HD)),
            _const_spec((CHUNK, 2 * CHUNK)),
        ],
        out_specs=pl.BlockSpec((TM, HW), lambda b, t: (b * tpb + t, 0)),
        out_shape=jax.ShapeDtypeStruct((r, HW), BF16),
        scratch_shapes=[pltpu.VMEM((HEADS, HD, HD), F32)],
        compiler_params=_params(("parallel", "arbitrary")),
        name="hgrn",
    )(zh, lb, one_m_lb, og, tri2)


META_STEP, FULL_STEP, DIAG_LO_STEP, DIAG_HI_STEP = range(4)


def _attn_steps(lp):
    kind, qi, kj, fin = [], [], [], []
    for i in range(lp // TQ):
        tiles = [1] + list(range(2, 2 * i + 2))
        for j in tiles:
            kind.append(META_STEP if j == 1 else DIAG_LO_STEP if j == 2 * i
                        else DIAG_HI_STEP if j == 2 * i + 1 else FULL_STEP)
            qi.append(i)
            kj.append(j)
            fin.append(int(j == tiles[-1]))
    return [np.asarray(a, np.int32) for a in (kind, qi, kj, fin)]


def _attn_body(fixed_shift, kind_ref, qi_ref, kj_ref, fin_ref, bound_ref, q_ref, k_ref, v_ref, o_ref,
               *scratch):
    if fixed_shift:
        l_ref, acc_ref = scratch
    else:
        m_ref, l_ref, acc_ref = scratch
    step = pl.program_id(1)
    kind = kind_ref[step]
    qi = qi_ref[step]
    kj = kj_ref[step]
    bound = bound_ref[0]

    def update(rows, keys, mask, first):
        for hd in range(HEADS):
            hq = slice(hd * QK_PAD, (hd + 1) * QK_PAD)
            hv = slice(hd * HD, (hd + 1) * HD)
            s = lax.dot_general(q_ref[rows, hq], k_ref[keys, hq], (((1,), (1,)), ((), ())),
                                preferred_element_type=F32)
            if fixed_shift:
                p = jnp.exp2(s - bound)
                if mask is not None:
                    p = jnp.where(mask, p, 0.0)
            else:
                if mask is not None:
                    s = jnp.where(mask, s, MASK_VALUE)
                m_prev = m_ref[hd, rows, :]
                m_new = jnp.maximum(m_prev, jnp.max(s, axis=-1, keepdims=True))
                alpha = jnp.exp2(m_prev - m_new)
                m_ref[hd, rows, :] = m_new
                p = jnp.exp2(s - jnp.tile(m_new, (1, s.shape[1] // HD)))
            part = p[:, 0:HD]
            for c in range(1, p.shape[1] // HD):
                part = part + p[:, c * HD:(c + 1) * HD]
            pv = jnp.dot(p.astype(BF16), v_ref[keys, hv], preferred_element_type=F32)
            if first and fixed_shift:
                l_ref[hd, rows, :] = part
                acc_ref[rows, hv] = pv
            elif fixed_shift:
                l_ref[hd, rows, :] += part
                acc_ref[rows, hv] += pv
            else:
                l_ref[hd, rows, :] = alpha * l_ref[hd, rows, :] + part
                acc_ref[rows, hv] = alpha * acc_ref[rows, hv] + pv

    all_rows = slice(0, TQ)
    all_keys = slice(0, TK)

    @pl.when(kind == META_STEP)
    def _():
        if not fixed_shift:
            m_ref[...] = jnp.full_like(m_ref, MASK_VALUE)
            l_ref[...] = jnp.zeros_like(l_ref)
            acc_ref[...] = jnp.zeros_like(acc_ref)
        col = lax.broadcasted_iota(jnp.int32, (TQ, HD), 1) + (kj * TK + TK - HD)
        update(all_rows, slice(TK - HD, TK), col >= PAD, True)

    @pl.when(kind == FULL_STEP)
    def _():
        update(all_rows, all_keys, None, False)

    @pl.when(kind == DIAG_LO_STEP)
    def _():
        row = lax.broadcasted_iota(jnp.int32, (TQ, TK), 0)
        col = lax.broadcasted_iota(jnp.int32, (TQ, TK), 1)
        update(all_rows, all_keys, col // CHUNK <= row // CHUNK, False)

    @pl.when(kind == DIAG_HI_STEP)
    def _():
        row = lax.broadcasted_iota(jnp.int32, (TQ - TK, TK), 0)
        col = lax.broadcasted_iota(jnp.int32, (TQ - TK, TK), 1)
        update(slice(TK, TQ), all_keys, col // CHUNK <= row // CHUNK, False)

    @pl.when(fin_ref[step] == 1)
    def _():
        row = lax.broadcasted_iota(jnp.int32, (TQ, HD), 0) + qi * TQ
        valid = row >= PAD
        for hd in range(HEADS):
            hv = slice(hd * HD, (hd + 1) * HD)
            o = acc_ref[:, hv] / jnp.sum(l_ref[hd], axis=-1, keepdims=True)
            o_ref[:, hv] = jnp.where(valid, o, 0.0).astype(BF16)


def _attn(q, k, v, bound, nbatch, lp, fixed_shift):
    r = q.shape[0]
    nq, nk = lp // TQ, lp // TK
    tables = _attn_steps(lp)
    stats = [pltpu.VMEM((HEADS, TQ, HD), F32)] * (1 if fixed_shift else 2)
    qmap = lambda b, s, kind, qi, kj, fin, bd: (b * nq + qi[s], 0)
    kmap = lambda b, s, kind, qi, kj, fin, bd: (b * nk + kj[s], 0)
    grid_spec = pltpu.PrefetchScalarGridSpec(
        num_scalar_prefetch=5,
        grid=(nbatch, len(tables[0])),
        in_specs=[
            pl.BlockSpec((TQ, HEADS * QK_PAD), qmap),
            pl.BlockSpec((TK, HEADS * QK_PAD), kmap),
            pl.BlockSpec((TK, HW), kmap),
        ],
        out_specs=pl.BlockSpec((TQ, HW), qmap),
        scratch_shapes=stats + [pltpu.VMEM((TQ, HW), F32)],
    )
    return pl.pallas_call(
        functools.partial(_attn_body, fixed_shift),
        grid_spec=grid_spec,
        out_shape=jax.ShapeDtypeStruct((r, HW), BF16),
        compiler_params=_params(("parallel", "arbitrary")),
        name="attn_fixed_shift" if fixed_shift else "attn_online",
    )(*[jnp.asarray(t) for t in tables], bound, q, k, v)


def _outproj_mlp_body(tpb, first_layer, *refs):
    if first_layer:
        meta_ref, refs = refs[0], refs[1:]
    h_ref, oa_ref, ob_ref, wo_ref, g_ref, wu_ref, wd_ref, out_ref = refs
    tt = pl.program_id(0) % tpb

    @pl.when(tt == 0)
    def _():
        out_ref[...] = jnp.zeros_like(out_ref)

    @pl.when(tt != 0)
    def _():
        h = _stream_tile(h_ref, meta_ref, tt) if first_layer else h_ref[...]
        mix = jnp.dot(oa_ref[...], wo_ref[:HW, :], preferred_element_type=F32)
        mix = mix + jnp.dot(ob_ref[...], wo_ref[HW:, :], preferred_element_type=F32)
        out_ref[...] = _mlp(h + mix, g_ref, wu_ref, wd_ref)


def _outproj_mlp(h, meta, oa, ob, wo, g, wu, wd, nbatch, lp):
    r = nbatch * lp
    tpb = lp // TM
    first_layer = meta is not None
    row = lambda i: (i, 0)
    h_spec = pl.BlockSpec((TM, D_MODEL), (lambda i: (_frame_tile(i, tpb), 0)) if first_layer else row)
    lead_specs = [_const_spec((N_META, D_MODEL))] if first_layer else []
    lead_args = [meta] if first_layer else []
    return pl.pallas_call(
        functools.partial(_outproj_mlp_body, tpb, first_layer),
        grid=(r // TM,),
        in_specs=lead_specs + [
            h_spec,
            pl.BlockSpec((TM, HW), row),
            pl.BlockSpec((TM, HW), row),
            _const_spec((2 * HW, D_MODEL)),
            _const_spec((1, D_MODEL)),
            _const_spec((D_MODEL, D_FF)),
            _const_spec((D_FF, D_MODEL)),
        ],
        out_specs=pl.BlockSpec((TM, D_MODEL), row),
        out_shape=jax.ShapeDtypeStruct((r, D_MODEL), F32),
        compiler_params=_params(("parallel",)),
        name="outproj_mlp",
    )(*lead_args, h, oa, ob, wo, g, wu, wd)


def _pool_mlp_body(tpb, h_ref, halo_ref, gm_ref, pw_ref, ps_ref, g_ref, wu_ref, wd_ref,
                   out_ref, u_ref, a_ref, b_ref):
    tt = pl.program_id(0) % tpb

    @pl.when(tt == 0)
    def _():
        out_ref[...] = jnp.zeros_like(out_ref)

    @pl.when(tt != 0)
    def _():
        h = h_ref[...]
        gm = gm_ref[...]
        u_ref[0:HALO, :] = _rms(halo_ref[...], gm)
        u = _rms(h, gm)
        u_ref[HALO:, :] = u
        n = TM + HALO
        g = POOL_G
        a_ref[8:n, :] = u_ref[8:n, :] + u_ref[7:n - 1, :]
        b_ref[16:n, g:] = a_ref[16:n, g:] + a_ref[14:n - 2, g:]
        a_ref[24:n, 2 * g:] = b_ref[24:n, 2 * g:] + b_ref[20:n - 4, 2 * g:]
        b_ref[32:n, 3 * g:] = a_ref[32:n, 3 * g:] + a_ref[24:n - 8, 3 * g:]
        wins = (a_ref[HALO:, 0:g], b_ref[HALO:, g:2 * g], a_ref[HALO:, 2 * g:3 * g], b_ref[HALO:, 3 * g:])
        pos = lax.broadcasted_iota(jnp.int32, (TM, g), 0) + (tt * TM - PAD)
        cnt = jnp.maximum(pos + 1, 1).astype(F32)
        ps = ps_ref[...]
        ys = []
        for gi, w in enumerate(POOL_WINDOWS):
            d = wins[gi] / jnp.minimum(cnt, float(w)) - u[:, gi * g:(gi + 1) * g]
            y = jnp.dot(d.astype(BF16), pw_ref[gi], preferred_element_type=F32)
            ys.append(y * ps[:, gi * g:(gi + 1) * g])
        out_ref[...] = _mlp(h + jnp.concatenate(ys, axis=1), g_ref, wu_ref, wd_ref)


def _pool_mlp(h, gm, pw, ps, g, wu, wd, nbatch, lp, to_frames):
    r = nbatch * lp
    tpb = lp // TM
    row = lambda i: (i, 0)
    out_rows = nbatch * (lp - LEAD) if to_frames else r
    return pl.pallas_call(
        functools.partial(_pool_mlp_body, tpb),
        grid=(r // TM,),
        in_specs=[
            pl.BlockSpec((TM, D_MODEL), row),
            pl.BlockSpec((HALO, D_MODEL), lambda i: (jnp.maximum(i * (TM // HALO) - 1, 0), 0)),
            _const_spec((1, D_MODEL)),
            _const_spec((len(POOL_WINDOWS), POOL_G, POOL_G)),
            _const_spec((1, D_MODEL)),
            _const_spec((1, D_MODEL)),
            _const_spec((D_MODEL, D_FF)),
            _const_spec((D_FF, D_MODEL)),
        ],
        out_specs=pl.BlockSpec((TM, D_MODEL), (lambda i: (_frame_tile(i, tpb), 0)) if to_frames else row),
        out_shape=jax.ShapeDtypeStruct((out_rows, D_MODEL), F32),
        scratch_shapes=[pltpu.VMEM((TM + HALO, D_MODEL), F32)] * 3,
        compiler_params=_params(("arbitrary",) if to_frames else ("parallel",)),
        name="pool_mlp",
    )(h, h, gm, pw, ps, g, wu, wd)


def _rope_cols(w):
    z = jnp.zeros(w.shape[:-1] + (32,), w.dtype)
    return jnp.concatenate([w[..., :32], z, w[..., 32:], z], axis=-1)


def _qk_cols(w):
    w = w.reshape(w.shape[:-1] + (HEADS, QK_DIM))
    w = jnp.concatenate([w[..., :HD], _rope_cols(w[..., HD:])], axis=-1)
    return w.reshape(w.shape[:-2] + (HEADS * QK_PAD,))


def _rope_tables(lp):
    half = ROPE // 2
    inv = ROPE_THETA ** (-jnp.arange(half, dtype=F32) / half)
    pos = jnp.maximum(jnp.arange(lp, dtype=F32) - PAD, 0.0)
    ang = pos[:, None] * inv[None, :]
    c, s = jnp.cos(ang), jnp.sin(ang)
    z = jnp.zeros_like(c)
    return (jnp.concatenate([c, z, c, z], axis=1), jnp.concatenate([-s, z, s, z], axis=1))


def kernel(x, meta_tokens, mix_norm, mlp_norm, w_mlp_up, w_mlp_down, w_in, hgrn_lb, hgrn_out_norm, mla_q_a_norm, mla_kv_a_norm, w_q_up, w_kv_up, q_norm, k_norm, w_out, pool_w, pool_scale):
    nbatch, seq, _ = x.shape
    depth = mix_norm.shape[0]
    assert seq % TQ == 0 and depth % 2 == 0
    lp = seq + LEAD

    cos_t, sin_t = _rope_tables(lp)
    lb_cum = jnp.cumsum(jax.nn.softmax(hgrn_lb.astype(F32), axis=0), axis=0)
    lower = lb_cum - lb_cum[0:1]
    tri = jnp.tril(jnp.ones((CHUNK, CHUNK), F32)).astype(BF16)
    tri2 = jnp.concatenate([tri, tri], axis=1)

    w_in_l = jnp.concatenate(
        [w_in[..., :4 * HW + Q_RANK + KV_RANK], _rope_cols(w_in[..., 4 * HW + Q_RANK + KV_RANK:])],
        axis=-1).astype(BF16)
    wq_l = _qk_cols(w_q_up).astype(BF16)
    wkv = w_kv_up.reshape(w_kv_up.shape[0], KV_RANK, HEADS, 2 * HD)
    wkv_l = jnp.concatenate([wkv[..., :HD].reshape(-1, KV_RANK, HW),
                             wkv[..., HD:].reshape(-1, KV_RANK, HW)], axis=-1).astype(BF16)
    qn_l = jnp.concatenate([q_norm[:, :HD], _rope_cols(q_norm[:, HD:])], axis=-1) * Q_SCALE
    kn_l = jnp.concatenate([k_norm[:, :HD], _rope_cols(k_norm[:, HD:])], axis=-1)
    wo_l = w_out.astype(BF16)
    wu_l = w_mlp_up.astype(BF16)
    wd_l = w_mlp_down.astype(BF16)
    pw_l = pool_w.astype(BF16)

    h = x.reshape(nbatch * seq, D_MODEL)
    meta = meta_tokens.astype(F32)
    for layer in range(depth):
        if layer % 2 == 0:
            e = layer // 2
            zh, q, k, v = _inproj(h, meta, mix_norm[layer][None], w_in_l[e], mla_q_a_norm[e][None],
                                  mla_kv_a_norm[e][None], wq_l[e], wkv_l[e], qn_l[e][None],
                                  kn_l[e][None], cos_t, sin_t, nbatch, lp)
            oa = _hgrn(zh, lower[e][None], 1.0 - lower[e][None], hgrn_out_norm[e][None], tri2, nbatch, lp)
            bound = (Q_SCALE * QK_DIM * 1.01) * jnp.max(jnp.abs(q_norm[e])) * jnp.max(jnp.abs(k_norm[e]))
            bound = bound.reshape(1).astype(F32)
            ob = lax.cond(bound[0] <= MAX_FIXED_SHIFT,
                          functools.partial(_attn, nbatch=nbatch, lp=lp, fixed_shift=True),
                          functools.partial(_attn, nbatch=nbatch, lp=lp, fixed_shift=False),
                          q, k, v, bound)
            h = _outproj_mlp(h, meta, oa, ob, wo_l[e], mlp_norm[layer][None], wu_l[layer], wd_l[layer],
                             nbatch, lp)
            meta = None
        else:
            o = layer // 2
            h = _pool_mlp(h, mix_norm[layer][None], pw_l[o], pool_scale[o][None],
                          mlp_norm[layer][None], wu_l[layer], wd_l[layer], nbatch, lp,
                          to_frames=layer == depth - 1)

    return h.reshape(nbatch, seq, D_MODEL)
```

```python
import functools

import numpy as np
import jax
import jax.numpy as jnp
from jax import lax
from jax.experimental import pallas as pl
from jax.experimental.pallas import tpu as pltpu

F32 = jnp.float32
BF16 = jnp.bfloat16

D_MODEL = 1024
D_FF = 4 * D_MODEL
EPS = 1e-6
N_META = 16
CHUNK = 64
HEADS = 4
HD = 128
HW = HEADS * HD
ROPE = 64
QK_DIM = HD + ROPE
QK_PAD = 256
Q_RANK = 256
KV_RANK = 256
ROPE_THETA = 10000.0
POOL_WINDOWS = (2, 4, 8, 16)
POOL_G = D_MODEL // len(POOL_WINDOWS)

LEAD = 1024
PAD = LEAD - N_META
TM = 512
LEAD_TILES = LEAD // TM
TQ = 1024
TK = 512
SUB = 16
HALO = 32
IN_COLS = 4 * HW + Q_RANK + KV_RANK + HD
Q_SCALE = QK_DIM ** -0.5 * float(np.log2(np.e))
MAX_FIXED_SHIFT = 56.0
MASK_VALUE = -1e30
EXP2_CLAMP = 115.0
TINY = 1e-37
VMEM_LIMIT = 56 * 1024 * 1024


def _rms(x, g):
    return x * lax.rsqrt(jnp.mean(x * x, axis=-1, keepdims=True) + EPS) * g


def _silu(x):
    hx = 0.5 * x
    return hx + hx * jnp.tanh(hx)


def _const_spec(shape):
    nd = len(shape)
    return pl.BlockSpec(shape, lambda *_: (0,) * nd, pipeline_mode=pl.Buffered(1))


def _params(sem):
    return pltpu.CompilerParams(dimension_semantics=sem, vmem_limit_bytes=VMEM_LIMIT)


def _frame_tile(i, tpb):
    return (i // tpb) * (tpb - LEAD_TILES) + jnp.maximum(i % tpb - LEAD_TILES, 0)


def _stream_tile(h_ref, meta_ref, tt):
    lead = jnp.concatenate([jnp.zeros((TM - N_META, D_MODEL), F32), meta_ref[...]], axis=0)
    lead = jnp.where(tt == LEAD_TILES - 1, lead, 0.0)
    return jnp.where(tt >= LEAD_TILES, h_ref[...], lead)


def _mlp(h, g_ref, wu_ref, wd_ref):
    hn = _rms(h, g_ref[...]).astype(BF16)
    acc = jnp.zeros_like(h)
    fc = 1024
    for c in range(D_FF // fc):
        a = jnp.dot(hn, wu_ref[:, c * fc:(c + 1) * fc], preferred_element_type=F32)
        a = jnp.maximum(a, 0.0)
        a = (a * a).astype(BF16)
        acc = acc + jnp.dot(a, wd_ref[c * fc:(c + 1) * fc, :], preferred_element_type=F32)
    return h + acc


def _rope(x, c, s):
    return x * c + pltpu.roll(x, 64, axis=1) * s


def _inproj_body(tpb, ntiles, first_layer, *refs):
    if first_layer:
        meta_ref, refs = refs[0], refs[1:]
    (h_ref, g_ref, win_ref, qag_ref, kvag_ref, wq_ref, wkv_ref, qn_ref, kn_ref, cos_ref, sin_ref,
     zh_ref, q_ref, k_ref, v_ref, lat_ref) = refs
    i = pl.program_id(0)

    @pl.when(i == 0)
    def _():
        lat_ref[1] = jnp.zeros(lat_ref.shape[1:], F32)

    lat = lat_ref[(i + 1) % 2]
    q = jnp.dot(_rms(lat[:, :Q_RANK], qag_ref[...]).astype(BF16), wq_ref[...], preferred_element_type=F32)
    kv = jnp.dot(_rms(lat[:, Q_RANK:Q_RANK + KV_RANK], kvag_ref[...]).astype(BF16), wkv_ref[...],
                 preferred_element_type=F32)
    kr = lat[:, Q_RANK + KV_RANK:]

    tt = jnp.minimum(i, ntiles - 1) % tpb
    h = _stream_tile(h_ref, meta_ref, tt) if first_layer else h_ref[...]
    z = jnp.dot(_rms(h, g_ref[...]).astype(BF16), win_ref[...], preferred_element_type=F32)
    zh_ref[...] = z[:, :4 * HW]
    lat_ref[i % 2] = z[:, 4 * HW:]

    v_ref[...] = kv[:, HW:].astype(BF16)
    c = cos_ref[...]
    s = sin_ref[...]
    qg = qn_ref[...]
    kg = kn_ref[...]
    kr_ss = jnp.sum(kr * kr, axis=-1, keepdims=True)
    for hd in range(HEADS):
        qa = q[:, hd * QK_PAD:hd * QK_PAD + HD]
        qb = q[:, hd * QK_PAD + HD:(hd + 1) * QK_PAD]
        ss = jnp.sum(qa * qa + qb * qb, axis=-1, keepdims=True)
        inv = lax.rsqrt(ss * (1.0 / QK_DIM) + EPS)
        q_ref[:, hd * QK_PAD:hd * QK_PAD + HD] = (qa * inv * qg[:, :HD]).astype(BF16)
        q_ref[:, hd * QK_PAD + HD:(hd + 1) * QK_PAD] = _rope(qb * inv * qg[:, HD:], c, s).astype(BF16)
        ka = kv[:, hd * HD:(hd + 1) * HD]
        ss = jnp.sum(ka * ka, axis=-1, keepdims=True) + kr_ss
        inv = lax.rsqrt(ss * (1.0 / QK_DIM) + EPS)
        k_ref[:, hd * QK_PAD:hd * QK_PAD + HD] = (ka * inv * kg[:, :HD]).astype(BF16)
        k_ref[:, hd * QK_PAD + HD:(hd + 1) * QK_PAD] = _rope(kr * inv * kg[:, HD:], c, s).astype(BF16)


def _inproj(h, meta, g, win, qag, kvag, wq, wkv, qn, kn, cos_t, sin_t, nbatch, lp):
    r = nbatch * lp
    tpb = lp // TM
    ntiles = r // TM
    first_layer = meta is not None
    cur = lambda i: jnp.minimum(i, ntiles - 1)
    prev = lambda i: jnp.maximum(i - 1, 0)
    row = lambda i: (prev(i), 0)
    tab = lambda i: (prev(i) % tpb, 0)
    h_spec = pl.BlockSpec((TM, D_MODEL),
                          (lambda i: (_frame_tile(cur(i), tpb), 0)) if first_layer else (lambda i: (cur(i), 0)))
    lead_specs = [_const_spec((N_META, D_MODEL))] if first_layer else []
    lead_args = [meta] if first_layer else []
    return pl.pallas_call(
        functools.partial(_inproj_body, tpb, ntiles, first_layer),
        grid=(ntiles + 1,),
        in_specs=lead_specs + [
            h_spec,
            _const_spec((1, D_MODEL)),
            _const_spec((D_MODEL, IN_COLS)),
            _const_spec((1, Q_RANK)),
            _const_spec((1, KV_RANK)),
            _const_spec((Q_RANK, HEADS * QK_PAD)),
            _const_spec((KV_RANK, 2 * HW)),
            _const_spec((1, QK_PAD)),
            _const_spec((1, QK_PAD)),
            pl.BlockSpec((TM, HD), tab),
            pl.BlockSpec((TM, HD), tab),
        ],
        out_specs=[
            pl.BlockSpec((TM, 4 * HW), lambda i: (cur(i), 0)),
            pl.BlockSpec((TM, HEADS * QK_PAD), row),
            pl.BlockSpec((TM, HEADS * QK_PAD), row),
            pl.BlockSpec((TM, HW), row),
        ],
        out_shape=[
            jax.ShapeDtypeStruct((r, 4 * HW), F32),
            jax.ShapeDtypeStruct((r, HEADS * QK_PAD), BF16),
            jax.ShapeDtypeStruct((r, HEADS * QK_PAD), BF16),
            jax.ShapeDtypeStruct((r, HW), BF16),
        ],
        scratch_shapes=[pltpu.VMEM((2, TM, IN_COLS - 4 * HW), F32)],
        compiler_params=_params(("arbitrary",)),
        name="inproj",
    )(*lead_args, h, g, win, qag, kvag, wq, wkv, qn, kn, cos_t, sin_t)


def _group_rows(rows):
    return jnp.concatenate([jnp.broadcast_to(r, (SUB, HD)) for r in rows], axis=0)


def _hgrn_stages(zh_ref, lb_ref, one_m_lb_ref, og_ref, tri_ref, st_ref, keep, o_ref):
    tri2 = tri_ref[...]
    lb = lb_ref[...]
    one_m_lb = one_m_lb_ref[...]
    og = og_ref[...]
    tt = lax.broadcasted_iota(jnp.int32, (CHUNK, CHUNK), 0)
    ss_ = lax.broadcasted_iota(jnp.int32, (CHUNK, CHUNK), 1)
    causal = ss_ <= tt
    nsub = CHUNK // SUB
    zero_row = jnp.zeros((1, HD), F32)
    zero_sub = jnp.zeros((SUB, HD), BF16)

    heads = range(HEADS)
    sls = [slice(hd * HD, (hd + 1) * HD) for hd in heads]

    chunk_rows = [slice(c * CHUNK, (c + 1) * CHUNK) for c in range(TM // CHUNK)]

    def gates_and_decay(rows):
        hq = zh_ref[rows, 0:HW]
        hf = zh_ref[rows, HW:2 * HW]
        hi = zh_ref[rows, 2 * HW:3 * HW]
        hg = zh_ref[rows, 3 * HW:4 * HW]
        q = _silu(hq)
        gate = _silu(hg)
        t = jnp.exp(-jnp.abs(hf))
        r = 1.0 / (1.0 + t)
        tr = t * r
        pos = hf >= 0.0
        log2f = jnp.log2(jnp.maximum(lb + one_m_lb * jnp.where(pos, r, tr), TINY))
        k = one_m_lb * jnp.where(pos, tr, r)
        g1 = log2f.astype(BF16)
        g2 = (log2f - g1.astype(F32)).astype(BF16)
        b = jnp.dot(tri2, jnp.concatenate([g1, g2], axis=0), preferred_element_type=F32)
        return q, k, b, hi.astype(BF16), gate

    def intra_chunk(q, k, b, vb, gate):
        att, q_in, k_out, decay = [], [], [], []
        for hd in heads:
            bh = b[:, sls[hd]]
            b_last = bh[CHUNK - 1:CHUNK, :]
            refs = [zero_row] + [bh[i * SUB - 1:i * SUB, :] for i in range(1, nsub)]
            dq = bh - _group_rows(refs)
            qe = q[:, sls[hd]] * jnp.exp2(dq)
            ke = k[:, sls[hd]] * jnp.exp2(jnp.minimum(-dq, EXP2_CLAMP))
            keb = ke.astype(BF16)
            qcat = []
            kcat = []
            for j in range(nsub):
                qcat.append(jnp.concatenate(
                    [zero_sub if i < j else
                     qe[i * SUB:(i + 1) * SUB].astype(BF16) if i == j else
                     (qe[i * SUB:(i + 1) * SUB] * jnp.exp2(refs[i] - refs[j])).astype(BF16)
                     for i in range(nsub)], axis=0))
                kcat.append(jnp.concatenate(
                    [keb[j * SUB:(j + 1) * SUB] if i == j else zero_sub for i in range(nsub)], axis=0))
            q_in.append(qcat[0])
            att.append(lax.dot_general(jnp.concatenate(qcat, axis=1), jnp.concatenate(kcat, axis=1),
                                       (((1,), (1,)), ((), ())), preferred_element_type=F32))
            to_end = [jnp.exp2(b_last - refs[i]) for i in range(nsub)]
            k_out.append((ke * _group_rows(to_end)).astype(BF16))
            decay.append(to_end[0])
        upd = [lax.dot_general(vb[:, sls[hd]], k_out[hd], (((0,), (0,)), ((), ())),
                               preferred_element_type=F32) for hd in heads]
        return att, q_in, upd, decay, vb, gate

    def stage1():
        return [gates_and_decay(rows) for rows in chunk_rows]

    def stage2(s1):
        return [intra_chunk(*c) for c in s1]

    def stage3(s2):
        st = [st_ref[hd] * keep for hd in heads]
        o = []
        for att, q_in, upd, decay, vb, gate in s2:
            o.append([lax.dot_general(q_in[hd], st[hd].astype(BF16), (((1,), (1,)), ((), ())),
                                      preferred_element_type=F32) for hd in heads])
            st = [decay[hd] * st[hd] + upd[hd] for hd in heads]
        for hd in heads:
            st_ref[hd] = st[hd]
        return o

    def stage4(s2, o):
        for rows, oc, (att, q_in, upd, decay, vb, gate) in zip(chunk_rows, o, s2):
            for hd in heads:
                a = jnp.where(causal, att[hd], 0.0).astype(BF16)
                oh = oc[hd] + jnp.dot(a, vb[:, sls[hd]], preferred_element_type=F32)
                on = _rms(oh, og) * gate[:, sls[hd]]
                o_ref[rows, sls[hd]] = on.astype(BF16)

    return stage1, stage2, stage3, stage4


META_STEP, FULL_STEP, DIAG_LO_STEP, DIAG_HI_STEP = range(4)


def _attn_steps(lp):
    kind, qi, kj, fin = [], [], [], []
    for i in range(lp // TQ):
        tiles = [1] + list(range(2, 2 * i + 2))
        for j in tiles:
            kind.append(META_STEP if j == 1 else DIAG_LO_STEP if j == 2 * i
                        else DIAG_HI_STEP if j == 2 * i + 1 else FULL_STEP)
            qi.append(i)
            kj.append(j)
            fin.append(int(j == tiles[-1]))
    return [np.asarray(a, np.int32) for a in (kind, qi, kj, fin)]


def _attn_body(fixed_shift, kind_ref, qi_ref, kj_ref, fin_ref, bound_ref, q_ref, k_ref, v_ref, o_ref,
               *scratch):
    if fixed_shift:
        l_ref, acc_ref = scratch
    else:
        m_ref, l_ref, acc_ref = scratch
    step = pl.program_id(1)
    kind = kind_ref[step]
    qi = qi_ref[step]
    kj = kj_ref[step]
    bound = bound_ref[0]

    def update(rows, keys, mask, first):
        for hd in range(HEADS):
            hq = slice(hd * QK_PAD, (hd + 1) * QK_PAD)
            hv = slice(hd * HD, (hd + 1) * HD)
            s = lax.dot_general(q_ref[rows, hq], k_ref[keys, hq], (((1,), (1,)), ((), ())),
                                preferred_element_type=F32)
            if fixed_shift:
                p = jnp.exp2(s - bound)
                if mask is not None:
                    p = jnp.where(mask, p, 0.0)
            else:
                if mask is not None:
                    s = jnp.where(mask, s, MASK_VALUE)
                m_prev = m_ref[hd, rows, :]
                m_new = jnp.maximum(m_prev, jnp.max(s, axis=-1, keepdims=True))
                alpha = jnp.exp2(m_prev - m_new)
                m_ref[hd, rows, :] = m_new
                p = jnp.exp2(s - jnp.tile(m_new, (1, s.shape[1] // HD)))
            part = p[:, 0:HD]
            for c in range(1, p.shape[1] // HD):
                part = part + p[:, c * HD:(c + 1) * HD]
            pv = jnp.dot(p.astype(BF16), v_ref[keys, hv], preferred_element_type=F32)
            if first and fixed_shift:
                l_ref[hd, rows, :] = part
                acc_ref[rows, hv] = pv
            elif fixed_shift:
                l_ref[hd, rows, :] += part
                acc_ref[rows, hv] += pv
            else:
                l_ref[hd, rows, :] = alpha * l_ref[hd, rows, :] + part
                acc_ref[rows, hv] = alpha * acc_ref[rows, hv] + pv

    all_rows = slice(0, TQ)
    all_keys = slice(0, TK)

    @pl.when(kind == META_STEP)
    def _():
        if not fixed_shift:
            m_ref[...] = jnp.full_like(m_ref, MASK_VALUE)
            l_ref[...] = jnp.zeros_like(l_ref)
            acc_ref[...] = jnp.zeros_like(acc_ref)
        col = lax.broadcasted_iota(jnp.int32, (TQ, HD), 1) + (kj * TK + TK - HD)
        update(all_rows, slice(TK - HD, TK), col >= PAD, True)

    @pl.when(kind == FULL_STEP)
    def _():
        update(all_rows, all_keys, None, False)

    @pl.when(kind == DIAG_LO_STEP)
    def _():
        row = lax.broadcasted_iota(jnp.int32, (TQ, TK), 0)
        col = lax.broadcasted_iota(jnp.int32, (TQ, TK), 1)
        update(all_rows, all_keys, col // CHUNK <= row // CHUNK, False)

    @pl.when(kind == DIAG_HI_STEP)
    def _():
        row = lax.broadcasted_iota(jnp.int32, (TQ - TK, TK), 0)
        col = lax.broadcasted_iota(jnp.int32, (TQ - TK, TK), 1)
        update(slice(TK, TQ), all_keys, col // CHUNK <= row // CHUNK, False)

    @pl.when(fin_ref[step] == 1)
    def _():
        row = lax.broadcasted_iota(jnp.int32, (TQ, HD), 0) + qi * TQ
        valid = row >= PAD
        for hd in range(HEADS):
            hv = slice(hd * HD, (hd + 1) * HD)
            o = acc_ref[:, hv] / jnp.sum(l_ref[hd], axis=-1, keepdims=True)
            o_ref[:, hv] = jnp.where(valid, o, 0.0).astype(BF16)


def _attn(q, k, v, bound, nbatch, lp, fixed_shift):
    r = q.shape[0]
    nq, nk = lp // TQ, lp // TK
    tables = _attn_steps(lp)
    stats = [pltpu.VMEM((HEADS, TQ, HD), F32)] * (1 if fixed_shift else 2)
    qmap = lambda b, s, kind, qi, kj, fin, bd: (b * nq + qi[s], 0)
    kmap = lambda b, s, kind, qi, kj, fin, bd: (b * nk + kj[s], 0)
    grid_spec = pltpu.PrefetchScalarGridSpec(
        num_scalar_prefetch=5,
        grid=(nbatch, len(tables[0])),
        in_specs=[
            pl.BlockSpec((TQ, HEADS * QK_PAD), qmap),
            pl.BlockSpec((TK, HEADS * QK_PAD), kmap),
            pl.BlockSpec((TK, HW), kmap),
        ],
        out_specs=pl.BlockSpec((TQ, HW), qmap),
        scratch_shapes=stats + [pltpu.VMEM((TQ, HW), F32)],
    )
    return pl.pallas_call(
        functools.partial(_attn_body, fixed_shift),
        grid_spec=grid_spec,
        out_shape=jax.ShapeDtypeStruct((r, HW), BF16),
        compiler_params=_params(("parallel", "arbitrary")),
        name="attn_fixed_shift" if fixed_shift else "attn_online",
    )(*[jnp.asarray(t) for t in tables], bound, q, k, v)


def _mix_tail_body(tpb, ntiles, first_layer, *refs):
    if first_layer:
        meta_ref, refs = refs[0], refs[1:]
    (zh_ref, h_ref, ob_ref, lb_ref, one_m_lb_ref, og_ref, tri_ref, wo_ref, g_ref, wu_ref, wd_ref,
     out_ref, st_ref, oa_ref) = refs
    i = pl.program_id(0)

    @pl.when(i == 0)
    def _():
        st_ref[...] = jnp.zeros_like(st_ref)
        oa_ref[1] = jnp.zeros(oa_ref.shape[1:], BF16)

    keep = (jnp.minimum(i, ntiles - 1) % tpb != 0).astype(F32)
    stage1, stage2, stage3, stage4 = _hgrn_stages(
        zh_ref, lb_ref, one_m_lb_ref, og_ref, tri_ref, st_ref, keep, oa_ref.at[i % 2])

    tt = jnp.maximum(i - 1, 0) % tpb
    h = _stream_tile(h_ref, meta_ref, tt) if first_layer else h_ref[...]
    mix = jnp.dot(oa_ref[(i + 1) % 2], wo_ref[:HW, :], preferred_element_type=F32)
    mix = mix + jnp.dot(ob_ref[...], wo_ref[HW:, :], preferred_element_type=F32)
    s1 = stage1()
    h = h + mix
    hn = _rms(h, g_ref[...]).astype(BF16)
    fc = D_FF // 4

    def mlp_part(c):
        a = jnp.dot(hn, wu_ref[:, c * fc:(c + 1) * fc], preferred_element_type=F32)
        a = jnp.maximum(a, 0.0)
        a = (a * a).astype(BF16)
        return jnp.dot(a, wd_ref[c * fc:(c + 1) * fc, :], preferred_element_type=F32)

    acc = mlp_part(0)
    s2 = stage2(s1)
    acc = acc + mlp_part(1)
    o = stage3(s2)
    acc = acc + mlp_part(2)
    stage4(s2, o)
    acc = acc + mlp_part(3)
    out_ref[...] = h + acc


def _mix_tail(zh, h, meta, ob, lb, one_m_lb, og, tri2, wo, g, wu, wd, nbatch, lp):
    r = nbatch * lp
    tpb = lp // TM
    ntiles = r // TM
    first_layer = meta is not None
    cur = lambda i: jnp.minimum(i, ntiles - 1)
    prev = lambda i: jnp.maximum(i - 1, 0)
    row = lambda i: (prev(i), 0)
    h_spec = pl.BlockSpec((TM, D_MODEL),
                          (lambda i: (_frame_tile(prev(i), tpb), 0)) if first_layer else row)
    lead_specs = [_const_spec((N_META, D_MODEL))] if first_layer else []
    lead_args = [meta] if first_layer else []
    return pl.pallas_call(
        functools.partial(_mix_tail_body, tpb, ntiles, first_layer),
        grid=(ntiles + 1,),
        in_specs=lead_specs + [
            pl.BlockSpec((TM, 4 * HW), lambda i: (cur(i), 0)),
            h_spec,
            pl.BlockSpec((TM, HW), row),
            _const_spec((1, HW)),
            _const_spec((1, HW)),
            _const_spec((1, HD)),
            _const_spec((CHUNK, 2 * CHUNK)),
            _const_spec((2 * HW, D_MODEL)),
            _const_spec((1, D_MODEL)),
            _const_spec((D_MODEL, D_FF)),
            _const_spec((D_FF, D_MODEL)),
        ],
        out_specs=pl.BlockSpec((TM, D_MODEL), row),
        out_shape=jax.ShapeDtypeStruct((r, D_MODEL), F32),
        scratch_shapes=[pltpu.VMEM((HEADS, HD, HD), F32), pltpu.VMEM((2, TM, HW), BF16)],
        compiler_params=_params(("arbitrary",)),
        name="mix_tail",
    )(*lead_args, zh, h, ob, lb, one_m_lb, og, tri2, wo, g, wu, wd)


def _pool_mlp_body(tpb, h_ref, halo_ref, gm_ref, pw_ref, ps_ref, g_ref, wu_ref, wd_ref,
                   out_ref, u_ref, a_ref, b_ref):
    tt = pl.program_id(0) % tpb

    @pl.when(tt == 0)
    def _():
        out_ref[...] = jnp.zeros_like(out_ref)

    @pl.when(tt != 0)
    def _():
        h = h_ref[...]
        gm = gm_ref[...]
        u_ref[0:HALO, :] = _rms(halo_ref[...], gm)
        u = _rms(h, gm)
        u_ref[HALO:, :] = u
        n = TM + HALO
        g = POOL_G
        a_ref[8:n, :] = u_ref[8:n, :] + u_ref[7:n - 1, :]
        b_ref[16:n, g:] = a_ref[16:n, g:] + a_ref[14:n - 2, g:]
        a_ref[24:n, 2 * g:] = b_ref[24:n, 2 * g:] + b_ref[20:n - 4, 2 * g:]
        b_ref[32:n, 3 * g:] = a_ref[32:n, 3 * g:] + a_ref[24:n - 8, 3 * g:]
        wins = (a_ref[HALO:, 0:g], b_ref[HALO:, g:2 * g], a_ref[HALO:, 2 * g:3 * g], b_ref[HALO:, 3 * g:])
        pos = lax.broadcasted_iota(jnp.int32, (TM, g), 0) + (tt * TM - PAD)
        cnt = jnp.maximum(pos + 1, 1).astype(F32)
        ps = ps_ref[...]
        ys = []
        for gi, w in enumerate(POOL_WINDOWS):
            d = wins[gi] / jnp.minimum(cnt, float(w)) - u[:, gi * g:(gi + 1) * g]
            y = jnp.dot(d.astype(BF16), pw_ref[gi], preferred_element_type=F32)
            ys.append(y * ps[:, gi * g:(gi + 1) * g])
        out_ref[...] = _mlp(h + jnp.concatenate(ys, axis=1), g_ref, wu_ref, wd_ref)


def _pool_mlp(h, gm, pw, ps, g, wu, wd, nbatch, lp, to_frames):
    r = nbatch * lp
    tpb = lp // TM
    row = lambda i: (i, 0)
    out_rows = nbatch * (lp - LEAD) if to_frames else r
    return pl.pallas_call(
        functools.partial(_pool_mlp_body, tpb),
        grid=(r // TM,),
        in_specs=[
            pl.BlockSpec((TM, D_MODEL), row),
            pl.BlockSpec((HALO, D_MODEL), lambda i: (jnp.maximum(i * (TM // HALO) - 1, 0), 0)),
            _const_spec((1, D_MODEL)),
            _const_spec((len(POOL_WINDOWS), POOL_G, POOL_G)),
            _const_spec((1, D_MODEL)),
            _const_spec((1, D_MODEL)),
            _const_spec((D_MODEL, D_FF)),
            _const_spec((D_FF, D_MODEL)),
        ],
        out_specs=pl.BlockSpec((TM, D_MODEL), (lambda i: (_frame_tile(i, tpb), 0)) if to_frames else row),
        out_shape=jax.ShapeDtypeStruct((out_rows, D_MODEL), F32),
        scratch_shapes=[pltpu.VMEM((TM + HALO, D_MODEL), F32)] * 3,
        compiler_params=_params(("arbitrary",) if to_frames else ("parallel",)),
        name="pool_mlp",
    )(h, h, gm, pw, ps, g, wu, wd)


def _rope_cols(w):
    z = jnp.zeros(w.shape[:-1] + (32,), w.dtype)
    return jnp.concatenate([w[..., :32], z, w[..., 32:], z], axis=-1)


def _qk_cols(w):
    w = w.reshape(w.shape[:-1] + (HEADS, QK_DIM))
    w = jnp.concatenate([w[..., :HD], _rope_cols(w[..., HD:])], axis=-1)
    return w.reshape(w.shape[:-2] + (HEADS * QK_PAD,))


def _rope_tables(lp):
    half = ROPE // 2
    inv = ROPE_THETA ** (-jnp.arange(half, dtype=F32) / half)
    pos = jnp.maximum(jnp.arange(lp, dtype=F32) - PAD, 0.0)
    ang = pos[:, None] * inv[None, :]
    c, s = jnp.cos(ang), jnp.sin(ang)
    z = jnp.zeros_like(c)
    return (jnp.concatenate([c, z, c, z], axis=1), jnp.concatenate([-s, z, s, z], axis=1))


def kernel(x, meta_tokens, mix_norm, mlp_norm, w_mlp_up, w_mlp_down, w_in, hgrn_lb, hgrn_out_norm, mla_q_a_norm, mla_kv_a_norm, w_q_up, w_kv_up, q_norm, k_norm, w_out, pool_w, pool_scale):
    nbatch, seq, _ = x.shape
    depth = mix_norm.shape[0]
    assert seq % TQ == 0 and depth % 2 == 0
    lp = seq + LEAD

    cos_t, sin_t = _rope_tables(lp)
    lb_cum = jnp.cumsum(jax.nn.softmax(hgrn_lb.astype(F32), axis=0), axis=0)
    lower = lb_cum - lb_cum[0:1]
    tri = jnp.tril(jnp.ones((CHUNK, CHUNK), F32)).astype(BF16)
    tri2 = jnp.concatenate([tri, tri], axis=1)

    w_in_l = jnp.concatenate(
        [w_in[..., :4 * HW + Q_RANK + KV_RANK], _rope_cols(w_in[..., 4 * HW + Q_RANK + KV_RANK:])],
        axis=-1).astype(BF16)
    wq_l = _qk_cols(w_q_up).astype(BF16)
    wkv = w_kv_up.reshape(w_kv_up.shape[0], KV_RANK, HEADS, 2 * HD)
    wkv_l = jnp.concatenate([wkv[..., :HD].reshape(-1, KV_RANK, HW),
                             wkv[..., HD:].reshape(-1, KV_RANK, HW)], axis=-1).astype(BF16)
    qn_l = jnp.concatenate([q_norm[:, :HD], _rope_cols(q_norm[:, HD:])], axis=-1) * Q_SCALE
    kn_l = jnp.concatenate([k_norm[:, :HD], _rope_cols(k_norm[:, HD:])], axis=-1)
    wo_l = w_out.astype(BF16)
    wu_l = w_mlp_up.astype(BF16)
    wd_l = w_mlp_down.astype(BF16)
    pw_l = pool_w.astype(BF16)

    h = x.reshape(nbatch * seq, D_MODEL)
    meta = meta_tokens.astype(F32)
    for layer in range(depth):
        if layer % 2 == 0:
            e = layer // 2
            zh, q, k, v = _inproj(h, meta, mix_norm[layer][None], w_in_l[e], mla_q_a_norm[e][None],
                                  mla_kv_a_norm[e][None], wq_l[e], wkv_l[e], qn_l[e][None],
                                  kn_l[e][None], cos_t, sin_t, nbatch, lp)
            bound = (Q_SCALE * QK_DIM * 1.01) * jnp.max(jnp.abs(q_norm[e])) * jnp.max(jnp.abs(k_norm[e]))
            bound = bound.reshape(1).astype(F32)
            ob = lax.cond(bound[0] <= MAX_FIXED_SHIFT,
                          functools.partial(_attn, nbatch=nbatch, lp=lp, fixed_shift=True),
                          functools.partial(_attn, nbatch=nbatch, lp=lp, fixed_shift=False),
                          q, k, v, bound)
            h = _mix_tail(zh, h, meta, ob, lower[e][None], 1.0 - lower[e][None], hgrn_out_norm[e][None],
                          tri2, wo_l[e], mlp_norm[layer][None], wu_l[layer], wd_l[layer], nbatch, lp)
            meta = None
        else:
            o = layer // 2
            h = _pool_mlp(h, mix_norm[layer][None], pw_l[o], pool_scale[o][None],
                          mlp_norm[layer][None], wu_l[layer], wd_l[layer], nbatch, lp,
                          to_frames=layer == depth - 1)

    return h.reshape(nbatch, seq, D_MODEL)
```

```python
import functools

import numpy as np
import jax
import jax.numpy as jnp
from jax import lax
from jax.experimental import pallas as pl
from jax.experimental.pallas import tpu as pltpu

F32 = jnp.float32
BF16 = jnp.bfloat16

D_MODEL = 1024
D_FF = 4 * D_MODEL
EPS = 1e-6
N_META = 16
CHUNK = 64
HEADS = 4
HD = 128
HW = HEADS * HD
ROPE = 64
QK_DIM = HD + ROPE
QK_PAD = 256
Q_RANK = 256
KV_RANK = 256
ROPE_THETA = 10000.0
POOL_WINDOWS = (2, 4, 8, 16)
POOL_G = D_MODEL // len(POOL_WINDOWS)

LEAD = 1024
PAD = LEAD - N_META
TM = 512
LEAD_TILES = LEAD // TM
TQ = 1024
TK = 1024
SUB = 16
HALO = 32
IN_COLS = 4 * HW + Q_RANK + KV_RANK + HD
Q_SCALE = QK_DIM ** -0.5 * float(np.log2(np.e))
MAX_FIXED_SHIFT = 56.0
MASK_VALUE = -1e30
EXP2_CLAMP = 115.0
TINY = 1e-37
VMEM_LIMIT = 56 * 1024 * 1024


def _rms(x, g):
    return x * lax.rsqrt(jnp.mean(x * x, axis=-1, keepdims=True) + EPS) * g


def _silu(x):
    hx = 0.5 * x
    return hx + hx * jnp.tanh(hx)


def _const_spec(shape, layer=None):
    nd = len(shape)
    if layer is None:
        return pl.BlockSpec(shape, lambda *_: (0,) * nd, pipeline_mode=pl.Buffered(1))
    return pl.BlockSpec((None,) + tuple(shape), lambda *_: (layer,) + (0,) * nd, pipeline_mode=pl.Buffered(1))


def _params(sem):
    return pltpu.CompilerParams(dimension_semantics=sem, vmem_limit_bytes=VMEM_LIMIT)


def _frame_tile(i, tpb):
    return (i // tpb) * (tpb - LEAD_TILES) + jnp.maximum(i % tpb - LEAD_TILES, 0)


def _stream_tile(h_ref, meta_ref, tt):
    lead = jnp.concatenate([jnp.zeros((TM - N_META, D_MODEL), F32), meta_ref[...]], axis=0)
    lead = jnp.where(tt == LEAD_TILES - 1, lead, 0.0)
    return jnp.where(tt >= LEAD_TILES, h_ref[...], lead)


def _mlp(h, g_ref, wu_ref, wd_ref):
    hn = _rms(h, g_ref[...]).astype(BF16)
    acc = jnp.zeros_like(h)
    fc = 1024
    for c in range(D_FF // fc):
        a = jnp.dot(hn, wu_ref[:, c * fc:(c + 1) * fc], preferred_element_type=F32)
        a = jnp.maximum(a, 0.0)
        a = (a * a).astype(BF16)
        acc = acc + jnp.dot(a, wd_ref[c * fc:(c + 1) * fc, :], preferred_element_type=F32)
    return h + acc


def _rope(x, c, s):
    return x * c + pltpu.roll(x, 64, axis=1) * s


def _inproj_body(tpb, ntiles, first_layer, *refs):
    if first_layer:
        meta_ref, refs = refs[0], refs[1:]
    (h_ref, g_ref, win_ref, qag_ref, kvag_ref, wq_ref, wkv_ref, qn_ref, kn_ref, cos_ref, sin_ref,
     zh_ref, q_ref, k_ref, v_ref, lat_ref) = refs
    i = pl.program_id(0)

    @pl.when(i == 0)
    def _():
        lat_ref[1] = jnp.zeros(lat_ref.shape[1:], F32)

    lat = lat_ref[(i + 1) % 2]
    q = jnp.dot(_rms(lat[:, :Q_RANK], qag_ref[...]).astype(BF16), wq_ref[...], preferred_element_type=F32)
    kv = jnp.dot(_rms(lat[:, Q_RANK:Q_RANK + KV_RANK], kvag_ref[...]).astype(BF16), wkv_ref[...],
                 preferred_element_type=F32)
    kr = lat[:, Q_RANK + KV_RANK:]

    tt = jnp.minimum(i, ntiles - 1) % tpb
    h = _stream_tile(h_ref, meta_ref, tt) if first_layer else h_ref[...]
    z = jnp.dot(_rms(h, g_ref[...]).astype(BF16), win_ref[...], preferred_element_type=F32)
    zh_ref[...] = z[:, :4 * HW]
    lat_ref[i % 2] = z[:, 4 * HW:]

    v_ref[...] = kv[:, HW:].astype(BF16)
    c = cos_ref[...]
    s = sin_ref[...]
    qg = qn_ref[...]
    kg = kn_ref[...]
    kr_ss = jnp.sum(kr * kr, axis=-1, keepdims=True)
    for hd in range(HEADS):
        qa = q[:, hd * QK_PAD:hd * QK_PAD + HD]
        qb = q[:, hd * QK_PAD + HD:(hd + 1) * QK_PAD]
        ss = jnp.sum(qa * qa + qb * qb, axis=-1, keepdims=True)
        inv = lax.rsqrt(ss * (1.0 / QK_DIM) + EPS)
        q_ref[:, hd * QK_PAD:hd * QK_PAD + HD] = (qa * inv * qg[:, :HD]).astype(BF16)
        q_ref[:, hd * QK_PAD + HD:(hd + 1) * QK_PAD] = _rope(qb * inv * qg[:, HD:], c, s).astype(BF16)
        ka = kv[:, hd * HD:(hd + 1) * HD]
        ss = jnp.sum(ka * ka, axis=-1, keepdims=True) + kr_ss
        inv = lax.rsqrt(ss * (1.0 / QK_DIM) + EPS)
        k_ref[:, hd * QK_PAD:hd * QK_PAD + HD] = (ka * inv * kg[:, :HD]).astype(BF16)
        k_ref[:, hd * QK_PAD + HD:(hd + 1) * QK_PAD] = _rope(kr * inv * kg[:, HD:], c, s).astype(BF16)


def _inproj(h, meta, g, win, qag, kvag, wq, wkv, qn, kn, cos_t, sin_t, e, nbatch, lp):
    r = nbatch * lp
    tpb = lp // TM
    ntiles = r // TM
    first_layer = meta is not None
    cur = lambda i: jnp.minimum(i, ntiles - 1)
    prev = lambda i: jnp.maximum(i - 1, 0)
    row = lambda i: (prev(i), 0)
    tab = lambda i: (prev(i) % tpb, 0)
    h_spec = pl.BlockSpec((TM, D_MODEL),
                          (lambda i: (_frame_tile(cur(i), tpb), 0)) if first_layer else (lambda i: (cur(i), 0)))
    lead_specs = [_const_spec((N_META, D_MODEL))] if first_layer else []
    lead_args = [meta] if first_layer else []
    return pl.pallas_call(
        functools.partial(_inproj_body, tpb, ntiles, first_layer),
        grid=(ntiles + 1,),
        in_specs=lead_specs + [
            h_spec,
            _const_spec((1, D_MODEL)),
            _const_spec((D_MODEL, IN_COLS), e),
            _const_spec((1, Q_RANK)),
            _const_spec((1, KV_RANK)),
            _const_spec((Q_RANK, HEADS * QK_PAD), e),
            _const_spec((KV_RANK, 2 * HW), e),
            _const_spec((1, QK_PAD)),
            _const_spec((1, QK_PAD)),
            pl.BlockSpec((TM, HD), tab),
            pl.BlockSpec((TM, HD), tab),
        ],
        out_specs=[
            pl.BlockSpec((TM, 4 * HW), lambda i: (cur(i), 0)),
            pl.BlockSpec((TM, HEADS * QK_PAD), row),
            pl.BlockSpec((TM, HEADS * QK_PAD), row),
            pl.BlockSpec((TM, HW), row),
        ],
        out_shape=[
            jax.ShapeDtypeStruct((r, 4 * HW), F32),
            jax.ShapeDtypeStruct((r, HEADS * QK_PAD), BF16),
            jax.ShapeDtypeStruct((r, HEADS * QK_PAD), BF16),
            jax.ShapeDtypeStruct((r, HW), BF16),
        ],
        scratch_shapes=[pltpu.VMEM((2, TM, IN_COLS - 4 * HW), F32)],
        compiler_params=_params(("arbitrary",)),
        name="inproj",
    )(*lead_args, h, g, win, qag, kvag, wq, wkv, qn, kn, cos_t, sin_t)


def _group_rows(rows):
    return jnp.concatenate([jnp.broadcast_to(r, (SUB, HD)) for r in rows], axis=0)


def _hgrn_stages(zh_ref, lb_ref, one_m_lb_ref, og_ref, tri_ref, st_ref, keep, o_ref):
    tri2 = tri_ref[...]
    lb = lb_ref[...]
    one_m_lb = one_m_lb_ref[...]
    og = og_ref[...]
    tt = lax.broadcasted_iota(jnp.int32, (CHUNK, CHUNK), 0)
    ss_ = lax.broadcasted_iota(jnp.int32, (CHUNK, CHUNK), 1)
    causal = ss_ <= tt
    nsub = CHUNK // SUB
    zero_row = jnp.zeros((1, HD), F32)
    zero_sub = jnp.zeros((SUB, HD), BF16)

    heads = range(HEADS)
    sls = [slice(hd * HD, (hd + 1) * HD) for hd in heads]

    chunk_rows = [slice(c * CHUNK, (c + 1) * CHUNK) for c in range(TM // CHUNK)]

    def gates_and_decay(rows):
        hq = zh_ref[rows, 0:HW]
        hf = zh_ref[rows, HW:2 * HW]
        hi = zh_ref[rows, 2 * HW:3 * HW]
        hg = zh_ref[rows, 3 * HW:4 * HW]
        q = _silu(hq)
        gate = _silu(hg)
        t = jnp.exp(-jnp.abs(hf))
        r = 1.0 / (1.0 + t)
        tr = t * r
        pos = hf >= 0.0
        log2f = jnp.log2(jnp.maximum(lb + one_m_lb * jnp.where(pos, r, tr), TINY))
        k = one_m_lb * jnp.where(pos, tr, r)
        g1 = log2f.astype(BF16)
        g2 = (log2f - g1.astype(F32)).astype(BF16)
        b = jnp.dot(tri2, jnp.concatenate([g1, g2], axis=0), preferred_element_type=F32)
        return q, k, b, hi.astype(BF16), gate

    def intra_chunk(q, k, b, vb, gate):
        att, q_in, k_out, decay = [], [], [], []
        for hd in heads:
            bh = b[:, sls[hd]]
            b_last = bh[CHUNK - 1:CHUNK, :]
            refs = [zero_row] + [bh[i * SUB - 1:i * SUB, :] for i in range(1, nsub)]
            dq = bh - _group_rows(refs)
            qe = q[:, sls[hd]] * jnp.exp2(dq)
            ke = k[:, sls[hd]] * jnp.exp2(jnp.minimum(-dq, EXP2_CLAMP))
            keb = ke.astype(BF16)
            qcat = []
            kcat = []
            for j in range(nsub):
                qcat.append(jnp.concatenate(
                    [zero_sub if i < j else
                     qe[i * SUB:(i + 1) * SUB].astype(BF16) if i == j else
                     (qe[i * SUB:(i + 1) * SUB] * jnp.exp2(refs[i] - refs[j])).astype(BF16)
                     for i in range(nsub)], axis=0))
                kcat.append(jnp.concatenate(
                    [keb[j * SUB:(j + 1) * SUB] if i == j else zero_sub for i in range(nsub)], axis=0))
            q_in.append(qcat[0])
            att.append(lax.dot_general(jnp.concatenate(qcat, axis=1), jnp.concatenate(kcat, axis=1),
                                       (((1,), (1,)), ((), ())), preferred_element_type=F32))
            to_end = [jnp.exp2(b_last - refs[i]) for i in range(nsub)]
            k_out.append((ke * _group_rows(to_end)).astype(BF16))
            decay.append(to_end[0])
        upd = [lax.dot_general(vb[:, sls[hd]], k_out[hd], (((0,), (0,)), ((), ())),
                               preferred_element_type=F32) for hd in heads]
        return att, q_in, upd, decay, vb, gate

    def stage1():
        return [gates_and_decay(rows) for rows in chunk_rows]

    def stage2(s1):
        return [intra_chunk(*c) for c in s1]

    def stage3(s2):
        st = [st_ref[hd] * keep for hd in heads]
        o = []
        for att, q_in, upd, decay, vb, gate in s2:
            o.append([lax.dot_general(q_in[hd], st[hd].astype(BF16), (((1,), (1,)), ((), ())),
                                      preferred_element_type=F32) for hd in heads])
            st = [decay[hd] * st[hd] + upd[hd] for hd in heads]
        for hd in heads:
            st_ref[hd] = st[hd]
        return o

    def stage4(s2, o):
        for rows, oc, (att, q_in, upd, decay, vb, gate) in zip(chunk_rows, o, s2):
            for hd in heads:
                a = jnp.where(causal, att[hd], 0.0).astype(BF16)
                oh = oc[hd] + jnp.dot(a, vb[:, sls[hd]], preferred_element_type=F32)
                on = _rms(oh, og) * gate[:, sls[hd]]
                o_ref[rows, sls[hd]] = on.astype(BF16)

    return stage1, stage2, stage3, stage4


META_STEP, FULL_STEP, DIAG_STEP = range(3)


def _attn_steps(lp):
    kind, qi, kj, fin = [], [], [], []
    for i in range(lp // TQ):
        for j in range(i + 1):
            kind.append(META_STEP if j == 0 else DIAG_STEP if j == i else FULL_STEP)
            qi.append(i)
            kj.append(j)
            fin.append(int(j == i))
    return [np.asarray(a, np.int32) for a in (kind, qi, kj, fin)]


def _attn_body(fixed_shift, kind_ref, qi_ref, kj_ref, fin_ref, bound_ref, q_ref, k_ref, v_ref, o_ref,
               *scratch):
    if fixed_shift:
        l_ref, acc_ref = scratch
    else:
        m_ref, l_ref, acc_ref = scratch
    step = pl.program_id(1)
    kind = kind_ref[step]
    qi = qi_ref[step]
    kj = kj_ref[step]
    bound = bound_ref[0]

    def update(rows, keys, mask, first):
        for hd in range(HEADS):
            hq = slice(hd * QK_PAD, (hd + 1) * QK_PAD)
            hv = slice(hd * HD, (hd + 1) * HD)
            s = lax.dot_general(q_ref[rows, hq], k_ref[keys, hq], (((1,), (1,)), ((), ())),
                                preferred_element_type=F32)
            if fixed_shift:
                p = jnp.exp2(s - bound)
                if mask is not None:
                    p = jnp.where(mask, p, 0.0)
            else:
                if mask is not None:
                    s = jnp.where(mask, s, MASK_VALUE)
                m_prev = m_ref[hd, rows, :]
                m_new = jnp.maximum(m_prev, jnp.max(s, axis=-1, keepdims=True))
                alpha = jnp.exp2(m_prev - m_new)
                m_ref[hd, rows, :] = m_new
                p = jnp.exp2(s - jnp.tile(m_new, (1, s.shape[1] // HD)))
            part = p[:, 0:HD]
            for c in range(1, p.shape[1] // HD):
                part = part + p[:, c * HD:(c + 1) * HD]
            pv = jnp.dot(p.astype(BF16), v_ref[keys, hv], preferred_element_type=F32)
            if first and fixed_shift:
                l_ref[hd, rows, :] = part
                acc_ref[rows, hv] = pv
            elif fixed_shift:
                l_ref[hd, rows, :] += part
                acc_ref[rows, hv] += pv
            else:
                l_ref[hd, rows, :] = alpha * l_ref[hd, rows, :] + part
                acc_ref[rows, hv] = alpha * acc_ref[rows, hv] + pv

    all_rows = slice(0, TQ)
    all_keys = slice(0, TK)

    @pl.when(kind == META_STEP)
    def _():
        if not fixed_shift:
            m_ref[...] = jnp.full_like(m_ref, MASK_VALUE)
            l_ref[...] = jnp.zeros_like(l_ref)
            acc_ref[...] = jnp.zeros_like(acc_ref)
        col = lax.broadcasted_iota(jnp.int32, (TQ, HD), 1) + (kj * TK + TK - HD)
        update(all_rows, slice(TK - HD, TK), col >= PAD, True)

    @pl.when(kind == FULL_STEP)
    def _():
        update(all_rows, all_keys, None, False)

    @pl.when(kind == DIAG_STEP)
    def _():
        half = TQ // 2
        row = lax.broadcasted_iota(jnp.int32, (TQ, half), 0)
        col = lax.broadcasted_iota(jnp.int32, (TQ, half), 1)
        update(all_rows, slice(0, half), col // CHUNK <= row // CHUNK, False)
        update(slice(half, TQ), slice(half, TK), (col // CHUNK <= row // CHUNK)[:half], False)

    @pl.when(fin_ref[step] == 1)
    def _():
        row = lax.broadcasted_iota(jnp.int32, (TQ, HD), 0) + qi * TQ
        valid = row >= PAD
        for hd in range(HEADS):
            hv = slice(hd * HD, (hd + 1) * HD)
            o = acc_ref[:, hv] / jnp.sum(l_ref[hd], axis=-1, keepdims=True)
            o_ref[:, hv] = jnp.where(valid, o, 0.0).astype(BF16)


def _attn(q, k, v, bound, nbatch, lp, fixed_shift):
    r = q.shape[0]
    nq, nk = lp // TQ, lp // TK
    tables = _attn_steps(lp)
    stats = [pltpu.VMEM((HEADS, TQ, HD), F32)] * (1 if fixed_shift else 2)
    qmap = lambda b, s, kind, qi, kj, fin, bd: (b * nq + qi[s], 0)
    kmap = lambda b, s, kind, qi, kj, fin, bd: (b * nk + kj[s], 0)
    grid_spec = pltpu.PrefetchScalarGridSpec(
        num_scalar_prefetch=5,
        grid=(nbatch, len(tables[0])),
        in_specs=[
            pl.BlockSpec((TQ, HEADS * QK_PAD), qmap),
            pl.BlockSpec((TK, HEADS * QK_PAD), kmap),
            pl.BlockSpec((TK, HW), kmap),
        ],
        out_specs=pl.BlockSpec((TQ, HW), qmap),
        scratch_shapes=stats + [pltpu.VMEM((TQ, HW), F32)],
    )
    return pl.pallas_call(
        functools.partial(_attn_body, fixed_shift),
        grid_spec=grid_spec,
        out_shape=jax.ShapeDtypeStruct((r, HW), BF16),
        compiler_params=_params(("parallel", "arbitrary")),
        name="attn_fixed_shift" if fixed_shift else "attn_online",
    )(*[jnp.asarray(t) for t in tables], bound, q, k, v)


def _mix_tail_body(tpb, ntiles, first_layer, *refs):
    if first_layer:
        meta_ref, refs = refs[0], refs[1:]
    (zh_ref, h_ref, ob_ref, lb_ref, one_m_lb_ref, og_ref, tri_ref, wo_ref, g_ref, wu_ref, wd_ref,
     out_ref, st_ref, oa_ref) = refs
    i = pl.program_id(0)

    @pl.when(i == 0)
    def _():
        st_ref[...] = jnp.zeros_like(st_ref)
        oa_ref[1] = jnp.zeros(oa_ref.shape[1:], BF16)

    keep = (jnp.minimum(i, ntiles - 1) % tpb != 0).astype(F32)
    stage1, stage2, stage3, stage4 = _hgrn_stages(
        zh_ref, lb_ref, one_m_lb_ref, og_ref, tri_ref, st_ref, keep, oa_ref.at[i % 2])

    tt = jnp.maximum(i - 1, 0) % tpb
    h = _stream_tile(h_ref, meta_ref, tt) if first_layer else h_ref[...]
    mix = jnp.dot(oa_ref[(i + 1) % 2], wo_ref[:HW, :], preferred_element_type=F32)
    mix = mix + jnp.dot(ob_ref[...], wo_ref[HW:, :], preferred_element_type=F32)
    s1 = stage1()
    h = h + mix
    hn = _rms(h, g_ref[...]).astype(BF16)
    fc = D_FF // 4

    def mlp_part(c):
        a = jnp.dot(hn, wu_ref[:, c * fc:(c + 1) * fc], preferred_element_type=F32)
        a = jnp.maximum(a, 0.0)
        a = (a * a).astype(BF16)
        return jnp.dot(a, wd_ref[c * fc:(c + 1) * fc, :], preferred_element_type=F32)

    acc = mlp_part(0)
    s2 = stage2(s1)
    acc = acc + mlp_part(1)
    o = stage3(s2)
    acc = acc + mlp_part(2)
    stage4(s2, o)
    acc = acc + mlp_part(3)
    out_ref[...] = h + acc


def _mix_tail(zh, h, meta, ob, lb, one_m_lb, og, tri2, wo, g, wu, wd, e, layer, nbatch, lp):
    r = nbatch * lp
    tpb = lp // TM
    ntiles = r // TM
    first_layer = meta is not None
    cur = lambda i: jnp.minimum(i, ntiles - 1)
    prev = lambda i: jnp.maximum(i - 1, 0)
    row = lambda i: (prev(i), 0)
    h_spec = pl.BlockSpec((TM, D_MODEL),
                          (lambda i: (_frame_tile(prev(i), tpb), 0)) if first_layer else row)
    lead_specs = [_const_spec((N_META, D_MODEL))] if first_layer else []
    lead_args = [meta] if first_layer else []
    return pl.pallas_call(
        functools.partial(_mix_tail_body, tpb, ntiles, first_layer),
        grid=(ntiles + 1,),
        in_specs=lead_specs + [
            pl.BlockSpec((TM, 4 * HW), lambda i: (cur(i), 0)),
            h_spec,
            pl.BlockSpec((TM, HW), row),
            _const_spec((1, HW)),
            _const_spec((1, HW)),
            _const_spec((1, HD)),
            _const_spec((CHUNK, 2 * CHUNK)),
            _const_spec((2 * HW, D_MODEL), e),
            _const_spec((1, D_MODEL)),
            _const_spec((D_MODEL, D_FF), layer),
            _const_spec((D_FF, D_MODEL), layer),
        ],
        out_specs=pl.BlockSpec((TM, D_MODEL), row),
        out_shape=jax.ShapeDtypeStruct((r, D_MODEL), F32),
        scratch_shapes=[pltpu.VMEM((HEADS, HD, HD), F32), pltpu.VMEM((2, TM, HW), BF16)],
        compiler_params=_params(("arbitrary",)),
        name="mix_tail",
    )(*lead_args, zh, h, ob, lb, one_m_lb, og, tri2, wo, g, wu, wd)


def _pool_mlp_body(tpb, h_ref, halo_ref, gm_ref, pw_ref, ps_ref, g_ref, wu_ref, wd_ref,
                   out_ref, u_ref, a_ref, b_ref):
    tt = pl.program_id(0) % tpb

    @pl.when(tt == 0)
    def _():
        out_ref[...] = jnp.zeros_like(out_ref)

    @pl.when(tt != 0)
    def _():
        h = h_ref[...]
        gm = gm_ref[...]
        u_ref[0:HALO, :] = _rms(halo_ref[...], gm)
        u = _rms(h, gm)
        u_ref[HALO:, :] = u
        n = TM + HALO
        g = POOL_G
        a_ref[8:n, :] = u_ref[8:n, :] + u_ref[7:n - 1, :]
        b_ref[16:n, g:] = a_ref[16:n, g:] + a_ref[14:n - 2, g:]
        a_ref[24:n, 2 * g:] = b_ref[24:n, 2 * g:] + b_ref[20:n - 4, 2 * g:]
        b_ref[32:n, 3 * g:] = a_ref[32:n, 3 * g:] + a_ref[24:n - 8, 3 * g:]
        wins = (a_ref[HALO:, 0:g], b_ref[HALO:, g:2 * g], a_ref[HALO:, 2 * g:3 * g], b_ref[HALO:, 3 * g:])
        pos = lax.broadcasted_iota(jnp.int32, (TM, g), 0) + (tt * TM - PAD)
        cnt = jnp.maximum(pos + 1, 1).astype(F32)
        ps = ps_ref[...]
        ys = []
        for gi, w in enumerate(POOL_WINDOWS):
            d = wins[gi] / jnp.minimum(cnt, float(w)) - u[:, gi * g:(gi + 1) * g]
            y = jnp.dot(d.astype(BF16), pw_ref[gi], preferred_element_type=F32)
            ys.append(y * ps[:, gi * g:(gi + 1) * g])
        out_ref[...] = _mlp(h + jnp.concatenate(ys, axis=1), g_ref, wu_ref, wd_ref)


def _pool_mlp(h, gm, pw, ps, g, wu, wd, o, layer, nbatch, lp, to_frames):
    r = nbatch * lp
    tpb = lp // TM
    row = lambda i: (i, 0)
    out_rows = nbatch * (lp - LEAD) if to_frames else r
    return pl.pallas_call(
        functools.partial(_pool_mlp_body, tpb),
        grid=(r // TM,),
        in_specs=[
            pl.BlockSpec((TM, D_MODEL), row),
            pl.BlockSpec((HALO, D_MODEL), lambda i: (jnp.maximum(i * (TM // HALO) - 1, 0), 0)),
            _const_spec((1, D_MODEL)),
            _const_spec((len(POOL_WINDOWS), POOL_G, POOL_G), o),
            _const_spec((1, D_MODEL)),
            _const_spec((1, D_MODEL)),
            _const_spec((D_MODEL, D_FF), layer),
            _const_spec((D_FF, D_MODEL), layer),
        ],
        out_specs=pl.BlockSpec((TM, D_MODEL), (lambda i: (_frame_tile(i, tpb), 0)) if to_frames else row),
        out_shape=jax.ShapeDtypeStruct((out_rows, D_MODEL), F32),
        scratch_shapes=[pltpu.VMEM((TM + HALO, D_MODEL), F32)] * 3,
        compiler_params=_params(("arbitrary",) if to_frames else ("parallel",)),
        name="pool_mlp",
    )(h, h, gm, pw, ps, g, wu, wd)


def _rope_cols(w):
    z = jnp.zeros(w.shape[:-1] + (32,), w.dtype)
    return jnp.concatenate([w[..., :32], z, w[..., 32:], z], axis=-1)


def _qk_cols(w):
    w = w.reshape(w.shape[:-1] + (HEADS, QK_DIM))
    w = jnp.concatenate([w[..., :HD], _rope_cols(w[..., HD:])], axis=-1)
    return w.reshape(w.shape[:-2] + (HEADS * QK_PAD,))


def _rope_tables(lp):
    half = ROPE // 2
    inv = ROPE_THETA ** (-np.arange(half, dtype=np.float64) / half)
    pos = np.maximum(np.arange(lp, dtype=np.float64) - PAD, 0.0)
    ang = pos[:, None] * inv[None, :]
    c = jnp.asarray(np.cos(ang).astype(np.float32))
    s = jnp.asarray(np.sin(ang).astype(np.float32))
    z = jnp.zeros_like(c)
    return (jnp.concatenate([c, z, c, z], axis=1), jnp.concatenate([-s, z, s, z], axis=1))


def kernel(x, meta_tokens, mix_norm, mlp_norm, w_mlp_up, w_mlp_down, w_in, hgrn_lb, hgrn_out_norm, mla_q_a_norm, mla_kv_a_norm, w_q_up, w_kv_up, q_norm, k_norm, w_out, pool_w, pool_scale):
    nbatch, seq, _ = x.shape
    depth = mix_norm.shape[0]
    assert seq % TQ == 0 and depth % 2 == 0
    lp = seq + LEAD

    cos_t, sin_t = _rope_tables(lp)
    lb_cum = jnp.cumsum(jax.nn.softmax(hgrn_lb.astype(F32), axis=0), axis=0)
    lower = lb_cum - lb_cum[0:1]
    tri = jnp.tril(jnp.ones((CHUNK, CHUNK), F32)).astype(BF16)
    tri2 = jnp.concatenate([tri, tri], axis=1)

    w_in_l = jnp.concatenate(
        [w_in[..., :4 * HW + Q_RANK + KV_RANK], _rope_cols(w_in[..., 4 * HW + Q_RANK + KV_RANK:])],
        axis=-1).astype(BF16)
    wq_l = _qk_cols(w_q_up).astype(BF16)
    wkv = w_kv_up.reshape(w_kv_up.shape[0], KV_RANK, HEADS, 2 * HD)
    wkv_l = jnp.concatenate([wkv[..., :HD].reshape(-1, KV_RANK, HW),
                             wkv[..., HD:].reshape(-1, KV_RANK, HW)], axis=-1).astype(BF16)
    qn_l = jnp.concatenate([q_norm[:, :HD], _rope_cols(q_norm[:, HD:])], axis=-1) * Q_SCALE
    kn_l = jnp.concatenate([k_norm[:, :HD], _rope_cols(k_norm[:, HD:])], axis=-1)
    wo_l = w_out.astype(BF16)
    wu_l = w_mlp_up.astype(BF16)
    wd_l = w_mlp_down.astype(BF16)
    pw_l = pool_w.astype(BF16)

    h = x.reshape(nbatch * seq, D_MODEL)
    meta = meta_tokens.astype(F32)
    for layer in range(depth):
        if layer % 2 == 0:
            e = layer // 2
            zh, q, k, v = _inproj(h, meta, mix_norm[layer][None], w_in_l, mla_q_a_norm[e][None],
                                  mla_kv_a_norm[e][None], wq_l, wkv_l, qn_l[e][None],
                                  kn_l[e][None], cos_t, sin_t, e, nbatch, lp)
            bound = (Q_SCALE * QK_DIM * 1.01) * jnp.max(jnp.abs(q_norm[e])) * jnp.max(jnp.abs(k_norm[e]))
            bound = bound.reshape(1).astype(F32)
            ob = lax.cond(bound[0] <= MAX_FIXED_SHIFT,
                          functools.partial(_attn, nbatch=nbatch, lp=lp, fixed_shift=True),
                          functools.partial(_attn, nbatch=nbatch, lp=lp, fixed_shift=False),
                          q, k, v, bound)
            h = _mix_tail(zh, h, meta, ob, lower[e][None], 1.0 - lower[e][None], hgrn_out_norm[e][None],
                          tri2, wo_l, mlp_norm[layer][None], wu_l, wd_l, e, layer, nbatch, lp)
            meta = None
        else:
            o = layer // 2
            h = _pool_mlp(h, mix_norm[layer][None], pw_l, pool_scale[o][None],
                          mlp_norm[layer][None], wu_l, wd_l, o, layer, nbatch, lp,
                          to_frames=layer == depth - 1)

    return h.reshape(nbatch, seq, D_MODEL)
```

```python
import functools

import numpy as np
import jax
import jax.numpy as jnp
from jax import lax
from jax.experimental import pallas as pl
from jax.experimental.pallas import tpu as pltpu

F32 = jnp.float32
BF16 = jnp.bfloat16

D_MODEL = 1024
D_FF = 4 * D_MODEL
EPS = 1e-6
N_META = 16
CHUNK = 64
HEADS = 4
HD = 128
HW = HEADS * HD
ROPE = 64
QK_DIM = HD + ROPE
QK_PAD = 256
Q_RANK = 256
KV_RANK = 256
ROPE_THETA = 10000.0
POOL_WINDOWS = (2, 4, 8, 16)
POOL_G = D_MODEL // len(POOL_WINDOWS)

LEAD = 1024
PAD = LEAD - N_META
TM = 512
LEAD_TILES = LEAD // TM
TQ = 1024
TK = 1024
SUB = 16
HALO = 32
IN_COLS = 4 * HW + Q_RANK + KV_RANK + HD
Q_SCALE = QK_DIM ** -0.5 * float(np.log2(np.e))
MAX_FIXED_SHIFT = 56.0
MASK_VALUE = -1e30
EXP2_CLAMP = 115.0
TINY = 1e-37
VMEM_LIMIT = 56 * 1024 * 1024


def _rms(x, g):
    return x * lax.rsqrt(jnp.mean(x * x, axis=-1, keepdims=True) + EPS) * g


def _silu(x):
    hx = 0.5 * x
    return hx + hx * jnp.tanh(hx)


def _const_spec(shape, layer=None):
    nd = len(shape)
    if layer is None:
        return pl.BlockSpec(shape, lambda *_: (0,) * nd, pipeline_mode=pl.Buffered(1))
    return pl.BlockSpec((None,) + tuple(shape), lambda *_: (layer,) + (0,) * nd, pipeline_mode=pl.Buffered(1))


def _params(sem):
    return pltpu.CompilerParams(dimension_semantics=sem, vmem_limit_bytes=VMEM_LIMIT)


def _frame_tile(i, tpb):
    return (i // tpb) * (tpb - LEAD_TILES) + jnp.maximum(i % tpb - LEAD_TILES, 0)


def _stream_tile(h_ref, meta_ref, tt):
    lead = jnp.concatenate([jnp.zeros((TM - N_META, D_MODEL), F32), meta_ref[...]], axis=0)
    lead = jnp.where(tt == LEAD_TILES - 1, lead, 0.0)
    return jnp.where(tt >= LEAD_TILES, h_ref[...], lead)


MLP_PARTS = 4
MLP_SLAB = D_FF // MLP_PARTS


def _mlp_up(hn, wu_ref, c):
    a = jnp.dot(hn, wu_ref[:, c * MLP_SLAB:(c + 1) * MLP_SLAB], preferred_element_type=F32)
    a = jnp.maximum(a, 0.0)
    return (a * a).astype(BF16)


def _mlp_down(a, wd_ref, c):
    return jnp.dot(a, wd_ref[c * MLP_SLAB:(c + 1) * MLP_SLAB, :], preferred_element_type=F32)


def _mlp_staggered(hn, wu_ref, wd_ref, between=()):
    a = _mlp_up(hn, wu_ref, 0)
    acc = None
    for c in range(MLP_PARTS):
        a_next = _mlp_up(hn, wu_ref, c + 1) if c + 1 < MLP_PARTS else None
        d = _mlp_down(a, wd_ref, c)
        acc = d if acc is None else acc + d
        if c < len(between):
            between[c]()
        a = a_next
    return acc


def _rope(x, c, s):
    return x * c + pltpu.roll(x, 64, axis=1) * s


def _inproj_body(tpb, ntiles, first_layer, *refs):
    if first_layer:
        meta_ref, refs = refs[0], refs[1:]
    (h_ref, g_ref, win_ref, qag_ref, kvag_ref, wq_ref, wkv_ref, qn_ref, kn_ref, cos_ref, sin_ref,
     zh_ref, q_ref, k_ref, v_ref, lat_ref) = refs
    i = pl.program_id(0)

    @pl.when(i == 0)
    def _():
        lat_ref[1] = jnp.zeros(lat_ref.shape[1:], F32)

    lat = lat_ref[(i + 1) % 2]
    q = jnp.dot(_rms(lat[:, :Q_RANK], qag_ref[...]).astype(BF16), wq_ref[...], preferred_element_type=F32)
    kv = jnp.dot(_rms(lat[:, Q_RANK:Q_RANK + KV_RANK], kvag_ref[...]).astype(BF16), wkv_ref[...],
                 preferred_element_type=F32)
    kr = lat[:, Q_RANK + KV_RANK:]

    tt = jnp.minimum(i, ntiles - 1) % tpb
    h = _stream_tile(h_ref, meta_ref, tt) if first_layer else h_ref[...]
    z = jnp.dot(_rms(h, g_ref[...]).astype(BF16), win_ref[...], preferred_element_type=F32)
    zh_ref[...] = z[:, :4 * HW]
    lat_ref[i % 2] = z[:, 4 * HW:]

    v_ref[...] = kv[:, HW:].astype(BF16)
    c = cos_ref[...]
    s = sin_ref[...]
    qg = qn_ref[...]
    kg = kn_ref[...]
    kr_ss = jnp.sum(kr * kr, axis=-1, keepdims=True)
    for hd in range(HEADS):
        qa = q[:, hd * QK_PAD:hd * QK_PAD + HD]
        qb = q[:, hd * QK_PAD + HD:(hd + 1) * QK_PAD]
        ss = jnp.sum(qa * qa + qb * qb, axis=-1, keepdims=True)
        inv = lax.rsqrt(ss * (1.0 / QK_DIM) + EPS)
        q_ref[:, hd * QK_PAD:hd * QK_PAD + HD] = (qa * inv * qg[:, :HD]).astype(BF16)
        q_ref[:, hd * QK_PAD + HD:(hd + 1) * QK_PAD] = _rope(qb * inv * qg[:, HD:], c, s).astype(BF16)
        ka = kv[:, hd * HD:(hd + 1) * HD]
        ss = jnp.sum(ka * ka, axis=-1, keepdims=True) + kr_ss
        inv = lax.rsqrt(ss * (1.0 / QK_DIM) + EPS)
        k_ref[:, hd * QK_PAD:hd * QK_PAD + HD] = (ka * inv * kg[:, :HD]).astype(BF16)
        k_ref[:, hd * QK_PAD + HD:(hd + 1) * QK_PAD] = _rope(kr * inv * kg[:, HD:], c, s).astype(BF16)


def _inproj(h, meta, g, win, qag, kvag, wq, wkv, qn, kn, cos_t, sin_t, e, nbatch, lp):
    r = nbatch * lp
    tpb = lp // TM
    ntiles = r // TM
    first_layer = meta is not None
    cur = lambda i: jnp.minimum(i, ntiles - 1)
    prev = lambda i: jnp.maximum(i - 1, 0)
    row = lambda i: (prev(i), 0)
    tab = lambda i: (prev(i) % tpb, 0)
    h_spec = pl.BlockSpec((TM, D_MODEL),
                          (lambda i: (_frame_tile(cur(i), tpb), 0)) if first_layer else (lambda i: (cur(i), 0)))
    lead_specs = [_const_spec((N_META, D_MODEL))] if first_layer else []
    lead_args = [meta] if first_layer else []
    return pl.pallas_call(
        functools.partial(_inproj_body, tpb, ntiles, first_layer),
        grid=(ntiles + 1,),
        in_specs=lead_specs + [
            h_spec,
            _const_spec((1, D_MODEL)),
            _const_spec((D_MODEL, IN_COLS), e),
            _const_spec((1, Q_RANK)),
            _const_spec((1, KV_RANK)),
            _const_spec((Q_RANK, HEADS * QK_PAD), e),
            _const_spec((KV_RANK, 2 * HW), e),
            _const_spec((1, QK_PAD)),
            _const_spec((1, QK_PAD)),
            pl.BlockSpec((TM, HD), tab),
            pl.BlockSpec((TM, HD), tab),
        ],
        out_specs=[
            pl.BlockSpec((TM, 4 * HW), lambda i: (cur(i), 0)),
            pl.BlockSpec((TM, HEADS * QK_PAD), row),
            pl.BlockSpec((TM, HEADS * QK_PAD), row),
            pl.BlockSpec((TM, HW), row),
        ],
        out_shape=[
            jax.ShapeDtypeStruct((r, 4 * HW), F32),
            jax.ShapeDtypeStruct((r, HEADS * QK_PAD), BF16),
            jax.ShapeDtypeStruct((r, HEADS * QK_PAD), BF16),
            jax.ShapeDtypeStruct((r, HW), BF16),
        ],
        scratch_shapes=[pltpu.VMEM((2, TM, IN_COLS - 4 * HW), F32)],
        compiler_params=_params(("arbitrary",)),
        name="inproj",
    )(*lead_args, h, g, win, qag, kvag, wq, wkv, qn, kn, cos_t, sin_t)


def _group_rows(rows):
    return jnp.concatenate([jnp.broadcast_to(r, (SUB, HD)) for r in rows], axis=0)


def _hgrn_stages(zh_ref, lb_ref, one_m_lb_ref, og_ref, tri_ref, st_ref, keep, o_ref):
    tri2 = tri_ref[...]
    lb = lb_ref[...]
    one_m_lb = one_m_lb_ref[...]
    og = og_ref[...]
    tt = lax.broadcasted_iota(jnp.int32, (CHUNK, CHUNK), 0)
    ss_ = lax.broadcasted_iota(jnp.int32, (CHUNK, CHUNK), 1)
    causal = ss_ <= tt
    nsub = CHUNK // SUB
    zero_row = jnp.zeros((1, HD), F32)
    zero_sub = jnp.zeros((SUB, HD), BF16)

    heads = range(HEADS)
    sls = [slice(hd * HD, (hd + 1) * HD) for hd in heads]

    chunk_rows = [slice(c * CHUNK, (c + 1) * CHUNK) for c in range(TM // CHUNK)]

    def gates_and_decay(rows):
        hq = zh_ref[rows, 0:HW]
        hf = zh_ref[rows, HW:2 * HW]
        hi = zh_ref[rows, 2 * HW:3 * HW]
        hg = zh_ref[rows, 3 * HW:4 * HW]
        q = _silu(hq)
        gate = _silu(hg)
        t = jnp.exp(-jnp.abs(hf))
        r = 1.0 / (1.0 + t)
        tr = t * r
        pos = hf >= 0.0
        log2f = jnp.log2(jnp.maximum(lb + one_m_lb * jnp.where(pos, r, tr), TINY))
        k = one_m_lb * jnp.where(pos, tr, r)
        g1 = log2f.astype(BF16)
        g2 = (log2f - g1.astype(F32)).astype(BF16)
        b = jnp.dot(tri2, jnp.concatenate([g1, g2], axis=0), preferred_element_type=F32)
        return q, k, b, hi.astype(BF16), gate

    def intra_chunk(q, k, b, vb, gate):
        att, q_in, k_out, decay = [], [], [], []
        for hd in heads:
            bh = b[:, sls[hd]]
            b_last = bh[CHUNK - 1:CHUNK, :]
            refs = [zero_row] + [bh[i * SUB - 1:i * SUB, :] for i in range(1, nsub)]
            dq = bh - _group_rows(refs)
            qe = q[:, sls[hd]] * jnp.exp2(dq)
            ke = k[:, sls[hd]] * jnp.exp2(jnp.minimum(-dq, EXP2_CLAMP))
            keb = ke.astype(BF16)
            qcat = []
            kcat = []
            for j in range(nsub):
                qcat.append(jnp.concatenate(
                    [zero_sub if i < j else
                     qe[i * SUB:(i + 1) * SUB].astype(BF16) if i == j else
                     (qe[i * SUB:(i + 1) * SUB] * jnp.exp2(refs[i] - refs[j])).astype(BF16)
                     for i in range(nsub)], axis=0))
                kcat.append(jnp.concatenate(
                    [keb[j * SUB:(j + 1) * SUB] if i == j else zero_sub for i in range(nsub)], axis=0))
            q_in.append(qcat[0])
            att.append(lax.dot_general(jnp.concatenate(qcat, axis=1), jnp.concatenate(kcat, axis=1),
                                       (((1,), (1,)), ((), ())), preferred_element_type=F32))
            to_end = [jnp.exp2(b_last - refs[i]) for i in range(nsub)]
            k_out.append((ke * _group_rows(to_end)).astype(BF16))
            decay.append(to_end[0])
        upd = [lax.dot_general(vb[:, sls[hd]], k_out[hd], (((0,), (0,)), ((), ())),
                               preferred_element_type=F32) for hd in heads]
        return att, q_in, upd, decay, vb, gate

    def stage1():
        return [gates_and_decay(rows) for rows in chunk_rows]

    def stage2(s1):
        return [intra_chunk(*c) for c in s1]

    def stage3(s2):
        st = [st_ref[hd] * keep for hd in heads]
        o = []
        for att, q_in, upd, decay, vb, gate in s2:
            o.append([lax.dot_general(q_in[hd], st[hd].astype(BF16), (((1,), (1,)), ((), ())),
                                      preferred_element_type=F32) for hd in heads])
            st = [decay[hd] * st[hd] + upd[hd] for hd in heads]
        for hd in heads:
            st_ref[hd] = st[hd]
        return o

    def stage4(s2, o):
        for rows, oc, (att, q_in, upd, decay, vb, gate) in zip(chunk_rows, o, s2):
            for hd in heads:
                a = jnp.where(causal, att[hd], 0.0).astype(BF16)
                oh = oc[hd] + jnp.dot(a, vb[:, sls[hd]], preferred_element_type=F32)
                on = _rms(oh, og) * gate[:, sls[hd]]
                o_ref[rows, sls[hd]] = on.astype(BF16)

    return stage1, stage2, stage3, stage4


META_STEP, FULL_STEP, DIAG_STEP = range(3)


def _attn_steps(lp):
    kind, qi, kj, fin = [], [], [], []
    for i in range(lp // TQ):
        for j in range(i + 1):
            kind.append(META_STEP if j == 0 else DIAG_STEP if j == i else FULL_STEP)
            qi.append(i)
            kj.append(j)
            fin.append(int(j == i))
    return [np.asarray(a, np.int32) for a in (kind, qi, kj, fin)]


def _attn_body(fixed_shift, kind_ref, qi_ref, kj_ref, fin_ref, bound_ref, q_ref, k_ref, v_ref, o_ref,
               *scratch):
    if fixed_shift:
        l_ref, acc_ref = scratch
    else:
        m_ref, l_ref, acc_ref = scratch
    step = pl.program_id(1)
    kind = kind_ref[step]
    qi = qi_ref[step]
    kj = kj_ref[step]
    bound = bound_ref[0]

    def update(rows, keys, mask, first):
        for hd in range(HEADS):
            hq = slice(hd * QK_PAD, (hd + 1) * QK_PAD)
            hv = slice(hd * HD, (hd + 1) * HD)
            s = lax.dot_general(q_ref[rows, hq], k_ref[keys, hq], (((1,), (1,)), ((), ())),
                                preferred_element_type=F32)
            if fixed_shift:
                p = jnp.exp2(s - bound)
                if mask is not None:
                    p = jnp.where(mask, p, 0.0)
            else:
                if mask is not None:
                    s = jnp.where(mask, s, MASK_VALUE)
                m_prev = m_ref[hd, rows, :]
                m_new = jnp.maximum(m_prev, jnp.max(s, axis=-1, keepdims=True))
                alpha = jnp.exp2(m_prev - m_new)
                m_ref[hd, rows, :] = m_new
                p = jnp.exp2(s - jnp.tile(m_new, (1, s.shape[1] // HD)))
            part = p[:, 0:HD]
            for c in range(1, p.shape[1] // HD):
                part = part + p[:, c * HD:(c + 1) * HD]
            pv = jnp.dot(p.astype(BF16), v_ref[keys, hv], preferred_element_type=F32)
            if first and fixed_shift:
                l_ref[hd, rows, :] = part
                acc_ref[rows, hv] = pv
            elif fixed_shift:
                l_ref[hd, rows, :] += part
                acc_ref[rows, hv] += pv
            else:
                l_ref[hd, rows, :] = alpha * l_ref[hd, rows, :] + part
                acc_ref[rows, hv] = alpha * acc_ref[rows, hv] + pv

    all_rows = slice(0, TQ)
    all_keys = slice(0, TK)

    @pl.when(kind == META_STEP)
    def _():
        if not fixed_shift:
            m_ref[...] = jnp.full_like(m_ref, MASK_VALUE)
            l_ref[...] = jnp.zeros_like(l_ref)
            acc_ref[...] = jnp.zeros_like(acc_ref)
        col = lax.broadcasted_iota(jnp.int32, (TQ, HD), 1) + (kj * TK + TK - HD)
        update(all_rows, slice(TK - HD, TK), col >= PAD, True)

    @pl.when(kind == FULL_STEP)
    def _():
        update(all_rows, all_keys, None, False)

    @pl.when(kind == DIAG_STEP)
    def _():
        half = TQ // 2
        row = lax.broadcasted_iota(jnp.int32, (TQ, half), 0)
        col = lax.broadcasted_iota(jnp.int32, (TQ, half), 1)
        update(all_rows, slice(0, half), col // CHUNK <= row // CHUNK, False)
        update(slice(half, TQ), slice(half, TK), (col // CHUNK <= row // CHUNK)[:half], False)

    @pl.when(fin_ref[step] == 1)
    def _():
        row = lax.broadcasted_iota(jnp.int32, (TQ, HD), 0) + qi * TQ
        valid = row >= PAD
        for hd in range(HEADS):
            hv = slice(hd * HD, (hd + 1) * HD)
            o = acc_ref[:, hv] / jnp.sum(l_ref[hd], axis=-1, keepdims=True)
            o_ref[:, hv] = jnp.where(valid, o, 0.0).astype(BF16)


def _attn(q, k, v, bound, nbatch, lp, fixed_shift):
    r = q.shape[0]
    nq, nk = lp // TQ, lp // TK
    tables = _attn_steps(lp)
    stats = [pltpu.VMEM((HEADS, TQ, HD), F32)] * (1 if fixed_shift else 2)
    qmap = lambda b, s, kind, qi, kj, fin, bd: (b * nq + qi[s], 0)
    kmap = lambda b, s, kind, qi, kj, fin, bd: (b * nk + kj[s], 0)
    grid_spec = pltpu.PrefetchScalarGridSpec(
        num_scalar_prefetch=5,
        grid=(nbatch, len(tables[0])),
        in_specs=[
            pl.BlockSpec((TQ, HEADS * QK_PAD), qmap),
            pl.BlockSpec((TK, HEADS * QK_PAD), kmap),
            pl.BlockSpec((TK, HW), kmap),
        ],
        out_specs=pl.BlockSpec((TQ, HW), qmap),
        scratch_shapes=stats + [pltpu.VMEM((TQ, HW), F32)],
    )
    return pl.pallas_call(
        functools.partial(_attn_body, fixed_shift),
        grid_spec=grid_spec,
        out_shape=jax.ShapeDtypeStruct((r, HW), BF16),
        compiler_params=_params(("parallel", "arbitrary")),
        name="attn_fixed_shift" if fixed_shift else "attn_online",
    )(*[jnp.asarray(t) for t in tables], bound, q, k, v)


def _mix_tail_body(tpb, ntiles, first_layer, *refs):
    if first_layer:
        meta_ref, refs = refs[0], refs[1:]
    (zh_ref, h_ref, ob_ref, lb_ref, one_m_lb_ref, og_ref, tri_ref, wo_ref, g_ref, wu_ref, wd_ref,
     out_ref, st_ref, oa_ref) = refs
    i = pl.program_id(0)

    @pl.when(i == 0)
    def _():
        st_ref[...] = jnp.zeros_like(st_ref)
        oa_ref[1] = jnp.zeros(oa_ref.shape[1:], BF16)

    keep = (jnp.minimum(i, ntiles - 1) % tpb != 0).astype(F32)
    stage1, stage2, stage3, stage4 = _hgrn_stages(
        zh_ref, lb_ref, one_m_lb_ref, og_ref, tri_ref, st_ref, keep, oa_ref.at[i % 2])

    tt = jnp.maximum(i - 1, 0) % tpb
    h = _stream_tile(h_ref, meta_ref, tt) if first_layer else h_ref[...]
    mix = jnp.dot(oa_ref[(i + 1) % 2], wo_ref[:HW, :], preferred_element_type=F32)
    mix = mix + jnp.dot(ob_ref[...], wo_ref[HW:, :], preferred_element_type=F32)
    s1 = stage1()
    h = h + mix
    hn = _rms(h, g_ref[...]).astype(BF16)
    carried = {}

    def run2():
        carried["s2"] = stage2(s1)

    def run3():
        carried["o"] = stage3(carried["s2"])

    def run4():
        stage4(carried["s2"], carried["o"])

    out_ref[...] = h + _mlp_staggered(hn, wu_ref, wd_ref, between=(run2, run3, run4))


def _mix_tail(zh, h, meta, ob, lb, one_m_lb, og, tri2, wo, g, wu, wd, e, layer, nbatch, lp):
    r = nbatch * lp
    tpb = lp // TM
    ntiles = r // TM
    first_layer = meta is not None
    cur = lambda i: jnp.minimum(i, ntiles - 1)
    prev = lambda i: jnp.maximum(i - 1, 0)
    row = lambda i: (prev(i), 0)
    h_spec = pl.BlockSpec((TM, D_MODEL),
                          (lambda i: (_frame_tile(prev(i), tpb), 0)) if first_layer else row)
    lead_specs = [_const_spec((N_META, D_MODEL))] if first_layer else []
    lead_args = [meta] if first_layer else []
    return pl.pallas_call(
        functools.partial(_mix_tail_body, tpb, ntiles, first_layer),
        grid=(ntiles + 1,),
        in_specs=lead_specs + [
            pl.BlockSpec((TM, 4 * HW), lambda i: (cur(i), 0)),
            h_spec,
            pl.BlockSpec((TM, HW), row),
            _const_spec((1, HW)),
            _const_spec((1, HW)),
            _const_spec((1, HD)),
            _const_spec((CHUNK, 2 * CHUNK)),
            _const_spec((2 * HW, D_MODEL), e),
            _const_spec((1, D_MODEL)),
            _const_spec((D_MODEL, D_FF), layer),
            _const_spec((D_FF, D_MODEL), layer),
        ],
        out_specs=pl.BlockSpec((TM, D_MODEL), row),
        out_shape=jax.ShapeDtypeStruct((r, D_MODEL), F32),
        scratch_shapes=[pltpu.VMEM((HEADS, HD, HD), F32), pltpu.VMEM((2, TM, HW), BF16)],
        compiler_params=_params(("arbitrary",)),
        name="mix_tail",
    )(*lead_args, zh, h, ob, lb, one_m_lb, og, tri2, wo, g, wu, wd)


def _pool_mlp_body(tpb, ntiles, h_ref, halo_ref, gm_ref, pw_ref, ps_ref, g_ref, wu_ref, wd_ref,
                   out_ref, u_ref, a_ref, b_ref, hm_ref, hn_ref):
    i = pl.program_id(0)

    @pl.when(i == 0)
    def _():
        hm_ref[1] = jnp.zeros(hm_ref.shape[1:], F32)
        hn_ref[1] = jnp.zeros(hn_ref.shape[1:], BF16)

    def pool(slot):
        tt = jnp.minimum(i, ntiles - 1) % tpb
        h = h_ref[...]
        gm = gm_ref[...]
        keep = (tt != 0).astype(F32)
        u_ref[0:HALO, :] = _rms(halo_ref[...], gm) * keep
        u = _rms(h, gm)
        u_ref[HALO:, :] = u
        n = TM + HALO
        g = POOL_G
        a_ref[8:n, :] = u_ref[8:n, :] + u_ref[7:n - 1, :]
        b_ref[16:n, g:] = a_ref[16:n, g:] + a_ref[14:n - 2, g:]
        a_ref[24:n, 2 * g:] = b_ref[24:n, 2 * g:] + b_ref[20:n - 4, 2 * g:]
        b_ref[32:n, 3 * g:] = a_ref[32:n, 3 * g:] + a_ref[24:n - 8, 3 * g:]
        wins = (a_ref[HALO:, 0:g], b_ref[HALO:, g:2 * g], a_ref[HALO:, 2 * g:3 * g], b_ref[HALO:, 3 * g:])
        pos = lax.broadcasted_iota(jnp.int32, (TM, g), 0) + (tt * TM - PAD)
        cnt = jnp.maximum(pos + 1, 1).astype(F32)
        ps = ps_ref[...]
        ys = []
        for gi, w in enumerate(POOL_WINDOWS):
            d = wins[gi] / jnp.minimum(cnt, float(w)) - u[:, gi * g:(gi + 1) * g]
            y = jnp.dot(d.astype(BF16), pw_ref[gi], preferred_element_type=F32)
            ys.append(y * ps[:, gi * g:(gi + 1) * g])
        hm = h + jnp.concatenate(ys, axis=1)
        hm_ref[slot] = hm
        hn_ref[slot] = _rms(hm, g_ref[...]).astype(BF16)

    for slot in range(2):
        @pl.when(i % 2 == slot)
        def _():
            acc = _mlp_staggered(hn_ref[1 - slot], wu_ref, wd_ref, between=(functools.partial(pool, slot),))
            out_ref[...] = hm_ref[1 - slot] + acc


def _pool_mlp(h, gm, pw, ps, g, wu, wd, o, layer, nbatch, lp, to_frames):
    r = nbatch * lp
    tpb = lp // TM
    ntiles = r // TM
    cur = lambda i: jnp.minimum(i, ntiles - 1)
    prev = lambda i: jnp.maximum(i - 1, 0)
    out_rows = nbatch * (lp - LEAD) if to_frames else r
    return pl.pallas_call(
        functools.partial(_pool_mlp_body, tpb, ntiles),
        grid=(ntiles + 1,),
        in_specs=[
            pl.BlockSpec((TM, D_MODEL), lambda i: (cur(i), 0)),
            pl.BlockSpec((HALO, D_MODEL), lambda i: (jnp.maximum(cur(i) * (TM // HALO) - 1, 0), 0)),
            _const_spec((1, D_MODEL)),
            _const_spec((len(POOL_WINDOWS), POOL_G, POOL_G), o),
            _const_spec((1, D_MODEL)),
            _const_spec((1, D_MODEL)),
            _const_spec((D_MODEL, D_FF), layer),
            _const_spec((D_FF, D_MODEL), layer),
        ],
        out_specs=pl.BlockSpec((TM, D_MODEL), (lambda i: (_frame_tile(prev(i), tpb), 0)) if to_frames
                               else (lambda i: (prev(i), 0))),
        out_shape=jax.ShapeDtypeStruct((out_rows, D_MODEL), F32),
        scratch_shapes=[pltpu.VMEM((TM + HALO, D_MODEL), F32)] * 3 + [
            pltpu.VMEM((2, TM, D_MODEL), F32), pltpu.VMEM((2, TM, D_MODEL), BF16)],
        compiler_params=_params(("arbitrary",)),
        name="pool_mlp",
    )(h, h, gm, pw, ps, g, wu, wd)


def _rope_cols(w):
    z = jnp.zeros(w.shape[:-1] + (32,), w.dtype)
    return jnp.concatenate([w[..., :32], z, w[..., 32:], z], axis=-1)


def _qk_cols(w):
    w = w.reshape(w.shape[:-1] + (HEADS, QK_DIM))
    w = jnp.concatenate([w[..., :HD], _rope_cols(w[..., HD:])], axis=-1)
    return w.reshape(w.shape[:-2] + (HEADS * QK_PAD,))


def _rope_tables(lp):
    half = ROPE // 2
    inv = ROPE_THETA ** (-np.arange(half, dtype=np.float64) / half)
    pos = np.maximum(np.arange(lp, dtype=np.float64) - PAD, 0.0)
    ang = pos[:, None] * inv[None, :]
    c = jnp.asarray(np.cos(ang).astype(np.float32))
    s = jnp.asarray(np.sin(ang).astype(np.float32))
    z = jnp.zeros_like(c)
    return (jnp.concatenate([c, z, c, z], axis=1), jnp.concatenate([-s, z, s, z], axis=1))


def kernel(x, meta_tokens, mix_norm, mlp_norm, w_mlp_up, w_mlp_down, w_in, hgrn_lb, hgrn_out_norm, mla_q_a_norm, mla_kv_a_norm, w_q_up, w_kv_up, q_norm, k_norm, w_out, pool_w, pool_scale):
    nbatch, seq, _ = x.shape
    depth = mix_norm.shape[0]
    assert seq % TQ == 0 and depth % 2 == 0
    lp = seq + LEAD

    cos_t, sin_t = _rope_tables(lp)
    lb_cum = jnp.cumsum(jax.nn.softmax(hgrn_lb.astype(F32), axis=0), axis=0)
    lower = lb_cum - lb_cum[0:1]
    tri = jnp.tril(jnp.ones((CHUNK, CHUNK), F32)).astype(BF16)
    tri2 = jnp.concatenate([tri, tri], axis=1)

    w_in_l = jnp.concatenate(
        [w_in[..., :4 * HW + Q_RANK + KV_RANK], _rope_cols(w_in[..., 4 * HW + Q_RANK + KV_RANK:])],
        axis=-1).astype(BF16)
    wq_l = _qk_cols(w_q_up).astype(BF16)
    wkv = w_kv_up.reshape(w_kv_up.shape[0], KV_RANK, HEADS, 2 * HD)
    wkv_l = jnp.concatenate([wkv[..., :HD].reshape(-1, KV_RANK, HW),
                             wkv[..., HD:].reshape(-1, KV_RANK, HW)], axis=-1).astype(BF16)
    qn_l = jnp.concatenate([q_norm[:, :HD], _rope_cols(q_norm[:, HD:])], axis=-1) * Q_SCALE
    kn_l = jnp.concatenate([k_norm[:, :HD], _rope_cols(k_norm[:, HD:])], axis=-1)
    wo_l = w_out.astype(BF16)
    wu_l = w_mlp_up.astype(BF16)
    wd_l = w_mlp_down.astype(BF16)
    pw_l = pool_w.astype(BF16)

    h = x.reshape(nbatch * seq, D_MODEL)
    meta = meta_tokens.astype(F32)
    for layer in range(depth):
        if layer % 2 == 0:
            e = layer // 2
            zh, q, k, v = _inproj(h, meta, mix_norm[layer][None], w_in_l, mla_q_a_norm[e][None],
                                  mla_kv_a_norm[e][None], wq_l, wkv_l, qn_l[e][None],
                                  kn_l[e][None], cos_t, sin_t, e, nbatch, lp)
            bound = (Q_SCALE * QK_DIM * 1.01) * jnp.max(jnp.abs(q_norm[e])) * jnp.max(jnp.abs(k_norm[e]))
            bound = bound.reshape(1).astype(F32)
            ob = lax.cond(bound[0] <= MAX_FIXED_SHIFT,
                          functools.partial(_attn, nbatch=nbatch, lp=lp, fixed_shift=True),
                          functools.partial(_attn, nbatch=nbatch, lp=lp, fixed_shift=False),
                          q, k, v, bound)
            h = _mix_tail(zh, h, meta, ob, lower[e][None], 1.0 - lower[e][None], hgrn_out_norm[e][None],
                          tri2, wo_l, mlp_norm[layer][None], wu_l, wd_l, e, layer, nbatch, lp)
            meta = None
        else:
            o = layer // 2
            h = _pool_mlp(h, mix_norm[layer][None], pw_l, pool_scale[o][None],
                          mlp_norm[layer][None], wu_l, wd_l, o, layer, nbatch, lp,
                          to_frames=layer == depth - 1)

    return h.reshape(nbatch, seq, D_MODEL)
```

```python
import functools

import numpy as np
import jax
import jax.numpy as jnp
from jax import lax
from jax.experimental import pallas as pl
from jax.experimental.pallas import tpu as pltpu

F32 = jnp.float32
BF16 = jnp.bfloat16

D_MODEL = 1024
D_FF = 4 * D_MODEL
EPS = 1e-6
N_META = 16
CHUNK = 64
HEADS = 4
HD = 128
HW = HEADS * HD
ROPE = 64
QK_DIM = HD + ROPE
QK_PAD = 256
Q_RANK = 256
KV_RANK = 256
ROPE_THETA = 10000.0
POOL_WINDOWS = (2, 4, 8, 16)
POOL_G = D_MODEL // len(POOL_WINDOWS)

LEAD = 1024
PAD = LEAD - N_META
TM = 512
LEAD_TILES = LEAD // TM
TQ = 1024
TK = 1024
SUB = 16
HALO = 32
IN_COLS = 4 * HW + Q_RANK + KV_RANK + HD
Q_SCALE = QK_DIM ** -0.5 * float(np.log2(np.e))
MAX_FIXED_SHIFT = 56.0
MASK_VALUE = -1e30
EXP2_CLAMP = 115.0
TINY = 1e-37
VMEM_LIMIT = 56 * 1024 * 1024


def _rms(x, g):
    return x * lax.rsqrt(jnp.mean(x * x, axis=-1, keepdims=True) + EPS) * g


def _silu(x):
    hx = 0.5 * x
    return hx + hx * jnp.tanh(hx)


def _const_spec(shape, layer=None):
    nd = len(shape)
    if layer is None:
        return pl.BlockSpec(shape, lambda *_: (0,) * nd, pipeline_mode=pl.Buffered(1))
    return pl.BlockSpec((None,) + tuple(shape), lambda *_: (layer,) + (0,) * nd, pipeline_mode=pl.Buffered(1))


def _params(sem):
    return pltpu.CompilerParams(dimension_semantics=sem, vmem_limit_bytes=VMEM_LIMIT)


def _frame_tile(i, tpb):
    return (i // tpb) * (tpb - LEAD_TILES) + jnp.maximum(i % tpb - LEAD_TILES, 0)


def _stream_tile(h_ref, meta_ref, tt):
    lead = jnp.concatenate([jnp.zeros((TM - N_META, D_MODEL), F32), meta_ref[...]], axis=0)
    lead = jnp.where(tt == LEAD_TILES - 1, lead, 0.0)
    return jnp.where(tt >= LEAD_TILES, h_ref[...], lead)


MLP_PARTS = 4
MLP_SLAB = D_FF // MLP_PARTS


def _mlp_up(hn, wu_ref, c):
    a = jnp.dot(hn, wu_ref[:, c * MLP_SLAB:(c + 1) * MLP_SLAB], preferred_element_type=F32)
    a = jnp.maximum(a, 0.0)
    return (a * a).astype(BF16)


def _mlp_down(a, wd_ref, c):
    return jnp.dot(a, wd_ref[c * MLP_SLAB:(c + 1) * MLP_SLAB, :], preferred_element_type=F32)


def _mlp_staggered(hn, wu_ref, wd_ref, between=()):
    a = _mlp_up(hn, wu_ref, 0)
    acc = None
    for c in range(MLP_PARTS):
        a_next = _mlp_up(hn, wu_ref, c + 1) if c + 1 < MLP_PARTS else None
        d = _mlp_down(a, wd_ref, c)
        acc = d if acc is None else acc + d
        if c < len(between):
            between[c]()
        a = a_next
    return acc


def _rope(x, c, s):
    return x * c + pltpu.roll(x, 64, axis=1) * s


def _inproj_body(tpb, ntiles, first_layer, *refs):
    if first_layer:
        meta_ref, refs = refs[0], refs[1:]
    (h_ref, g_ref, win_ref, qag_ref, kvag_ref, wq_ref, wkv_ref, qn_ref, kn_ref, cos_ref, sin_ref,
     zh_ref, q_ref, k_ref, vt_ref, lat_ref) = refs
    i = pl.program_id(0)

    @pl.when(i == 0)
    def _():
        lat_ref[1] = jnp.zeros(lat_ref.shape[1:], F32)

    lat = lat_ref[(i + 1) % 2]
    q = jnp.dot(_rms(lat[:, :Q_RANK], qag_ref[...]).astype(BF16), wq_ref[...], preferred_element_type=F32)
    kv = jnp.dot(_rms(lat[:, Q_RANK:Q_RANK + KV_RANK], kvag_ref[...]).astype(BF16), wkv_ref[...],
                 preferred_element_type=F32)
    kr = lat[:, Q_RANK + KV_RANK:]

    tt = jnp.minimum(i, ntiles - 1) % tpb
    h = _stream_tile(h_ref, meta_ref, tt) if first_layer else h_ref[...]
    z = jnp.dot(_rms(h, g_ref[...]).astype(BF16), win_ref[...], preferred_element_type=F32)
    zh_ref[...] = z[:, :4 * HW]
    lat_ref[i % 2] = z[:, 4 * HW:]

    vt_ref[...] = kv[:, HW:].T.astype(BF16)
    c = cos_ref[...]
    s = sin_ref[...]
    qg = qn_ref[...]
    kg = kn_ref[...]
    kr_ss = jnp.sum(kr * kr, axis=-1, keepdims=True)
    for hd in range(HEADS):
        qa = q[:, hd * QK_PAD:hd * QK_PAD + HD]
        qb = q[:, hd * QK_PAD + HD:(hd + 1) * QK_PAD]
        ss = jnp.sum(qa * qa + qb * qb, axis=-1, keepdims=True)
        inv = lax.rsqrt(ss * (1.0 / QK_DIM) + EPS)
        q_ref[:, hd * QK_PAD:hd * QK_PAD + HD] = (qa * inv * qg[:, :HD]).astype(BF16)
        q_ref[:, hd * QK_PAD + HD:(hd + 1) * QK_PAD] = _rope(qb * inv * qg[:, HD:], c, s).astype(BF16)
        ka = kv[:, hd * HD:(hd + 1) * HD]
        ss = jnp.sum(ka * ka, axis=-1, keepdims=True) + kr_ss
        inv = lax.rsqrt(ss * (1.0 / QK_DIM) + EPS)
        k_ref[:, hd * QK_PAD:hd * QK_PAD + HD] = (ka * inv * kg[:, :HD]).astype(BF16)
        k_ref[:, hd * QK_PAD + HD:(hd + 1) * QK_PAD] = _rope(kr * inv * kg[:, HD:], c, s).astype(BF16)


def _inproj(h, meta, g, win, qag, kvag, wq, wkv, qn, kn, cos_t, sin_t, e, nbatch, lp):
    r = nbatch * lp
    tpb = lp // TM
    ntiles = r // TM
    first_layer = meta is not None
    cur = lambda i: jnp.minimum(i, ntiles - 1)
    prev = lambda i: jnp.maximum(i - 1, 0)
    row = lambda i: (prev(i), 0)
    tab = lambda i: (prev(i) % tpb, 0)
    h_spec = pl.BlockSpec((TM, D_MODEL),
                          (lambda i: (_frame_tile(cur(i), tpb), 0)) if first_layer else (lambda i: (cur(i), 0)))
    lead_specs = [_const_spec((N_META, D_MODEL))] if first_layer else []
    lead_args = [meta] if first_layer else []
    return pl.pallas_call(
        functools.partial(_inproj_body, tpb, ntiles, first_layer),
        grid=(ntiles + 1,),
        in_specs=lead_specs + [
            h_spec,
            _const_spec((1, D_MODEL)),
            _const_spec((D_MODEL, IN_COLS), e),
            _const_spec((1, Q_RANK)),
            _const_spec((1, KV_RANK)),
            _const_spec((Q_RANK, HEADS * QK_PAD), e),
            _const_spec((KV_RANK, 2 * HW), e),
            _const_spec((1, QK_PAD)),
            _const_spec((1, QK_PAD)),
            pl.BlockSpec((TM, HD), tab),
            pl.BlockSpec((TM, HD), tab),
        ],
        out_specs=[
            pl.BlockSpec((TM, 4 * HW), lambda i: (cur(i), 0)),
            pl.BlockSpec((TM, HEADS * QK_PAD), row),
            pl.BlockSpec((TM, HEADS * QK_PAD), row),
            pl.BlockSpec((HW, TM), lambda i: (0, prev(i))),
        ],
        out_shape=[
            jax.ShapeDtypeStruct((r, 4 * HW), F32),
            jax.ShapeDtypeStruct((r, HEADS * QK_PAD), BF16),
            jax.ShapeDtypeStruct((r, HEADS * QK_PAD), BF16),
            jax.ShapeDtypeStruct((HW, r), BF16),
        ],
        scratch_shapes=[pltpu.VMEM((2, TM, IN_COLS - 4 * HW), F32)],
        compiler_params=_params(("arbitrary",)),
        name="inproj",
    )(*lead_args, h, g, win, qag, kvag, wq, wkv, qn, kn, cos_t, sin_t)


def _group_rows(rows):
    return jnp.concatenate([jnp.broadcast_to(r, (SUB, HD)) for r in rows], axis=0)


def _hgrn_stages(zh_ref, lb_ref, one_m_lb_ref, og_ref, tri_ref, st_ref, keep, o_ref):
    tri2 = tri_ref[...]
    lb = lb_ref[...]
    one_m_lb = one_m_lb_ref[...]
    og = og_ref[...]
    tt = lax.broadcasted_iota(jnp.int32, (CHUNK, CHUNK), 0)
    ss_ = lax.broadcasted_iota(jnp.int32, (CHUNK, CHUNK), 1)
    causal = ss_ <= tt
    nsub = CHUNK // SUB
    zero_row = jnp.zeros((1, HD), F32)
    zero_sub = jnp.zeros((SUB, HD), BF16)

    heads = range(HEADS)
    sls = [slice(hd * HD, (hd + 1) * HD) for hd in heads]

    chunk_rows = [slice(c * CHUNK, (c + 1) * CHUNK) for c in range(TM // CHUNK)]

    def gates_and_decay(rows):
        hq = zh_ref[rows, 0:HW]
        hf = zh_ref[rows, HW:2 * HW]
        hi = zh_ref[rows, 2 * HW:3 * HW]
        hg = zh_ref[rows, 3 * HW:4 * HW]
        q = _silu(hq)
        gate = _silu(hg)
        t = jnp.exp(-jnp.abs(hf))
        r = 1.0 / (1.0 + t)
        tr = t * r
        pos = hf >= 0.0
        log2f = jnp.log2(jnp.maximum(lb + one_m_lb * jnp.where(pos, r, tr), TINY))
        k = one_m_lb * jnp.where(pos, tr, r)
        g1 = log2f.astype(BF16)
        g2 = (log2f - g1.astype(F32)).astype(BF16)
        b = jnp.dot(tri2, jnp.concatenate([g1, g2], axis=0), preferred_element_type=F32)
        return q, k, b, hi.astype(BF16), gate

    def intra_chunk(q, k, b, vb, gate):
        att, q_in, k_out, decay = [], [], [], []
        for hd in heads:
            bh = b[:, sls[hd]]
            b_last = bh[CHUNK - 1:CHUNK, :]
            refs = [zero_row] + [bh[i * SUB - 1:i * SUB, :] for i in range(1, nsub)]
            dq = bh - _group_rows(refs)
            qe = q[:, sls[hd]] * jnp.exp2(dq)
            ke = k[:, sls[hd]] * jnp.exp2(jnp.minimum(-dq, EXP2_CLAMP))
            keb = ke.astype(BF16)
            qcat = []
            kcat = []
            for j in range(nsub):
                qcat.append(jnp.concatenate(
                    [zero_sub if i < j else
                     qe[i * SUB:(i + 1) * SUB].astype(BF16) if i == j else
                     (qe[i * SUB:(i + 1) * SUB] * jnp.exp2(refs[i] - refs[j])).astype(BF16)
                     for i in range(nsub)], axis=0))
                kcat.append(jnp.concatenate(
                    [keb[j * SUB:(j + 1) * SUB] if i == j else zero_sub for i in range(nsub)], axis=0))
            q_in.append(qcat[0])
            att.append(lax.dot_general(jnp.concatenate(qcat, axis=1), jnp.concatenate(kcat, axis=1),
                                       (((1,), (1,)), ((), ())), preferred_element_type=F32))
            to_end = [jnp.exp2(b_last - refs[i]) for i in range(nsub)]
            k_out.append((ke * _group_rows(to_end)).astype(BF16))
            decay.append(to_end[0])
        upd = [lax.dot_general(vb[:, sls[hd]], k_out[hd], (((0,), (0,)), ((), ())),
                               preferred_element_type=F32) for hd in heads]
        return att, q_in, upd, decay, vb, gate

    def stage1():
        return [gates_and_decay(rows) for rows in chunk_rows]

    def stage2(s1):
        return [intra_chunk(*c) for c in s1]

    def stage3(s2):
        st = [st_ref[hd] * keep for hd in heads]
        o = []
        for att, q_in, upd, decay, vb, gate in s2:
            o.append([lax.dot_general(q_in[hd], st[hd].astype(BF16), (((1,), (1,)), ((), ())),
                                      preferred_element_type=F32) for hd in heads])
            st = [decay[hd] * st[hd] + upd[hd] for hd in heads]
        for hd in heads:
            st_ref[hd] = st[hd]
        return o

    def stage4(s2, o):
        for rows, oc, (att, q_in, upd, decay, vb, gate) in zip(chunk_rows, o, s2):
            for hd in heads:
                a = jnp.where(causal, att[hd], 0.0).astype(BF16)
                oh = oc[hd] + jnp.dot(a, vb[:, sls[hd]], preferred_element_type=F32)
                on = _rms(oh, og) * gate[:, sls[hd]]
                o_ref[rows, sls[hd]] = on.astype(BF16)

    return stage1, stage2, stage3, stage4


META_STEP, FULL_STEP, DIAG_STEP = range(3)


def _attn_steps(lp):
    kind, qi, kj, fin = [], [], [], []
    for i in range(lp // TQ):
        for j in range(i + 1):
            kind.append(META_STEP if j == 0 else DIAG_STEP if j == i else FULL_STEP)
            qi.append(i)
            kj.append(j)
            fin.append(int(j == i))
    return [np.asarray(a, np.int32) for a in (kind, qi, kj, fin)]


def _attn_body(fixed_shift, kind_ref, qi_ref, kj_ref, fin_ref, bound_ref, q_ref, k_ref, vt_ref, o_ref,
               *scratch):
    if fixed_shift:
        l_ref, acc_ref = scratch
    else:
        m_ref, l_ref, acc_ref = scratch
    step = pl.program_id(1)
    kind = kind_ref[step]
    qi = qi_ref[step]
    kj = kj_ref[step]
    bound = bound_ref[0]

    def update(qs, keys, mask, first):
        nk, nq = keys.stop - keys.start, qs.stop - qs.start
        for hd in range(HEADS):
            hq = slice(hd * QK_PAD, (hd + 1) * QK_PAD)
            hv = slice(hd * HD, (hd + 1) * HD)
            st = lax.dot_general(k_ref[keys, hq], q_ref[qs, hq], (((1,), (1,)), ((), ())),
                                 preferred_element_type=F32)
            if fixed_shift:
                p = jnp.exp2(st - bound)
                if mask is not None:
                    p = jnp.where(mask, p, 0.0)
            else:
                if mask is not None:
                    st = jnp.where(mask, st, MASK_VALUE)
                m_prev = m_ref[hd, :, qs]
                m_new = jnp.maximum(m_prev, jnp.max(st, axis=0, keepdims=True))
                alpha = jnp.exp2(m_prev - m_new)
                m_ref[hd, :, qs] = m_new
                p = jnp.exp2(st - m_new[0:1, :])
            part = jnp.sum(p.reshape(nk // 8, 8, nq), axis=0)
            pv = jnp.dot(vt_ref[hv, keys], p.astype(BF16), preferred_element_type=F32)
            if first and fixed_shift:
                l_ref[hd, :, qs] = part
                acc_ref[hv, qs] = pv
            elif fixed_shift:
                l_ref[hd, :, qs] += part
                acc_ref[hv, qs] += pv
            else:
                l_ref[hd, :, qs] = alpha * l_ref[hd, :, qs] + part
                acc_ref[hv, qs] = alpha[0:1, :] * acc_ref[hv, qs] + pv

    all_q = slice(0, TQ)
    all_keys = slice(0, TK)

    @pl.when(kind == META_STEP)
    def _():
        if not fixed_shift:
            m_ref[...] = jnp.full_like(m_ref, MASK_VALUE)
            l_ref[...] = jnp.zeros_like(l_ref)
            acc_ref[...] = jnp.zeros_like(acc_ref)
        key = lax.broadcasted_iota(jnp.int32, (HD, TQ), 0) + (kj * TK + TK - HD)
        update(all_q, slice(TK - HD, TK), key >= PAD, True)

    @pl.when(kind == FULL_STEP)
    def _():
        update(all_q, all_keys, None, False)

    @pl.when(kind == DIAG_STEP)
    def _():
        half = TQ // 2
        key = lax.broadcasted_iota(jnp.int32, (half, TQ), 0)
        qry = lax.broadcasted_iota(jnp.int32, (half, TQ), 1)
        update(all_q, slice(0, half), key // CHUNK <= qry // CHUNK, False)
        update(slice(half, TQ), slice(half, TK), (key // CHUNK <= qry // CHUNK)[:, :half], False)

    @pl.when(fin_ref[step] == 1)
    def _():
        row = lax.broadcasted_iota(jnp.int32, (TQ, HD), 0) + qi * TQ
        valid = row >= PAD
        for hd in range(HEADS):
            hv = slice(hd * HD, (hd + 1) * HD)
            ot = acc_ref[hv, :] / jnp.sum(l_ref[hd], axis=0, keepdims=True)
            o_ref[:, hv] = jnp.where(valid, ot.T, 0.0).astype(BF16)


def _attn(q, k, vt, bound, nbatch, lp, fixed_shift):
    r = q.shape[0]
    nq, nk = lp // TQ, lp // TK
    tables = _attn_steps(lp)
    stats = [pltpu.VMEM((HEADS, 8, TQ), F32)] * (1 if fixed_shift else 2)
    qmap = lambda b, s, kind, qi, kj, fin, bd: (b * nq + qi[s], 0)
    kmap = lambda b, s, kind, qi, kj, fin, bd: (b * nk + kj[s], 0)
    grid_spec = pltpu.PrefetchScalarGridSpec(
        num_scalar_prefetch=5,
        grid=(nbatch, len(tables[0])),
        in_specs=[
            pl.BlockSpec((TQ, HEADS * QK_PAD), qmap),
            pl.BlockSpec((TK, HEADS * QK_PAD), kmap),
            pl.BlockSpec((HW, TK), lambda b, s, kind, qi, kj, fin, bd: (0, b * nk + kj[s])),
        ],
        out_specs=pl.BlockSpec((TQ, HW), qmap),
        scratch_shapes=stats + [pltpu.VMEM((HW, TQ), F32)],
    )
    return pl.pallas_call(
        functools.partial(_attn_body, fixed_shift),
        grid_spec=grid_spec,
        out_shape=jax.ShapeDtypeStruct((r, HW), BF16),
        compiler_params=_params(("parallel", "arbitrary")),
        name="attn_fixed_shift" if fixed_shift else "attn_online",
    )(*[jnp.asarray(t) for t in tables], bound, q, k, vt)


def _mix_tail_body(tpb, ntiles, first_layer, *refs):
    if first_layer:
        meta_ref, refs = refs[0], refs[1:]
    (zh_ref, h_ref, ob_ref, lb_ref, one_m_lb_ref, og_ref, tri_ref, wo_ref, g_ref, wu_ref, wd_ref,
     out_ref, st_ref, oa_ref) = refs
    i = pl.program_id(0)

    @pl.when(i == 0)
    def _():
        st_ref[...] = jnp.zeros_like(st_ref)
        oa_ref[1] = jnp.zeros(oa_ref.shape[1:], BF16)

    keep = (jnp.minimum(i, ntiles - 1) % tpb != 0).astype(F32)
    stage1, stage2, stage3, stage4 = _hgrn_stages(
        zh_ref, lb_ref, one_m_lb_ref, og_ref, tri_ref, st_ref, keep, oa_ref.at[i % 2])

    tt = jnp.maximum(i - 1, 0) % tpb
    h = _stream_tile(h_ref, meta_ref, tt) if first_layer else h_ref[...]
    mix = jnp.dot(oa_ref[(i + 1) % 2], wo_ref[:HW, :], preferred_element_type=F32)
    mix = mix + jnp.dot(ob_ref[...], wo_ref[HW:, :], preferred_element_type=F32)
    s1 = stage1()
    h = h + mix
    hn = _rms(h, g_ref[...]).astype(BF16)
    carried = {}

    def run2():
        carried["s2"] = stage2(s1)

    def run3():
        carried["o"] = stage3(carried["s2"])

    def run4():
        stage4(carried["s2"], carried["o"])

    out_ref[...] = h + _mlp_staggered(hn, wu_ref, wd_ref, between=(run2, run3, run4))


def _mix_tail(zh, h, meta, ob, lb, one_m_lb, og, tri2, wo, g, wu, wd, e, layer, nbatch, lp):
    r = nbatch * lp
    tpb = lp // TM
    ntiles = r // TM
    first_layer = meta is not None
    cur = lambda i: jnp.minimum(i, ntiles - 1)
    prev = lambda i: jnp.maximum(i - 1, 0)
    row = lambda i: (prev(i), 0)
    h_spec = pl.BlockSpec((TM, D_MODEL),
                          (lambda i: (_frame_tile(prev(i), tpb), 0)) if first_layer else row)
    lead_specs = [_const_spec((N_META, D_MODEL))] if first_layer else []
    lead_args = [meta] if first_layer else []
    return pl.pallas_call(
        functools.partial(_mix_tail_body, tpb, ntiles, first_layer),
        grid=(ntiles + 1,),
        in_specs=lead_specs + [
            pl.BlockSpec((TM, 4 * HW), lambda i: (cur(i), 0)),
            h_spec,
            pl.BlockSpec((TM, HW), row),
            _const_spec((1, HW)),
            _const_spec((1, HW)),
            _const_spec((1, HD)),
            _const_spec((CHUNK, 2 * CHUNK)),
            _const_spec((2 * HW, D_MODEL), e),
            _const_spec((1, D_MODEL)),
            _const_spec((D_MODEL, D_FF), layer),
            _const_spec((D_FF, D_MODEL), layer),
        ],
        out_specs=pl.BlockSpec((TM, D_MODEL), row),
        out_shape=jax.ShapeDtypeStruct((r, D_MODEL), F32),
        scratch_shapes=[pltpu.VMEM((HEADS, HD, HD), F32), pltpu.VMEM((2, TM, HW), BF16)],
        compiler_params=_params(("arbitrary",)),
        name="mix_tail",
    )(*lead_args, zh, h, ob, lb, one_m_lb, og, tri2, wo, g, wu, wd)


def _pool_mlp_body(tpb, h_ref, halo_ref, gm_ref, pw_ref, ps_ref, g_ref, wu_ref, wd_ref,
                   out_ref, u_ref, a_ref, b_ref):
    tt = pl.program_id(0) % tpb

    @pl.when(tt == 0)
    def _():
        out_ref[...] = jnp.zeros_like(out_ref)

    @pl.when(tt != 0)
    def _():
        h = h_ref[...]
        gm = gm_ref[...]
        u_ref[0:HALO, :] = _rms(halo_ref[...], gm)
        u = _rms(h, gm)
        u_ref[HALO:, :] = u
        n = TM + HALO
        g = POOL_G
        a_ref[8:n, :] = u_ref[8:n, :] + u_ref[7:n - 1, :]
        b_ref[16:n, g:] = a_ref[16:n, g:] + a_ref[14:n - 2, g:]
        a_ref[24:n, 2 * g:] = b_ref[24:n, 2 * g:] + b_ref[20:n - 4, 2 * g:]
        b_ref[32:n, 3 * g:] = a_ref[32:n, 3 * g:] + a_ref[24:n - 8, 3 * g:]
        wins = (a_ref[HALO:, 0:g], b_ref[HALO:, g:2 * g], a_ref[HALO:, 2 * g:3 * g], b_ref[HALO:, 3 * g:])
        pos = lax.broadcasted_iota(jnp.int32, (TM, g), 0) + (tt * TM - PAD)
        cnt = jnp.maximum(pos + 1, 1).astype(F32)
        ps = ps_ref[...]
        ys = []
        for gi, w in enumerate(POOL_WINDOWS):
            d = wins[gi] / jnp.minimum(cnt, float(w)) - u[:, gi * g:(gi + 1) * g]
            y = jnp.dot(d.astype(BF16), pw_ref[gi], preferred_element_type=F32)
            ys.append(y * ps[:, gi * g:(gi + 1) * g])
        h = h + jnp.concatenate(ys, axis=1)
        out_ref[...] = h + _mlp_staggered(_rms(h, g_ref[...]).astype(BF16), wu_ref, wd_ref)


def _pool_mlp(h, gm, pw, ps, g, wu, wd, o, layer, nbatch, lp, to_frames):
    r = nbatch * lp
    tpb = lp // TM
    row = lambda i: (i, 0)
    out_rows = nbatch * (lp - LEAD) if to_frames else r
    return pl.pallas_call(
        functools.partial(_pool_mlp_body, tpb),
        grid=(r // TM,),
        in_specs=[
            pl.BlockSpec((TM, D_MODEL), row),
            pl.BlockSpec((HALO, D_MODEL), lambda i: (jnp.maximum(i * (TM // HALO) - 1, 0), 0)),
            _const_spec((1, D_MODEL)),
            _const_spec((len(POOL_WINDOWS), POOL_G, POOL_G), o),
            _const_spec((1, D_MODEL)),
            _const_spec((1, D_MODEL)),
            _const_spec((D_MODEL, D_FF), layer),
            _const_spec((D_FF, D_MODEL), layer),
        ],
        out_specs=pl.BlockSpec((TM, D_MODEL), (lambda i: (_frame_tile(i, tpb), 0)) if to_frames else row),
        out_shape=jax.ShapeDtypeStruct((out_rows, D_MODEL), F32),
        scratch_shapes=[pltpu.VMEM((TM + HALO, D_MODEL), F32)] * 3,
        compiler_params=_params(("arbitrary",) if to_frames else ("parallel",)),
        name="pool_mlp",
    )(h, h, gm, pw, ps, g, wu, wd)


def _rope_cols(w):
    z = jnp.zeros(w.shape[:-1] + (32,), w.dtype)
    return jnp.concatenate([w[..., :32], z, w[..., 32:], z], axis=-1)


def _qk_cols(w):
    w = w.reshape(w.shape[:-1] + (HEADS, QK_DIM))
    w = jnp.concatenate([w[..., :HD], _rope_cols(w[..., HD:])], axis=-1)
    return w.reshape(w.shape[:-2] + (HEADS * QK_PAD,))


def _rope_tables(lp):
    half = ROPE // 2
    inv = ROPE_THETA ** (-np.arange(half, dtype=np.float64) / half)
    pos = np.maximum(np.arange(lp, dtype=np.float64) - PAD, 0.0)
    ang = pos[:, None] * inv[None, :]
    c = jnp.asarray(np.cos(ang).astype(np.float32))
    s = jnp.asarray(np.sin(ang).astype(np.float32))
    z = jnp.zeros_like(c)
    return (jnp.concatenate([c, z, c, z], axis=1), jnp.concatenate([-s, z, s, z], axis=1))


def kernel(x, meta_tokens, mix_norm, mlp_norm, w_mlp_up, w_mlp_down, w_in, hgrn_lb, hgrn_out_norm, mla_q_a_norm, mla_kv_a_norm, w_q_up, w_kv_up, q_norm, k_norm, w_out, pool_w, pool_scale):
    nbatch, seq, _ = x.shape
    depth = mix_norm.shape[0]
    assert seq % TQ == 0 and depth % 2 == 0
    lp = seq + LEAD

    cos_t, sin_t = _rope_tables(lp)
    lb_cum = jnp.cumsum(jax.nn.softmax(hgrn_lb.astype(F32), axis=0), axis=0)
    lower = lb_cum - lb_cum[0:1]
    tri = jnp.tril(jnp.ones((CHUNK, CHUNK), F32)).astype(BF16)
    tri2 = jnp.concatenate([tri, tri], axis=1)

    w_in_l = jnp.concatenate(
        [w_in[..., :4 * HW + Q_RANK + KV_RANK], _rope_cols(w_in[..., 4 * HW + Q_RANK + KV_RANK:])],
        axis=-1).astype(BF16)
    wq_l = _qk_cols(w_q_up).astype(BF16)
    wkv = w_kv_up.reshape(w_kv_up.shape[0], KV_RANK, HEADS, 2 * HD)
    wkv_l = jnp.concatenate([wkv[..., :HD].reshape(-1, KV_RANK, HW),
                             wkv[..., HD:].reshape(-1, KV_RANK, HW)], axis=-1).astype(BF16)
    qn_l = jnp.concatenate([q_norm[:, :HD], _rope_cols(q_norm[:, HD:])], axis=-1) * Q_SCALE
    kn_l = jnp.concatenate([k_norm[:, :HD], _rope_cols(k_norm[:, HD:])], axis=-1)
    wo_l = w_out.astype(BF16)
    wu_l = w_mlp_up.astype(BF16)
    wd_l = w_mlp_down.astype(BF16)
    pw_l = pool_w.astype(BF16)

    h = x.reshape(nbatch * seq, D_MODEL)
    meta = meta_tokens.astype(F32)
    for layer in range(depth):
        if layer % 2 == 0:
            e = layer // 2
            zh, q, k, vt = _inproj(h, meta, mix_norm[layer][None], w_in_l, mla_q_a_norm[e][None],
                                  mla_kv_a_norm[e][None], wq_l, wkv_l, qn_l[e][None],
                                  kn_l[e][None], cos_t, sin_t, e, nbatch, lp)
            bound = (Q_SCALE * QK_DIM * 1.01) * jnp.max(jnp.abs(q_norm[e])) * jnp.max(jnp.abs(k_norm[e]))
            bound = bound.reshape(1).astype(F32)
            ob = lax.cond(bound[0] <= MAX_FIXED_SHIFT,
                          functools.partial(_attn, nbatch=nbatch, lp=lp, fixed_shift=True),
                          functools.partial(_attn, nbatch=nbatch, lp=lp, fixed_shift=False),
                          q, k, vt, bound)
            h = _mix_tail(zh, h, meta, ob, lower[e][None], 1.0 - lower[e][None], hgrn_out_norm[e][None],
                          tri2, wo_l, mlp_norm[layer][None], wu_l, wd_l, e, layer, nbatch, lp)
            meta = None
        else:
            o = layer // 2
            h = _pool_mlp(h, mix_norm[layer][None], pw_l, pool_scale[o][None],
                          mlp_norm[layer][None], wu_l, wd_l, o, layer, nbatch, lp,
                          to_frames=layer == depth - 1)

    return h.reshape(nbatch, seq, D_MODEL)
```

```python
import functools

import numpy as np
import jax
import jax.numpy as jnp
from jax import lax
from jax.experimental import pallas as pl
from jax.experimental.pallas import tpu as pltpu

F32 = jnp.float32
BF16 = jnp.bfloat16

D_MODEL = 1024
D_FF = 4 * D_MODEL
EPS = 1e-6
N_META = 16
CHUNK = 64
HEADS = 4
HD = 128
HW = HEADS * HD
ROPE = 64
QK_DIM = HD + ROPE
QK_PAD = 256
Q_RANK = 256
KV_RANK = 256
ROPE_THETA = 10000.0
POOL_WINDOWS = (2, 4, 8, 16)
POOL_G = D_MODEL // len(POOL_WINDOWS)

LEAD = 1024
PAD = LEAD - N_META
TM = 512
LEAD_TILES = LEAD // TM
TQ = 1024
TK = 1024
SUB = 16
HALO = 32
IN_COLS = 4 * HW + Q_RANK + KV_RANK + HD
Q_SCALE = QK_DIM ** -0.5 * float(np.log2(np.e))
MAX_FIXED_SHIFT = 56.0
MASK_VALUE = -1e30
EXP2_CLAMP = 115.0
TINY = 1e-37
VMEM_LIMIT = 56 * 1024 * 1024


def _rms(x, g):
    return x * lax.rsqrt(jnp.mean(x * x, axis=-1, keepdims=True) + EPS) * g


def _silu(x):
    hx = 0.5 * x
    return hx + hx * jnp.tanh(hx)


def _const_spec(shape, layer=None):
    nd = len(shape)
    if layer is None:
        return pl.BlockSpec(shape, lambda *_: (0,) * nd, pipeline_mode=pl.Buffered(1))
    return pl.BlockSpec((None,) + tuple(shape), lambda *_: (layer,) + (0,) * nd, pipeline_mode=pl.Buffered(1))


def _params(sem):
    return pltpu.CompilerParams(dimension_semantics=sem, vmem_limit_bytes=VMEM_LIMIT)


def _frame_tile(i, tpb):
    return (i // tpb) * (tpb - LEAD_TILES) + jnp.maximum(i % tpb - LEAD_TILES, 0)


def _stream_tile(h_ref, meta_ref, tt):
    lead = jnp.concatenate([jnp.zeros((TM - N_META, D_MODEL), F32), meta_ref[...]], axis=0)
    lead = jnp.where(tt == LEAD_TILES - 1, lead, 0.0)
    return jnp.where(tt >= LEAD_TILES, h_ref[...], lead)


MLP_PARTS = 4
MLP_SLAB = D_FF // MLP_PARTS


def _mlp_up(hn, wu_ref, c):
    a = jnp.dot(hn, wu_ref[:, c * MLP_SLAB:(c + 1) * MLP_SLAB], preferred_element_type=F32)
    a = jnp.maximum(a, 0.0)
    return (a * a).astype(BF16)


def _mlp_down(a, wd_ref, c):
    return jnp.dot(a, wd_ref[c * MLP_SLAB:(c + 1) * MLP_SLAB, :], preferred_element_type=F32)


def _mlp_staggered(hn, wu_ref, wd_ref, between=()):
    a = _mlp_up(hn, wu_ref, 0)
    acc = None
    for c in range(MLP_PARTS):
        a_next = _mlp_up(hn, wu_ref, c + 1) if c + 1 < MLP_PARTS else None
        d = _mlp_down(a, wd_ref, c)
        acc = d if acc is None else acc + d
        if c < len(between):
            between[c]()
        a = a_next
    return acc


def _rope(x, c, s):
    return x * c + pltpu.roll(x, 64, axis=1) * s


def _inproj_body(tpb, ntiles, first_layer, *refs):
    if first_layer:
        meta_ref, refs = refs[0], refs[1:]
    (h_ref, g_ref, win_ref, qag_ref, kvag_ref, wq_ref, wkv_ref, qn_ref, kn_ref, cos_ref, sin_ref,
     zh_ref, q_ref, k_ref, vt_ref, lat_ref) = refs
    i = pl.program_id(0)

    @pl.when(i == 0)
    def _():
        lat_ref[1] = jnp.zeros(lat_ref.shape[1:], F32)

    lat = lat_ref[(i + 1) % 2]
    q = jnp.dot(_rms(lat[:, :Q_RANK], qag_ref[...]).astype(BF16), wq_ref[...], preferred_element_type=F32)
    kv = jnp.dot(_rms(lat[:, Q_RANK:Q_RANK + KV_RANK], kvag_ref[...]).astype(BF16), wkv_ref[...],
                 preferred_element_type=F32)
    kr = lat[:, Q_RANK + KV_RANK:]

    tt = jnp.minimum(i, ntiles - 1) % tpb
    h = _stream_tile(h_ref, meta_ref, tt) if first_layer else h_ref[...]
    z = jnp.dot(_rms(h, g_ref[...]).astype(BF16), win_ref[...], preferred_element_type=F32)
    zh_ref[...] = z[:, :4 * HW]
    lat_ref[i % 2] = z[:, 4 * HW:]

    vt_ref[...] = kv[:, HW:].T.astype(BF16)
    c = cos_ref[...]
    s = sin_ref[...]
    qg = qn_ref[...]
    kg = kn_ref[...]
    kr_ss = jnp.sum(kr * kr, axis=-1, keepdims=True)
    for hd in range(HEADS):
        qa = q[:, hd * QK_PAD:hd * QK_PAD + HD]
        qb = q[:, hd * QK_PAD + HD:(hd + 1) * QK_PAD]
        ss = jnp.sum(qa * qa + qb * qb, axis=-1, keepdims=True)
        inv = lax.rsqrt(ss * (1.0 / QK_DIM) + EPS)
        q_ref[:, hd * QK_PAD:hd * QK_PAD + HD] = (qa * inv * qg[:, :HD]).astype(BF16)
        q_ref[:, hd * QK_PAD + HD:(hd + 1) * QK_PAD] = _rope(qb * inv * qg[:, HD:], c, s).astype(BF16)
        ka = kv[:, hd * HD:(hd + 1) * HD]
        ss = jnp.sum(ka * ka, axis=-1, keepdims=True) + kr_ss
        inv = lax.rsqrt(ss * (1.0 / QK_DIM) + EPS)
        k_ref[:, hd * QK_PAD:hd * QK_PAD + HD] = (ka * inv * kg[:, :HD]).astype(BF16)
        k_ref[:, hd * QK_PAD + HD:(hd + 1) * QK_PAD] = _rope(kr * inv * kg[:, HD:], c, s).astype(BF16)


def _inproj(h, meta, g, win, qag, kvag, wq, wkv, qn, kn, cos_t, sin_t, e, nbatch, lp):
    r = nbatch * lp
    tpb = lp // TM
    ntiles = r // TM
    first_layer = meta is not None
    cur = lambda i: jnp.minimum(i, ntiles - 1)
    prev = lambda i: jnp.maximum(i - 1, 0)
    row = lambda i: (prev(i), 0)
    tab = lambda i: (prev(i) % tpb, 0)
    h_spec = pl.BlockSpec((TM, D_MODEL),
                          (lambda i: (_frame_tile(cur(i), tpb), 0)) if first_layer else (lambda i: (cur(i), 0)))
    lead_specs = [_const_spec((N_META, D_MODEL))] if first_layer else []
    lead_args = [meta] if first_layer else []
    return pl.pallas_call(
        functools.partial(_inproj_body, tpb, ntiles, first_layer),
        grid=(ntiles + 1,),
        in_specs=lead_specs + [
            h_spec,
            _const_spec((1, D_MODEL)),
            _const_spec((D_MODEL, IN_COLS), e),
            _const_spec((1, Q_RANK)),
            _const_spec((1, KV_RANK)),
            _const_spec((Q_RANK, HEADS * QK_PAD), e),
            _const_spec((KV_RANK, 2 * HW), e),
            _const_spec((1, QK_PAD)),
            _const_spec((1, QK_PAD)),
            pl.BlockSpec((TM, HD), tab),
            pl.BlockSpec((TM, HD), tab),
        ],
        out_specs=[
            pl.BlockSpec((TM, 4 * HW), lambda i: (cur(i), 0)),
            pl.BlockSpec((TM, HEADS * QK_PAD), row),
            pl.BlockSpec((TM, HEADS * QK_PAD), row),
            pl.BlockSpec((HW, TM), lambda i: (0, prev(i))),
        ],
        out_shape=[
            jax.ShapeDtypeStruct((r, 4 * HW), F32),
            jax.ShapeDtypeStruct((r, HEADS * QK_PAD), BF16),
            jax.ShapeDtypeStruct((r, HEADS * QK_PAD), BF16),
            jax.ShapeDtypeStruct((HW, r), BF16),
        ],
        scratch_shapes=[pltpu.VMEM((2, TM, IN_COLS - 4 * HW), F32)],
        compiler_params=_params(("arbitrary",)),
        name="inproj",
    )(*lead_args, h, g, win, qag, kvag, wq, wkv, qn, kn, cos_t, sin_t)


def _group_rows(rows):
    return jnp.concatenate([jnp.broadcast_to(r, (SUB, HD)) for r in rows], axis=0)


def _hgrn_stages(zh_ref, lb_ref, one_m_lb_ref, og_ref, tri_ref, st_ref, keep, o_ref):
    tri2 = tri_ref[...]
    lb = lb_ref[...]
    one_m_lb = one_m_lb_ref[...]
    og = og_ref[...]
    tt = lax.broadcasted_iota(jnp.int32, (CHUNK, CHUNK), 0)
    ss_ = lax.broadcasted_iota(jnp.int32, (CHUNK, CHUNK), 1)
    causal = ss_ <= tt
    nsub = CHUNK // SUB
    zero_row = jnp.zeros((1, HD), F32)
    zero_sub = jnp.zeros((SUB, HD), BF16)

    heads = range(HEADS)
    sls = [slice(hd * HD, (hd + 1) * HD) for hd in heads]

    chunk_rows = [slice(c * CHUNK, (c + 1) * CHUNK) for c in range(TM // CHUNK)]

    def gates_and_decay(rows):
        hq = zh_ref[rows, 0:HW]
        hf = zh_ref[rows, HW:2 * HW]
        hi = zh_ref[rows, 2 * HW:3 * HW]
        hg = zh_ref[rows, 3 * HW:4 * HW]
        q = _silu(hq)
        gate = _silu(hg)
        t = jnp.exp(-jnp.abs(hf))
        r = 1.0 / (1.0 + t)
        tr = t * r
        pos = hf >= 0.0
        log2f = jnp.log2(jnp.maximum(lb + one_m_lb * jnp.where(pos, r, tr), TINY))
        k = one_m_lb * jnp.where(pos, tr, r)
        g1 = log2f.astype(BF16)
        g2 = (log2f - g1.astype(F32)).astype(BF16)
        b = jnp.dot(tri2, jnp.concatenate([g1, g2], axis=0), preferred_element_type=F32)
        return q, k, b, hi.astype(BF16), gate

    def intra_chunk(q, k, b, vb, gate):
        att, q_in, k_out, decay = [], [], [], []
        for hd in heads:
            bh = b[:, sls[hd]]
            b_last = bh[CHUNK - 1:CHUNK, :]
            refs = [zero_row] + [bh[i * SUB - 1:i * SUB, :] for i in range(1, nsub)]
            dq = bh - _group_rows(refs)
            qe = q[:, sls[hd]] * jnp.exp2(dq)
            ke = k[:, sls[hd]] * jnp.exp2(jnp.minimum(-dq, EXP2_CLAMP))
            keb = ke.astype(BF16)
            qcat = []
            kcat = []
            for j in range(nsub):
                qcat.append(jnp.concatenate(
                    [zero_sub if i < j else
                     qe[i * SUB:(i + 1) * SUB].astype(BF16) if i == j else
                     (qe[i * SUB:(i + 1) * SUB] * jnp.exp2(refs[i] - refs[j])).astype(BF16)
                     for i in range(nsub)], axis=0))
                kcat.append(jnp.concatenate(
                    [keb[j * SUB:(j + 1) * SUB] if i == j else zero_sub for i in range(nsub)], axis=0))
            q_in.append(qcat[0])
            att.append(lax.dot_general(jnp.concatenate(qcat, axis=1), jnp.concatenate(kcat, axis=1),
                                       (((1,), (1,)), ((), ())), preferred_element_type=F32))
            to_end = [jnp.exp2(b_last - refs[i]) for i in range(nsub)]
            k_out.append((ke * _group_rows(to_end)).astype(BF16))
            decay.append(to_end[0])
        upd = [lax.dot_general(vb[:, sls[hd]], k_out[hd], (((0,), (0,)), ((), ())),
                               preferred_element_type=F32) for hd in heads]
        return att, q_in, upd, decay, vb, gate

    def stage1():
        return [gates_and_decay(rows) for rows in chunk_rows]

    def stage2(s1):
        return [intra_chunk(*c) for c in s1]

    def stage3(s2):
        st = [st_ref[hd] * keep for hd in heads]
        o = []
        for att, q_in, upd, decay, vb, gate in s2:
            o.append([lax.dot_general(q_in[hd], st[hd].astype(BF16), (((1,), (1,)), ((), ())),
                                      preferred_element_type=F32) for hd in heads])
            st = [decay[hd] * st[hd] + upd[hd] for hd in heads]
        for hd in heads:
            st_ref[hd] = st[hd]
        return o

    def stage4(s2, o):
        for rows, oc, (att, q_in, upd, decay, vb, gate) in zip(chunk_rows, o, s2):
            for hd in heads:
                a = jnp.where(causal, att[hd], 0.0).astype(BF16)
                oh = oc[hd] + jnp.dot(a, vb[:, sls[hd]], preferred_element_type=F32)
                on = _rms(oh, og) * gate[:, sls[hd]]
                o_ref[rows, sls[hd]] = on.astype(BF16)

    return stage1, stage2, stage3, stage4


META_STEP, FULL_STEP, DIAG_STEP, NO_STEP = range(4)


def _attn_steps(lp):
    qi, fin, kind_a, kj_a, kind_b, kj_b = [], [], [], [], [], []
    for i in range(lp // TQ):
        tiles = [(META_STEP if j == 0 else DIAG_STEP if j == i else FULL_STEP, j) for j in range(i + 1)]
        for s in range(0, len(tiles), 2):
            a = tiles[s]
            b = tiles[s + 1] if s + 1 < len(tiles) else (NO_STEP, a[1])
            qi.append(i)
            fin.append(int(s + 2 >= len(tiles)))
            kind_a.append(a[0])
            kj_a.append(a[1])
            kind_b.append(b[0])
            kj_b.append(b[1])
    return [np.asarray(t, np.int32) for t in (qi, fin, kind_a, kj_a, kind_b, kj_b)]


def _attn_body(fixed_shift, qi_ref, fin_ref, kind_a_ref, kj_a_ref, kind_b_ref, kj_b_ref, bound_ref,
               q_ref, ka_ref, vta_ref, kb_ref, vtb_ref, o_ref, *scratch):
    if fixed_shift:
        l_ref, acc_ref = scratch
    else:
        m_ref, l_ref, acc_ref = scratch
    step = pl.program_id(1)
    qi = qi_ref[step]
    bound = bound_ref[0]

    def update(k_ref, vt_ref, qs, keys, mask, first):
        nk, nq = keys.stop - keys.start, qs.stop - qs.start
        for hd in range(HEADS):
            hq = slice(hd * QK_PAD, (hd + 1) * QK_PAD)
            hv = slice(hd * HD, (hd + 1) * HD)
            st = lax.dot_general(k_ref[keys, hq], q_ref[qs, hq], (((1,), (1,)), ((), ())),
                                 preferred_element_type=F32)
            if fixed_shift:
                p = jnp.exp2(st - bound)
                if mask is not None:
                    p = jnp.where(mask, p, 0.0)
            else:
                if mask is not None:
                    st = jnp.where(mask, st, MASK_VALUE)
                m_prev = m_ref[hd, :, qs]
                m_new = jnp.maximum(m_prev, jnp.max(st, axis=0, keepdims=True))
                alpha = jnp.exp2(m_prev - m_new)
                m_ref[hd, :, qs] = m_new
                p = jnp.exp2(st - m_new[0:1, :])
            part = jnp.sum(p.reshape(nk // 8, 8, nq), axis=0)
            pv = jnp.dot(vt_ref[hv, keys], p.astype(BF16), preferred_element_type=F32)
            if first and fixed_shift:
                l_ref[hd, :, qs] = part
                acc_ref[hv, qs] = pv
            elif fixed_shift:
                l_ref[hd, :, qs] += part
                acc_ref[hv, qs] += pv
            else:
                l_ref[hd, :, qs] = alpha * l_ref[hd, :, qs] + part
                acc_ref[hv, qs] = alpha[0:1, :] * acc_ref[hv, qs] + pv

    all_q = slice(0, TQ)
    all_keys = slice(0, TK)

    def key_tile(kind, kj, k_ref, vt_ref):
        @pl.when(kind == META_STEP)
        def _():
            if not fixed_shift:
                m_ref[...] = jnp.full_like(m_ref, MASK_VALUE)
                l_ref[...] = jnp.zeros_like(l_ref)
                acc_ref[...] = jnp.zeros_like(acc_ref)
            key = lax.broadcasted_iota(jnp.int32, (HD, TQ), 0) + (kj * TK + TK - HD)
            update(k_ref, vt_ref, all_q, slice(TK - HD, TK), key >= PAD, True)

        @pl.when(kind == FULL_STEP)
        def _():
            update(k_ref, vt_ref, all_q, all_keys, None, False)

        @pl.when(kind == DIAG_STEP)
        def _():
            half = TQ // 2
            key = lax.broadcasted_iota(jnp.int32, (half, TQ), 0)
            qry = lax.broadcasted_iota(jnp.int32, (half, TQ), 1)
            update(k_ref, vt_ref, all_q, slice(0, half), key // CHUNK <= qry // CHUNK, False)
            update(k_ref, vt_ref, slice(half, TQ), slice(half, TK),
                   (key // CHUNK <= qry // CHUNK)[:, :half], False)

    key_tile(kind_a_ref[step], kj_a_ref[step], ka_ref, vta_ref)
    key_tile(kind_b_ref[step], kj_b_ref[step], kb_ref, vtb_ref)

    @pl.when(fin_ref[step] == 1)
    def _():
        row = lax.broadcasted_iota(jnp.int32, (TQ, HD), 0) + qi * TQ
        valid = row >= PAD
        for hd in range(HEADS):
            hv = slice(hd * HD, (hd + 1) * HD)
            ot = acc_ref[hv, :] / jnp.sum(l_ref[hd], axis=0, keepdims=True)
            o_ref[:, hv] = jnp.where(valid, ot.T, 0.0).astype(BF16)


def _attn(q, k, vt, bound, nbatch, lp, fixed_shift):
    r = q.shape[0]
    nq, nk = lp // TQ, lp // TK
    tables = _attn_steps(lp)
    stats = [pltpu.VMEM((HEADS, 8, TQ), F32)] * (1 if fixed_shift else 2)
    qmap = lambda b, s, qi, fin, kind_a, kj_a, kind_b, kj_b, bd: (b * nq + qi[s], 0)
    grid_spec = pltpu.PrefetchScalarGridSpec(
        num_scalar_prefetch=7,
        grid=(nbatch, len(tables[0])),
        in_specs=[
            pl.BlockSpec((TQ, HEADS * QK_PAD), qmap),
            pl.BlockSpec((TK, HEADS * QK_PAD),
                         lambda b, s, qi, fin, kind_a, kj_a, kind_b, kj_b, bd: (b * nk + kj_a[s], 0)),
            pl.BlockSpec((HW, TK),
                         lambda b, s, qi, fin, kind_a, kj_a, kind_b, kj_b, bd: (0, b * nk + kj_a[s])),
            pl.BlockSpec((TK, HEADS * QK_PAD),
                         lambda b, s, qi, fin, kind_a, kj_a, kind_b, kj_b, bd: (b * nk + kj_b[s], 0)),
            pl.BlockSpec((HW, TK),
                         lambda b, s, qi, fin, kind_a, kj_a, kind_b, kj_b, bd: (0, b * nk + kj_b[s])),
        ],
        out_specs=pl.BlockSpec((TQ, HW), qmap),
        scratch_shapes=stats + [pltpu.VMEM((HW, TQ), F32)],
    )
    return pl.pallas_call(
        functools.partial(_attn_body, fixed_shift),
        grid_spec=grid_spec,
        out_shape=jax.ShapeDtypeStruct((r, HW), BF16),
        compiler_params=_params(("parallel", "arbitrary")),
        name="attn_fixed_shift" if fixed_shift else "attn_online",
    )(*[jnp.asarray(t) for t in tables], bound, q, k, vt, k, vt)


def _mix_tail_body(tpb, ntiles, first_layer, *refs):
    if first_layer:
        meta_ref, refs = refs[0], refs[1:]
    (zh_ref, h_ref, ob_ref, lb_ref, one_m_lb_ref, og_ref, tri_ref, wo_ref, g_ref, wu_ref, wd_ref,
     out_ref, st_ref, oa_ref) = refs
    i = pl.program_id(0)

    @pl.when(i == 0)
    def _():
        st_ref[...] = jnp.zeros_like(st_ref)
        oa_ref[1] = jnp.zeros(oa_ref.shape[1:], BF16)

    keep = (jnp.minimum(i, ntiles - 1) % tpb != 0).astype(F32)
    stage1, stage2, stage3, stage4 = _hgrn_stages(
        zh_ref, lb_ref, one_m_lb_ref, og_ref, tri_ref, st_ref, keep, oa_ref.at[i % 2])

    tt = jnp.maximum(i - 1, 0) % tpb
    h = _stream_tile(h_ref, meta_ref, tt) if first_layer else h_ref[...]
    mix = jnp.dot(oa_ref[(i + 1) % 2], wo_ref[:HW, :], preferred_element_type=F32)
    mix = mix + jnp.dot(ob_ref[...], wo_ref[HW:, :], preferred_element_type=F32)
    s1 = stage1()
    h = h + mix
    hn = _rms(h, g_ref[...]).astype(BF16)
    carried = {}

    def run2():
        carried["s2"] = stage2(s1)

    def run3():
        carried["o"] = stage3(carried["s2"])

    def run4():
        stage4(carried["s2"], carried["o"])

    out_ref[...] = h + _mlp_staggered(hn, wu_ref, wd_ref, between=(run2, run3, run4))


def _mix_tail(zh, h, meta, ob, lb, one_m_lb, og, tri2, wo, g, wu, wd, e, layer, nbatch, lp):
    r = nbatch * lp
    tpb = lp // TM
    ntiles = r // TM
    first_layer = meta is not None
    cur = lambda i: jnp.minimum(i, ntiles - 1)
    prev = lambda i: jnp.maximum(i - 1, 0)
    row = lambda i: (prev(i), 0)
    h_spec = pl.BlockSpec((TM, D_MODEL),
                          (lambda i: (_frame_tile(prev(i), tpb), 0)) if first_layer else row)
    lead_specs = [_const_spec((N_META, D_MODEL))] if first_layer else []
    lead_args = [meta] if first_layer else []
    return pl.pallas_call(
        functools.partial(_mix_tail_body, tpb, ntiles, first_layer),
        grid=(ntiles + 1,),
        in_specs=lead_specs + [
            pl.BlockSpec((TM, 4 * HW), lambda i: (cur(i), 0)),
            h_spec,
            pl.BlockSpec((TM, HW), row),
            _const_spec((1, HW)),
            _const_spec((1, HW)),
            _const_spec((1, HD)),
            _const_spec((CHUNK, 2 * CHUNK)),
            _const_spec((2 * HW, D_MODEL), e),
            _const_spec((1, D_MODEL)),
            _const_spec((D_MODEL, D_FF), layer),
            _const_spec((D_FF, D_MODEL), layer),
        ],
        out_specs=pl.BlockSpec((TM, D_MODEL), row),
        out_shape=jax.ShapeDtypeStruct((r, D_MODEL), F32),
        scratch_shapes=[pltpu.VMEM((HEADS, HD, HD), F32), pltpu.VMEM((2, TM, HW), BF16)],
        compiler_params=_params(("arbitrary",)),
        name="mix_tail",
    )(*lead_args, zh, h, ob, lb, one_m_lb, og, tri2, wo, g, wu, wd)


def _pool_mlp_body(tpb, h_ref, halo_ref, gm_ref, pw_ref, ps_ref, g_ref, wu_ref, wd_ref,
                   out_ref, u_ref, a_ref, b_ref):
    tt = pl.program_id(0) % tpb

    @pl.when(tt == 0)
    def _():
        out_ref[...] = jnp.zeros_like(out_ref)

    @pl.when(tt != 0)
    def _():
        h = h_ref[...]
        gm = gm_ref[...]
        u_ref[0:HALO, :] = _rms(halo_ref[...], gm)
        u = _rms(h, gm)
        u_ref[HALO:, :] = u
        n = TM + HALO
        g = POOL_G
        a_ref[8:n, :] = u_ref[8:n, :] + u_ref[7:n - 1, :]
        b_ref[16:n, g:] = a_ref[16:n, g:] + a_ref[14:n - 2, g:]
        a_ref[24:n, 2 * g:] = b_ref[24:n, 2 * g:] + b_ref[20:n - 4, 2 * g:]
        b_ref[32:n, 3 * g:] = a_ref[32:n, 3 * g:] + a_ref[24:n - 8, 3 * g:]
        wins = (a_ref[HALO:, 0:g], b_ref[HALO:, g:2 * g], a_ref[HALO:, 2 * g:3 * g], b_ref[HALO:, 3 * g:])
        pos = lax.broadcasted_iota(jnp.int32, (TM, g), 0) + (tt * TM - PAD)
        cnt = jnp.maximum(pos + 1, 1).astype(F32)
        ps = ps_ref[...]
        ys = []
        for gi, w in enumerate(POOL_WINDOWS):
            d = wins[gi] / jnp.minimum(cnt, float(w)) - u[:, gi * g:(gi + 1) * g]
            y = jnp.dot(d.astype(BF16), pw_ref[gi], preferred_element_type=F32)
            ys.append(y * ps[:, gi * g:(gi + 1) * g])
        h = h + jnp.concatenate(ys, axis=1)
        out_ref[...] = h + _mlp_staggered(_rms(h, g_ref[...]).astype(BF16), wu_ref, wd_ref)


def _pool_mlp(h, gm, pw, ps, g, wu, wd, o, layer, nbatch, lp, to_frames):
    r = nbatch * lp
    tpb = lp // TM
    row = lambda i: (i, 0)
    out_rows = nbatch * (lp - LEAD) if to_frames else r
    return pl.pallas_call(
        functools.partial(_pool_mlp_body, tpb),
        grid=(r // TM,),
        in_specs=[
            pl.BlockSpec((TM, D_MODEL), row),
            pl.BlockSpec((HALO, D_MODEL), lambda i: (jnp.maximum(i * (TM // HALO) - 1, 0), 0)),
            _const_spec((1, D_MODEL)),
            _const_spec((len(POOL_WINDOWS), POOL_G, POOL_G), o),
            _const_spec((1, D_MODEL)),
            _const_spec((1, D_MODEL)),
            _const_spec((D_MODEL, D_FF), layer),
            _const_spec((D_FF, D_MODEL), layer),
        ],
        out_specs=pl.BlockSpec((TM, D_MODEL), (lambda i: (_frame_tile(i, tpb), 0)) if to_frames else row),
        out_shape=jax.ShapeDtypeStruct((out_rows, D_MODEL), F32),
        scratch_shapes=[pltpu.VMEM((TM + HALO, D_MODEL), F32)] * 3,
        compiler_params=_params(("arbitrary",) if to_frames else ("parallel",)),
        name="pool_mlp",
    )(h, h, gm, pw, ps, g, wu, wd)


def _rope_cols(w):
    z = jnp.zeros(w.shape[:-1] + (32,), w.dtype)
    return jnp.concatenate([w[..., :32], z, w[..., 32:], z], axis=-1)


def _qk_cols(w):
    w = w.reshape(w.shape[:-1] + (HEADS, QK_DIM))
    w = jnp.concatenate([w[..., :HD], _rope_cols(w[..., HD:])], axis=-1)
    return w.reshape(w.shape[:-2] + (HEADS * QK_PAD,))


def _rope_tables(lp):
    half = ROPE // 2
    inv = ROPE_THETA ** (-np.arange(half, dtype=np.float64) / half)
    pos = np.maximum(np.arange(lp, dtype=np.float64) - PAD, 0.0)
    ang = pos[:, None] * inv[None, :]
    c = jnp.asarray(np.cos(ang).astype(np.float32))
    s = jnp.asarray(np.sin(ang).astype(np.float32))
    z = jnp.zeros_like(c)
    return (jnp.concatenate([c, z, c, z], axis=1), jnp.concatenate([-s, z, s, z], axis=1))


def kernel(x, meta_tokens, mix_norm, mlp_norm, w_mlp_up, w_mlp_down, w_in, hgrn_lb, hgrn_out_norm, mla_q_a_norm, mla_kv_a_norm, w_q_up, w_kv_up, q_norm, k_norm, w_out, pool_w, pool_scale):
    nbatch, seq, _ = x.shape
    depth = mix_norm.shape[0]
    assert seq % TQ == 0 and depth % 2 == 0
    lp = seq + LEAD

    cos_t, sin_t = _rope_tables(lp)
    lb_cum = jnp.cumsum(jax.nn.softmax(hgrn_lb.astype(F32), axis=0), axis=0)
    lower = lb_cum - lb_cum[0:1]
    tri = jnp.tril(jnp.ones((CHUNK, CHUNK), F32)).astype(BF16)
    tri2 = jnp.concatenate([tri, tri], axis=1)

    w_in_l = jnp.concatenate(
        [w_in[..., :4 * HW + Q_RANK + KV_RANK], _rope_cols(w_in[..., 4 * HW + Q_RANK + KV_RANK:])],
        axis=-1).astype(BF16)
    wq_l = _qk_cols(w_q_up).astype(BF16)
    wkv = w_kv_up.reshape(w_kv_up.shape[0], KV_RANK, HEADS, 2 * HD)
    wkv_l = jnp.concatenate([wkv[..., :HD].reshape(-1, KV_RANK, HW),
                             wkv[..., HD:].reshape(-1, KV_RANK, HW)], axis=-1).astype(BF16)
    qn_l = jnp.concatenate([q_norm[:, :HD], _rope_cols(q_norm[:, HD:])], axis=-1) * Q_SCALE
    kn_l = jnp.concatenate([k_norm[:, :HD], _rope_cols(k_norm[:, HD:])], axis=-1)
    wo_l = w_out.astype(BF16)
    wu_l = w_mlp_up.astype(BF16)
    wd_l = w_mlp_down.astype(BF16)
    pw_l = pool_w.astype(BF16)

    h = x.reshape(nbatch * seq, D_MODEL)
    meta = meta_tokens.astype(F32)
    for layer in range(depth):
        if layer % 2 == 0:
            e = layer // 2
            zh, q, k, vt = _inproj(h, meta, mix_norm[layer][None], w_in_l, mla_q_a_norm[e][None],
                                  mla_kv_a_norm[e][None], wq_l, wkv_l, qn_l[e][None],
                                  kn_l[e][None], cos_t, sin_t, e, nbatch, lp)
            bound = (Q_SCALE * QK_DIM * 1.01) * jnp.max(jnp.abs(q_norm[e])) * jnp.max(jnp.abs(k_norm[e]))
            bound = bound.reshape(1).astype(F32)
            ob = lax.cond(bound[0] <= MAX_FIXED_SHIFT,
                          functools.partial(_attn, nbatch=nbatch, lp=lp, fixed_shift=True),
                          functools.partial(_attn, nbatch=nbatch, lp=lp, fixed_shift=False),
                          q, k, vt, bound)
            h = _mix_tail(zh, h, meta, ob, lower[e][None], 1.0 - lower[e][None], hgrn_out_norm[e][None],
                          tri2, wo_l, mlp_norm[layer][None], wu_l, wd_l, e, layer, nbatch, lp)
            meta = None
        else:
            o = layer // 2
            h = _pool_mlp(h, mix_norm[layer][None], pw_l, pool_scale[o][None],
                          mlp_norm[layer][None], wu_l, wd_l, o, layer, nbatch, lp,
                          to_frames=layer == depth - 1)

    return h.reshape(nbatch, seq, D_MODEL)
```

```python
import functools

import numpy as np
import jax
import jax.numpy as jnp
from jax import lax
from jax.experimental import pallas as pl
from jax.experimental.pallas import tpu as pltpu

F32 = jnp.float32
BF16 = jnp.bfloat16

D_MODEL = 1024
D_FF = 4 * D_MODEL
EPS = 1e-6
N_META = 16
CHUNK = 64
HEADS = 4
HD = 128
HW = HEADS * HD
ROPE = 64
QK_DIM = HD + ROPE
QK_PAD = 256
Q_RANK = 256
KV_RANK = 256
ROPE_THETA = 10000.0
POOL_WINDOWS = (2, 4, 8, 16)
POOL_G = D_MODEL // len(POOL_WINDOWS)

LEAD = 1024
PAD = LEAD - N_META
TM = 512
LEAD_TILES = LEAD // TM
TQ = 1024
TK = 1024
SUB = 16
HALO = 32
IN_COLS = 4 * HW + Q_RANK + KV_RANK + HD
Q_SCALE = QK_DIM ** -0.5 * float(np.log2(np.e))
MAX_FIXED_SHIFT = 56.0
MASK_VALUE = -1e30
EXP2_CLAMP = 115.0
TINY = 1e-37
SUBLANES = 8
V7X_VMEM_BYTES = 64 * 1024 * 1024
VMEM_LIMIT = V7X_VMEM_BYTES - 8 * 1024 * 1024


def _rms(x, g):
    return x * lax.rsqrt(jnp.mean(x * x, axis=-1, keepdims=True) + EPS) * g


def _silu(x):
    hx = 0.5 * x
    return hx + hx * jnp.tanh(hx)


def _const_spec(shape, layer=None):
    nd = len(shape)
    if layer is None:
        return pl.BlockSpec(shape, lambda *_: (0,) * nd, pipeline_mode=pl.Buffered(1))
    return pl.BlockSpec((None,) + tuple(shape), lambda *_: (layer,) + (0,) * nd, pipeline_mode=pl.Buffered(1))


def _params(sem):
    return pltpu.CompilerParams(dimension_semantics=sem, vmem_limit_bytes=VMEM_LIMIT)


def _frame_tile(i, tpb):
    return (i // tpb) * (tpb - LEAD_TILES) + jnp.maximum(i % tpb - LEAD_TILES, 0)


def _stream_tile(h_ref, meta_ref, tt):
    lead = jnp.concatenate([jnp.zeros((TM - N_META, D_MODEL), F32), meta_ref[...]], axis=0)
    lead = jnp.where(tt == LEAD_TILES - 1, lead, 0.0)
    return jnp.where(tt >= LEAD_TILES, h_ref[...], lead)


MLP_PARTS = 4
MLP_SLAB = D_FF // MLP_PARTS


def _mlp_up(hn, wu_ref, c):
    a = jnp.dot(hn, wu_ref[:, c * MLP_SLAB:(c + 1) * MLP_SLAB], preferred_element_type=F32)
    a = jnp.maximum(a, 0.0)
    return (a * a).astype(BF16)


def _mlp_down(a, wd_ref, c):
    return jnp.dot(a, wd_ref[c * MLP_SLAB:(c + 1) * MLP_SLAB, :], preferred_element_type=F32)


def _mlp_staggered(hn, wu_ref, wd_ref, between=()):
    a = _mlp_up(hn, wu_ref, 0)
    acc = None
    for c in range(MLP_PARTS):
        a_next = _mlp_up(hn, wu_ref, c + 1) if c + 1 < MLP_PARTS else None
        d = _mlp_down(a, wd_ref, c)
        acc = d if acc is None else acc + d
        if c < len(between):
            between[c]()
        a = a_next
    return acc


def _rope(x, c, s):
    return x * c + pltpu.roll(x, HD // 2, axis=1) * s


def _inproj_body(tpb, ntiles, first_layer, *refs):
    if first_layer:
        meta_ref, refs = refs[0], refs[1:]
    (h_ref, g_ref, win_ref, qag_ref, kvag_ref, wq_ref, wkv_ref, qn_ref, kn_ref, cos_ref, sin_ref,
     zh_ref, q_ref, k_ref, vt_ref, lat_ref) = refs
    i = pl.program_id(0)

    @pl.when(i == 0)
    def _():
        lat_ref[1] = jnp.zeros(lat_ref.shape[1:], F32)

    lat = lat_ref[(i + 1) % 2]
    q = jnp.dot(_rms(lat[:, :Q_RANK], qag_ref[...]).astype(BF16), wq_ref[...], preferred_element_type=F32)
    kv = jnp.dot(_rms(lat[:, Q_RANK:Q_RANK + KV_RANK], kvag_ref[...]).astype(BF16), wkv_ref[...],
                 preferred_element_type=F32)
    kr = lat[:, Q_RANK + KV_RANK:]

    tt = jnp.minimum(i, ntiles - 1) % tpb
    h = _stream_tile(h_ref, meta_ref, tt) if first_layer else h_ref[...]
    z = jnp.dot(_rms(h, g_ref[...]).astype(BF16), win_ref[...], preferred_element_type=F32)
    zh_ref[...] = z[:, :4 * HW]
    lat_ref[i % 2] = z[:, 4 * HW:]

    vt_ref[...] = kv[:, HW:].T.astype(BF16)
    c = cos_ref[...]
    s = sin_ref[...]
    qg = qn_ref[...]
    kg = kn_ref[...]
    kr_ss = jnp.sum(kr * kr, axis=-1, keepdims=True)
    for hd in range(HEADS):
        qa = q[:, hd * QK_PAD:hd * QK_PAD + HD]
        qb = q[:, hd * QK_PAD + HD:(hd + 1) * QK_PAD]
        ss = jnp.sum(qa * qa + qb * qb, axis=-1, keepdims=True)
        inv = lax.rsqrt(ss * (1.0 / QK_DIM) + EPS)
        q_ref[:, hd * QK_PAD:hd * QK_PAD + HD] = (qa * inv * qg[:, :HD]).astype(BF16)
        q_ref[:, hd * QK_PAD + HD:(hd + 1) * QK_PAD] = _rope(qb * inv * qg[:, HD:], c, s).astype(BF16)
        ka = kv[:, hd * HD:(hd + 1) * HD]
        ss = jnp.sum(ka * ka, axis=-1, keepdims=True) + kr_ss
        inv = lax.rsqrt(ss * (1.0 / QK_DIM) + EPS)
        k_ref[:, hd * QK_PAD:hd * QK_PAD + HD] = (ka * inv * kg[:, :HD]).astype(BF16)
        k_ref[:, hd * QK_PAD + HD:(hd + 1) * QK_PAD] = _rope(kr * inv * kg[:, HD:], c, s).astype(BF16)


def _inproj(h, meta, g, win, qag, kvag, wq, wkv, qn, kn, cos_t, sin_t, e, nbatch, lp):
    r = nbatch * lp
    tpb = lp // TM
    ntiles = r // TM
    first_layer = meta is not None
    cur = lambda i: jnp.minimum(i, ntiles - 1)
    prev = lambda i: jnp.maximum(i - 1, 0)
    row = lambda i: (prev(i), 0)
    tab = lambda i: (prev(i) % tpb, 0)
    h_spec = pl.BlockSpec((TM, D_MODEL),
                          (lambda i: (_frame_tile(cur(i), tpb), 0)) if first_layer else (lambda i: (cur(i), 0)))
    lead_specs = [_const_spec((N_META, D_MODEL))] if first_layer else []
    lead_args = [meta] if first_layer else []
    return pl.pallas_call(
        functools.partial(_inproj_body, tpb, ntiles, first_layer),
        grid=(ntiles + 1,),
        in_specs=lead_specs + [
            h_spec,
            _const_spec((1, D_MODEL)),
            _const_spec((D_MODEL, IN_COLS), e),
            _const_spec((1, Q_RANK)),
            _const_spec((1, KV_RANK)),
            _const_spec((Q_RANK, HEADS * QK_PAD), e),
            _const_spec((KV_RANK, 2 * HW), e),
            _const_spec((1, QK_PAD)),
            _const_spec((1, QK_PAD)),
            pl.BlockSpec((TM, HD), tab),
            pl.BlockSpec((TM, HD), tab),
        ],
        out_specs=[
            pl.BlockSpec((TM, 4 * HW), lambda i: (cur(i), 0)),
            pl.BlockSpec((TM, HEADS * QK_PAD), row),
            pl.BlockSpec((TM, HEADS * QK_PAD), row),
            pl.BlockSpec((HW, TM), lambda i: (0, prev(i))),
        ],
        out_shape=[
            jax.ShapeDtypeStruct((r, 4 * HW), F32),
            jax.ShapeDtypeStruct((r, HEADS * QK_PAD), BF16),
            jax.ShapeDtypeStruct((r, HEADS * QK_PAD), BF16),
            jax.ShapeDtypeStruct((HW, r), BF16),
        ],
        scratch_shapes=[pltpu.VMEM((2, TM, IN_COLS - 4 * HW), F32)],
        compiler_params=_params(("arbitrary",)),
        name="inproj",
    )(*lead_args, h, g, win, qag, kvag, wq, wkv, qn, kn, cos_t, sin_t)


def _group_rows(rows):
    return jnp.concatenate([jnp.broadcast_to(r, (SUB, HD)) for r in rows], axis=0)


def _hgrn_stages(zh_ref, lb_ref, one_m_lb_ref, og_ref, tri_ref, st_ref, keep, o_ref):
    tri2 = tri_ref[...]
    lb = lb_ref[...]
    one_m_lb = one_m_lb_ref[...]
    og = og_ref[...]
    tt = lax.broadcasted_iota(jnp.int32, (CHUNK, CHUNK), 0)
    ss_ = lax.broadcasted_iota(jnp.int32, (CHUNK, CHUNK), 1)
    causal = ss_ <= tt
    nsub = CHUNK // SUB
    zero_row = jnp.zeros((1, HD), F32)
    zero_sub = jnp.zeros((SUB, HD), BF16)

    heads = range(HEADS)
    sls = [slice(hd * HD, (hd + 1) * HD) for hd in heads]

    chunk_rows = [slice(c * CHUNK, (c + 1) * CHUNK) for c in range(TM // CHUNK)]

    def gates_and_decay(rows):
        hq = zh_ref[rows, 0:HW]
        hf = zh_ref[rows, HW:2 * HW]
        hi = zh_ref[rows, 2 * HW:3 * HW]
        hg = zh_ref[rows, 3 * HW:4 * HW]
        q = _silu(hq)
        gate = _silu(hg)
        t = jnp.exp(-jnp.abs(hf))
        r = 1.0 / (1.0 + t)
        tr = t * r
        pos = hf >= 0.0
        log2f = jnp.log2(jnp.maximum(lb + one_m_lb * jnp.where(pos, r, tr), TINY))
        k = one_m_lb * jnp.where(pos, tr, r)
        g1 = log2f.astype(BF16)
        g2 = (log2f - g1.astype(F32)).astype(BF16)
        b = jnp.dot(tri2, jnp.concatenate([g1, g2], axis=0), preferred_element_type=F32)
        return q, k, b, hi.astype(BF16), gate

    def intra_chunk(q, k, b, vb, gate):
        att, q_in, k_out, decay = [], [], [], []
        for hd in heads:
            bh = b[:, sls[hd]]
            b_last = bh[CHUNK - 1:CHUNK, :]
            refs = [zero_row] + [bh[i * SUB - 1:i * SUB, :] for i in range(1, nsub)]
            dq = bh - _group_rows(refs)
            qe = q[:, sls[hd]] * jnp.exp2(dq)
            ke = k[:, sls[hd]] * jnp.exp2(jnp.minimum(-dq, EXP2_CLAMP))
            keb = ke.astype(BF16)
            qcat = []
            kcat = []
            for j in range(nsub):
                qcat.append(jnp.concatenate(
                    [zero_sub if i < j else
                     qe[i * SUB:(i + 1) * SUB].astype(BF16) if i == j else
                     (qe[i * SUB:(i + 1) * SUB] * jnp.exp2(refs[i] - refs[j])).astype(BF16)
                     for i in range(nsub)], axis=0))
                kcat.append(jnp.concatenate(
                    [keb[j * SUB:(j + 1) * SUB] if i == j else zero_sub for i in range(nsub)], axis=0))
            q_in.append(qcat[0])
            att.append(lax.dot_general(jnp.concatenate(qcat, axis=1), jnp.concatenate(kcat, axis=1),
                                       (((1,), (1,)), ((), ())), preferred_element_type=F32))
            to_end = [jnp.exp2(b_last - refs[i]) for i in range(nsub)]
            k_out.append((ke * _group_rows(to_end)).astype(BF16))
            decay.append(to_end[0])
        upd = [lax.dot_general(vb[:, sls[hd]], k_out[hd], (((0,), (0,)), ((), ())),
                               preferred_element_type=F32) for hd in heads]
        return att, q_in, upd, decay, vb, gate

    def stage1():
        return [gates_and_decay(rows) for rows in chunk_rows]

    def stage2(s1):
        return [intra_chunk(*c) for c in s1]

    def stage3(s2):
        st = [st_ref[hd] * keep for hd in heads]
        o = []
        for att, q_in, upd, decay, vb, gate in s2:
            o.append([lax.dot_general(q_in[hd], st[hd].astype(BF16), (((1,), (1,)), ((), ())),
                                      preferred_element_type=F32) for hd in heads])
            st = [decay[hd] * st[hd] + upd[hd] for hd in heads]
        for hd in heads:
            st_ref[hd] = st[hd]
        return o

    def stage4(s2, o):
        for rows, oc, (att, q_in, upd, decay, vb, gate) in zip(chunk_rows, o, s2):
            for hd in heads:
                a = jnp.where(causal, att[hd], 0.0).astype(BF16)
                oh = oc[hd] + jnp.dot(a, vb[:, sls[hd]], preferred_element_type=F32)
                on = _rms(oh, og) * gate[:, sls[hd]]
                o_ref[rows, sls[hd]] = on.astype(BF16)

    return stage1, stage2, stage3, stage4


META_STEP, FULL_STEP, DIAG_STEP, NO_STEP = range(4)


def _attn_steps(lp):
    qi, fin, kind_a, kj_a, kind_b, kj_b = [], [], [], [], [], []
    for i in range(lp // TQ):
        tiles = [(META_STEP if j == 0 else DIAG_STEP if j == i else FULL_STEP, j) for j in range(i + 1)]
        for s in range(0, len(tiles), 2):
            a = tiles[s]
            b = tiles[s + 1] if s + 1 < len(tiles) else (NO_STEP, a[1])
            qi.append(i)
            fin.append(int(s + 2 >= len(tiles)))
            kind_a.append(a[0])
            kj_a.append(a[1])
            kind_b.append(b[0])
            kj_b.append(b[1])
    return [np.asarray(t, np.int32) for t in (qi, fin, kind_a, kj_a, kind_b, kj_b)]


def _attn_body(*refs):
    use_fixed_shift = refs[6][0] <= MAX_FIXED_SHIFT

    @pl.when(use_fixed_shift)
    def _():
        _attn_step(True, *refs)

    @pl.when(jnp.logical_not(use_fixed_shift))
    def _():
        _attn_step(False, *refs)


def _attn_step(fixed_shift, qi_ref, fin_ref, kind_a_ref, kj_a_ref, kind_b_ref, kj_b_ref, bound_ref,
               q_ref, ka_ref, vta_ref, kb_ref, vtb_ref, o_ref, m_ref, l_ref, acc_ref):
    step = pl.program_id(1)
    qi = qi_ref[step]
    bound = bound_ref[0]

    def update(k_ref, vt_ref, qs, keys, mask, first):
        nk, nq = keys.stop - keys.start, qs.stop - qs.start
        for hd in range(HEADS):
            hq = slice(hd * QK_PAD, (hd + 1) * QK_PAD)
            hv = slice(hd * HD, (hd + 1) * HD)
            st = lax.dot_general(k_ref[keys, hq], q_ref[qs, hq], (((1,), (1,)), ((), ())),
                                 preferred_element_type=F32)
            if fixed_shift:
                p = jnp.exp2(st - bound)
                if mask is not None:
                    p = jnp.where(mask, p, 0.0)
            else:
                if mask is not None:
                    st = jnp.where(mask, st, MASK_VALUE)
                m_prev = m_ref[hd, :, qs]
                m_new = jnp.maximum(m_prev, jnp.max(st, axis=0, keepdims=True))
                alpha = jnp.exp2(m_prev - m_new)
                m_ref[hd, :, qs] = m_new
                p = jnp.exp2(st - m_new[0:1, :])
            part = jnp.sum(p.reshape(nk // SUBLANES, SUBLANES, nq), axis=0)
            pv = jnp.dot(vt_ref[hv, keys], p.astype(BF16), preferred_element_type=F32)
            if first and fixed_shift:
                l_ref[hd, :, qs] = part
                acc_ref[hv, qs] = pv
            elif fixed_shift:
                l_ref[hd, :, qs] += part
                acc_ref[hv, qs] += pv
            else:
                l_ref[hd, :, qs] = alpha * l_ref[hd, :, qs] + part
                acc_ref[hv, qs] = alpha[0:1, :] * acc_ref[hv, qs] + pv

    all_q = slice(0, TQ)
    all_keys = slice(0, TK)

    def key_tile(kind, kj, k_ref, vt_ref):
        @pl.when(kind == META_STEP)
        def _():
            if not fixed_shift:
                m_ref[...] = jnp.full_like(m_ref, MASK_VALUE)
                l_ref[...] = jnp.zeros_like(l_ref)
                acc_ref[...] = jnp.zeros_like(acc_ref)
            key = lax.broadcasted_iota(jnp.int32, (HD, TQ), 0) + (kj * TK + TK - HD)
            update(k_ref, vt_ref, all_q, slice(TK - HD, TK), key >= PAD, True)

        @pl.when(kind == FULL_STEP)
        def _():
            update(k_ref, vt_ref, all_q, all_keys, None, False)

        @pl.when(kind == DIAG_STEP)
        def _():
            half = TQ // 2
            key = lax.broadcasted_iota(jnp.int32, (half, TQ), 0)
            qry = lax.broadcasted_iota(jnp.int32, (half, TQ), 1)
            update(k_ref, vt_ref, all_q, slice(0, half), key // CHUNK <= qry // CHUNK, False)
            update(k_ref, vt_ref, slice(half, TQ), slice(half, TK),
                   (key // CHUNK <= qry // CHUNK)[:, :half], False)

    key_tile(kind_a_ref[step], kj_a_ref[step], ka_ref, vta_ref)
    key_tile(kind_b_ref[step], kj_b_ref[step], kb_ref, vtb_ref)

    @pl.when(fin_ref[step] == 1)
    def _():
        row = lax.broadcasted_iota(jnp.int32, (TQ, HD), 0) + qi * TQ
        valid = row >= PAD
        for hd in range(HEADS):
            hv = slice(hd * HD, (hd + 1) * HD)
            ot = acc_ref[hv, :] / jnp.sum(l_ref[hd], axis=0, keepdims=True)
            o_ref[:, hv] = jnp.where(valid, ot.T, 0.0).astype(BF16)


def _attn(q, k, vt, bound, nbatch, lp):
    r = q.shape[0]
    nq, nk = lp // TQ, lp // TK
    tables = _attn_steps(lp)
    stats = [pltpu.VMEM((HEADS, SUBLANES, TQ), F32)] * 2
    qmap = lambda b, s, qi, fin, kind_a, kj_a, kind_b, kj_b, bd: (b * nq + qi[s], 0)
    grid_spec = pltpu.PrefetchScalarGridSpec(
        num_scalar_prefetch=7,
        grid=(nbatch, len(tables[0])),
        in_specs=[
            pl.BlockSpec((TQ, HEADS * QK_PAD), qmap),
            pl.BlockSpec((TK, HEADS * QK_PAD),
                         lambda b, s, qi, fin, kind_a, kj_a, kind_b, kj_b, bd: (b * nk + kj_a[s], 0)),
            pl.BlockSpec((HW, TK),
                         lambda b, s, qi, fin, kind_a, kj_a, kind_b, kj_b, bd: (0, b * nk + kj_a[s])),
            pl.BlockSpec((TK, HEADS * QK_PAD),
                         lambda b, s, qi, fin, kind_a, kj_a, kind_b, kj_b, bd: (b * nk + kj_b[s], 0)),
            pl.BlockSpec((HW, TK),
                         lambda b, s, qi, fin, kind_a, kj_a, kind_b, kj_b, bd: (0, b * nk + kj_b[s])),
        ],
        out_specs=pl.BlockSpec((TQ, HW), qmap),
        scratch_shapes=stats + [pltpu.VMEM((HW, TQ), F32)],
    )
    return pl.pallas_call(
        _attn_body,
        grid_spec=grid_spec,
        out_shape=jax.ShapeDtypeStruct((r, HW), BF16),
        compiler_params=_params(("parallel", "arbitrary")),
        name="attn",
    )(*[jnp.asarray(t) for t in tables], bound, q, k, vt, k, vt)


def _mix_tail_body(tpb, ntiles, first_layer, *refs):
    if first_layer:
        meta_ref, refs = refs[0], refs[1:]
    (zh_ref, h_ref, ob_ref, lb_ref, one_m_lb_ref, og_ref, tri_ref, wo_ref, g_ref, wu_ref, wd_ref,
     out_ref, st_ref, oa_ref) = refs
    i = pl.program_id(0)

    @pl.when(i == 0)
    def _():
        st_ref[...] = jnp.zeros_like(st_ref)
        oa_ref[1] = jnp.zeros(oa_ref.shape[1:], BF16)

    keep = (jnp.minimum(i, ntiles - 1) % tpb != 0).astype(F32)
    stage1, stage2, stage3, stage4 = _hgrn_stages(
        zh_ref, lb_ref, one_m_lb_ref, og_ref, tri_ref, st_ref, keep, oa_ref.at[i % 2])

    tt = jnp.maximum(i - 1, 0) % tpb
    h = _stream_tile(h_ref, meta_ref, tt) if first_layer else h_ref[...]
    mix = jnp.dot(oa_ref[(i + 1) % 2], wo_ref[:HW, :], preferred_element_type=F32)
    mix = mix + jnp.dot(ob_ref[...], wo_ref[HW:, :], preferred_element_type=F32)
    s1 = stage1()
    h = h + mix
    hn = _rms(h, g_ref[...]).astype(BF16)
    carried = {}

    def run2():
        carried["s2"] = stage2(s1)

    def run3():
        carried["o"] = stage3(carried["s2"])

    def run4():
        stage4(carried["s2"], carried["o"])

    out_ref[...] = h + _mlp_staggered(hn, wu_ref, wd_ref, between=(run2, run3, run4))


def _mix_tail(zh, h, meta, ob, lb, one_m_lb, og, tri2, wo, g, wu, wd, e, layer, nbatch, lp):
    r = nbatch * lp
    tpb = lp // TM
    ntiles = r // TM
    first_layer = meta is not None
    cur = lambda i: jnp.minimum(i, ntiles - 1)
    prev = lambda i: jnp.maximum(i - 1, 0)
    row = lambda i: (prev(i), 0)
    h_spec = pl.BlockSpec((TM, D_MODEL),
                          (lambda i: (_frame_tile(prev(i), tpb), 0)) if first_layer else row)
    lead_specs = [_const_spec((N_META, D_MODEL))] if first_layer else []
    lead_args = [meta] if first_layer else []
    return pl.pallas_call(
        functools.partial(_mix_tail_body, tpb, ntiles, first_layer),
        grid=(ntiles + 1,),
        in_specs=lead_specs + [
            pl.BlockSpec((TM, 4 * HW), lambda i: (cur(i), 0)),
            h_spec,
            pl.BlockSpec((TM, HW), row),
            _const_spec((1, HW)),
            _const_spec((1, HW)),
            _const_spec((1, HD)),
            _const_spec((CHUNK, 2 * CHUNK)),
            _const_spec((2 * HW, D_MODEL), e),
            _const_spec((1, D_MODEL)),
            _const_spec((D_MODEL, D_FF), layer),
            _const_spec((D_FF, D_MODEL), layer),
        ],
        out_specs=pl.BlockSpec((TM, D_MODEL), row),
        out_shape=jax.ShapeDtypeStruct((r, D_MODEL), F32),
        scratch_shapes=[pltpu.VMEM((HEADS, HD, HD), F32), pltpu.VMEM((2, TM, HW), BF16)],
        compiler_params=_params(("arbitrary",)),
        name="mix_tail",
    )(*lead_args, zh, h, ob, lb, one_m_lb, og, tri2, wo, g, wu, wd)


def _pool_mlp_body(tpb, h_ref, halo_ref, gm_ref, pw_ref, ps_ref, g_ref, wu_ref, wd_ref,
                   out_ref, u_ref, a_ref, b_ref):
    tt = pl.program_id(0) % tpb

    @pl.when(tt == 0)
    def _():
        out_ref[...] = jnp.zeros_like(out_ref)

    @pl.when(tt != 0)
    def _():
        h = h_ref[...]
        gm = gm_ref[...]
        u_ref[0:HALO, :] = _rms(halo_ref[...], gm)
        u = _rms(h, gm)
        u_ref[HALO:, :] = u
        n = TM + HALO
        g = POOL_G
        a_ref[8:n, :] = u_ref[8:n, :] + u_ref[7:n - 1, :]
        b_ref[16:n, g:] = a_ref[16:n, g:] + a_ref[14:n - 2, g:]
        a_ref[24:n, 2 * g:] = b_ref[24:n, 2 * g:] + b_ref[20:n - 4, 2 * g:]
        b_ref[32:n, 3 * g:] = a_ref[32:n, 3 * g:] + a_ref[24:n - 8, 3 * g:]
        wins = (a_ref[HALO:, 0:g], b_ref[HALO:, g:2 * g], a_ref[HALO:, 2 * g:3 * g], b_ref[HALO:, 3 * g:])
        pos = lax.broadcasted_iota(jnp.int32, (TM, g), 0) + (tt * TM - PAD)
        cnt = jnp.maximum(pos + 1, 1).astype(F32)
        ps = ps_ref[...]
        ys = []
        for gi, w in enumerate(POOL_WINDOWS):
            d = wins[gi] / jnp.minimum(cnt, float(w)) - u[:, gi * g:(gi + 1) * g]
            y = jnp.dot(d.astype(BF16), pw_ref[gi], preferred_element_type=F32)
            ys.append(y * ps[:, gi * g:(gi + 1) * g])
        h = h + jnp.concatenate(ys, axis=1)
        out_ref[...] = h + _mlp_staggered(_rms(h, g_ref[...]).astype(BF16), wu_ref, wd_ref)


def _pool_mlp(h, gm, pw, ps, g, wu, wd, o, layer, nbatch, lp, to_frames):
    r = nbatch * lp
    tpb = lp // TM
    row = lambda i: (i, 0)
    out_rows = nbatch * (lp - LEAD) if to_frames else r
    return pl.pallas_call(
        functools.partial(_pool_mlp_body, tpb),
        grid=(r // TM,),
        in_specs=[
            pl.BlockSpec((TM, D_MODEL), row),
            pl.BlockSpec((HALO, D_MODEL), lambda i: (jnp.maximum(i * (TM // HALO) - 1, 0), 0)),
            _const_spec((1, D_MODEL)),
            _const_spec((len(POOL_WINDOWS), POOL_G, POOL_G), o),
            _const_spec((1, D_MODEL)),
            _const_spec((1, D_MODEL)),
            _const_spec((D_MODEL, D_FF), layer),
            _const_spec((D_FF, D_MODEL), layer),
        ],
        out_specs=pl.BlockSpec((TM, D_MODEL), (lambda i: (_frame_tile(i, tpb), 0)) if to_frames else row),
        out_shape=jax.ShapeDtypeStruct((out_rows, D_MODEL), F32),
        scratch_shapes=[pltpu.VMEM((TM + HALO, D_MODEL), F32)] * 3,
        compiler_params=_params(("arbitrary",) if to_frames else ("parallel",)),
        name="pool_mlp",
    )(h, h, gm, pw, ps, g, wu, wd)


def _rope_cols(w):
    half = ROPE // 2
    z = jnp.zeros(w.shape[:-1] + (half,), w.dtype)
    return jnp.concatenate([w[..., :half], z, w[..., half:], z], axis=-1)


def _qk_cols(w):
    w = w.reshape(w.shape[:-1] + (HEADS, QK_DIM))
    w = jnp.concatenate([w[..., :HD], _rope_cols(w[..., HD:])], axis=-1)
    return w.reshape(w.shape[:-2] + (HEADS * QK_PAD,))


def _rope_tables(lp):
    half = ROPE // 2
    inv = ROPE_THETA ** (-np.arange(half, dtype=np.float64) / half)
    pos = np.maximum(np.arange(lp, dtype=np.float64) - PAD, 0.0)
    ang = pos[:, None] * inv[None, :]
    c = jnp.asarray(np.cos(ang).astype(np.float32))
    s = jnp.asarray(np.sin(ang).astype(np.float32))
    z = jnp.zeros_like(c)
    return (jnp.concatenate([c, z, c, z], axis=1), jnp.concatenate([-s, z, s, z], axis=1))


def kernel(x, meta_tokens, mix_norm, mlp_norm, w_mlp_up, w_mlp_down, w_in, hgrn_lb, hgrn_out_norm, mla_q_a_norm, mla_kv_a_norm, w_q_up, w_kv_up, q_norm, k_norm, w_out, pool_w, pool_scale):
    nbatch, seq, _ = x.shape
    depth = mix_norm.shape[0]
    assert seq % TQ == 0 and depth % 2 == 0
    lp = seq + LEAD

    cos_t, sin_t = _rope_tables(lp)
    lb_cum = jnp.cumsum(jax.nn.softmax(hgrn_lb.astype(F32), axis=0), axis=0)
    lower = lb_cum - lb_cum[0:1]
    tri = jnp.tril(jnp.ones((CHUNK, CHUNK), F32)).astype(BF16)
    tri2 = jnp.concatenate([tri, tri], axis=1)

    w_in_l = jnp.concatenate(
        [w_in[..., :4 * HW + Q_RANK + KV_RANK], _rope_cols(w_in[..., 4 * HW + Q_RANK + KV_RANK:])],
        axis=-1).astype(BF16)
    wq_l = _qk_cols(w_q_up).astype(BF16)
    wkv = w_kv_up.reshape(w_kv_up.shape[0], KV_RANK, HEADS, 2 * HD)
    wkv_l = jnp.concatenate([wkv[..., :HD].reshape(-1, KV_RANK, HW),
                             wkv[..., HD:].reshape(-1, KV_RANK, HW)], axis=-1).astype(BF16)
    qn_l = jnp.concatenate([q_norm[:, :HD], _rope_cols(q_norm[:, HD:])], axis=-1) * Q_SCALE
    kn_l = jnp.concatenate([k_norm[:, :HD], _rope_cols(k_norm[:, HD:])], axis=-1)
    wo_l = w_out.astype(BF16)
    wu_l = w_mlp_up.astype(BF16)
    wd_l = w_mlp_down.astype(BF16)
    pw_l = pool_w.astype(BF16)

    h = x.reshape(nbatch * seq, D_MODEL)
    meta = meta_tokens.astype(F32)
    for layer in range(depth):
        if layer % 2 == 0:
            e = layer // 2
            zh, q, k, vt = _inproj(h, meta, mix_norm[layer][None], w_in_l, mla_q_a_norm[e][None],
                                  mla_kv_a_norm[e][None], wq_l, wkv_l, qn_l[e][None],
                                  kn_l[e][None], cos_t, sin_t, e, nbatch, lp)
            bound = (Q_SCALE * QK_DIM * 1.01) * jnp.max(jnp.abs(q_norm[e])) * jnp.max(jnp.abs(k_norm[e]))
            bound = bound.reshape(1).astype(F32)
            ob = _attn(q, k, vt, bound, nbatch, lp)
            h = _mix_tail(zh, h, meta, ob, lower[e][None], 1.0 - lower[e][None], hgrn_out_norm[e][None],
                          tri2, wo_l, mlp_norm[layer][None], wu_l, wd_l, e, layer, nbatch, lp)
            meta = None
        else:
            o = layer // 2
            h = _pool_mlp(h, mix_norm[layer][None], pw_l, pool_scale[o][None],
                          mlp_norm[layer][None], wu_l, wd_l, o, layer, nbatch, lp,
                          to_frames=layer == depth - 1)

    return h.reshape(nbatch, seq, D_MODEL)
```

```python
import functools

import numpy as np
import jax
import jax.numpy as jnp
from jax import lax
from jax.experimental import pallas as pl
from jax.experimental.pallas import tpu as pltpu

F32 = jnp.float32
BF16 = jnp.bfloat16

D_MODEL = 1024
D_FF = 4 * D_MODEL
EPS = 1e-6
N_META = 16
CHUNK = 64
HEADS = 4
HD = 128
HW = HEADS * HD
ROPE = 64
QK_DIM = HD + ROPE
QK_PAD = 256
Q_RANK = 256
KV_RANK = 256
ROPE_THETA = 10000.0
POOL_WINDOWS = (2, 4, 8, 16)
POOL_G = D_MODEL // len(POOL_WINDOWS)

LEAD = 1024
PAD = LEAD - N_META
TM = 512
LEAD_TILES = LEAD // TM
TQ = 1024
TK = 1024
SUB = 16
HALO = 32
IN_COLS = 4 * HW + Q_RANK + KV_RANK + HD
Q_SCALE = QK_DIM ** -0.5 * float(np.log2(np.e))
MAX_FIXED_SHIFT = 56.0
MASK_VALUE = -1e30
EXP2_CLAMP = 115.0
TINY = 1e-37
SUBLANES = 8
V7X_VMEM_BYTES = 64 * 1024 * 1024
VMEM_LIMIT = V7X_VMEM_BYTES - 8 * 1024 * 1024


def _rms(x, g):
    return x * lax.rsqrt(jnp.mean(x * x, axis=-1, keepdims=True) + EPS) * g


def _silu(x):
    hx = 0.5 * x
    return hx + hx * jnp.tanh(hx)


def _const_spec(shape, layer=None):
    nd = len(shape)
    if layer is None:
        return pl.BlockSpec(shape, lambda *_: (0,) * nd, pipeline_mode=pl.Buffered(1))
    return pl.BlockSpec((None,) + tuple(shape), lambda *_: (layer,) + (0,) * nd, pipeline_mode=pl.Buffered(1))


def _params(sem):
    return pltpu.CompilerParams(dimension_semantics=sem, vmem_limit_bytes=VMEM_LIMIT)


def _frame_tile(i, tpb):
    return (i // tpb) * (tpb - LEAD_TILES) + jnp.maximum(i % tpb - LEAD_TILES, 0)


def _stream_tile(h_ref, meta_ref, tt):
    lead = jnp.concatenate([jnp.zeros((TM - N_META, D_MODEL), F32), meta_ref[...]], axis=0)
    lead = jnp.where(tt == LEAD_TILES - 1, lead, 0.0)
    return jnp.where(tt >= LEAD_TILES, h_ref[...], lead)


MLP_PARTS = 4
MLP_SLAB = D_FF // MLP_PARTS


def _mlp_up(hn, wu_ref, c):
    a = jnp.dot(hn, wu_ref[:, c * MLP_SLAB:(c + 1) * MLP_SLAB], preferred_element_type=F32)
    a = jnp.maximum(a, 0.0)
    return (a * a).astype(BF16)


def _mlp_down(a, wd_ref, c):
    return jnp.dot(a, wd_ref[c * MLP_SLAB:(c + 1) * MLP_SLAB, :], preferred_element_type=F32)


def _mlp_staggered(hn, wu_ref, wd_ref, between=()):
    a = _mlp_up(hn, wu_ref, 0)
    acc = None
    for c in range(MLP_PARTS):
        a_next = _mlp_up(hn, wu_ref, c + 1) if c + 1 < MLP_PARTS else None
        d = _mlp_down(a, wd_ref, c)
        acc = d if acc is None else acc + d
        if c < len(between):
            between[c]()
        a = a_next
    return acc


def _rope(x, c, s):
    return x * c + pltpu.roll(x, HD // 2, axis=1) * s


def _inproj_body(tpb, ntiles, first_layer, *refs):
    if first_layer:
        meta_ref, refs = refs[0], refs[1:]
    (h_ref, g_ref, win_ref, qag_ref, kvag_ref, wq_ref, wkv_ref, qn_ref, kn_ref, cos_ref, sin_ref,
     zh_ref, q_ref, k_ref, vt_ref, lat_ref) = refs
    i = pl.program_id(0)

    @pl.when(i == 0)
    def _():
        lat_ref[1] = jnp.zeros(lat_ref.shape[1:], F32)

    lat = lat_ref[(i + 1) % 2]
    q = jnp.dot(_rms(lat[:, :Q_RANK], qag_ref[...]).astype(BF16), wq_ref[...], preferred_element_type=F32)
    kv = jnp.dot(_rms(lat[:, Q_RANK:Q_RANK + KV_RANK], kvag_ref[...]).astype(BF16), wkv_ref[...],
                 preferred_element_type=F32)
    kr = lat[:, Q_RANK + KV_RANK:]

    tt = jnp.minimum(i, ntiles - 1) % tpb
    h = _stream_tile(h_ref, meta_ref, tt) if first_layer else h_ref[...]
    z = jnp.dot(_rms(h, g_ref[...]).astype(BF16), win_ref[...], preferred_element_type=F32)
    zh_ref[...] = z[:, :4 * HW]
    lat_ref[i % 2] = z[:, 4 * HW:]

    vt_ref[...] = kv[:, HW:].T.astype(BF16)
    c = cos_ref[...]
    s = sin_ref[...]
    qg = qn_ref[...]
    kg = kn_ref[...]
    kr_ss = jnp.sum(kr * kr, axis=-1, keepdims=True)
    for hd in range(HEADS):
        qa = q[:, hd * QK_PAD:hd * QK_PAD + HD]
        qb = q[:, hd * QK_PAD + HD:(hd + 1) * QK_PAD]
        ss = jnp.sum(qa * qa + qb * qb, axis=-1, keepdims=True)
        inv = lax.rsqrt(ss * (1.0 / QK_DIM) + EPS)
        q_ref[:, hd * QK_PAD:hd * QK_PAD + HD] = (qa * inv * qg[:, :HD]).astype(BF16)
        q_ref[:, hd * QK_PAD + HD:(hd + 1) * QK_PAD] = _rope(qb * inv * qg[:, HD:], c, s).astype(BF16)
        ka = kv[:, hd * HD:(hd + 1) * HD]
        ss = jnp.sum(ka * ka, axis=-1, keepdims=True) + kr_ss
        inv = lax.rsqrt(ss * (1.0 / QK_DIM) + EPS)
        k_ref[:, hd * QK_PAD:hd * QK_PAD + HD] = (ka * inv * kg[:, :HD]).astype(BF16)
        k_ref[:, hd * QK_PAD + HD:(hd + 1) * QK_PAD] = _rope(kr * inv * kg[:, HD:], c, s).astype(BF16)


def _inproj(h, meta, g, win, qag, kvag, wq, wkv, qn, kn, cos_t, sin_t, e, nbatch, lp):
    r = nbatch * lp
    tpb = lp // TM
    ntiles = r // TM
    first_layer = meta is not None
    cur = lambda i: jnp.minimum(i, ntiles - 1)
    prev = lambda i: jnp.maximum(i - 1, 0)
    row = lambda i: (prev(i), 0)
    tab = lambda i: (prev(i) % tpb, 0)
    h_spec = pl.BlockSpec((TM, D_MODEL),
                          (lambda i: (_frame_tile(cur(i), tpb), 0)) if first_layer else (lambda i: (cur(i), 0)))
    lead_specs = [_const_spec((N_META, D_MODEL))] if first_layer else []
    lead_args = [meta] if first_layer else []
    return pl.pallas_call(
        functools.partial(_inproj_body, tpb, ntiles, first_layer),
        grid=(ntiles + 1,),
        in_specs=lead_specs + [
            h_spec,
            _const_spec((1, D_MODEL)),
            _const_spec((D_MODEL, IN_COLS), e),
            _const_spec((1, Q_RANK)),
            _const_spec((1, KV_RANK)),
            _const_spec((Q_RANK, HEADS * QK_PAD), e),
            _const_spec((KV_RANK, 2 * HW), e),
            _const_spec((1, QK_PAD)),
            _const_spec((1, QK_PAD)),
            pl.BlockSpec((TM, HD), tab),
            pl.BlockSpec((TM, HD), tab),
        ],
        out_specs=[
            pl.BlockSpec((TM, 4 * HW), lambda i: (cur(i), 0)),
            pl.BlockSpec((TM, HEADS * QK_PAD), row),
            pl.BlockSpec((TM, HEADS * QK_PAD), row),
            pl.BlockSpec((HW, TM), lambda i: (0, prev(i))),
        ],
        out_shape=[
            jax.ShapeDtypeStruct((r, 4 * HW), F32),
            jax.ShapeDtypeStruct((r, HEADS * QK_PAD), BF16),
            jax.ShapeDtypeStruct((r, HEADS * QK_PAD), BF16),
            jax.ShapeDtypeStruct((HW, r), BF16),
        ],
        scratch_shapes=[pltpu.VMEM((2, TM, IN_COLS - 4 * HW), F32)],
        compiler_params=_params(("arbitrary",)),
        name="inproj",
    )(*lead_args, h, g, win, qag, kvag, wq, wkv, qn, kn, cos_t, sin_t)


def _group_rows(rows):
    return jnp.concatenate([jnp.broadcast_to(r, (SUB, HD)) for r in rows], axis=0)


def _hgrn_stages(zh_ref, lb_ref, one_m_lb_ref, og_ref, tri_ref, st_ref, keep, o_ref):
    tri2 = tri_ref[...]
    lb = lb_ref[...]
    one_m_lb = one_m_lb_ref[...]
    og = og_ref[...]
    tt = lax.broadcasted_iota(jnp.int32, (CHUNK, CHUNK), 0)
    ss_ = lax.broadcasted_iota(jnp.int32, (CHUNK, CHUNK), 1)
    causal = ss_ <= tt
    nsub = CHUNK // SUB
    zero_row = jnp.zeros((1, HD), F32)
    zero_sub = jnp.zeros((SUB, HD), BF16)

    heads = range(HEADS)
    sls = [slice(hd * HD, (hd + 1) * HD) for hd in heads]

    chunk_rows = [slice(c * CHUNK, (c + 1) * CHUNK) for c in range(TM // CHUNK)]

    def gates_and_decay(rows):
        hq = zh_ref[rows, 0:HW]
        hf = zh_ref[rows, HW:2 * HW]
        hi = zh_ref[rows, 2 * HW:3 * HW]
        hg = zh_ref[rows, 3 * HW:4 * HW]
        q = _silu(hq)
        gate = _silu(hg)
        t = jnp.exp(-jnp.abs(hf))
        r = 1.0 / (1.0 + t)
        tr = t * r
        pos = hf >= 0.0
        log2f = jnp.log2(jnp.maximum(lb + one_m_lb * jnp.where(pos, r, tr), TINY))
        k = one_m_lb * jnp.where(pos, tr, r)
        g1 = log2f.astype(BF16)
        g2 = (log2f - g1.astype(F32)).astype(BF16)
        b = jnp.dot(tri2, jnp.concatenate([g1, g2], axis=0), preferred_element_type=F32)
        return q, k, b, hi.astype(BF16), gate

    def intra_chunk(q, k, b, vb, gate):
        att, q_in, k_out, decay = [], [], [], []
        for hd in heads:
            bh = b[:, sls[hd]]
            b_last = bh[CHUNK - 1:CHUNK, :]
            refs = [zero_row] + [bh[i * SUB - 1:i * SUB, :] for i in range(1, nsub)]
            dq = bh - _group_rows(refs)
            qe = q[:, sls[hd]] * jnp.exp2(dq)
            ke = k[:, sls[hd]] * jnp.exp2(jnp.minimum(-dq, EXP2_CLAMP))
            keb = ke.astype(BF16)
            qcat = []
            kcat = []
            for j in range(nsub):
                qcat.append(jnp.concatenate(
                    [zero_sub if i < j else
                     qe[i * SUB:(i + 1) * SUB].astype(BF16) if i == j else
                     (qe[i * SUB:(i + 1) * SUB] * jnp.exp2(refs[i] - refs[j])).astype(BF16)
                     for i in range(nsub)], axis=0))
                kcat.append(jnp.concatenate(
                    [keb[j * SUB:(j + 1) * SUB] if i == j else zero_sub for i in range(nsub)], axis=0))
            q_in.append(qcat[0])
            att.append(lax.dot_general(jnp.concatenate(qcat, axis=1), jnp.concatenate(kcat, axis=1),
                                       (((1,), (1,)), ((), ())), preferred_element_type=F32))
            to_end = [jnp.exp2(b_last - refs[i]) for i in range(nsub)]
            k_out.append((ke * _group_rows(to_end)).astype(BF16))
            decay.append(to_end[0])
        upd = [lax.dot_general(vb[:, sls[hd]], k_out[hd], (((0,), (0,)), ((), ())),
                               preferred_element_type=F32) for hd in heads]
        return att, q_in, upd, decay, vb, gate

    def stage1():
        return [gates_and_decay(rows) for rows in chunk_rows]

    def stage2(s1):
        return [intra_chunk(*c) for c in s1]

    def stage3(s2):
        st = [st_ref[hd] * keep for hd in heads]
        o = []
        for att, q_in, upd, decay, vb, gate in s2:
            o.append([lax.dot_general(q_in[hd], st[hd].astype(BF16), (((1,), (1,)), ((), ())),
                                      preferred_element_type=F32) for hd in heads])
            st = [decay[hd] * st[hd] + upd[hd] for hd in heads]
        for hd in heads:
            st_ref[hd] = st[hd]
        return o

    def stage4(s2, o):
        for rows, oc, (att, q_in, upd, decay, vb, gate) in zip(chunk_rows, o, s2):
            for hd in heads:
                a = jnp.where(causal, att[hd], 0.0).astype(BF16)
                oh = oc[hd] + jnp.dot(a, vb[:, sls[hd]], preferred_element_type=F32)
                on = _rms(oh, og) * gate[:, sls[hd]]
                o_ref[rows, sls[hd]] = on.astype(BF16)

    return stage1, stage2, stage3, stage4


META_STEP, FULL_STEP, DIAG_STEP, NO_STEP = range(4)


def _attn_steps(lp):
    qi, fin, kind_a, kj_a, kind_b, kj_b = [], [], [], [], [], []
    for i in range(lp // TQ):
        tiles = [(META_STEP if j == 0 else DIAG_STEP if j == i else FULL_STEP, j) for j in range(i + 1)]
        for s in range(0, len(tiles), 2):
            a = tiles[s]
            b = tiles[s + 1] if s + 1 < len(tiles) else (NO_STEP, a[1])
            qi.append(i)
            fin.append(int(s + 2 >= len(tiles)))
            kind_a.append(a[0])
            kj_a.append(a[1])
            kind_b.append(b[0])
            kj_b.append(b[1])
    return [np.asarray(t, np.int32) for t in (qi, fin, kind_a, kj_a, kind_b, kj_b)]


def _attn_body(fixed_shift, qi_ref, fin_ref, kind_a_ref, kj_a_ref, kind_b_ref, kj_b_ref, bound_ref,
               q_ref, ka_ref, vta_ref, kb_ref, vtb_ref, o_ref, *scratch):
    if fixed_shift:
        l_ref, acc_ref = scratch
    else:
        m_ref, l_ref, acc_ref = scratch
    step = pl.program_id(1)
    qi = qi_ref[step]
    bound = bound_ref[0]

    def update(k_ref, vt_ref, qs, keys, mask, first):
        nk, nq = keys.stop - keys.start, qs.stop - qs.start
        for hd in range(HEADS):
            hq = slice(hd * QK_PAD, (hd + 1) * QK_PAD)
            hv = slice(hd * HD, (hd + 1) * HD)
            st = lax.dot_general(k_ref[keys, hq], q_ref[qs, hq], (((1,), (1,)), ((), ())),
                                 preferred_element_type=F32)
            if fixed_shift:
                p = jnp.exp2(st - bound)
                if mask is not None:
                    p = jnp.where(mask, p, 0.0)
            else:
                if mask is not None:
                    st = jnp.where(mask, st, MASK_VALUE)
                m_prev = m_ref[hd, :, qs]
                m_new = jnp.maximum(m_prev, jnp.max(st, axis=0, keepdims=True))
                alpha = jnp.exp2(m_prev - m_new)
                m_ref[hd, :, qs] = m_new
                p = jnp.exp2(st - m_new[0:1, :])
            part = jnp.sum(p.reshape(nk // SUBLANES, SUBLANES, nq), axis=0)
            pv = jnp.dot(vt_ref[hv, keys], p.astype(BF16), preferred_element_type=F32)
            if first and fixed_shift:
                l_ref[hd, :, qs] = part
                acc_ref[hv, qs] = pv
            elif fixed_shift:
                l_ref[hd, :, qs] += part
                acc_ref[hv, qs] += pv
            else:
                l_ref[hd, :, qs] = alpha * l_ref[hd, :, qs] + part
                acc_ref[hv, qs] = alpha[0:1, :] * acc_ref[hv, qs] + pv

    all_q = slice(0, TQ)
    all_keys = slice(0, TK)

    def key_tile(kind, kj, k_ref, vt_ref):
        @pl.when(kind == META_STEP)
        def _():
            if not fixed_shift:
                m_ref[...] = jnp.full_like(m_ref, MASK_VALUE)
                l_ref[...] = jnp.zeros_like(l_ref)
                acc_ref[...] = jnp.zeros_like(acc_ref)
            key = lax.broadcasted_iota(jnp.int32, (HD, TQ), 0) + (kj * TK + TK - HD)
            update(k_ref, vt_ref, all_q, slice(TK - HD, TK), key >= PAD, True)

        @pl.when(kind == FULL_STEP)
        def _():
            update(k_ref, vt_ref, all_q, all_keys, None, False)

        @pl.when(kind == DIAG_STEP)
        def _():
            half = TQ // 2
            key = lax.broadcasted_iota(jnp.int32, (half, TQ), 0)
            qry = lax.broadcasted_iota(jnp.int32, (half, TQ), 1)
            update(k_ref, vt_ref, all_q, slice(0, half), key // CHUNK <= qry // CHUNK, False)
            update(k_ref, vt_ref, slice(half, TQ), slice(half, TK),
                   (key // CHUNK <= qry // CHUNK)[:, :half], False)

    key_tile(kind_a_ref[step], kj_a_ref[step], ka_ref, vta_ref)
    key_tile(kind_b_ref[step], kj_b_ref[step], kb_ref, vtb_ref)

    @pl.when(fin_ref[step] == 1)
    def _():
        row = lax.broadcasted_iota(jnp.int32, (TQ, HD), 0) + qi * TQ
        valid = row >= PAD
        for hd in range(HEADS):
            hv = slice(hd * HD, (hd + 1) * HD)
            ot = acc_ref[hv, :] / jnp.sum(l_ref[hd], axis=0, keepdims=True)
            o_ref[:, hv] = jnp.where(valid, ot.T, 0.0).astype(BF16)


def _attn(q, k, vt, bound, nbatch, lp, fixed_shift):
    r = q.shape[0]
    nq, nk = lp // TQ, lp // TK
    tables = _attn_steps(lp)
    stats = [pltpu.VMEM((HEADS, SUBLANES, TQ), F32)] * (1 if fixed_shift else 2)
    qmap = lambda b, s, qi, fin, kind_a, kj_a, kind_b, kj_b, bd: (b * nq + qi[s], 0)
    grid_spec = pltpu.PrefetchScalarGridSpec(
        num_scalar_prefetch=7,
        grid=(nbatch, len(tables[0])),
        in_specs=[
            pl.BlockSpec((TQ, HEADS * QK_PAD), qmap),
            pl.BlockSpec((TK, HEADS * QK_PAD),
                         lambda b, s, qi, fin, kind_a, kj_a, kind_b, kj_b, bd: (b * nk + kj_a[s], 0)),
            pl.BlockSpec((HW, TK),
                         lambda b, s, qi, fin, kind_a, kj_a, kind_b, kj_b, bd: (0, b * nk + kj_a[s])),
            pl.BlockSpec((TK, HEADS * QK_PAD),
                         lambda b, s, qi, fin, kind_a, kj_a, kind_b, kj_b, bd: (b * nk + kj_b[s], 0)),
            pl.BlockSpec((HW, TK),
                         lambda b, s, qi, fin, kind_a, kj_a, kind_b, kj_b, bd: (0, b * nk + kj_b[s])),
        ],
        out_specs=pl.BlockSpec((TQ, HW), qmap),
        scratch_shapes=stats + [pltpu.VMEM((HW, TQ), F32)],
    )
    return pl.pallas_call(
        functools.partial(_attn_body, fixed_shift),
        grid_spec=grid_spec,
        out_shape=jax.ShapeDtypeStruct((r, HW), BF16),
        compiler_params=_params(("parallel", "arbitrary")),
        name="attn_fixed_shift" if fixed_shift else "attn_online",
    )(*[jnp.asarray(t) for t in tables], bound, q, k, vt, k, vt)


def _mix_tail_body(tpb, ntiles, first_layer, *refs):
    if first_layer:
        meta_ref, refs = refs[0], refs[1:]
    (zh_ref, h_ref, ob_ref, lb_ref, one_m_lb_ref, og_ref, tri_ref, wo_ref, g_ref, wu_ref, wd_ref,
     out_ref, st_ref, oa_ref) = refs
    i = pl.program_id(0)

    @pl.when(i == 0)
    def _():
        st_ref[...] = jnp.zeros_like(st_ref)
        oa_ref[1] = jnp.zeros(oa_ref.shape[1:], BF16)

    keep = (jnp.minimum(i, ntiles - 1) % tpb != 0).astype(F32)
    stage1, stage2, stage3, stage4 = _hgrn_stages(
        zh_ref, lb_ref, one_m_lb_ref, og_ref, tri_ref, st_ref, keep, oa_ref.at[i % 2])

    tt = jnp.maximum(i - 1, 0) % tpb
    h = _stream_tile(h_ref, meta_ref, tt) if first_layer else h_ref[...]
    mix = jnp.dot(oa_ref[(i + 1) % 2], wo_ref[:HW, :], preferred_element_type=F32)
    mix = mix + jnp.dot(ob_ref[...], wo_ref[HW:, :], preferred_element_type=F32)
    s1 = stage1()
    h = h + mix
    hn = _rms(h, g_ref[...]).astype(BF16)
    carried = {}

    def run2():
        carried["s2"] = stage2(s1)

    def run3():
        carried["o"] = stage3(carried["s2"])

    def run4():
        stage4(carried["s2"], carried["o"])

    out_ref[...] = h + _mlp_staggered(hn, wu_ref, wd_ref, between=(run2, run3, run4))


def _mix_tail(zh, h, meta, ob, lb, one_m_lb, og, tri2, wo, g, wu, wd, e, layer, nbatch, lp):
    r = nbatch * lp
    tpb = lp // TM
    ntiles = r // TM
    first_layer = meta is not None
    cur = lambda i: jnp.minimum(i, ntiles - 1)
    prev = lambda i: jnp.maximum(i - 1, 0)
    row = lambda i: (prev(i), 0)
    h_spec = pl.BlockSpec((TM, D_MODEL),
                          (lambda i: (_frame_tile(prev(i), tpb), 0)) if first_layer else row)
    lead_specs = [_const_spec((N_META, D_MODEL))] if first_layer else []
    lead_args = [meta] if first_layer else []
    return pl.pallas_call(
        functools.partial(_mix_tail_body, tpb, ntiles, first_layer),
        grid=(ntiles + 1,),
        in_specs=lead_specs + [
            pl.BlockSpec((TM, 4 * HW), lambda i: (cur(i), 0)),
            h_spec,
            pl.BlockSpec((TM, HW), row),
            _const_spec((1, HW)),
            _const_spec((1, HW)),
            _const_spec((1, HD)),
            _const_spec((CHUNK, 2 * CHUNK)),
            _const_spec((2 * HW, D_MODEL), e),
            _const_spec((1, D_MODEL)),
            _const_spec((D_MODEL, D_FF), layer),
            _const_spec((D_FF, D_MODEL), layer),
        ],
        out_specs=pl.BlockSpec((TM, D_MODEL), row),
        out_shape=jax.ShapeDtypeStruct((r, D_MODEL), F32),
        scratch_shapes=[pltpu.VMEM((HEADS, HD, HD), F32), pltpu.VMEM((2, TM, HW), BF16)],
        compiler_params=_params(("arbitrary",)),
        name="mix_tail",
    )(*lead_args, zh, h, ob, lb, one_m_lb, og, tri2, wo, g, wu, wd)


def _pool_mlp_body(tpb, h_ref, halo_ref, gm_ref, pw_ref, ps_ref, g_ref, wu_ref, wd_ref,
                   out_ref, u_ref, a_ref, b_ref):
    tt = pl.program_id(0) % tpb

    @pl.when(tt == 0)
    def _():
        out_ref[...] = jnp.zeros_like(out_ref)

    @pl.when(tt != 0)
    def _():
        h = h_ref[...]
        gm = gm_ref[...]
        u_ref[0:HALO, :] = _rms(halo_ref[...], gm)
        u = _rms(h, gm)
        u_ref[HALO:, :] = u
        n = TM + HALO
        g = POOL_G
        a_ref[8:n, :] = u_ref[8:n, :] + u_ref[7:n - 1, :]
        b_ref[16:n, g:] = a_ref[16:n, g:] + a_ref[14:n - 2, g:]
        a_ref[24:n, 2 * g:] = b_ref[24:n, 2 * g:] + b_ref[20:n - 4, 2 * g:]
        b_ref[32:n, 3 * g:] = a_ref[32:n, 3 * g:] + a_ref[24:n - 8, 3 * g:]
        wins = (a_ref[HALO:, 0:g], b_ref[HALO:, g:2 * g], a_ref[HALO:, 2 * g:3 * g], b_ref[HALO:, 3 * g:])
        pos = lax.broadcasted_iota(jnp.int32, (TM, g), 0) + (tt * TM - PAD)
        cnt = jnp.maximum(pos + 1, 1).astype(F32)
        ps = ps_ref[...]
        ys = []
        for gi, w in enumerate(POOL_WINDOWS):
            d = wins[gi] / jnp.minimum(cnt, float(w)) - u[:, gi * g:(gi + 1) * g]
            y = jnp.dot(d.astype(BF16), pw_ref[gi], preferred_element_type=F32)
            ys.append(y * ps[:, gi * g:(gi + 1) * g])
        h = h + jnp.concatenate(ys, axis=1)
        out_ref[...] = h + _mlp_staggered(_rms(h, g_ref[...]).astype(BF16), wu_ref, wd_ref)


def _pool_mlp(h, gm, pw, ps, g, wu, wd, o, layer, nbatch, lp, to_frames):
    r = nbatch * lp
    tpb = lp // TM
    row = lambda i: (i, 0)
    out_rows = nbatch * (lp - LEAD) if to_frames else r
    return pl.pallas_call(
        functools.partial(_pool_mlp_body, tpb),
        grid=(r // TM,),
        in_specs=[
            pl.BlockSpec((TM, D_MODEL), row),
            pl.BlockSpec((HALO, D_MODEL), lambda i: (jnp.maximum(i * (TM // HALO) - 1, 0), 0)),
            _const_spec((1, D_MODEL)),
            _const_spec((len(POOL_WINDOWS), POOL_G, POOL_G), o),
            _const_spec((1, D_MODEL)),
            _const_spec((1, D_MODEL)),
            _const_spec((D_MODEL, D_FF), layer),
            _const_spec((D_FF, D_MODEL), layer),
        ],
        out_specs=pl.BlockSpec((TM, D_MODEL), (lambda i: (_frame_tile(i, tpb), 0)) if to_frames else row),
        out_shape=jax.ShapeDtypeStruct((out_rows, D_MODEL), F32),
        scratch_shapes=[pltpu.VMEM((TM + HALO, D_MODEL), F32)] * 3,
        compiler_params=_params(("arbitrary",) if to_frames else ("parallel",)),
        name="pool_mlp",
    )(h, h, gm, pw, ps, g, wu, wd)


def _rope_cols(w):
    half = ROPE // 2
    z = jnp.zeros(w.shape[:-1] + (half,), w.dtype)
    return jnp.concatenate([w[..., :half], z, w[..., half:], z], axis=-1)


def _qk_cols(w):
    w = w.reshape(w.shape[:-1] + (HEADS, QK_DIM))
    w = jnp.concatenate([w[..., :HD], _rope_cols(w[..., HD:])], axis=-1)
    return w.reshape(w.shape[:-2] + (HEADS * QK_PAD,))


def _rope_tables(lp):
    half = ROPE // 2
    inv = ROPE_THETA ** (-np.arange(half, dtype=np.float64) / half)
    pos = np.maximum(np.arange(lp, dtype=np.float64) - PAD, 0.0)
    ang = pos[:, None] * inv[None, :]
    c = jnp.asarray(np.cos(ang).astype(np.float32))
    s = jnp.asarray(np.sin(ang).astype(np.float32))
    z = jnp.zeros_like(c)
    return (jnp.concatenate([c, z, c, z], axis=1), jnp.concatenate([-s, z, s, z], axis=1))


def kernel(x, meta_tokens, mix_norm, mlp_norm, w_mlp_up, w_mlp_down, w_in, hgrn_lb, hgrn_out_norm, mla_q_a_norm, mla_kv_a_norm, w_q_up, w_kv_up, q_norm, k_norm, w_out, pool_w, pool_scale):
    nbatch, seq, _ = x.shape
    depth = mix_norm.shape[0]
    assert seq % TQ == 0 and depth % 2 == 0
    lp = seq + LEAD

    cos_t, sin_t = _rope_tables(lp)
    lb_cum = jnp.cumsum(jax.nn.softmax(hgrn_lb.astype(F32), axis=0), axis=0)
    lower = lb_cum - lb_cum[0:1]
    tri = jnp.tril(jnp.ones((CHUNK, CHUNK), F32)).astype(BF16)
    tri2 = jnp.concatenate([tri, tri], axis=1)

    w_in_l = jnp.concatenate(
        [w_in[..., :4 * HW + Q_RANK + KV_RANK], _rope_cols(w_in[..., 4 * HW + Q_RANK + KV_RANK:])],
        axis=-1).astype(BF16)
    wq_l = _qk_cols(w_q_up).astype(BF16)
    wkv = w_kv_up.reshape(w_kv_up.shape[0], KV_RANK, HEADS, 2 * HD)
    wkv_l = jnp.concatenate([wkv[..., :HD].reshape(-1, KV_RANK, HW),
                             wkv[..., HD:].reshape(-1, KV_RANK, HW)], axis=-1).astype(BF16)
    qn_l = jnp.concatenate([q_norm[:, :HD], _rope_cols(q_norm[:, HD:])], axis=-1) * Q_SCALE
    kn_l = jnp.concatenate([k_norm[:, :HD], _rope_cols(k_norm[:, HD:])], axis=-1)
    wo_l = w_out.astype(BF16)
    wu_l = w_mlp_up.astype(BF16)
    wd_l = w_mlp_down.astype(BF16)
    pw_l = pool_w.astype(BF16)

    h = x.reshape(nbatch * seq, D_MODEL)
    meta = meta_tokens.astype(F32)
    for layer in range(depth):
        if layer % 2 == 0:
            e = layer // 2
            zh, q, k, vt = _inproj(h, meta, mix_norm[layer][None], w_in_l, mla_q_a_norm[e][None],
                                  mla_kv_a_norm[e][None], wq_l, wkv_l, qn_l[e][None],
                                  kn_l[e][None], cos_t, sin_t, e, nbatch, lp)
            bound = (Q_SCALE * QK_DIM * 1.01) * jnp.max(jnp.abs(q_norm[e])) * jnp.max(jnp.abs(k_norm[e]))
            bound = bound.reshape(1).astype(F32)
            ob = lax.cond(bound[0] <= MAX_FIXED_SHIFT,
                          functools.partial(_attn, nbatch=nbatch, lp=lp, fixed_shift=True),
                          functools.partial(_attn, nbatch=nbatch, lp=lp, fixed_shift=False),
                          q, k, vt, bound)
            h = _mix_tail(zh, h, meta, ob, lower[e][None], 1.0 - lower[e][None], hgrn_out_norm[e][None],
                          tri2, wo_l, mlp_norm[layer][None], wu_l, wd_l, e, layer, nbatch, lp)
            meta = None
        else:
            o = layer // 2
            h = _pool_mlp(h, mix_norm[layer][None], pw_l, pool_scale[o][None],
                          mlp_norm[layer][None], wu_l, wd_l, o, layer, nbatch, lp,
                          to_frames=layer == depth - 1)

    return h.reshape(nbatch, seq, D_MODEL)
```

```python
import functools

import numpy as np
import jax
import jax.numpy as jnp
from jax import lax
from jax.experimental import pallas as pl
from jax.experimental.pallas import tpu as pltpu

F32 = jnp.float32
BF16 = jnp.bfloat16

D_MODEL = 1024
D_FF = 4 * D_MODEL
EPS = 1e-6
N_META = 16
CHUNK = 64
HEADS = 4
HD = 128
HW = HEADS * HD
ROPE = 64
QK_DIM = HD + ROPE
QK_PAD = 256
Q_RANK = 256
KV_RANK = 256
ROPE_THETA = 10000.0
POOL_WINDOWS = (2, 4, 8, 16)
POOL_G = D_MODEL // len(POOL_WINDOWS)

LEAD = 1024
PAD = LEAD - N_META
TM = 512
LEAD_TILES = LEAD // TM
TQ = 1024
TK = 1024
SUB = 16
HALO = 32
IN_COLS = 4 * HW + Q_RANK + KV_RANK + ROPE
Q_SCALE = QK_DIM ** -0.5 * float(np.log2(np.e))
MAX_FIXED_SHIFT = 56.0
MASK_VALUE = -1e30
EXP2_CLAMP = 115.0
TINY = 1e-37
SUBLANES = 8
V7X_VMEM_BYTES = 64 * 1024 * 1024
VMEM_LIMIT = V7X_VMEM_BYTES - 8 * 1024 * 1024


def _rms(x, g):
    return x * lax.rsqrt(jnp.mean(x * x, axis=-1, keepdims=True) + EPS) * g


def _silu(x):
    hx = 0.5 * x
    return hx + hx * jnp.tanh(hx)


def _const_spec(shape, layer=None):
    nd = len(shape)
    if layer is None:
        return pl.BlockSpec(shape, lambda *_: (0,) * nd, pipeline_mode=pl.Buffered(1))
    return pl.BlockSpec((None,) + tuple(shape), lambda *_: (layer,) + (0,) * nd, pipeline_mode=pl.Buffered(1))


def _params(sem):
    return pltpu.CompilerParams(dimension_semantics=sem, vmem_limit_bytes=VMEM_LIMIT)


def _frame_tile(i, tpb):
    return (i // tpb) * (tpb - LEAD_TILES) + jnp.maximum(i % tpb - LEAD_TILES, 0)


def _stream_tile(h_ref, meta_ref, tt):
    lead = jnp.concatenate([jnp.zeros((TM - N_META, D_MODEL), F32), meta_ref[...]], axis=0)
    lead = jnp.where(tt == LEAD_TILES - 1, lead, 0.0)
    return jnp.where(tt >= LEAD_TILES, h_ref[...], lead)


MLP_PARTS = 4
MLP_SLAB = D_FF // MLP_PARTS


def _mlp_up(hn, wu_ref, c):
    a = jnp.dot(hn, wu_ref[:, c * MLP_SLAB:(c + 1) * MLP_SLAB], preferred_element_type=F32)
    a = jnp.maximum(a, 0.0)
    return (a * a).astype(BF16)


def _mlp_down(a, wd_ref, c):
    return jnp.dot(a, wd_ref[c * MLP_SLAB:(c + 1) * MLP_SLAB, :], preferred_element_type=F32)


def _mlp_staggered(hn, wu_ref, wd_ref, between=()):
    a = _mlp_up(hn, wu_ref, 0)
    acc = None
    for c in range(MLP_PARTS):
        a_next = _mlp_up(hn, wu_ref, c + 1) if c + 1 < MLP_PARTS else None
        d = _mlp_down(a, wd_ref, c)
        acc = d if acc is None else acc + d
        if c < len(between):
            between[c]()
        a = a_next
    return acc


def _rope(x, c, s):
    return x * c + pltpu.roll(x, HD // 2, axis=1) * s


def _inproj_body(tpb, ntiles, first_layer, *refs):
    if first_layer:
        meta_ref, refs = refs[0], refs[1:]
    (h_ref, g_ref, win_ref, qag_ref, kvag_ref, wq_ref, wkv_ref, qn_ref, kn_ref, cos_ref, sin_ref,
     zh_ref, q_ref, k_ref, vt_ref, lat_ref) = refs
    i = pl.program_id(0)

    @pl.when(i == 0)
    def _():
        lat_ref[1] = jnp.zeros(lat_ref.shape[1:], F32)

    lat = lat_ref[(i + 1) % 2]
    q = jnp.dot(_rms(lat[:, :Q_RANK], qag_ref[...]).astype(BF16), wq_ref[...], preferred_element_type=F32)
    kv = jnp.dot(_rms(lat[:, Q_RANK:Q_RANK + KV_RANK], kvag_ref[...]).astype(BF16), wkv_ref[...],
                 preferred_element_type=F32)
    kr = _rope_cols(lat[:, Q_RANK + KV_RANK:])

    tt = jnp.minimum(i, ntiles - 1) % tpb
    h = _stream_tile(h_ref, meta_ref, tt) if first_layer else h_ref[...]
    z = jnp.dot(_rms(h, g_ref[...]).astype(BF16), win_ref[...], preferred_element_type=F32)
    zh_ref[...] = z[:, :4 * HW]
    lat_ref[i % 2] = z[:, 4 * HW:]

    vt_ref[...] = kv[:, HW:].T.astype(BF16)
    c = cos_ref[...]
    s = sin_ref[...]
    qg = qn_ref[...]
    kg = kn_ref[...]
    kr_ss = jnp.sum(kr * kr, axis=-1, keepdims=True)
    for hd in range(HEADS):
        qa = q[:, hd * QK_PAD:hd * QK_PAD + HD]
        qb = q[:, hd * QK_PAD + HD:(hd + 1) * QK_PAD]
        ss = jnp.sum(qa * qa + qb * qb, axis=-1, keepdims=True)
        inv = lax.rsqrt(ss * (1.0 / QK_DIM) + EPS)
        q_ref[:, hd * QK_PAD:hd * QK_PAD + HD] = (qa * inv * qg[:, :HD]).astype(BF16)
        q_ref[:, hd * QK_PAD + HD:(hd + 1) * QK_PAD] = _rope(qb * inv * qg[:, HD:], c, s).astype(BF16)
        ka = kv[:, hd * HD:(hd + 1) * HD]
        ss = jnp.sum(ka * ka, axis=-1, keepdims=True) + kr_ss
        inv = lax.rsqrt(ss * (1.0 / QK_DIM) + EPS)
        k_ref[:, hd * QK_PAD:hd * QK_PAD + HD] = (ka * inv * kg[:, :HD]).astype(BF16)
        k_ref[:, hd * QK_PAD + HD:(hd + 1) * QK_PAD] = _rope(kr * inv * kg[:, HD:], c, s).astype(BF16)


def _inproj(h, meta, g, win, qag, kvag, wq, wkv, qn, kn, cos_t, sin_t, e, nbatch, lp):
    r = nbatch * lp
    tpb = lp // TM
    ntiles = r // TM
    first_layer = meta is not None
    cur = lambda i: jnp.minimum(i, ntiles - 1)
    prev = lambda i: jnp.maximum(i - 1, 0)
    row = lambda i: (prev(i), 0)
    tab = lambda i: (prev(i) % tpb, 0)
    h_spec = pl.BlockSpec((TM, D_MODEL),
                          (lambda i: (_frame_tile(cur(i), tpb), 0)) if first_layer else (lambda i: (cur(i), 0)))
    lead_specs = [_const_spec((N_META, D_MODEL))] if first_layer else []
    lead_args = [meta] if first_layer else []
    return pl.pallas_call(
        functools.partial(_inproj_body, tpb, ntiles, first_layer),
        grid=(ntiles + 1,),
        in_specs=lead_specs + [
            h_spec,
            _const_spec((1, D_MODEL)),
            _const_spec((D_MODEL, IN_COLS), e),
            _const_spec((1, Q_RANK)),
            _const_spec((1, KV_RANK)),
            _const_spec((Q_RANK, HEADS * QK_PAD), e),
            _const_spec((KV_RANK, 2 * HW), e),
            _const_spec((1, QK_PAD)),
            _const_spec((1, QK_PAD)),
            pl.BlockSpec((TM, HD), tab),
            pl.BlockSpec((TM, HD), tab),
        ],
        out_specs=[
            pl.BlockSpec((TM, 4 * HW), lambda i: (cur(i), 0)),
            pl.BlockSpec((TM, HEADS * QK_PAD), row),
            pl.BlockSpec((TM, HEADS * QK_PAD), row),
            pl.BlockSpec((HW, TM), lambda i: (0, prev(i))),
        ],
        out_shape=[
            jax.ShapeDtypeStruct((r, 4 * HW), F32),
            jax.ShapeDtypeStruct((r, HEADS * QK_PAD), BF16),
            jax.ShapeDtypeStruct((r, HEADS * QK_PAD), BF16),
            jax.ShapeDtypeStruct((HW, r), BF16),
        ],
        scratch_shapes=[pltpu.VMEM((2, TM, IN_COLS - 4 * HW), F32)],
        compiler_params=_params(("arbitrary",)),
        name="inproj",
    )(*lead_args, h, g, win, qag, kvag, wq, wkv, qn, kn, cos_t, sin_t)


def _group_rows(rows):
    return jnp.concatenate([jnp.broadcast_to(r, (SUB, HD)) for r in rows], axis=0)


def _hgrn_stages(zh_ref, lb_ref, one_m_lb_ref, og_ref, tri_ref, st_ref, keep, o_ref):
    tri2 = tri_ref[...]
    lb = lb_ref[...]
    one_m_lb = one_m_lb_ref[...]
    og = og_ref[...]
    tt = lax.broadcasted_iota(jnp.int32, (CHUNK, CHUNK), 0)
    ss_ = lax.broadcasted_iota(jnp.int32, (CHUNK, CHUNK), 1)
    causal = ss_ <= tt
    nsub = CHUNK // SUB
    zero_row = jnp.zeros((1, HD), F32)
    zero_sub = jnp.zeros((SUB, HD), BF16)

    heads = range(HEADS)
    sls = [slice(hd * HD, (hd + 1) * HD) for hd in heads]

    chunk_rows = [slice(c * CHUNK, (c + 1) * CHUNK) for c in range(TM // CHUNK)]

    def gates_and_decay(rows):
        hq = zh_ref[rows, 0:HW]
        hf = zh_ref[rows, HW:2 * HW]
        hi = zh_ref[rows, 2 * HW:3 * HW]
        hg = zh_ref[rows, 3 * HW:4 * HW]
        q = _silu(hq)
        gate = _silu(hg)
        t = jnp.exp(-jnp.abs(hf))
        r = 1.0 / (1.0 + t)
        tr = t * r
        pos = hf >= 0.0
        log2f = jnp.log2(jnp.maximum(lb + one_m_lb * jnp.where(pos, r, tr), TINY))
        k = one_m_lb * jnp.where(pos, tr, r)
        g1 = log2f.astype(BF16)
        g2 = (log2f - g1.astype(F32)).astype(BF16)
        b = jnp.dot(tri2, jnp.concatenate([g1, g2], axis=0), preferred_element_type=F32)
        return q, k, b, hi.astype(BF16), gate

    def intra_chunk(q, k, b, vb, gate):
        att, q_in, k_out, decay = [], [], [], []
        for hd in heads:
            bh = b[:, sls[hd]]
            b_last = bh[CHUNK - 1:CHUNK, :]
            refs = [zero_row] + [bh[i * SUB - 1:i * SUB, :] for i in range(1, nsub)]
            dq = bh - _group_rows(refs)
            qe = q[:, sls[hd]] * jnp.exp2(dq)
            ke = k[:, sls[hd]] * jnp.exp2(jnp.minimum(-dq, EXP2_CLAMP))
            keb = ke.astype(BF16)
            qcat = []
            kcat = []
            for j in range(nsub):
                qcat.append(jnp.concatenate(
                    [zero_sub if i < j else
                     qe[i * SUB:(i + 1) * SUB].astype(BF16) if i == j else
                     (qe[i * SUB:(i + 1) * SUB] * jnp.exp2(refs[i] - refs[j])).astype(BF16)
                     for i in range(nsub)], axis=0))
                kcat.append(jnp.concatenate(
                    [keb[j * SUB:(j + 1) * SUB] if i == j else zero_sub for i in range(nsub)], axis=0))
            q_in.append(qcat[0])
            att.append(lax.dot_general(jnp.concatenate(qcat, axis=1), jnp.concatenate(kcat, axis=1),
                                       (((1,), (1,)), ((), ())), preferred_element_type=F32))
            to_end = [jnp.exp2(b_last - refs[i]) for i in range(nsub)]
            k_out.append((ke * _group_rows(to_end)).astype(BF16))
            decay.append(to_end[0])
        upd = [lax.dot_general(vb[:, sls[hd]], k_out[hd], (((0,), (0,)), ((), ())),
                               preferred_element_type=F32) for hd in heads]
        return att, q_in, upd, decay, vb, gate

    def stage1():
        return [gates_and_decay(rows) for rows in chunk_rows]

    def stage2(s1):
        return [intra_chunk(*c) for c in s1]

    def stage3(s2):
        st = [st_ref[hd] * keep for hd in heads]
        o = []
        for att, q_in, upd, decay, vb, gate in s2:
            o.append([lax.dot_general(q_in[hd], st[hd].astype(BF16), (((1,), (1,)), ((), ())),
                                      preferred_element_type=F32) for hd in heads])
            st = [decay[hd] * st[hd] + upd[hd] for hd in heads]
        for hd in heads:
            st_ref[hd] = st[hd]
        return o

    def stage4(s2, o):
        for rows, oc, (att, q_in, upd, decay, vb, gate) in zip(chunk_rows, o, s2):
            for hd in heads:
                a = jnp.where(causal, att[hd], 0.0).astype(BF16)
                oh = oc[hd] + jnp.dot(a, vb[:, sls[hd]], preferred_element_type=F32)
                on = _rms(oh, og) * gate[:, sls[hd]]
                o_ref[rows, sls[hd]] = on.astype(BF16)

    return stage1, stage2, stage3, stage4


META_STEP, FULL_STEP, DIAG_STEP, NO_STEP = range(4)


def _attn_steps(lp):
    qi, fin, kind_a, kj_a, kind_b, kj_b = [], [], [], [], [], []
    for i in range(lp // TQ):
        tiles = [(META_STEP if j == 0 else DIAG_STEP if j == i else FULL_STEP, j) for j in range(i + 1)]
        for s in range(0, len(tiles), 2):
            a = tiles[s]
            b = tiles[s + 1] if s + 1 < len(tiles) else (NO_STEP, a[1])
            qi.append(i)
            fin.append(int(s + 2 >= len(tiles)))
            kind_a.append(a[0])
            kj_a.append(a[1])
            kind_b.append(b[0])
            kj_b.append(b[1])
    return [np.asarray(t, np.int32) for t in (qi, fin, kind_a, kj_a, kind_b, kj_b)]


def _attn_body(fixed_shift, qi_ref, fin_ref, kind_a_ref, kj_a_ref, kind_b_ref, kj_b_ref, bound_ref,
               q_ref, ka_ref, vta_ref, kb_ref, vtb_ref, o_ref, *scratch):
    if fixed_shift:
        l_ref, acc_ref = scratch
    else:
        m_ref, l_ref, acc_ref = scratch
    step = pl.program_id(1)
    qi = qi_ref[step]
    bound = bound_ref[0]

    def update(k_ref, vt_ref, qs, keys, mask, first):
        nk, nq = keys.stop - keys.start, qs.stop - qs.start
        for hd in range(HEADS):
            hq = slice(hd * QK_PAD, (hd + 1) * QK_PAD)
            hv = slice(hd * HD, (hd + 1) * HD)
            st = lax.dot_general(k_ref[keys, hq], q_ref[qs, hq], (((1,), (1,)), ((), ())),
                                 preferred_element_type=F32)
            if fixed_shift:
                p = jnp.exp2(st - bound)
                if mask is not None:
                    p = jnp.where(mask, p, 0.0)
            else:
                if mask is not None:
                    st = jnp.where(mask, st, MASK_VALUE)
                m_prev = m_ref[hd, :, qs]
                m_new = jnp.maximum(m_prev, jnp.max(st, axis=0, keepdims=True))
                alpha = jnp.exp2(m_prev - m_new)
                m_ref[hd, :, qs] = m_new
                p = jnp.exp2(st - m_new[0:1, :])
            part = jnp.sum(p.reshape(nk // SUBLANES, SUBLANES, nq), axis=0)
            pv = jnp.dot(vt_ref[hv, keys], p.astype(BF16), preferred_element_type=F32)
            if first and fixed_shift:
                l_ref[hd, :, qs] = part
                acc_ref[hv, qs] = pv
            elif fixed_shift:
                l_ref[hd, :, qs] += part
                acc_ref[hv, qs] += pv
            else:
                l_ref[hd, :, qs] = alpha * l_ref[hd, :, qs] + part
                acc_ref[hv, qs] = alpha[0:1, :] * acc_ref[hv, qs] + pv

    all_q = slice(0, TQ)
    all_keys = slice(0, TK)

    def key_tile(kind, kj, k_ref, vt_ref):
        @pl.when(kind == META_STEP)
        def _():
            if not fixed_shift:
                m_ref[...] = jnp.full_like(m_ref, MASK_VALUE)
                l_ref[...] = jnp.zeros_like(l_ref)
                acc_ref[...] = jnp.zeros_like(acc_ref)
            key = lax.broadcasted_iota(jnp.int32, (HD, TQ), 0) + (kj * TK + TK - HD)
            update(k_ref, vt_ref, all_q, slice(TK - HD, TK), key >= PAD, True)

        @pl.when(kind == FULL_STEP)
        def _():
            update(k_ref, vt_ref, all_q, all_keys, None, False)

        @pl.when(kind == DIAG_STEP)
        def _():
            half = TQ // 2
            key = lax.broadcasted_iota(jnp.int32, (half, TQ), 0)
            qry = lax.broadcasted_iota(jnp.int32, (half, TQ), 1)
            update(k_ref, vt_ref, all_q, slice(0, half), key // CHUNK <= qry // CHUNK, False)
            update(k_ref, vt_ref, slice(half, TQ), slice(half, TK),
                   (key // CHUNK <= qry // CHUNK)[:, :half], False)

    key_tile(kind_a_ref[step], kj_a_ref[step], ka_ref, vta_ref)
    key_tile(kind_b_ref[step], kj_b_ref[step], kb_ref, vtb_ref)

    @pl.when(fin_ref[step] == 1)
    def _():
        row = lax.broadcasted_iota(jnp.int32, (TQ, HD), 0) + qi * TQ
        valid = row >= PAD
        for hd in range(HEADS):
            hv = slice(hd * HD, (hd + 1) * HD)
            ot = acc_ref[hv, :] / jnp.sum(l_ref[hd], axis=0, keepdims=True)
            o_ref[:, hv] = jnp.where(valid, ot.T, 0.0).astype(BF16)


def _attn(q, k, vt, bound, nbatch, lp, fixed_shift):
    r = q.shape[0]
    nq, nk = lp // TQ, lp // TK
    tables = _attn_steps(lp)
    stats = [pltpu.VMEM((HEADS, SUBLANES, TQ), F32)] * (1 if fixed_shift else 2)
    qmap = lambda b, s, qi, fin, kind_a, kj_a, kind_b, kj_b, bd: (b * nq + qi[s], 0)
    grid_spec = pltpu.PrefetchScalarGridSpec(
        num_scalar_prefetch=7,
        grid=(nbatch, len(tables[0])),
        in_specs=[
            pl.BlockSpec((TQ, HEADS * QK_PAD), qmap),
            pl.BlockSpec((TK, HEADS * QK_PAD),
                         lambda b, s, qi, fin, kind_a, kj_a, kind_b, kj_b, bd: (b * nk + kj_a[s], 0)),
            pl.BlockSpec((HW, TK),
                         lambda b, s, qi, fin, kind_a, kj_a, kind_b, kj_b, bd: (0, b * nk + kj_a[s])),
            pl.BlockSpec((TK, HEADS * QK_PAD),
                         lambda b, s, qi, fin, kind_a, kj_a, kind_b, kj_b, bd: (b * nk + kj_b[s], 0)),
            pl.BlockSpec((HW, TK),
                         lambda b, s, qi, fin, kind_a, kj_a, kind_b, kj_b, bd: (0, b * nk + kj_b[s])),
        ],
        out_specs=pl.BlockSpec((TQ, HW), qmap),
        scratch_shapes=stats + [pltpu.VMEM((HW, TQ), F32)],
    )
    return pl.pallas_call(
        functools.partial(_attn_body, fixed_shift),
        grid_spec=grid_spec,
        out_shape=jax.ShapeDtypeStruct((r, HW), BF16),
        compiler_params=_params(("parallel", "arbitrary")),
        name="attn_fixed_shift" if fixed_shift else "attn_online",
    )(*[jnp.asarray(t) for t in tables], bound, q, k, vt, k, vt)


def _mix_tail_body(tpb, ntiles, first_layer, *refs):
    if first_layer:
        meta_ref, refs = refs[0], refs[1:]
    (zh_ref, h_ref, ob_ref, lb_ref, one_m_lb_ref, og_ref, tri_ref, wo_ref, g_ref, wu_ref, wd_ref,
     out_ref, st_ref, oa_ref) = refs
    i = pl.program_id(0)

    @pl.when(i == 0)
    def _():
        st_ref[...] = jnp.zeros_like(st_ref)
        oa_ref[1] = jnp.zeros(oa_ref.shape[1:], BF16)

    keep = (jnp.minimum(i, ntiles - 1) % tpb != 0).astype(F32)
    stage1, stage2, stage3, stage4 = _hgrn_stages(
        zh_ref, lb_ref, one_m_lb_ref, og_ref, tri_ref, st_ref, keep, oa_ref.at[i % 2])

    tt = jnp.maximum(i - 1, 0) % tpb
    h = _stream_tile(h_ref, meta_ref, tt) if first_layer else h_ref[...]
    mix = jnp.dot(oa_ref[(i + 1) % 2], wo_ref[:HW, :], preferred_element_type=F32)
    mix = mix + jnp.dot(ob_ref[...], wo_ref[HW:, :], preferred_element_type=F32)
    s1 = stage1()
    h = h + mix
    hn = _rms(h, g_ref[...]).astype(BF16)
    carried = {}

    def run2():
        carried["s2"] = stage2(s1)

    def run3():
        carried["o"] = stage3(carried["s2"])

    def run4():
        stage4(carried["s2"], carried["o"])

    out_ref[...] = h + _mlp_staggered(hn, wu_ref, wd_ref, between=(run2, run3, run4))


def _mix_tail(zh, h, meta, ob, lb, one_m_lb, og, tri2, wo, g, wu, wd, e, layer, nbatch, lp):
    r = nbatch * lp
    tpb = lp // TM
    ntiles = r // TM
    first_layer = meta is not None
    cur = lambda i: jnp.minimum(i, ntiles - 1)
    prev = lambda i: jnp.maximum(i - 1, 0)
    row = lambda i: (prev(i), 0)
    h_spec = pl.BlockSpec((TM, D_MODEL),
                          (lambda i: (_frame_tile(prev(i), tpb), 0)) if first_layer else row)
    lead_specs = [_const_spec((N_META, D_MODEL))] if first_layer else []
    lead_args = [meta] if first_layer else []
    return pl.pallas_call(
        functools.partial(_mix_tail_body, tpb, ntiles, first_layer),
        grid=(ntiles + 1,),
        in_specs=lead_specs + [
            pl.BlockSpec((TM, 4 * HW), lambda i: (cur(i), 0)),
            h_spec,
            pl.BlockSpec((TM, HW), row),
            _const_spec((1, HW)),
            _const_spec((1, HW)),
            _const_spec((1, HD)),
            _const_spec((CHUNK, 2 * CHUNK)),
            _const_spec((2 * HW, D_MODEL), e),
            _const_spec((1, D_MODEL)),
            _const_spec((D_MODEL, D_FF), layer),
            _const_spec((D_FF, D_MODEL), layer),
        ],
        out_specs=pl.BlockSpec((TM, D_MODEL), row),
        out_shape=jax.ShapeDtypeStruct((r, D_MODEL), F32),
        scratch_shapes=[pltpu.VMEM((HEADS, HD, HD), F32), pltpu.VMEM((2, TM, HW), BF16)],
        compiler_params=_params(("arbitrary",)),
        name="mix_tail",
    )(*lead_args, zh, h, ob, lb, one_m_lb, og, tri2, wo, g, wu, wd)


def _pool_mlp_body(tpb, h_ref, halo_ref, gm_ref, pw_ref, ps_ref, g_ref, wu_ref, wd_ref,
                   out_ref, u_ref, a_ref, b_ref):
    tt = pl.program_id(0) % tpb

    @pl.when(tt == 0)
    def _():
        out_ref[...] = jnp.zeros_like(out_ref)

    @pl.when(tt != 0)
    def _():
        h = h_ref[...]
        gm = gm_ref[...]
        u_ref[0:HALO, :] = _rms(halo_ref[...], gm)
        u = _rms(h, gm)
        u_ref[HALO:, :] = u
        n = TM + HALO
        g = POOL_G
        a_ref[8:n, :] = u_ref[8:n, :] + u_ref[7:n - 1, :]
        b_ref[16:n, g:] = a_ref[16:n, g:] + a_ref[14:n - 2, g:]
        a_ref[24:n, 2 * g:] = b_ref[24:n, 2 * g:] + b_ref[20:n - 4, 2 * g:]
        b_ref[32:n, 3 * g:] = a_ref[32:n, 3 * g:] + a_ref[24:n - 8, 3 * g:]
        wins = (a_ref[HALO:, 0:g], b_ref[HALO:, g:2 * g], a_ref[HALO:, 2 * g:3 * g], b_ref[HALO:, 3 * g:])
        pos = lax.broadcasted_iota(jnp.int32, (TM, g), 0) + (tt * TM - PAD)
        cnt = jnp.maximum(pos + 1, 1).astype(F32)
        ps = ps_ref[...]
        ys = []
        for gi, w in enumerate(POOL_WINDOWS):
            d = wins[gi] / jnp.minimum(cnt, float(w)) - u[:, gi * g:(gi + 1) * g]
            y = jnp.dot(d.astype(BF16), pw_ref[gi], preferred_element_type=F32)
            ys.append(y * ps[:, gi * g:(gi + 1) * g])
        h = h + jnp.concatenate(ys, axis=1)
        out_ref[...] = h + _mlp_staggered(_rms(h, g_ref[...]).astype(BF16), wu_ref, wd_ref)


def _pool_mlp(h, gm, pw, ps, g, wu, wd, o, layer, nbatch, lp, to_frames):
    r = nbatch * lp
    tpb = lp // TM
    row = lambda i: (i, 0)
    out_rows = nbatch * (lp - LEAD) if to_frames else r
    return pl.pallas_call(
        functools.partial(_pool_mlp_body, tpb),
        grid=(r // TM,),
        in_specs=[
            pl.BlockSpec((TM, D_MODEL), row),
            pl.BlockSpec((HALO, D_MODEL), lambda i: (jnp.maximum(i * (TM // HALO) - 1, 0), 0)),
            _const_spec((1, D_MODEL)),
            _const_spec((len(POOL_WINDOWS), POOL_G, POOL_G), o),
            _const_spec((1, D_MODEL)),
            _const_spec((1, D_MODEL)),
            _const_spec((D_MODEL, D_FF), layer),
            _const_spec((D_FF, D_MODEL), layer),
        ],
        out_specs=pl.BlockSpec((TM, D_MODEL), (lambda i: (_frame_tile(i, tpb), 0)) if to_frames else row),
        out_shape=jax.ShapeDtypeStruct((out_rows, D_MODEL), F32),
        scratch_shapes=[pltpu.VMEM((TM + HALO, D_MODEL), F32)] * 3,
        compiler_params=_params(("arbitrary",) if to_frames else ("parallel",)),
        name="pool_mlp",
    )(h, h, gm, pw, ps, g, wu, wd)


def _rope_cols(w):
    half = ROPE // 2
    z = jnp.zeros(w.shape[:-1] + (half,), w.dtype)
    return jnp.concatenate([w[..., :half], z, w[..., half:], z], axis=-1)


def _qk_cols(w):
    w = w.reshape(w.shape[:-1] + (HEADS, QK_DIM))
    w = jnp.concatenate([w[..., :HD], _rope_cols(w[..., HD:])], axis=-1)
    return w.reshape(w.shape[:-2] + (HEADS * QK_PAD,))


def _rope_tables(lp):
    half = ROPE // 2
    inv = ROPE_THETA ** (-np.arange(half, dtype=np.float64) / half)
    pos = np.maximum(np.arange(lp, dtype=np.float64) - PAD, 0.0)
    ang = pos[:, None] * inv[None, :]
    c = jnp.asarray(np.cos(ang).astype(np.float32))
    s = jnp.asarray(np.sin(ang).astype(np.float32))
    z = jnp.zeros_like(c)
    return (jnp.concatenate([c, z, c, z], axis=1), jnp.concatenate([-s, z, s, z], axis=1))


def kernel(x, meta_tokens, mix_norm, mlp_norm, w_mlp_up, w_mlp_down, w_in, hgrn_lb, hgrn_out_norm, mla_q_a_norm, mla_kv_a_norm, w_q_up, w_kv_up, q_norm, k_norm, w_out, pool_w, pool_scale):
    nbatch, seq, _ = x.shape
    depth = mix_norm.shape[0]
    assert seq % TQ == 0 and depth % 2 == 0
    lp = seq + LEAD

    cos_t, sin_t = _rope_tables(lp)
    lb_cum = jnp.cumsum(jax.nn.softmax(hgrn_lb.astype(F32), axis=0), axis=0)
    lower = lb_cum - lb_cum[0:1]
    tri = jnp.tril(jnp.ones((CHUNK, CHUNK), F32)).astype(BF16)
    tri2 = jnp.concatenate([tri, tri], axis=1)

    w_in_l = w_in.astype(BF16)
    wq_l = _qk_cols(w_q_up).astype(BF16)
    wkv = w_kv_up.reshape(w_kv_up.shape[0], KV_RANK, HEADS, 2 * HD)
    wkv_l = jnp.concatenate([wkv[..., :HD].reshape(-1, KV_RANK, HW),
                             wkv[..., HD:].reshape(-1, KV_RANK, HW)], axis=-1).astype(BF16)
    qn_l = jnp.concatenate([q_norm[:, :HD], _rope_cols(q_norm[:, HD:])], axis=-1) * Q_SCALE
    kn_l = jnp.concatenate([k_norm[:, :HD], _rope_cols(k_norm[:, HD:])], axis=-1)
    wo_l = w_out.astype(BF16)
    wu_l = w_mlp_up.astype(BF16)
    wd_l = w_mlp_down.astype(BF16)
    pw_l = pool_w.astype(BF16)

    h = x.reshape(nbatch * seq, D_MODEL)
    meta = meta_tokens.astype(F32)
    for layer in range(depth):
        if layer % 2 == 0:
            e = layer // 2
            zh, q, k, vt = _inproj(h, meta, mix_norm[layer][None], w_in_l, mla_q_a_norm[e][None],
                                  mla_kv_a_norm[e][None], wq_l, wkv_l, qn_l[e][None],
                                  kn_l[e][None], cos_t, sin_t, e, nbatch, lp)
            bound = (Q_SCALE * QK_DIM * 1.01) * jnp.max(jnp.abs(q_norm[e])) * jnp.max(jnp.abs(k_norm[e]))
            bound = bound.reshape(1).astype(F32)
            ob = lax.cond(bound[0] <= MAX_FIXED_SHIFT,
                          functools.partial(_attn, nbatch=nbatch, lp=lp, fixed_shift=True),
                          functools.partial(_attn, nbatch=nbatch, lp=lp, fixed_shift=False),
                          q, k, vt, bound)
            h = _mix_tail(zh, h, meta, ob, lower[e][None], 1.0 - lower[e][None], hgrn_out_norm[e][None],
                          tri2, wo_l, mlp_norm[layer][None], wu_l, wd_l, e, layer, nbatch, lp)
            meta = None
        else:
            o = layer // 2
            h = _pool_mlp(h, mix_norm[layer][None], pw_l, pool_scale[o][None],
                          mlp_norm[layer][None], wu_l, wd_l, o, layer, nbatch, lp,
                          to_frames=layer == depth - 1)

    return h.reshape(nbatch, seq, D_MODEL)
```

```python
import functools

import numpy as np
import jax
import jax.numpy as jnp
from jax import lax
from jax.experimental import pallas as pl
from jax.experimental.pallas import tpu as pltpu

F32 = jnp.float32
BF16 = jnp.bfloat16

D_MODEL = 1024
D_FF = 4 * D_MODEL
EPS = 1e-6
N_META = 16
CHUNK = 64
HEADS = 4
HD = 128
HW = HEADS * HD
ROPE = 64
QK_DIM = HD + ROPE
QK_PAD = 256
Q_RANK = 256
KV_RANK = 256
ROPE_THETA = 10000.0
POOL_WINDOWS = (2, 4, 8, 16)
POOL_G = D_MODEL // len(POOL_WINDOWS)

LEAD = 1024
PAD = LEAD - N_META
TM = 512
LEAD_TILES = LEAD // TM
TQ = 1024
TK = 1024
SUB = 16
HALO = 32
IN_COLS = 4 * HW + Q_RANK + KV_RANK + ROPE
Q_SCALE = QK_DIM ** -0.5 * float(np.log2(np.e))
MAX_FIXED_SHIFT = -1.0
MASK_VALUE = -1e30
EXP2_CLAMP = 115.0
TINY = 1e-37
SUBLANES = 8
V7X_VMEM_BYTES = 64 * 1024 * 1024
VMEM_LIMIT = V7X_VMEM_BYTES - 8 * 1024 * 1024


def _rms(x, g):
    return x * lax.rsqrt(jnp.mean(x * x, axis=-1, keepdims=True) + EPS) * g


def _silu(x):
    hx = 0.5 * x
    return hx + hx * jnp.tanh(hx)


def _const_spec(shape, layer=None):
    nd = len(shape)
    if layer is None:
        return pl.BlockSpec(shape, lambda *_: (0,) * nd, pipeline_mode=pl.Buffered(1))
    return pl.BlockSpec((None,) + tuple(shape), lambda *_: (layer,) + (0,) * nd, pipeline_mode=pl.Buffered(1))


def _params(sem):
    return pltpu.CompilerParams(dimension_semantics=sem, vmem_limit_bytes=VMEM_LIMIT)


def _frame_tile(i, tpb):
    return (i // tpb) * (tpb - LEAD_TILES) + jnp.maximum(i % tpb - LEAD_TILES, 0)


def _stream_tile(h_ref, meta_ref, tt):
    lead = jnp.concatenate([jnp.zeros((TM - N_META, D_MODEL), F32), meta_ref[...]], axis=0)
    lead = jnp.where(tt == LEAD_TILES - 1, lead, 0.0)
    return jnp.where(tt >= LEAD_TILES, h_ref[...], lead)


MLP_PARTS = 4
MLP_SLAB = D_FF // MLP_PARTS


def _mlp_up(hn, wu_ref, c):
    a = jnp.dot(hn, wu_ref[:, c * MLP_SLAB:(c + 1) * MLP_SLAB], preferred_element_type=F32)
    a = jnp.maximum(a, 0.0)
    return (a * a).astype(BF16)


def _mlp_down(a, wd_ref, c):
    return jnp.dot(a, wd_ref[c * MLP_SLAB:(c + 1) * MLP_SLAB, :], preferred_element_type=F32)


def _mlp_staggered(hn, wu_ref, wd_ref, between=()):
    a = _mlp_up(hn, wu_ref, 0)
    acc = None
    for c in range(MLP_PARTS):
        a_next = _mlp_up(hn, wu_ref, c + 1) if c + 1 < MLP_PARTS else None
        d = _mlp_down(a, wd_ref, c)
        acc = d if acc is None else acc + d
        if c < len(between):
            between[c]()
        a = a_next
    return acc


def _rope(x, c, s):
    return x * c + pltpu.roll(x, HD // 2, axis=1) * s


def _inproj_body(tpb, ntiles, first_layer, *refs):
    if first_layer:
        meta_ref, refs = refs[0], refs[1:]
    (h_ref, g_ref, win_ref, qag_ref, kvag_ref, wq_ref, wkv_ref, qn_ref, kn_ref, cos_ref, sin_ref,
     zh_ref, q_ref, k_ref, vt_ref, lat_ref) = refs
    i = pl.program_id(0)

    @pl.when(i == 0)
    def _():
        lat_ref[1] = jnp.zeros(lat_ref.shape[1:], F32)

    lat = lat_ref[(i + 1) % 2]
    q = jnp.dot(_rms(lat[:, :Q_RANK], qag_ref[...]).astype(BF16), wq_ref[...], preferred_element_type=F32)
    kv = jnp.dot(_rms(lat[:, Q_RANK:Q_RANK + KV_RANK], kvag_ref[...]).astype(BF16), wkv_ref[...],
                 preferred_element_type=F32)
    kr = _rope_cols(lat[:, Q_RANK + KV_RANK:])

    tt = jnp.minimum(i, ntiles - 1) % tpb
    h = _stream_tile(h_ref, meta_ref, tt) if first_layer else h_ref[...]
    z = jnp.dot(_rms(h, g_ref[...]).astype(BF16), win_ref[...], preferred_element_type=F32)
    zh_ref[...] = z[:, :4 * HW]
    lat_ref[i % 2] = z[:, 4 * HW:]

    vt_ref[...] = kv[:, HW:].T.astype(BF16)
    c = cos_ref[...]
    s = sin_ref[...]
    qg = qn_ref[...]
    kg = kn_ref[...]
    kr_ss = jnp.sum(kr * kr, axis=-1, keepdims=True)
    for hd in range(HEADS):
        qa = q[:, hd * QK_PAD:hd * QK_PAD + HD]
        qb = q[:, hd * QK_PAD + HD:(hd + 1) * QK_PAD]
        ss = jnp.sum(qa * qa + qb * qb, axis=-1, keepdims=True)
        inv = lax.rsqrt(ss * (1.0 / QK_DIM) + EPS)
        q_ref[:, hd * QK_PAD:hd * QK_PAD + HD] = (qa * inv * qg[:, :HD]).astype(BF16)
        q_ref[:, hd * QK_PAD + HD:(hd + 1) * QK_PAD] = _rope(qb * inv * qg[:, HD:], c, s).astype(BF16)
        ka = kv[:, hd * HD:(hd + 1) * HD]
        ss = jnp.sum(ka * ka, axis=-1, keepdims=True) + kr_ss
        inv = lax.rsqrt(ss * (1.0 / QK_DIM) + EPS)
        k_ref[:, hd * QK_PAD:hd * QK_PAD + HD] = (ka * inv * kg[:, :HD]).astype(BF16)
        k_ref[:, hd * QK_PAD + HD:(hd + 1) * QK_PAD] = _rope(kr * inv * kg[:, HD:], c, s).astype(BF16)


def _inproj(h, meta, g, win, qag, kvag, wq, wkv, qn, kn, cos_t, sin_t, e, nbatch, lp):
    r = nbatch * lp
    tpb = lp // TM
    ntiles = r // TM
    first_layer = meta is not None
    cur = lambda i: jnp.minimum(i, ntiles - 1)
    prev = lambda i: jnp.maximum(i - 1, 0)
    row = lambda i: (prev(i), 0)
    tab = lambda i: (prev(i) % tpb, 0)
    h_spec = pl.BlockSpec((TM, D_MODEL),
                          (lambda i: (_frame_tile(cur(i), tpb), 0)) if first_layer else (lambda i: (cur(i), 0)))
    lead_specs = [_const_spec((N_META, D_MODEL))] if first_layer else []
    lead_args = [meta] if first_layer else []
    return pl.pallas_call(
        functools.partial(_inproj_body, tpb, ntiles, first_layer),
        grid=(ntiles + 1,),
        in_specs=lead_specs + [
            h_spec,
            _const_spec((1, D_MODEL)),
            _const_spec((D_MODEL, IN_COLS), e),
            _const_spec((1, Q_RANK)),
            _const_spec((1, KV_RANK)),
            _const_spec((Q_RANK, HEADS * QK_PAD), e),
            _const_spec((KV_RANK, 2 * HW), e),
            _const_spec((1, QK_PAD)),
            _const_spec((1, QK_PAD)),
            pl.BlockSpec((TM, HD), tab),
            pl.BlockSpec((TM, HD), tab),
        ],
        out_specs=[
            pl.BlockSpec((TM, 4 * HW), lambda i: (cur(i), 0)),
            pl.BlockSpec((TM, HEADS * QK_PAD), row),
            pl.BlockSpec((TM, HEADS * QK_PAD), row),
            pl.BlockSpec((HW, TM), lambda i: (0, prev(i))),
        ],
        out_shape=[
            jax.ShapeDtypeStruct((r, 4 * HW), F32),
            jax.ShapeDtypeStruct((r, HEADS * QK_PAD), BF16),
            jax.ShapeDtypeStruct((r, HEADS * QK_PAD), BF16),
            jax.ShapeDtypeStruct((HW, r), BF16),
        ],
        scratch_shapes=[pltpu.VMEM((2, TM, IN_COLS - 4 * HW), F32)],
        compiler_params=_params(("arbitrary",)),
        name="inproj",
    )(*lead_args, h, g, win, qag, kvag, wq, wkv, qn, kn, cos_t, sin_t)


def _group_rows(rows):
    return jnp.concatenate([jnp.broadcast_to(r, (SUB, HD)) for r in rows], axis=0)


def _hgrn_stages(zh_ref, lb_ref, one_m_lb_ref, og_ref, tri_ref, st_ref, keep, o_ref):
    tri2 = tri_ref[...]
    lb = lb_ref[...]
    one_m_lb = one_m_lb_ref[...]
    og = og_ref[...]
    tt = lax.broadcasted_iota(jnp.int32, (CHUNK, CHUNK), 0)
    ss_ = lax.broadcasted_iota(jnp.int32, (CHUNK, CHUNK), 1)
    causal = ss_ <= tt
    nsub = CHUNK // SUB
    zero_row = jnp.zeros((1, HD), F32)
    zero_sub = jnp.zeros((SUB, HD), BF16)

    heads = range(HEADS)
    sls = [slice(hd * HD, (hd + 1) * HD) for hd in heads]

    chunk_rows = [slice(c * CHUNK, (c + 1) * CHUNK) for c in range(TM // CHUNK)]

    def gates_and_decay(rows):
        hq = zh_ref[rows, 0:HW]
        hf = zh_ref[rows, HW:2 * HW]
        hi = zh_ref[rows, 2 * HW:3 * HW]
        hg = zh_ref[rows, 3 * HW:4 * HW]
        q = _silu(hq)
        gate = _silu(hg)
        t = jnp.exp(-jnp.abs(hf))
        r = 1.0 / (1.0 + t)
        tr = t * r
        pos = hf >= 0.0
        log2f = jnp.log2(jnp.maximum(lb + one_m_lb * jnp.where(pos, r, tr), TINY))
        k = one_m_lb * jnp.where(pos, tr, r)
        g1 = log2f.astype(BF16)
        g2 = (log2f - g1.astype(F32)).astype(BF16)
        b = jnp.dot(tri2, jnp.concatenate([g1, g2], axis=0), preferred_element_type=F32)
        return q, k, b, hi.astype(BF16), gate

    def intra_chunk(q, k, b, vb, gate):
        att, q_in, k_out, decay = [], [], [], []
        for hd in heads:
            bh = b[:, sls[hd]]
            b_last = bh[CHUNK - 1:CHUNK, :]
            refs = [zero_row] + [bh[i * SUB - 1:i * SUB, :] for i in range(1, nsub)]
            dq = bh - _group_rows(refs)
            qe = q[:, sls[hd]] * jnp.exp2(dq)
            ke = k[:, sls[hd]] * jnp.exp2(jnp.minimum(-dq, EXP2_CLAMP))
            keb = ke.astype(BF16)
            qcat = []
            kcat = []
            for j in range(nsub):
                qcat.append(jnp.concatenate(
                    [zero_sub if i < j else
                     qe[i * SUB:(i + 1) * SUB].astype(BF16) if i == j else
                     (qe[i * SUB:(i + 1) * SUB] * jnp.exp2(refs[i] - refs[j])).astype(BF16)
                     for i in range(nsub)], axis=0))
                kcat.append(jnp.concatenate(
                    [keb[j * SUB:(j + 1) * SUB] if i == j else zero_sub for i in range(nsub)], axis=0))
            q_in.append(qcat[0])
            att.append(lax.dot_general(jnp.concatenate(qcat, axis=1), jnp.concatenate(kcat, axis=1),
                                       (((1,), (1,)), ((), ())), preferred_element_type=F32))
            to_end = [jnp.exp2(b_last - refs[i]) for i in range(nsub)]
            k_out.append((ke * _group_rows(to_end)).astype(BF16))
            decay.append(to_end[0])
        upd = [lax.dot_general(vb[:, sls[hd]], k_out[hd], (((0,), (0,)), ((), ())),
                               preferred_element_type=F32) for hd in heads]
        return att, q_in, upd, decay, vb, gate

    def stage1():
        return [gates_and_decay(rows) for rows in chunk_rows]

    def stage2(s1):
        return [intra_chunk(*c) for c in s1]

    def stage3(s2):
        st = [st_ref[hd] * keep for hd in heads]
        o = []
        for att, q_in, upd, decay, vb, gate in s2:
            o.append([lax.dot_general(q_in[hd], st[hd].astype(BF16), (((1,), (1,)), ((), ())),
                                      preferred_element_type=F32) for hd in heads])
            st = [decay[hd] * st[hd] + upd[hd] for hd in heads]
        for hd in heads:
            st_ref[hd] = st[hd]
        return o

    def stage4(s2, o):
        for rows, oc, (att, q_in, upd, decay, vb, gate) in zip(chunk_rows, o, s2):
            for hd in heads:
                a = jnp.where(causal, att[hd], 0.0).astype(BF16)
                oh = oc[hd] + jnp.dot(a, vb[:, sls[hd]], preferred_element_type=F32)
                on = _rms(oh, og) * gate[:, sls[hd]]
                o_ref[rows, sls[hd]] = on.astype(BF16)

    return stage1, stage2, stage3, stage4


META_STEP, FULL_STEP, DIAG_STEP, NO_STEP = range(4)


def _attn_steps(lp):
    qi, fin, kind_a, kj_a, kind_b, kj_b = [], [], [], [], [], []
    for i in range(lp // TQ):
        tiles = [(META_STEP if j == 0 else DIAG_STEP if j == i else FULL_STEP, j) for j in range(i + 1)]
        for s in range(0, len(tiles), 2):
            a = tiles[s]
            b = tiles[s + 1] if s + 1 < len(tiles) else (NO_STEP, a[1])
            qi.append(i)
            fin.append(int(s + 2 >= len(tiles)))
            kind_a.append(a[0])
            kj_a.append(a[1])
            kind_b.append(b[0])
            kj_b.append(b[1])
    return [np.asarray(t, np.int32) for t in (qi, fin, kind_a, kj_a, kind_b, kj_b)]


def _attn_body(fixed_shift, qi_ref, fin_ref, kind_a_ref, kj_a_ref, kind_b_ref, kj_b_ref, bound_ref,
               q_ref, ka_ref, vta_ref, kb_ref, vtb_ref, o_ref, *scratch):
    if fixed_shift:
        l_ref, acc_ref = scratch
    else:
        m_ref, l_ref, acc_ref = scratch
    step = pl.program_id(1)
    qi = qi_ref[step]
    bound = bound_ref[0]

    def update(k_ref, vt_ref, qs, keys, mask, first):
        nk, nq = keys.stop - keys.start, qs.stop - qs.start
        for hd in range(HEADS):
            hq = slice(hd * QK_PAD, (hd + 1) * QK_PAD)
            hv = slice(hd * HD, (hd + 1) * HD)
            st = lax.dot_general(k_ref[keys, hq], q_ref[qs, hq], (((1,), (1,)), ((), ())),
                                 preferred_element_type=F32)
            if fixed_shift:
                p = jnp.exp2(st - bound)
                if mask is not None:
                    p = jnp.where(mask, p, 0.0)
            else:
                if mask is not None:
                    st = jnp.where(mask, st, MASK_VALUE)
                m_prev = m_ref[hd, :, qs]
                m_new = jnp.maximum(m_prev, jnp.max(st, axis=0, keepdims=True))
                alpha = jnp.exp2(m_prev - m_new)
                m_ref[hd, :, qs] = m_new
                p = jnp.exp2(st - m_new[0:1, :])
            part = jnp.sum(p.reshape(nk // SUBLANES, SUBLANES, nq), axis=0)
            pv = jnp.dot(vt_ref[hv, keys], p.astype(BF16), preferred_element_type=F32)
            if first and fixed_shift:
                l_ref[hd, :, qs] = part
                acc_ref[hv, qs] = pv
            elif fixed_shift:
                l_ref[hd, :, qs] += part
                acc_ref[hv, qs] += pv
            else:
                l_ref[hd, :, qs] = alpha * l_ref[hd, :, qs] + part
                acc_ref[hv, qs] = alpha[0:1, :] * acc_ref[hv, qs] + pv

    all_q = slice(0, TQ)
    all_keys = slice(0, TK)

    def key_tile(kind, kj, k_ref, vt_ref):
        @pl.when(kind == META_STEP)
        def _():
            if not fixed_shift:
                m_ref[...] = jnp.full_like(m_ref, MASK_VALUE)
                l_ref[...] = jnp.zeros_like(l_ref)
                acc_ref[...] = jnp.zeros_like(acc_ref)
            key = lax.broadcasted_iota(jnp.int32, (HD, TQ), 0) + (kj * TK + TK - HD)
            update(k_ref, vt_ref, all_q, slice(TK - HD, TK), key >= PAD, True)

        @pl.when(kind == FULL_STEP)
        def _():
            update(k_ref, vt_ref, all_q, all_keys, None, False)

        @pl.when(kind == DIAG_STEP)
        def _():
            half = TQ // 2
            key = lax.broadcasted_iota(jnp.int32, (half, TQ), 0)
            qry = lax.broadcasted_iota(jnp.int32, (half, TQ), 1)
            update(k_ref, vt_ref, all_q, slice(0, half), key // CHUNK <= qry // CHUNK, False)
            update(k_ref, vt_ref, slice(half, TQ), slice(half, TK),
                   (key // CHUNK <= qry // CHUNK)[:, :half], False)

    key_tile(kind_a_ref[step], kj_a_ref[step], ka_ref, vta_ref)
    key_tile(kind_b_ref[step], kj_b_ref[step], kb_ref, vtb_ref)

    @pl.when(fin_ref[step] == 1)
    def _():
        row = lax.broadcasted_iota(jnp.int32, (TQ, HD), 0) + qi * TQ
        valid = row >= PAD
        for hd in range(HEADS):
            hv = slice(hd * HD, (hd + 1) * HD)
            ot = acc_ref[hv, :] / jnp.sum(l_ref[hd], axis=0, keepdims=True)
            o_ref[:, hv] = jnp.where(valid, ot.T, 0.0).astype(BF16)


def _attn(q, k, vt, bound, nbatch, lp, fixed_shift):
    r = q.shape[0]
    nq, nk = lp // TQ, lp // TK
    tables = _attn_steps(lp)
    stats = [pltpu.VMEM((HEADS, SUBLANES, TQ), F32)] * (1 if fixed_shift else 2)
    qmap = lambda b, s, qi, fin, kind_a, kj_a, kind_b, kj_b, bd: (b * nq + qi[s], 0)
    grid_spec = pltpu.PrefetchScalarGridSpec(
        num_scalar_prefetch=7,
        grid=(nbatch, len(tables[0])),
        in_specs=[
            pl.BlockSpec((TQ, HEADS * QK_PAD), qmap),
            pl.BlockSpec((TK, HEADS * QK_PAD),
                         lambda b, s, qi, fin, kind_a, kj_a, kind_b, kj_b, bd: (b * nk + kj_a[s], 0)),
            pl.BlockSpec((HW, TK),
                         lambda b, s, qi, fin, kind_a, kj_a, kind_b, kj_b, bd: (0, b * nk + kj_a[s])),
            pl.BlockSpec((TK, HEADS * QK_PAD),
                         lambda b, s, qi, fin, kind_a, kj_a, kind_b, kj_b, bd: (b * nk + kj_b[s], 0)),
            pl.BlockSpec((HW, TK),
                         lambda b, s, qi, fin, kind_a, kj_a, kind_b, kj_b, bd: (0, b * nk + kj_b[s])),
        ],
        out_specs=pl.BlockSpec((TQ, HW), qmap),
        scratch_shapes=stats + [pltpu.VMEM((HW, TQ), F32)],
    )
    return pl.pallas_call(
        functools.partial(_attn_body, fixed_shift),
        grid_spec=grid_spec,
        out_shape=jax.ShapeDtypeStruct((r, HW), BF16),
        compiler_params=_params(("parallel", "arbitrary")),
        name="attn_fixed_shift" if fixed_shift else "attn_online",
    )(*[jnp.asarray(t) for t in tables], bound, q, k, vt, k, vt)


def _mix_tail_body(tpb, ntiles, first_layer, *refs):
    if first_layer:
        meta_ref, refs = refs[0], refs[1:]
    (zh_ref, h_ref, ob_ref, lb_ref, one_m_lb_ref, og_ref, tri_ref, wo_ref, g_ref, wu_ref, wd_ref,
     out_ref, st_ref, oa_ref) = refs
    i = pl.program_id(0)

    @pl.when(i == 0)
    def _():
        st_ref[...] = jnp.zeros_like(st_ref)
        oa_ref[1] = jnp.zeros(oa_ref.shape[1:], BF16)

    keep = (jnp.minimum(i, ntiles - 1) % tpb != 0).astype(F32)
    stage1, stage2, stage3, stage4 = _hgrn_stages(
        zh_ref, lb_ref, one_m_lb_ref, og_ref, tri_ref, st_ref, keep, oa_ref.at[i % 2])

    tt = jnp.maximum(i - 1, 0) % tpb
    h = _stream_tile(h_ref, meta_ref, tt) if first_layer else h_ref[...]
    mix = jnp.dot(oa_ref[(i + 1) % 2], wo_ref[:HW, :], preferred_element_type=F32)
    mix = mix + jnp.dot(ob_ref[...], wo_ref[HW:, :], preferred_element_type=F32)
    s1 = stage1()
    h = h + mix
    hn = _rms(h, g_ref[...]).astype(BF16)
    carried = {}

    def run2():
        carried["s2"] = stage2(s1)

    def run3():
        carried["o"] = stage3(carried["s2"])

    def run4():
        stage4(carried["s2"], carried["o"])

    out_ref[...] = h + _mlp_staggered(hn, wu_ref, wd_ref, between=(run2, run3, run4))


def _mix_tail(zh, h, meta, ob, lb, one_m_lb, og, tri2, wo, g, wu, wd, e, layer, nbatch, lp):
    r = nbatch * lp
    tpb = lp // TM
    ntiles = r // TM
    first_layer = meta is not None
    cur = lambda i: jnp.minimum(i, ntiles - 1)
    prev = lambda i: jnp.maximum(i - 1, 0)
    row = lambda i: (prev(i), 0)
    h_spec = pl.BlockSpec((TM, D_MODEL),
                          (lambda i: (_frame_tile(prev(i), tpb), 0)) if first_layer else row)
    lead_specs = [_const_spec((N_META, D_MODEL))] if first_layer else []
    lead_args = [meta] if first_layer else []
    return pl.pallas_call(
        functools.partial(_mix_tail_body, tpb, ntiles, first_layer),
        grid=(ntiles + 1,),
        in_specs=lead_specs + [
            pl.BlockSpec((TM, 4 * HW), lambda i: (cur(i), 0)),
            h_spec,
            pl.BlockSpec((TM, HW), row),
            _const_spec((1, HW)),
            _const_spec((1, HW)),
            _const_spec((1, HD)),
            _const_spec((CHUNK, 2 * CHUNK)),
            _const_spec((2 * HW, D_MODEL), e),
            _const_spec((1, D_MODEL)),
            _const_spec((D_MODEL, D_FF), layer),
            _const_spec((D_FF, D_MODEL), layer),
        ],
        out_specs=pl.BlockSpec((TM, D_MODEL), row),
        out_shape=jax.ShapeDtypeStruct((r, D_MODEL), F32),
        scratch_shapes=[pltpu.VMEM((HEADS, HD, HD), F32), pltpu.VMEM((2, TM, HW), BF16)],
        compiler_params=_params(("arbitrary",)),
        name="mix_tail",
    )(*lead_args, zh, h, ob, lb, one_m_lb, og, tri2, wo, g, wu, wd)


def _pool_mlp_body(tpb, h_ref, halo_ref, gm_ref, pw_ref, ps_ref, g_ref, wu_ref, wd_ref,
                   out_ref, u_ref, a_ref, b_ref):
    tt = pl.program_id(0) % tpb

    @pl.when(tt == 0)
    def _():
        out_ref[...] = jnp.zeros_like(out_ref)

    @pl.when(tt != 0)
    def _():
        h = h_ref[...]
        gm = gm_ref[...]
        u_ref[0:HALO, :] = _rms(halo_ref[...], gm)
        u = _rms(h, gm)
        u_ref[HALO:, :] = u
        n = TM + HALO
        g = POOL_G
        a_ref[8:n, :] = u_ref[8:n, :] + u_ref[7:n - 1, :]
        b_ref[16:n, g:] = a_ref[16:n, g:] + a_ref[14:n - 2, g:]
        a_ref[24:n, 2 * g:] = b_ref[24:n, 2 * g:] + b_ref[20:n - 4, 2 * g:]
        b_ref[32:n, 3 * g:] = a_ref[32:n, 3 * g:] + a_ref[24:n - 8, 3 * g:]
        wins = (a_ref[HALO:, 0:g], b_ref[HALO:, g:2 * g], a_ref[HALO:, 2 * g:3 * g], b_ref[HALO:, 3 * g:])
        pos = lax.broadcasted_iota(jnp.int32, (TM, g), 0) + (tt * TM - PAD)
        cnt = jnp.maximum(pos + 1, 1).astype(F32)
        ps = ps_ref[...]
        ys = []
        for gi, w in enumerate(POOL_WINDOWS):
            d = wins[gi] / jnp.minimum(cnt, float(w)) - u[:, gi * g:(gi + 1) * g]
            y = jnp.dot(d.astype(BF16), pw_ref[gi], preferred_element_type=F32)
            ys.append(y * ps[:, gi * g:(gi + 1) * g])
        h = h + jnp.concatenate(ys, axis=1)
        out_ref[...] = h + _mlp_staggered(_rms(h, g_ref[...]).astype(BF16), wu_ref, wd_ref)


def _pool_mlp(h, gm, pw, ps, g, wu, wd, o, layer, nbatch, lp, to_frames):
    r = nbatch * lp
    tpb = lp // TM
    row = lambda i: (i, 0)
    out_rows = nbatch * (lp - LEAD) if to_frames else r
    return pl.pallas_call(
        functools.partial(_pool_mlp_body, tpb),
        grid=(r // TM,),
        in_specs=[
            pl.BlockSpec((TM, D_MODEL), row),
            pl.BlockSpec((HALO, D_MODEL), lambda i: (jnp.maximum(i * (TM // HALO) - 1, 0), 0)),
            _const_spec((1, D_MODEL)),
            _const_spec((len(POOL_WINDOWS), POOL_G, POOL_G), o),
            _const_spec((1, D_MODEL)),
            _const_spec((1, D_MODEL)),
            _const_spec((D_MODEL, D_FF), layer),
            _const_spec((D_FF, D_MODEL), layer),
        ],
        out_specs=pl.BlockSpec((TM, D_MODEL), (lambda i: (_frame_tile(i, tpb), 0)) if to_frames else row),
        out_shape=jax.ShapeDtypeStruct((out_rows, D_MODEL), F32),
        scratch_shapes=[pltpu.VMEM((TM + HALO, D_MODEL), F32)] * 3,
        compiler_params=_params(("arbitrary",) if to_frames else ("parallel",)),
        name="pool_mlp",
    )(h, h, gm, pw, ps, g, wu, wd)


def _rope_cols(w):
    half = ROPE // 2
    z = jnp.zeros(w.shape[:-1] + (half,), w.dtype)
    return jnp.concatenate([w[..., :half], z, w[..., half:], z], axis=-1)


def _qk_cols(w):
    w = w.reshape(w.shape[:-1] + (HEADS, QK_DIM))
    w = jnp.concatenate([w[..., :HD], _rope_cols(w[..., HD:])], axis=-1)
    return w.reshape(w.shape[:-2] + (HEADS * QK_PAD,))


def _rope_tables(lp):
    half = ROPE // 2
    inv = ROPE_THETA ** (-np.arange(half, dtype=np.float64) / half)
    pos = np.maximum(np.arange(lp, dtype=np.float64) - PAD, 0.0)
    ang = pos[:, None] * inv[None, :]
    c = jnp.asarray(np.cos(ang).astype(np.float32))
    s = jnp.asarray(np.sin(ang).astype(np.float32))
    z = jnp.zeros_like(c)
    return (jnp.concatenate([c, z, c, z], axis=1), jnp.concatenate([-s, z, s, z], axis=1))


def kernel(x, meta_tokens, mix_norm, mlp_norm, w_mlp_up, w_mlp_down, w_in, hgrn_lb, hgrn_out_norm, mla_q_a_norm, mla_kv_a_norm, w_q_up, w_kv_up, q_norm, k_norm, w_out, pool_w, pool_scale):
    nbatch, seq, _ = x.shape
    depth = mix_norm.shape[0]
    assert seq % TQ == 0 and depth % 2 == 0
    lp = seq + LEAD

    cos_t, sin_t = _rope_tables(lp)
    lb_cum = jnp.cumsum(jax.nn.softmax(hgrn_lb.astype(F32), axis=0), axis=0)
    lower = lb_cum - lb_cum[0:1]
    tri = jnp.tril(jnp.ones((CHUNK, CHUNK), F32)).astype(BF16)
    tri2 = jnp.concatenate([tri, tri], axis=1)

    w_in_l = w_in.astype(BF16)
    wq_l = _qk_cols(w_q_up).astype(BF16)
    wkv = w_kv_up.reshape(w_kv_up.shape[0], KV_RANK, HEADS, 2 * HD)
    wkv_l = jnp.concatenate([wkv[..., :HD].reshape(-1, KV_RANK, HW),
                             wkv[..., HD:].reshape(-1, KV_RANK, HW)], axis=-1).astype(BF16)
    qn_l = jnp.concatenate([q_norm[:, :HD], _rope_cols(q_norm[:, HD:])], axis=-1) * Q_SCALE
    kn_l = jnp.concatenate([k_norm[:, :HD], _rope_cols(k_norm[:, HD:])], axis=-1)
    wo_l = w_out.astype(BF16)
    wu_l = w_mlp_up.astype(BF16)
    wd_l = w_mlp_down.astype(BF16)
    pw_l = pool_w.astype(BF16)

    h = x.reshape(nbatch * seq, D_MODEL)
    meta = meta_tokens.astype(F32)
    for layer in range(depth):
        if layer % 2 == 0:
            e = layer // 2
            zh, q, k, vt = _inproj(h, meta, mix_norm[layer][None], w_in_l, mla_q_a_norm[e][None],
                                  mla_kv_a_norm[e][None], wq_l, wkv_l, qn_l[e][None],
                                  kn_l[e][None], cos_t, sin_t, e, nbatch, lp)
            bound = (Q_SCALE * QK_DIM * 1.01) * jnp.max(jnp.abs(q_norm[e])) * jnp.max(jnp.abs(k_norm[e]))
            bound = bound.reshape(1).astype(F32)
            ob = lax.cond(bound[0] <= MAX_FIXED_SHIFT,
                          functools.partial(_attn, nbatch=nbatch, lp=lp, fixed_shift=True),
                          functools.partial(_attn, nbatch=nbatch, lp=lp, fixed_shift=False),
                          q, k, vt, bound)
            h = _mix_tail(zh, h, meta, ob, lower[e][None], 1.0 - lower[e][None], hgrn_out_norm[e][None],
                          tri2, wo_l, mlp_norm[layer][None], wu_l, wd_l, e, layer, nbatch, lp)
            meta = None
        else:
            o = layer // 2
            h = _pool_mlp(h, mix_norm[layer][None], pw_l, pool_scale[o][None],
                          mlp_norm[layer][None], wu_l, wd_l, o, layer, nbatch, lp,
                          to_frames=layer == depth - 1)

    return h.reshape(nbatch, seq, D_MODEL)
```

```python
import functools

import numpy as np
import jax
import jax.numpy as jnp
from jax import lax
from jax.experimental import pallas as pl
from jax.experimental.pallas import tpu as pltpu

F32 = jnp.float32
BF16 = jnp.bfloat16

D_MODEL = 1024
D_FF = 4 * D_MODEL
EPS = 1e-6
N_META = 16
CHUNK = 64
HEADS = 4
HD = 128
HW = HEADS * HD
ROPE = 64
QK_DIM = HD + ROPE
QK_PAD = 256
Q_RANK = 256
KV_RANK = 256
ROPE_THETA = 10000.0
POOL_WINDOWS = (2, 4, 8, 16)
POOL_G = D_MODEL // len(POOL_WINDOWS)

LEAD = 1024
PAD = LEAD - N_META
TM = 512
LEAD_TILES = LEAD // TM
TQ = 1024
TK = 1024
SUB = 16
HALO = 32
IN_COLS = 4 * HW + Q_RANK + KV_RANK + ROPE
Q_SCALE = QK_DIM ** -0.5 * float(np.log2(np.e))
MAX_FIXED_SHIFT = 56.0
MASK_VALUE = -1e30
EXP2_CLAMP = 115.0
TINY = 1e-37
SUBLANES = 8
V7X_VMEM_BYTES = 64 * 1024 * 1024
VMEM_LIMIT = V7X_VMEM_BYTES - 8 * 1024 * 1024


def _rms(x, g):
    return x * lax.rsqrt(jnp.mean(x * x, axis=-1, keepdims=True) + EPS) * g


def _silu(x):
    hx = 0.5 * x
    return hx + hx * jnp.tanh(hx)


def _const_spec(shape, layer=None):
    nd = len(shape)
    if layer is None:
        return pl.BlockSpec(shape, lambda *_: (0,) * nd, pipeline_mode=pl.Buffered(1))
    return pl.BlockSpec((None,) + tuple(shape), lambda *_: (layer,) + (0,) * nd, pipeline_mode=pl.Buffered(1))


def _params(sem):
    return pltpu.CompilerParams(dimension_semantics=sem, vmem_limit_bytes=VMEM_LIMIT)


def _frame_tile(i, tpb):
    return (i // tpb) * (tpb - LEAD_TILES) + jnp.maximum(i % tpb - LEAD_TILES, 0)


def _stream_tile(h_ref, meta_ref, tt):
    lead = jnp.concatenate([jnp.zeros((TM - N_META, D_MODEL), F32), meta_ref[...]], axis=0)
    lead = jnp.where(tt == LEAD_TILES - 1, lead, 0.0)
    return jnp.where(tt >= LEAD_TILES, h_ref[...], lead)


MLP_PARTS = 4
MLP_SLAB = D_FF // MLP_PARTS


def _mlp_up(hn, wu_ref, c):
    a = jnp.dot(hn, wu_ref[:, c * MLP_SLAB:(c + 1) * MLP_SLAB], preferred_element_type=F32)
    a = jnp.maximum(a, 0.0)
    return (a * a).astype(BF16)


def _mlp_down(a, wd_ref, c):
    return jnp.dot(a, wd_ref[c * MLP_SLAB:(c + 1) * MLP_SLAB, :], preferred_element_type=F32)


def _mlp_staggered(hn, wu_ref, wd_ref, between=()):
    a = _mlp_up(hn, wu_ref, 0)
    acc = None
    for c in range(MLP_PARTS):
        a_next = _mlp_up(hn, wu_ref, c + 1) if c + 1 < MLP_PARTS else None
        d = _mlp_down(a, wd_ref, c)
        acc = d if acc is None else acc + d
        if c < len(between):
            between[c]()
        a = a_next
    return acc


def _rope(x, c, s):
    return x * c + pltpu.roll(x, HD // 2, axis=1) * s


def _inproj_body(tpb, ntiles, first_layer, *refs):
    if first_layer:
        meta_ref, refs = refs[0], refs[1:]
    (h_ref, g_ref, win_ref, qag_ref, kvag_ref, wq_ref, wkv_ref, qn_ref, kn_ref, cos_ref, sin_ref,
     zh_ref, q_ref, k_ref, vt_ref, lat_ref) = refs
    i = pl.program_id(0)

    @pl.when(i == 0)
    def _():
        lat_ref[1] = jnp.zeros(lat_ref.shape[1:], F32)

    lat = lat_ref[(i + 1) % 2]
    q = jnp.dot(_rms(lat[:, :Q_RANK], qag_ref[...]).astype(BF16), wq_ref[...], preferred_element_type=F32)
    kv = jnp.dot(_rms(lat[:, Q_RANK:Q_RANK + KV_RANK], kvag_ref[...]).astype(BF16), wkv_ref[...],
                 preferred_element_type=F32)
    kr = _rope_cols(lat[:, Q_RANK + KV_RANK:])

    tt = jnp.minimum(i, ntiles - 1) % tpb
    h = _stream_tile(h_ref, meta_ref, tt) if first_layer else h_ref[...]
    z = jnp.dot(_rms(h, g_ref[...]).astype(BF16), win_ref[...], preferred_element_type=F32)
    zh_ref[...] = z[:, :4 * HW]
    lat_ref[i % 2] = z[:, 4 * HW:]

    vt_ref[...] = kv[:, HW:].T.astype(BF16)
    c = cos_ref[...]
    s = sin_ref[...]
    qg = qn_ref[...]
    kg = kn_ref[...]
    kr_ss = jnp.sum(kr * kr, axis=-1, keepdims=True)
    for hd in range(HEADS):
        qa = q[:, hd * QK_PAD:hd * QK_PAD + HD]
        qb = q[:, hd * QK_PAD + HD:(hd + 1) * QK_PAD]
        ss = jnp.sum(qa * qa + qb * qb, axis=-1, keepdims=True)
        inv = lax.rsqrt(ss * (1.0 / QK_DIM) + EPS)
        q_ref[:, hd * QK_PAD:hd * QK_PAD + HD] = (qa * inv * qg[:, :HD]).astype(BF16)
        q_ref[:, hd * QK_PAD + HD:(hd + 1) * QK_PAD] = _rope(qb * inv * qg[:, HD:], c, s).astype(BF16)
        ka = kv[:, hd * HD:(hd + 1) * HD]
        ss = jnp.sum(ka * ka, axis=-1, keepdims=True) + kr_ss
        inv = lax.rsqrt(ss * (1.0 / QK_DIM) + EPS)
        k_ref[:, hd * QK_PAD:hd * QK_PAD + HD] = (ka * inv * kg[:, :HD]).astype(BF16)
        k_ref[:, hd * QK_PAD + HD:(hd + 1) * QK_PAD] = _rope(kr * inv * kg[:, HD:], c, s).astype(BF16)


def _inproj(h, meta, g, win, qag, kvag, wq, wkv, qn, kn, cos_t, sin_t, e, nbatch, lp):
    r = nbatch * lp
    tpb = lp // TM
    ntiles = r // TM
    first_layer = meta is not None
    cur = lambda i: jnp.minimum(i, ntiles - 1)
    prev = lambda i: jnp.maximum(i - 1, 0)
    row = lambda i: (prev(i), 0)
    tab = lambda i: (prev(i) % tpb, 0)
    h_spec = pl.BlockSpec((TM, D_MODEL),
                          (lambda i: (_frame_tile(cur(i), tpb), 0)) if first_layer else (lambda i: (cur(i), 0)))
    lead_specs = [_const_spec((N_META, D_MODEL))] if first_layer else []
    lead_args = [meta] if first_layer else []
    return pl.pallas_call(
        functools.partial(_inproj_body, tpb, ntiles, first_layer),
        grid=(ntiles + 1,),
        in_specs=lead_specs + [
            h_spec,
            _const_spec((1, D_MODEL)),
            _const_spec((D_MODEL, IN_COLS), e),
            _const_spec((1, Q_RANK)),
            _const_spec((1, KV_RANK)),
            _const_spec((Q_RANK, HEADS * QK_PAD), e),
            _const_spec((KV_RANK, 2 * HW), e),
            _const_spec((1, QK_PAD)),
            _const_spec((1, QK_PAD)),
            pl.BlockSpec((TM, HD), tab),
            pl.BlockSpec((TM, HD), tab),
        ],
        out_specs=[
            pl.BlockSpec((TM, 4 * HW), lambda i: (cur(i), 0)),
            pl.BlockSpec((TM, HEADS * QK_PAD), row),
            pl.BlockSpec((TM, HEADS * QK_PAD), row),
            pl.BlockSpec((HW, TM), lambda i: (0, prev(i))),
        ],
        out_shape=[
            jax.ShapeDtypeStruct((r, 4 * HW), F32),
            jax.ShapeDtypeStruct((r, HEADS * QK_PAD), BF16),
            jax.ShapeDtypeStruct((r, HEADS * QK_PAD), BF16),
            jax.ShapeDtypeStruct((HW, r), BF16),
        ],
        scratch_shapes=[pltpu.VMEM((2, TM, IN_COLS - 4 * HW), F32)],
        compiler_params=_params(("arbitrary",)),
        name="inproj",
    )(*lead_args, h, g, win, qag, kvag, wq, wkv, qn, kn, cos_t, sin_t)


def _group_rows(rows):
    return jnp.concatenate([jnp.broadcast_to(r, (SUB, HD)) for r in rows], axis=0)


def _hgrn_stages(zh_ref, lb_ref, one_m_lb_ref, og_ref, tri_ref, st_ref, keep, o_ref):
    tri2 = tri_ref[...]
    lb = lb_ref[...]
    one_m_lb = one_m_lb_ref[...]
    og = og_ref[...]
    tt = lax.broadcasted_iota(jnp.int32, (CHUNK, CHUNK), 0)
    ss_ = lax.broadcasted_iota(jnp.int32, (CHUNK, CHUNK), 1)
    causal = ss_ <= tt
    nsub = CHUNK // SUB
    zero_row = jnp.zeros((1, HD), F32)
    zero_sub = jnp.zeros((SUB, HD), BF16)

    heads = range(HEADS)
    sls = [slice(hd * HD, (hd + 1) * HD) for hd in heads]

    chunk_rows = [slice(c * CHUNK, (c + 1) * CHUNK) for c in range(TM // CHUNK)]

    def gates_and_decay(rows):
        hq = zh_ref[rows, 0:HW]
        hf = zh_ref[rows, HW:2 * HW]
        hi = zh_ref[rows, 2 * HW:3 * HW]
        hg = zh_ref[rows, 3 * HW:4 * HW]
        q = _silu(hq)
        gate = _silu(hg)
        t = jnp.exp(-jnp.abs(hf))
        r = 1.0 / (1.0 + t)
        tr = t * r
        pos = hf >= 0.0
        log2f = jnp.log2(jnp.maximum(lb + one_m_lb * jnp.where(pos, r, tr), TINY))
        k = one_m_lb * jnp.where(pos, tr, r)
        g1 = log2f.astype(BF16)
        g2 = (log2f - g1.astype(F32)).astype(BF16)
        b = jnp.dot(tri2, jnp.concatenate([g1, g2], axis=0), preferred_element_type=F32)
        vt = [hi[:, sls[hd]].T.astype(BF16) for hd in heads]
        return q, k, b, vt, gate

    def intra_chunk(q, k, b, vt, gate):
        att, q_in, k_out, decay = [], [], [], []
        for hd in heads:
            bh = b[:, sls[hd]]
            b_last = bh[CHUNK - 1:CHUNK, :]
            refs = [zero_row] + [bh[i * SUB - 1:i * SUB, :] for i in range(1, nsub)]
            dq = bh - _group_rows(refs)
            qe = q[:, sls[hd]] * jnp.exp2(dq)
            ke = k[:, sls[hd]] * jnp.exp2(jnp.minimum(-dq, EXP2_CLAMP))
            keb = ke.astype(BF16)
            qcat = []
            kcat = []
            for j in range(nsub):
                qcat.append(jnp.concatenate(
                    [zero_sub if i < j else
                     qe[i * SUB:(i + 1) * SUB].astype(BF16) if i == j else
                     (qe[i * SUB:(i + 1) * SUB] * jnp.exp2(refs[i] - refs[j])).astype(BF16)
                     for i in range(nsub)], axis=0))
                kcat.append(jnp.concatenate(
                    [keb[j * SUB:(j + 1) * SUB] if i == j else zero_sub for i in range(nsub)], axis=0))
            q_in.append(qcat[0])
            att.append(lax.dot_general(jnp.concatenate(qcat, axis=1), jnp.concatenate(kcat, axis=1),
                                       (((1,), (1,)), ((), ())), preferred_element_type=F32))
            to_end = [jnp.exp2(b_last - refs[i]) for i in range(nsub)]
            k_out.append((ke * _group_rows(to_end)).astype(BF16))
            decay.append(to_end[0])
        upd = [jnp.dot(vt[hd], k_out[hd], preferred_element_type=F32) for hd in heads]
        return att, q_in, upd, decay, vt, gate

    def stage1():
        return [gates_and_decay(rows) for rows in chunk_rows]

    def stage2(s1):
        return [intra_chunk(*c) for c in s1]

    def stage3(s2):
        st = [st_ref[hd] * keep for hd in heads]
        starts = []
        for att, q_in, upd, decay, vt, gate in s2:
            starts.append([st[hd].astype(BF16) for hd in heads])
            st = [decay[hd] * st[hd] + upd[hd] for hd in heads]
        for hd in heads:
            st_ref[hd] = st[hd]
        return starts

    def stage4(s2, starts):
        for rows, st0, (att, q_in, upd, decay, vt, gate) in zip(chunk_rows, starts, s2):
            for hd in heads:
                a = jnp.where(causal, att[hd], 0.0).astype(BF16)
                oh = lax.dot_general(jnp.concatenate([q_in[hd], a], axis=1),
                                     jnp.concatenate([st0[hd], vt[hd]], axis=1),
                                     (((1,), (1,)), ((), ())), preferred_element_type=F32)
                on = _rms(oh, og) * gate[:, sls[hd]]
                o_ref[rows, sls[hd]] = on.astype(BF16)

    return stage1, stage2, stage3, stage4


META_STEP, FULL_STEP, DIAG_STEP, NO_STEP = range(4)


def _attn_steps(lp):
    qi, fin, kind_a, kj_a, kind_b, kj_b = [], [], [], [], [], []
    for i in range(lp // TQ):
        tiles = [(META_STEP if j == 0 else DIAG_STEP if j == i else FULL_STEP, j) for j in range(i + 1)]
        for s in range(0, len(tiles), 2):
            a = tiles[s]
            b = tiles[s + 1] if s + 1 < len(tiles) else (NO_STEP, a[1])
            qi.append(i)
            fin.append(int(s + 2 >= len(tiles)))
            kind_a.append(a[0])
            kj_a.append(a[1])
            kind_b.append(b[0])
            kj_b.append(b[1])
    return [np.asarray(t, np.int32) for t in (qi, fin, kind_a, kj_a, kind_b, kj_b)]


def _attn_body(fixed_shift, qi_ref, fin_ref, kind_a_ref, kj_a_ref, kind_b_ref, kj_b_ref, bound_ref,
               q_ref, ka_ref, vta_ref, kb_ref, vtb_ref, o_ref, *scratch):
    if fixed_shift:
        l_ref, acc_ref = scratch
    else:
        m_ref, l_ref, acc_ref = scratch
    step = pl.program_id(1)
    qi = qi_ref[step]
    bound = bound_ref[0]

    def update(k_ref, vt_ref, qs, keys, mask, first):
        nk, nq = keys.stop - keys.start, qs.stop - qs.start
        for hd in range(HEADS):
            hq = slice(hd * QK_PAD, (hd + 1) * QK_PAD)
            hv = slice(hd * HD, (hd + 1) * HD)
            st = lax.dot_general(k_ref[keys, hq], q_ref[qs, hq], (((1,), (1,)), ((), ())),
                                 preferred_element_type=F32)
            if fixed_shift:
                p = jnp.exp2(st - bound)
                if mask is not None:
                    p = jnp.where(mask, p, 0.0)
            else:
                if mask is not None:
                    st = jnp.where(mask, st, MASK_VALUE)
                m_prev = m_ref[hd, :, qs]
                m_new = jnp.maximum(m_prev, jnp.max(st, axis=0, keepdims=True))
                alpha = jnp.exp2(m_prev - m_new)
                m_ref[hd, :, qs] = m_new
                p = jnp.exp2(st - m_new[0:1, :])
            part = jnp.sum(p.reshape(nk // SUBLANES, SUBLANES, nq), axis=0)
            pv = jnp.dot(vt_ref[hv, keys], p.astype(BF16), preferred_element_type=F32)
            if first and fixed_shift:
                l_ref[hd, :, qs] = part
                acc_ref[hv, qs] = pv
            elif fixed_shift:
                l_ref[hd, :, qs] += part
                acc_ref[hv, qs] += pv
            else:
                l_ref[hd, :, qs] = alpha * l_ref[hd, :, qs] + part
                acc_ref[hv, qs] = alpha[0:1, :] * acc_ref[hv, qs] + pv

    all_q = slice(0, TQ)
    all_keys = slice(0, TK)

    def key_tile(kind, kj, k_ref, vt_ref):
        @pl.when(kind == META_STEP)
        def _():
            if not fixed_shift:
                m_ref[...] = jnp.full_like(m_ref, MASK_VALUE)
                l_ref[...] = jnp.zeros_like(l_ref)
                acc_ref[...] = jnp.zeros_like(acc_ref)
            key = lax.broadcasted_iota(jnp.int32, (HD, TQ), 0) + (kj * TK + TK - HD)
            update(k_ref, vt_ref, all_q, slice(TK - HD, TK), key >= PAD, True)

        @pl.when(kind == FULL_STEP)
        def _():
            update(k_ref, vt_ref, all_q, all_keys, None, False)

        @pl.when(kind == DIAG_STEP)
        def _():
            half = TQ // 2
            key = lax.broadcasted_iota(jnp.int32, (half, TQ), 0)
            qry = lax.broadcasted_iota(jnp.int32, (half, TQ), 1)
            update(k_ref, vt_ref, all_q, slice(0, half), key // CHUNK <= qry // CHUNK, False)
            update(k_ref, vt_ref, slice(half, TQ), slice(half, TK),
                   (key // CHUNK <= qry // CHUNK)[:, :half], False)

    key_tile(kind_a_ref[step], kj_a_ref[step], ka_ref, vta_ref)
    key_tile(kind_b_ref[step], kj_b_ref[step], kb_ref, vtb_ref)

    @pl.when(fin_ref[step] == 1)
    def _():
        row = lax.broadcasted_iota(jnp.int32, (TQ, HD), 0) + qi * TQ
        valid = row >= PAD
        for hd in range(HEADS):
            hv = slice(hd * HD, (hd + 1) * HD)
            ot = acc_ref[hv, :] / jnp.sum(l_ref[hd], axis=0, keepdims=True)
            o_ref[:, hv] = jnp.where(valid, ot.T, 0.0).astype(BF16)


def _attn(q, k, vt, bound, nbatch, lp, fixed_shift):
    r = q.shape[0]
    nq, nk = lp // TQ, lp // TK
    tables = _attn_steps(lp)
    stats = [pltpu.VMEM((HEADS, SUBLANES, TQ), F32)] * (1 if fixed_shift else 2)
    qmap = lambda b, s, qi, fin, kind_a, kj_a, kind_b, kj_b, bd: (b * nq + qi[s], 0)
    grid_spec = pltpu.PrefetchScalarGridSpec(
        num_scalar_prefetch=7,
        grid=(nbatch, len(tables[0])),
        in_specs=[
            pl.BlockSpec((TQ, HEADS * QK_PAD), qmap),
            pl.BlockSpec((TK, HEADS * QK_PAD),
                         lambda b, s, qi, fin, kind_a, kj_a, kind_b, kj_b, bd: (b * nk + kj_a[s], 0)),
            pl.BlockSpec((HW, TK),
                         lambda b, s, qi, fin, kind_a, kj_a, kind_b, kj_b, bd: (0, b * nk + kj_a[s])),
            pl.BlockSpec((TK, HEADS * QK_PAD),
                         lambda b, s, qi, fin, kind_a, kj_a, kind_b, kj_b, bd: (b * nk + kj_b[s], 0)),
            pl.BlockSpec((HW, TK),
                         lambda b, s, qi, fin, kind_a, kj_a, kind_b, kj_b, bd: (0, b * nk + kj_b[s])),
        ],
        out_specs=pl.BlockSpec((TQ, HW), qmap),
        scratch_shapes=stats + [pltpu.VMEM((HW, TQ), F32)],
    )
    return pl.pallas_call(
        functools.partial(_attn_body, fixed_shift),
        grid_spec=grid_spec,
        out_shape=jax.ShapeDtypeStruct((r, HW), BF16),
        compiler_params=_params(("parallel", "arbitrary")),
        name="attn_fixed_shift" if fixed_shift else "attn_online",
    )(*[jnp.asarray(t) for t in tables], bound, q, k, vt, k, vt)


def _mix_tail_body(tpb, ntiles, first_layer, *refs):
    if first_layer:
        meta_ref, refs = refs[0], refs[1:]
    (zh_ref, h_ref, ob_ref, lb_ref, one_m_lb_ref, og_ref, tri_ref, wo_ref, g_ref, wu_ref, wd_ref,
     out_ref, st_ref, oa_ref) = refs
    i = pl.program_id(0)

    @pl.when(i == 0)
    def _():
        st_ref[...] = jnp.zeros_like(st_ref)
        oa_ref[1] = jnp.zeros(oa_ref.shape[1:], BF16)

    keep = (jnp.minimum(i, ntiles - 1) % tpb != 0).astype(F32)
    stage1, stage2, stage3, stage4 = _hgrn_stages(
        zh_ref, lb_ref, one_m_lb_ref, og_ref, tri_ref, st_ref, keep, oa_ref.at[i % 2])

    tt = jnp.maximum(i - 1, 0) % tpb
    h = _stream_tile(h_ref, meta_ref, tt) if first_layer else h_ref[...]
    mix = jnp.dot(oa_ref[(i + 1) % 2], wo_ref[:HW, :], preferred_element_type=F32)
    mix = mix + jnp.dot(ob_ref[...], wo_ref[HW:, :], preferred_element_type=F32)
    s1 = stage1()
    h = h + mix
    hn = _rms(h, g_ref[...]).astype(BF16)
    carried = {}

    def run2():
        carried["s2"] = stage2(s1)

    def run3():
        carried["o"] = stage3(carried["s2"])

    def run4():
        stage4(carried["s2"], carried["o"])

    out_ref[...] = h + _mlp_staggered(hn, wu_ref, wd_ref, between=(run2, run3, run4))


def _mix_tail(zh, h, meta, ob, lb, one_m_lb, og, tri2, wo, g, wu, wd, e, layer, nbatch, lp):
    r = nbatch * lp
    tpb = lp // TM
    ntiles = r // TM
    first_layer = meta is not None
    cur = lambda i: jnp.minimum(i, ntiles - 1)
    prev = lambda i: jnp.maximum(i - 1, 0)
    row = lambda i: (prev(i), 0)
    h_spec = pl.BlockSpec((TM, D_MODEL),
                          (lambda i: (_frame_tile(prev(i), tpb), 0)) if first_layer else row)
    lead_specs = [_const_spec((N_META, D_MODEL))] if first_layer else []
    lead_args = [meta] if first_layer else []
    return pl.pallas_call(
        functools.partial(_mix_tail_body, tpb, ntiles, first_layer),
        grid=(ntiles + 1,),
        in_specs=lead_specs + [
            pl.BlockSpec((TM, 4 * HW), lambda i: (cur(i), 0)),
            h_spec,
            pl.BlockSpec((TM, HW), row),
            _const_spec((1, HW)),
            _const_spec((1, HW)),
            _const_spec((1, HD)),
            _const_spec((CHUNK, 2 * CHUNK)),
            _const_spec((2 * HW, D_MODEL), e),
            _const_spec((1, D_MODEL)),
            _const_spec((D_MODEL, D_FF), layer),
            _const_spec((D_FF, D_MODEL), layer),
        ],
        out_specs=pl.BlockSpec((TM, D_MODEL), row),
        out_shape=jax.ShapeDtypeStruct((r, D_MODEL), F32),
        scratch_shapes=[pltpu.VMEM((HEADS, HD, HD), F32), pltpu.VMEM((2, TM, HW), BF16)],
        compiler_params=_params(("arbitrary",)),
        name="mix_tail",
    )(*lead_args, zh, h, ob, lb, one_m_lb, og, tri2, wo, g, wu, wd)


def _pool_mlp_body(tpb, h_ref, halo_ref, gm_ref, pw_ref, ps_ref, g_ref, wu_ref, wd_ref,
                   out_ref, u_ref, a_ref, b_ref):
    tt = pl.program_id(0) % tpb

    @pl.when(tt == 0)
    def _():
        out_ref[...] = jnp.zeros_like(out_ref)

    @pl.when(tt != 0)
    def _():
        h = h_ref[...]
        gm = gm_ref[...]
        u_ref[0:HALO, :] = _rms(halo_ref[...], gm)
        u = _rms(h, gm)
        u_ref[HALO:, :] = u
        n = TM + HALO
        g = POOL_G
        a_ref[8:n, :] = u_ref[8:n, :] + u_ref[7:n - 1, :]
        b_ref[16:n, g:] = a_ref[16:n, g:] + a_ref[14:n - 2, g:]
        a_ref[24:n, 2 * g:] = b_ref[24:n, 2 * g:] + b_ref[20:n - 4, 2 * g:]
        b_ref[32:n, 3 * g:] = a_ref[32:n, 3 * g:] + a_ref[24:n - 8, 3 * g:]
        wins = (a_ref[HALO:, 0:g], b_ref[HALO:, g:2 * g], a_ref[HALO:, 2 * g:3 * g], b_ref[HALO:, 3 * g:])
        pos = lax.broadcasted_iota(jnp.int32, (TM, g), 0) + (tt * TM - PAD)
        cnt = jnp.maximum(pos + 1, 1).astype(F32)
        ps = ps_ref[...]
        ys = []
        for gi, w in enumerate(POOL_WINDOWS):
            d = wins[gi] / jnp.minimum(cnt, float(w)) - u[:, gi * g:(gi + 1) * g]
            y = jnp.dot(d.astype(BF16), pw_ref[gi], preferred_element_type=F32)
            ys.append(y * ps[:, gi * g:(gi + 1) * g])
        h = h + jnp.concatenate(ys, axis=1)
        out_ref[...] = h + _mlp_staggered(_rms(h, g_ref[...]).astype(BF16), wu_ref, wd_ref)


def _pool_mlp(h, gm, pw, ps, g, wu, wd, o, layer, nbatch, lp, to_frames):
    r = nbatch * lp
    tpb = lp // TM
    row = lambda i: (i, 0)
    out_rows = nbatch * (lp - LEAD) if to_frames else r
    return pl.pallas_call(
        functools.partial(_pool_mlp_body, tpb),
        grid=(r // TM,),
        in_specs=[
            pl.BlockSpec((TM, D_MODEL), row),
            pl.BlockSpec((HALO, D_MODEL), lambda i: (jnp.maximum(i * (TM // HALO) - 1, 0), 0)),
            _const_spec((1, D_MODEL)),
            _const_spec((len(POOL_WINDOWS), POOL_G, POOL_G), o),
            _const_spec((1, D_MODEL)),
            _const_spec((1, D_MODEL)),
            _const_spec((D_MODEL, D_FF), layer),
            _const_spec((D_FF, D_MODEL), layer),
        ],
        out_specs=pl.BlockSpec((TM, D_MODEL), (lambda i: (_frame_tile(i, tpb), 0)) if to_frames else row),
        out_shape=jax.ShapeDtypeStruct((out_rows, D_MODEL), F32),
        scratch_shapes=[pltpu.VMEM((TM + HALO, D_MODEL), F32)] * 3,
        compiler_params=_params(("arbitrary",) if to_frames else ("parallel",)),
        name="pool_mlp",
    )(h, h, gm, pw, ps, g, wu, wd)


def _rope_cols(w):
    half = ROPE // 2
    z = jnp.zeros(w.shape[:-1] + (half,), w.dtype)
    return jnp.concatenate([w[..., :half], z, w[..., half:], z], axis=-1)


def _qk_cols(w):
    w = w.reshape(w.shape[:-1] + (HEADS, QK_DIM))
    w = jnp.concatenate([w[..., :HD], _rope_cols(w[..., HD:])], axis=-1)
    return w.reshape(w.shape[:-2] + (HEADS * QK_PAD,))


def _rope_tables(lp):
    half = ROPE // 2
    inv = ROPE_THETA ** (-np.arange(half, dtype=np.float64) / half)
    pos = np.maximum(np.arange(lp, dtype=np.float64) - PAD, 0.0)
    ang = pos[:, None] * inv[None, :]
    c = jnp.asarray(np.cos(ang).astype(np.float32))
    s = jnp.asarray(np.sin(ang).astype(np.float32))
    z = jnp.zeros_like(c)
    return (jnp.concatenate([c, z, c, z], axis=1), jnp.concatenate([-s, z, s, z], axis=1))


def kernel(x, meta_tokens, mix_norm, mlp_norm, w_mlp_up, w_mlp_down, w_in, hgrn_lb, hgrn_out_norm, mla_q_a_norm, mla_kv_a_norm, w_q_up, w_kv_up, q_norm, k_norm, w_out, pool_w, pool_scale):
    nbatch, seq, _ = x.shape
    depth = mix_norm.shape[0]
    assert seq % TQ == 0 and depth % 2 == 0
    lp = seq + LEAD

    cos_t, sin_t = _rope_tables(lp)
    lb_cum = jnp.cumsum(jax.nn.softmax(hgrn_lb.astype(F32), axis=0), axis=0)
    lower = lb_cum - lb_cum[0:1]
    tri = jnp.tril(jnp.ones((CHUNK, CHUNK), F32)).astype(BF16)
    tri2 = jnp.concatenate([tri, tri], axis=1)

    w_in_l = w_in.astype(BF16)
    wq_l = _qk_cols(w_q_up).astype(BF16)
    wkv = w_kv_up.reshape(w_kv_up.shape[0], KV_RANK, HEADS, 2 * HD)
    wkv_l = jnp.concatenate([wkv[..., :HD].reshape(-1, KV_RANK, HW),
                             wkv[..., HD:].reshape(-1, KV_RANK, HW)], axis=-1).astype(BF16)
    qn_l = jnp.concatenate([q_norm[:, :HD], _rope_cols(q_norm[:, HD:])], axis=-1) * Q_SCALE
    kn_l = jnp.concatenate([k_norm[:, :HD], _rope_cols(k_norm[:, HD:])], axis=-1)
    wo_l = w_out.astype(BF16)
    wu_l = w_mlp_up.astype(BF16)
    wd_l = w_mlp_down.astype(BF16)
    pw_l = pool_w.astype(BF16)

    h = x.reshape(nbatch * seq, D_MODEL)
    meta = meta_tokens.astype(F32)
    for layer in range(depth):
        if layer % 2 == 0:
            e = layer // 2
            zh, q, k, vt = _inproj(h, meta, mix_norm[layer][None], w_in_l, mla_q_a_norm[e][None],
                                  mla_kv_a_norm[e][None], wq_l, wkv_l, qn_l[e][None],
                                  kn_l[e][None], cos_t, sin_t, e, nbatch, lp)
            bound = (Q_SCALE * QK_DIM * 1.01) * jnp.max(jnp.abs(q_norm[e])) * jnp.max(jnp.abs(k_norm[e]))
            bound = bound.reshape(1).astype(F32)
            ob = lax.cond(bound[0] <= MAX_FIXED_SHIFT,
                          functools.partial(_attn, nbatch=nbatch, lp=lp, fixed_shift=True),
                          functools.partial(_attn, nbatch=nbatch, lp=lp, fixed_shift=False),
                          q, k, vt, bound)
            h = _mix_tail(zh, h, meta, ob, lower[e][None], 1.0 - lower[e][None], hgrn_out_norm[e][None],
                          tri2, wo_l, mlp_norm[layer][None], wu_l, wd_l, e, layer, nbatch, lp)
            meta = None
        else:
            o = layer // 2
            h = _pool_mlp(h, mix_norm[layer][None], pw_l, pool_scale[o][None],
                          mlp_norm[layer][None], wu_l, wd_l, o, layer, nbatch, lp,
                          to_frames=layer == depth - 1)

    return h.reshape(nbatch, seq, D_MODEL)
```

```python
import functools

import numpy as np
import jax
import jax.numpy as jnp
from jax import lax
from jax.experimental import pallas as pl
from jax.experimental.pallas import tpu as pltpu

F32 = jnp.float32
BF16 = jnp.bfloat16

D_MODEL = 1024
D_FF = 4 * D_MODEL
EPS = 1e-6
N_META = 16
CHUNK = 64
HEADS = 4
HD = 128
HW = HEADS * HD
ROPE = 64
QK_DIM = HD + ROPE
QK_PAD = 256
Q_RANK = 256
KV_RANK = 256
ROPE_THETA = 10000.0
POOL_WINDOWS = (2, 4, 8, 16)
POOL_G = D_MODEL // len(POOL_WINDOWS)

LEAD = 1024
PAD = LEAD - N_META
TM = 512
LEAD_TILES = LEAD // TM
TQ = 1024
TK = 1024
SUB = 16
HALO = 32
IN_COLS = 4 * HW + Q_RANK + KV_RANK + ROPE
Q_SCALE = QK_DIM ** -0.5 * float(np.log2(np.e))
MAX_FIXED_SHIFT = 56.0
MASK_VALUE = -1e30
EXP2_CLAMP = 115.0
TINY = 1e-37
SUBLANES = 8
V7X_VMEM_BYTES = 64 * 1024 * 1024
VMEM_LIMIT = V7X_VMEM_BYTES - 8 * 1024 * 1024


def _rms(x, g):
    return x * lax.rsqrt(jnp.mean(x * x, axis=-1, keepdims=True) + EPS) * g


def _silu(x):
    hx = 0.5 * x
    return hx + hx * jnp.tanh(hx)


def _const_spec(shape, layer=None):
    nd = len(shape)
    if layer is None:
        return pl.BlockSpec(shape, lambda *_: (0,) * nd, pipeline_mode=pl.Buffered(1))
    return pl.BlockSpec((None,) + tuple(shape), lambda *_: (layer,) + (0,) * nd, pipeline_mode=pl.Buffered(1))


def _params(sem):
    return pltpu.CompilerParams(dimension_semantics=sem, vmem_limit_bytes=VMEM_LIMIT)


def _frame_tile(i, tpb):
    return (i // tpb) * (tpb - LEAD_TILES) + jnp.maximum(i % tpb - LEAD_TILES, 0)


def _stream_tile(h_ref, meta_ref, tt):
    lead = jnp.concatenate([jnp.zeros((TM - N_META, D_MODEL), F32), meta_ref[...]], axis=0)
    lead = jnp.where(tt == LEAD_TILES - 1, lead, 0.0)
    return jnp.where(tt >= LEAD_TILES, h_ref[...], lead)


MLP_PARTS = 4
MLP_SLAB = D_FF // MLP_PARTS


def _mlp_up(hn, wu_ref, c):
    a = jnp.dot(hn, wu_ref[:, c * MLP_SLAB:(c + 1) * MLP_SLAB], preferred_element_type=F32)
    a = jnp.maximum(a, 0.0)
    return (a * a).astype(BF16)


def _mlp_down(a, wd_ref, c):
    return jnp.dot(a, wd_ref[c * MLP_SLAB:(c + 1) * MLP_SLAB, :], preferred_element_type=F32)


def _mlp_staggered(hn, wu_ref, wd_ref, between=()):
    a = _mlp_up(hn, wu_ref, 0)
    acc = None
    for c in range(MLP_PARTS):
        a_next = _mlp_up(hn, wu_ref, c + 1) if c + 1 < MLP_PARTS else None
        d = _mlp_down(a, wd_ref, c)
        acc = d if acc is None else acc + d
        if c < len(between):
            between[c]()
        a = a_next
    return acc


def _rope(x, c, s):
    return x * c + pltpu.roll(x, HD // 2, axis=1) * s


def _inproj_body(tpb, ntiles, first_layer, *refs):
    if first_layer:
        meta_ref, refs = refs[0], refs[1:]
    (h_ref, g_ref, win_ref, qag_ref, kvag_ref, wq_ref, wkv_ref, qn_ref, kn_ref, cos_ref, sin_ref,
     zh_ref, q_ref, k_ref, vt_ref, lat_ref) = refs
    i = pl.program_id(0)

    @pl.when(i == 0)
    def _():
        lat_ref[1] = jnp.zeros(lat_ref.shape[1:], F32)

    lat = lat_ref[(i + 1) % 2]
    q = jnp.dot(_rms(lat[:, :Q_RANK], qag_ref[...]).astype(BF16), wq_ref[...], preferred_element_type=F32)
    kv = jnp.dot(_rms(lat[:, Q_RANK:Q_RANK + KV_RANK], kvag_ref[...]).astype(BF16), wkv_ref[...],
                 preferred_element_type=F32)
    kr = _rope_cols(lat[:, Q_RANK + KV_RANK:])

    tt = jnp.minimum(i, ntiles - 1) % tpb
    h = _stream_tile(h_ref, meta_ref, tt) if first_layer else h_ref[...]
    z = jnp.dot(_rms(h, g_ref[...]).astype(BF16), win_ref[...], preferred_element_type=F32)
    zh_ref[...] = z[:, :4 * HW]
    lat_ref[i % 2] = z[:, 4 * HW:]

    vt_ref[...] = kv[:, HW:].T.astype(BF16)
    c = cos_ref[...]
    s = sin_ref[...]
    qg = qn_ref[...]
    kg = kn_ref[...]
    kr_ss = jnp.sum(kr * kr, axis=-1, keepdims=True)
    for hd in range(HEADS):
        qa = q[:, hd * QK_PAD:hd * QK_PAD + HD]
        qb = q[:, hd * QK_PAD + HD:(hd + 1) * QK_PAD]
        ss = jnp.sum(qa * qa + qb * qb, axis=-1, keepdims=True)
        inv = lax.rsqrt(ss * (1.0 / QK_DIM) + EPS)
        q_ref[:, hd * QK_PAD:hd * QK_PAD + HD] = (qa * inv * qg[:, :HD]).astype(BF16)
        q_ref[:, hd * QK_PAD + HD:(hd + 1) * QK_PAD] = _rope(qb * inv * qg[:, HD:], c, s).astype(BF16)
        ka = kv[:, hd * HD:(hd + 1) * HD]
        ss = jnp.sum(ka * ka, axis=-1, keepdims=True) + kr_ss
        inv = lax.rsqrt(ss * (1.0 / QK_DIM) + EPS)
        k_ref[:, hd * QK_PAD:hd * QK_PAD + HD] = (ka * inv * kg[:, :HD]).astype(BF16)
        k_ref[:, hd * QK_PAD + HD:(hd + 1) * QK_PAD] = _rope(kr * inv * kg[:, HD:], c, s).astype(BF16)


def _inproj(h, meta, g, win, qag, kvag, wq, wkv, qn, kn, cos_t, sin_t, e, nbatch, lp):
    r = nbatch * lp
    tpb = lp // TM
    ntiles = r // TM
    first_layer = meta is not None
    cur = lambda i: jnp.minimum(i, ntiles - 1)
    prev = lambda i: jnp.maximum(i - 1, 0)
    row = lambda i: (prev(i), 0)
    tab = lambda i: (prev(i) % tpb, 0)
    h_spec = pl.BlockSpec((TM, D_MODEL),
                          (lambda i: (_frame_tile(cur(i), tpb), 0)) if first_layer else (lambda i: (cur(i), 0)))
    lead_specs = [_const_spec((N_META, D_MODEL))] if first_layer else []
    lead_args = [meta] if first_layer else []
    return pl.pallas_call(
        functools.partial(_inproj_body, tpb, ntiles, first_layer),
        grid=(ntiles + 1,),
        in_specs=lead_specs + [
            h_spec,
            _const_spec((1, D_MODEL)),
            _const_spec((D_MODEL, IN_COLS), e),
            _const_spec((1, Q_RANK)),
            _const_spec((1, KV_RANK)),
            _const_spec((Q_RANK, HEADS * QK_PAD), e),
            _const_spec((KV_RANK, 2 * HW), e),
            _const_spec((1, QK_PAD)),
            _const_spec((1, QK_PAD)),
            pl.BlockSpec((TM, HD), tab),
            pl.BlockSpec((TM, HD), tab),
        ],
        out_specs=[
            pl.BlockSpec((TM, 4 * HW), lambda i: (cur(i), 0)),
            pl.BlockSpec((TM, HEADS * QK_PAD), row),
            pl.BlockSpec((TM, HEADS * QK_PAD), row),
            pl.BlockSpec((HW, TM), lambda i: (0, prev(i))),
        ],
        out_shape=[
            jax.ShapeDtypeStruct((r, 4 * HW), F32),
            jax.ShapeDtypeStruct((r, HEADS * QK_PAD), BF16),
            jax.ShapeDtypeStruct((r, HEADS * QK_PAD), BF16),
            jax.ShapeDtypeStruct((HW, r), BF16),
        ],
        scratch_shapes=[pltpu.VMEM((2, TM, IN_COLS - 4 * HW), F32)],
        compiler_params=_params(("arbitrary",)),
        name="inproj",
    )(*lead_args, h, g, win, qag, kvag, wq, wkv, qn, kn, cos_t, sin_t)


def _group_rows(rows):
    return jnp.concatenate([jnp.broadcast_to(r, (SUB, HD)) for r in rows], axis=0)


def _hgrn_stages(zh_ref, lb_ref, one_m_lb_ref, og_ref, tri_ref, st_ref, keep, o_ref):
    tri2 = tri_ref[...]
    lb = lb_ref[...]
    one_m_lb = one_m_lb_ref[...]
    og = og_ref[...]
    tt = lax.broadcasted_iota(jnp.int32, (CHUNK, CHUNK), 0)
    ss_ = lax.broadcasted_iota(jnp.int32, (CHUNK, CHUNK), 1)
    causal = ss_ <= tt
    nsub = CHUNK // SUB
    zero_row = jnp.zeros((1, HD), F32)
    zero_sub = jnp.zeros((SUB, HD), BF16)

    heads = range(HEADS)
    sls = [slice(hd * HD, (hd + 1) * HD) for hd in heads]

    chunk_rows = [slice(c * CHUNK, (c + 1) * CHUNK) for c in range(TM // CHUNK)]

    def gates_and_decay(rows):
        hq = zh_ref[rows, 0:HW]
        hf = zh_ref[rows, HW:2 * HW]
        hi = zh_ref[rows, 2 * HW:3 * HW]
        hg = zh_ref[rows, 3 * HW:4 * HW]
        q = _silu(hq)
        gate = _silu(hg)
        t = jnp.exp(-jnp.abs(hf))
        r = 1.0 / (1.0 + t)
        tr = t * r
        pos = hf >= 0.0
        log2f = jnp.log2(jnp.maximum(lb + one_m_lb * jnp.where(pos, r, tr), TINY))
        k = one_m_lb * jnp.where(pos, tr, r)
        g1 = log2f.astype(BF16)
        g2 = (log2f - g1.astype(F32)).astype(BF16)
        b = jnp.dot(tri2, jnp.concatenate([g1, g2], axis=0), preferred_element_type=F32)
        vt = [hi[:, sls[hd]].T.astype(BF16) for hd in heads]
        return q, k, b, vt, gate

    def intra_chunk(q, k, b, vt, gate):
        att, q_in, k_out, decay = [], [], [], []
        for hd in heads:
            bh = b[:, sls[hd]]
            b_last = bh[CHUNK - 1:CHUNK, :]
            refs = [zero_row] + [bh[i * SUB - 1:i * SUB, :] for i in range(1, nsub)]
            dq = bh - _group_rows(refs)
            qe = q[:, sls[hd]] * jnp.exp2(dq)
            ke = k[:, sls[hd]] * jnp.exp2(jnp.minimum(-dq, EXP2_CLAMP))
            keb = ke.astype(BF16)
            qcat = []
            kcat = []
            for j in range(nsub):
                qcat.append(jnp.concatenate(
                    [zero_sub if i < j else
                     qe[i * SUB:(i + 1) * SUB].astype(BF16) if i == j else
                     (qe[i * SUB:(i + 1) * SUB] * jnp.exp2(refs[i] - refs[j])).astype(BF16)
                     for i in range(nsub)], axis=0))
                kcat.append(jnp.concatenate(
                    [keb[j * SUB:(j + 1) * SUB] if i == j else zero_sub for i in range(nsub)], axis=0))
            q_in.append(qcat[0])
            att.append(lax.dot_general(jnp.concatenate(qcat, axis=1), jnp.concatenate(kcat, axis=1),
                                       (((1,), (1,)), ((), ())), preferred_element_type=F32))
            to_end = [jnp.exp2(b_last - refs[i]) for i in range(nsub)]
            k_out.append((ke * _group_rows(to_end)).astype(BF16))
            decay.append(to_end[0])
        upd = [jnp.dot(vt[hd], k_out[hd], preferred_element_type=F32) for hd in heads]
        return att, q_in, upd, decay, vt, gate

    def stage1():
        return [gates_and_decay(rows) for rows in chunk_rows]

    def stage2(s1):
        return [intra_chunk(*c) for c in s1]

    def stage3(s2):
        st = [st_ref[hd] * keep for hd in heads]
        starts = []
        for att, q_in, upd, decay, vt, gate in s2:
            starts.append([st[hd].astype(BF16) for hd in heads])
            st = [decay[hd] * st[hd] + upd[hd] for hd in heads]
        for hd in heads:
            st_ref[hd] = st[hd]
        return starts

    def stage4(s2, starts):
        for rows, st0, (att, q_in, upd, decay, vt, gate) in zip(chunk_rows, starts, s2):
            for hd in heads:
                a = jnp.where(causal, att[hd], 0.0).astype(BF16)
                oh = lax.dot_general(jnp.concatenate([q_in[hd], a], axis=1),
                                     jnp.concatenate([st0[hd], vt[hd]], axis=1),
                                     (((1,), (1,)), ((), ())), preferred_element_type=F32)
                on = _rms(oh, og) * gate[:, sls[hd]]
                o_ref[rows, sls[hd]] = on.astype(BF16)

    return stage1, stage2, stage3, stage4


META_STEP, FULL_STEP, DIAG_STEP, NO_STEP = range(4)


def _attn_steps(lp):
    qi, fin, kind_a, kj_a, kind_b, kj_b = [], [], [], [], [], []
    for i in range(lp // TQ):
        tiles = [(META_STEP if j == 0 else DIAG_STEP if j == i else FULL_STEP, j) for j in range(i + 1)]
        for s in range(0, len(tiles), 2):
            a = tiles[s]
            b = tiles[s + 1] if s + 1 < len(tiles) else (NO_STEP, a[1])
            qi.append(i)
            fin.append(int(s + 2 >= len(tiles)))
            kind_a.append(a[0])
            kj_a.append(a[1])
            kind_b.append(b[0])
            kj_b.append(b[1])
    return [np.asarray(t, np.int32) for t in (qi, fin, kind_a, kj_a, kind_b, kj_b)]


def _attn_body(fixed_shift, qi_ref, fin_ref, kind_a_ref, kj_a_ref, kind_b_ref, kj_b_ref, bound_ref,
               q_ref, ka_ref, vta_ref, kb_ref, vtb_ref, o_ref, *scratch):
    if fixed_shift:
        l_ref, acc_ref = scratch
    else:
        m_ref, l_ref, acc_ref = scratch
    step = pl.program_id(1)
    qi = qi_ref[step]
    bound = bound_ref[0]

    def update(k_ref, vt_ref, qs, keys, mask, first):
        nk, nq = keys.stop - keys.start, qs.stop - qs.start
        for hd in range(HEADS):
            hq = slice(hd * QK_PAD, (hd + 1) * QK_PAD)
            hv = slice(hd * HD, (hd + 1) * HD)
            st = lax.dot_general(k_ref[keys, hq], q_ref[qs, hq], (((1,), (1,)), ((), ())),
                                 preferred_element_type=F32)
            if fixed_shift:
                p = jnp.exp2(st - bound)
                if mask is not None:
                    p = jnp.where(mask, p, 0.0)
            else:
                if mask is not None:
                    st = jnp.where(mask, st, MASK_VALUE)
                m_prev = m_ref[hd, :, qs]
                m_new = jnp.maximum(m_prev, jnp.max(st, axis=0, keepdims=True))
                alpha = jnp.exp2(m_prev - m_new)
                m_ref[hd, :, qs] = m_new
                p = jnp.exp2(st - m_new[0:1, :])
            part = jnp.sum(p.reshape(nk // SUBLANES, SUBLANES, nq), axis=0)
            pv = jnp.dot(vt_ref[hv, keys], p.astype(BF16), preferred_element_type=F32)
            if first and fixed_shift:
                l_ref[hd, :, qs] = part
                acc_ref[hv, qs] = pv
            elif fixed_shift:
                l_ref[hd, :, qs] += part
                acc_ref[hv, qs] += pv
            else:
                l_ref[hd, :, qs] = alpha * l_ref[hd, :, qs] + part
                acc_ref[hv, qs] = alpha[0:1, :] * acc_ref[hv, qs] + pv

    all_q = slice(0, TQ)
    all_keys = slice(0, TK)

    def key_tile(kind, kj, k_ref, vt_ref):
        @pl.when(kind == META_STEP)
        def _():
            if not fixed_shift:
                m_ref[...] = jnp.full_like(m_ref, MASK_VALUE)
                l_ref[...] = jnp.zeros_like(l_ref)
                acc_ref[...] = jnp.zeros_like(acc_ref)
            key = lax.broadcasted_iota(jnp.int32, (HD, TQ), 0) + (kj * TK + TK - HD)
            update(k_ref, vt_ref, all_q, slice(TK - HD, TK), key >= PAD, True)

        @pl.when(kind == FULL_STEP)
        def _():
            update(k_ref, vt_ref, all_q, all_keys, None, False)

        @pl.when(kind == DIAG_STEP)
        def _():
            half = TQ // 2
            key = lax.broadcasted_iota(jnp.int32, (half, TQ), 0)
            qry = lax.broadcasted_iota(jnp.int32, (half, TQ), 1)
            update(k_ref, vt_ref, all_q, slice(0, half), key // CHUNK <= qry // CHUNK, False)
            update(k_ref, vt_ref, slice(half, TQ), slice(half, TK),
                   (key // CHUNK <= qry // CHUNK)[:, :half], False)

    key_tile(kind_a_ref[step], kj_a_ref[step], ka_ref, vta_ref)
    key_tile(kind_b_ref[step], kj_b_ref[step], kb_ref, vtb_ref)

    @pl.when(fin_ref[step] == 1)
    def _():
        row = lax.broadcasted_iota(jnp.int32, (TQ, HD), 0) + qi * TQ
        valid = row >= PAD
        for hd in range(HEADS):
            hv = slice(hd * HD, (hd + 1) * HD)
            ot = acc_ref[hv, :] / jnp.sum(l_ref[hd], axis=0, keepdims=True)
            o_ref[:, hv] = jnp.where(valid, ot.T, 0.0).astype(BF16)


def _attn(q, k, vt, bound, nbatch, lp, fixed_shift):
    r = q.shape[0]
    nq, nk = lp // TQ, lp // TK
    tables = _attn_steps(lp)
    stats = [pltpu.VMEM((HEADS, SUBLANES, TQ), F32)] * (1 if fixed_shift else 2)
    qmap = lambda b, s, qi, fin, kind_a, kj_a, kind_b, kj_b, bd: (b * nq + qi[s], 0)
    grid_spec = pltpu.PrefetchScalarGridSpec(
        num_scalar_prefetch=7,
        grid=(nbatch, len(tables[0])),
        in_specs=[
            pl.BlockSpec((TQ, HEADS * QK_PAD), qmap),
            pl.BlockSpec((TK, HEADS * QK_PAD),
                         lambda b, s, qi, fin, kind_a, kj_a, kind_b, kj_b, bd: (b * nk + kj_a[s], 0)),
            pl.BlockSpec((HW, TK),
                         lambda b, s, qi, fin, kind_a, kj_a, kind_b, kj_b, bd: (0, b * nk + kj_a[s])),
            pl.BlockSpec((TK, HEADS * QK_PAD),
                         lambda b, s, qi, fin, kind_a, kj_a, kind_b, kj_b, bd: (b * nk + kj_b[s], 0)),
            pl.BlockSpec((HW, TK),
                         lambda b, s, qi, fin, kind_a, kj_a, kind_b, kj_b, bd: (0, b * nk + kj_b[s])),
        ],
        out_specs=pl.BlockSpec((TQ, HW), qmap),
        scratch_shapes=stats + [pltpu.VMEM((HW, TQ), F32)],
    )
    return pl.pallas_call(
        functools.partial(_attn_body, fixed_shift),
        grid_spec=grid_spec,
        out_shape=jax.ShapeDtypeStruct((r, HW), BF16),
        compiler_params=_params(("parallel", "arbitrary")),
        name="attn_fixed_shift" if fixed_shift else "attn_online",
    )(*[jnp.asarray(t) for t in tables], bound, q, k, vt, k, vt)


def _mix_tail_body(tpb, ntiles, first_layer, *refs):
    if first_layer:
        meta_ref, refs = refs[0], refs[1:]
    (zh_ref, h_ref, ob_ref, lb_ref, one_m_lb_ref, og_ref, tri_ref, wo_ref, g_ref, wu_ref, wd_ref,
     out_ref, st_ref, oa_ref) = refs
    i = pl.program_id(0)

    @pl.when(i == 0)
    def _():
        st_ref[...] = jnp.zeros_like(st_ref)
        oa_ref[1] = jnp.zeros(oa_ref.shape[1:], BF16)

    keep = (jnp.minimum(i, ntiles - 1) % tpb != 0).astype(F32)
    stage1, stage2, stage3, stage4 = _hgrn_stages(
        zh_ref, lb_ref, one_m_lb_ref, og_ref, tri_ref, st_ref, keep, oa_ref.at[i % 2])

    tt = jnp.maximum(i - 1, 0) % tpb
    h = _stream_tile(h_ref, meta_ref, tt) if first_layer else h_ref[...]
    mix = jnp.dot(oa_ref[(i + 1) % 2], wo_ref[:HW, :], preferred_element_type=F32)
    mix = mix + jnp.dot(ob_ref[...], wo_ref[HW:, :], preferred_element_type=F32)
    s1 = stage1()
    h = h + mix
    hn = _rms(h, g_ref[...]).astype(BF16)
    carried = {}

    def run2():
        carried["s2"] = stage2(s1)

    def run3():
        carried["o"] = stage3(carried["s2"])

    def run4():
        stage4(carried["s2"], carried["o"])

    out_ref[...] = h + _mlp_staggered(hn, wu_ref, wd_ref, between=(run2, run3, run4))


def _mix_tail(zh, h, meta, ob, lb, one_m_lb, og, tri2, wo, g, wu, wd, e, layer, nbatch, lp):
    r = nbatch * lp
    tpb = lp // TM
    ntiles = r // TM
    first_layer = meta is not None
    cur = lambda i: jnp.minimum(i, ntiles - 1)
    prev = lambda i: jnp.maximum(i - 1, 0)
    row = lambda i: (prev(i), 0)
    h_spec = pl.BlockSpec((TM, D_MODEL),
                          (lambda i: (_frame_tile(prev(i), tpb), 0)) if first_layer else row)
    lead_specs = [_const_spec((N_META, D_MODEL))] if first_layer else []
    lead_args = [meta] if first_layer else []
    return pl.pallas_call(
        functools.partial(_mix_tail_body, tpb, ntiles, first_layer),
        grid=(ntiles + 1,),
        in_specs=lead_specs + [
            pl.BlockSpec((TM, 4 * HW), lambda i: (cur(i), 0)),
            h_spec,
            pl.BlockSpec((TM, HW), row),
            _const_spec((1, HW)),
            _const_spec((1, HW)),
            _const_spec((1, HD)),
            _const_spec((CHUNK, 2 * CHUNK)),
            _const_spec((2 * HW, D_MODEL), e),
            _const_spec((1, D_MODEL)),
            _const_spec((D_MODEL, D_FF), layer),
            _const_spec((D_FF, D_MODEL), layer),
        ],
        out_specs=pl.BlockSpec((TM, D_MODEL), row),
        out_shape=jax.ShapeDtypeStruct((r, D_MODEL), F32),
        scratch_shapes=[pltpu.VMEM((HEADS, HD, HD), F32), pltpu.VMEM((2, TM, HW), BF16)],
        compiler_params=_params(("arbitrary",)),
        name="mix_tail",
    )(*lead_args, zh, h, ob, lb, one_m_lb, og, tri2, wo, g, wu, wd)


def _pool_mlp_body(tpb, ntiles, h_ref, halo_ref, gm_ref, pw_ref, ps_ref, g_ref, wu_ref, wd_ref,
                   out_ref, u_ref, a_ref, b_ref, hm_ref, hn_ref):
    i = pl.program_id(0)
    tt = jnp.minimum(i, ntiles - 1) % tpb
    n = TM + HALO
    g = POOL_G
    ys = [None] * len(POOL_WINDOWS)

    def normalise():
        gm = gm_ref[...]
        keep = (tt != 0).astype(F32)
        u_ref[0:HALO, :] = _rms(halo_ref[...], gm) * keep
        u_ref[HALO:, :] = _rms(h_ref[...], gm)
        a_ref[8:n, :] = u_ref[8:n, :] + u_ref[7:n - 1, :]

    def group(gi):
        w = POOL_WINDOWS[gi]
        cols = slice(gi * g, (gi + 1) * g)
        win = (a_ref, b_ref, a_ref, b_ref)[gi][HALO:, cols]
        pos = lax.broadcasted_iota(jnp.int32, (TM, g), 0) + (tt * TM - PAD)
        cnt = jnp.minimum(jnp.maximum(pos + 1, 1).astype(F32), float(w))
        d = win / cnt - u_ref[HALO:, cols]
        ys[gi] = jnp.dot(d.astype(BF16), pw_ref[gi], preferred_element_type=F32) * ps_ref[:, cols]

    def first_groups():
        b_ref[16:n, g:] = a_ref[16:n, g:] + a_ref[14:n - 2, g:]
        group(0)
        group(1)

    def last_groups():
        a_ref[24:n, 2 * g:] = b_ref[24:n, 2 * g:] + b_ref[20:n - 4, 2 * g:]
        b_ref[32:n, 3 * g:] = a_ref[32:n, 3 * g:] + a_ref[24:n - 8, 3 * g:]
        group(2)
        group(3)

    def finish():
        hm = h_ref[...] + jnp.concatenate(ys, axis=1)
        hm_ref[i % 2] = hm
        hn_ref[i % 2] = _rms(hm, g_ref[...]).astype(BF16)

    mlp_is_zero = jnp.maximum(i - 1, 0) % tpb == 0

    @pl.when(mlp_is_zero)
    def _():
        normalise()
        first_groups()
        last_groups()
        finish()
        out_ref[...] = jnp.zeros_like(out_ref)

    @pl.when(jnp.logical_not(mlp_is_zero))
    def _():
        acc = _mlp_staggered(hn_ref[(i + 1) % 2], wu_ref, wd_ref,
                             between=(normalise, first_groups, last_groups, finish))
        out_ref[...] = hm_ref[(i + 1) % 2] + acc


def _pool_mlp(h, gm, pw, ps, g, wu, wd, o, layer, nbatch, lp, to_frames):
    r = nbatch * lp
    tpb = lp // TM
    ntiles = r // TM
    cur = lambda i: jnp.minimum(i, ntiles - 1)
    prev = lambda i: jnp.maximum(i - 1, 0)
    out_rows = nbatch * (lp - LEAD) if to_frames else r
    return pl.pallas_call(
        functools.partial(_pool_mlp_body, tpb, ntiles),
        grid=(ntiles + 1,),
        in_specs=[
            pl.BlockSpec((TM, D_MODEL), lambda i: (cur(i), 0)),
            pl.BlockSpec((HALO, D_MODEL), lambda i: (jnp.maximum(cur(i) * (TM // HALO) - 1, 0), 0)),
            _const_spec((1, D_MODEL)),
            _const_spec((len(POOL_WINDOWS), POOL_G, POOL_G), o),
            _const_spec((1, D_MODEL)),
            _const_spec((1, D_MODEL)),
            _const_spec((D_MODEL, D_FF), layer),
            _const_spec((D_FF, D_MODEL), layer),
        ],
        out_specs=pl.BlockSpec((TM, D_MODEL), (lambda i: (_frame_tile(prev(i), tpb), 0)) if to_frames
                               else (lambda i: (prev(i), 0))),
        out_shape=jax.ShapeDtypeStruct((out_rows, D_MODEL), F32),
        scratch_shapes=[pltpu.VMEM((TM + HALO, D_MODEL), F32)] * 3 + [
            pltpu.VMEM((2, TM, D_MODEL), F32), pltpu.VMEM((2, TM, D_MODEL), BF16)],
        compiler_params=_params(("arbitrary",)),
        name="pool_mlp",
    )(h, h, gm, pw, ps, g, wu, wd)


def _rope_cols(w):
    half = ROPE // 2
    z = jnp.zeros(w.shape[:-1] + (half,), w.dtype)
    return jnp.concatenate([w[..., :half], z, w[..., half:], z], axis=-1)


def _qk_cols(w):
    w = w.reshape(w.shape[:-1] + (HEADS, QK_DIM))
    w = jnp.concatenate([w[..., :HD], _rope_cols(w[..., HD:])], axis=-1)
    return w.reshape(w.shape[:-2] + (HEADS * QK_PAD,))


def _rope_tables(lp):
    half = ROPE // 2
    inv = ROPE_THETA ** (-np.arange(half, dtype=np.float64) / half)
    pos = np.maximum(np.arange(lp, dtype=np.float64) - PAD, 0.0)
    ang = pos[:, None] * inv[None, :]
    c = jnp.asarray(np.cos(ang).astype(np.float32))
    s = jnp.asarray(np.sin(ang).astype(np.float32))
    z = jnp.zeros_like(c)
    return (jnp.concatenate([c, z, c, z], axis=1), jnp.concatenate([-s, z, s, z], axis=1))


def kernel(x, meta_tokens, mix_norm, mlp_norm, w_mlp_up, w_mlp_down, w_in, hgrn_lb, hgrn_out_norm, mla_q_a_norm, mla_kv_a_norm, w_q_up, w_kv_up, q_norm, k_norm, w_out, pool_w, pool_scale):
    nbatch, seq, _ = x.shape
    depth = mix_norm.shape[0]
    assert seq % TQ == 0 and depth % 2 == 0
    lp = seq + LEAD

    cos_t, sin_t = _rope_tables(lp)
    lb_cum = jnp.cumsum(jax.nn.softmax(hgrn_lb.astype(F32), axis=0), axis=0)
    lower = lb_cum - lb_cum[0:1]
    tri = jnp.tril(jnp.ones((CHUNK, CHUNK), F32)).astype(BF16)
    tri2 = jnp.concatenate([tri, tri], axis=1)

    w_in_l = w_in.astype(BF16)
    wq_l = _qk_cols(w_q_up).astype(BF16)
    wkv = w_kv_up.reshape(w_kv_up.shape[0], KV_RANK, HEADS, 2 * HD)
    wkv_l = jnp.concatenate([wkv[..., :HD].reshape(-1, KV_RANK, HW),
                             wkv[..., HD:].reshape(-1, KV_RANK, HW)], axis=-1).astype(BF16)
    qn_l = jnp.concatenate([q_norm[:, :HD], _rope_cols(q_norm[:, HD:])], axis=-1) * Q_SCALE
    kn_l = jnp.concatenate([k_norm[:, :HD], _rope_cols(k_norm[:, HD:])], axis=-1)
    wo_l = w_out.astype(BF16)
    wu_l = w_mlp_up.astype(BF16)
    wd_l = w_mlp_down.astype(BF16)
    pw_l = pool_w.astype(BF16)

    h = x.reshape(nbatch * seq, D_MODEL)
    meta = meta_tokens.astype(F32)
    for layer in range(depth):
        if layer % 2 == 0:
            e = layer // 2
            zh, q, k, vt = _inproj(h, meta, mix_norm[layer][None], w_in_l, mla_q_a_norm[e][None],
                                  mla_kv_a_norm[e][None], wq_l, wkv_l, qn_l[e][None],
                                  kn_l[e][None], cos_t, sin_t, e, nbatch, lp)
            bound = (Q_SCALE * QK_DIM * 1.01) * jnp.max(jnp.abs(q_norm[e])) * jnp.max(jnp.abs(k_norm[e]))
            bound = bound.reshape(1).astype(F32)
            ob = lax.cond(bound[0] <= MAX_FIXED_SHIFT,
                          functools.partial(_attn, nbatch=nbatch, lp=lp, fixed_shift=True),
                          functools.partial(_attn, nbatch=nbatch, lp=lp, fixed_shift=False),
                          q, k, vt, bound)
            h = _mix_tail(zh, h, meta, ob, lower[e][None], 1.0 - lower[e][None], hgrn_out_norm[e][None],
                          tri2, wo_l, mlp_norm[layer][None], wu_l, wd_l, e, layer, nbatch, lp)
            meta = None
        else:
            o = layer // 2
            h = _pool_mlp(h, mix_norm[layer][None], pw_l, pool_scale[o][None],
                          mlp_norm[layer][None], wu_l, wd_l, o, layer, nbatch, lp,
                          to_frames=layer == depth - 1)

    return h.reshape(nbatch, seq, D_MODEL)
```

```python
import functools

import numpy as np
import jax
import jax.numpy as jnp
from jax import lax
from jax.experimental import pallas as pl
from jax.experimental.pallas import tpu as pltpu

F32 = jnp.float32
BF16 = jnp.bfloat16

D_MODEL = 1024
D_FF = 4 * D_MODEL
EPS = 1e-6
N_META = 16
CHUNK = 64
HEADS = 4
HD = 128
HW = HEADS * HD
ROPE = 64
QK_DIM = HD + ROPE
QK_PAD = 256
Q_RANK = 256
KV_RANK = 256
ROPE_THETA = 10000.0
POOL_WINDOWS = (2, 4, 8, 16)
POOL_G = D_MODEL // len(POOL_WINDOWS)

LEAD = 1024
PAD = LEAD - N_META
TM = 512
LEAD_TILES = LEAD // TM
TQ = 1024
TK = 1024
SUB = 16
HALO = 32
IN_COLS = 4 * HW + Q_RANK + KV_RANK + ROPE
Q_SCALE = QK_DIM ** -0.5 * float(np.log2(np.e))
MAX_FIXED_SHIFT = 56.0
MASK_VALUE = -1e30
EXP2_CLAMP = 115.0
TINY = 1e-37
SUBLANES = 8
V7X_VMEM_BYTES = 64 * 1024 * 1024
VMEM_LIMIT = V7X_VMEM_BYTES - 8 * 1024 * 1024


def _rms(x, g):
    return x * lax.rsqrt(jnp.mean(x * x, axis=-1, keepdims=True) + EPS) * g


def _silu(x):
    hx = 0.5 * x
    return hx + hx * jnp.tanh(hx)


def _const_spec(shape, layer=None):
    nd = len(shape)
    if layer is None:
        return pl.BlockSpec(shape, lambda *_: (0,) * nd, pipeline_mode=pl.Buffered(1))
    return pl.BlockSpec((None,) + tuple(shape), lambda *_: (layer,) + (0,) * nd, pipeline_mode=pl.Buffered(1))


def _params(sem):
    return pltpu.CompilerParams(dimension_semantics=sem, vmem_limit_bytes=VMEM_LIMIT)


def _frame_tile(i, tpb):
    return (i // tpb) * (tpb - LEAD_TILES) + jnp.maximum(i % tpb - LEAD_TILES, 0)


def _stream_tile(h_ref, meta_ref, tt):
    lead = jnp.concatenate([jnp.zeros((TM - N_META, D_MODEL), F32), meta_ref[...]], axis=0)
    lead = jnp.where(tt == LEAD_TILES - 1, lead, 0.0)
    return jnp.where(tt >= LEAD_TILES, h_ref[...], lead)


MLP_PARTS = 4
MLP_SLAB = D_FF // MLP_PARTS


def _mlp_up(hn, wu_ref, c):
    a = jnp.dot(hn, wu_ref[:, c * MLP_SLAB:(c + 1) * MLP_SLAB], preferred_element_type=F32)
    a = jnp.maximum(a, 0.0)
    return (a * a).astype(BF16)


def _mlp_down(a, wd_ref, c):
    return jnp.dot(a, wd_ref[c * MLP_SLAB:(c + 1) * MLP_SLAB, :], preferred_element_type=F32)


def _mlp_staggered(hn, wu_ref, wd_ref, between=()):
    a = _mlp_up(hn, wu_ref, 0)
    acc = None
    for c in range(MLP_PARTS):
        a_next = _mlp_up(hn, wu_ref, c + 1) if c + 1 < MLP_PARTS else None
        d = _mlp_down(a, wd_ref, c)
        acc = d if acc is None else acc + d
        if c < len(between):
            between[c]()
        a = a_next
    return acc


def _rope(x, c, s):
    return x * c + pltpu.roll(x, HD // 2, axis=1) * s


def _inproj_body(tpb, ntiles, first_layer, *refs):
    if first_layer:
        meta_ref, refs = refs[0], refs[1:]
    (h_ref, g_ref, win_ref, qag_ref, kvag_ref, wq_ref, wkv_ref, qn_ref, kn_ref, cos_ref, sin_ref,
     zh_ref, q_ref, k_ref, vt_ref, lat_ref) = refs
    i = pl.program_id(0)

    @pl.when(i == 0)
    def _():
        lat_ref[1] = jnp.zeros(lat_ref.shape[1:], F32)

    lat = lat_ref[(i + 1) % 2]
    q = jnp.dot(_rms(lat[:, :Q_RANK], qag_ref[...]).astype(BF16), wq_ref[...], preferred_element_type=F32)
    kv = jnp.dot(_rms(lat[:, Q_RANK:Q_RANK + KV_RANK], kvag_ref[...]).astype(BF16), wkv_ref[...],
                 preferred_element_type=F32)
    kr = _rope_cols(lat[:, Q_RANK + KV_RANK:])

    tt = jnp.minimum(i, ntiles - 1) % tpb
    h = _stream_tile(h_ref, meta_ref, tt) if first_layer else h_ref[...]
    z = jnp.dot(_rms(h, g_ref[...]).astype(BF16), win_ref[...], preferred_element_type=F32)
    zh_ref[...] = z[:, :4 * HW]
    lat_ref[i % 2] = z[:, 4 * HW:]

    vt_ref[...] = kv[:, HW:].T.astype(BF16)
    c = cos_ref[...]
    s = sin_ref[...]
    qg = qn_ref[...]
    kg = kn_ref[...]
    kr_ss = jnp.sum(kr * kr, axis=-1, keepdims=True)
    for hd in range(HEADS):
        qa = q[:, hd * QK_PAD:hd * QK_PAD + HD]
        qb = q[:, hd * QK_PAD + HD:(hd + 1) * QK_PAD]
        ss = jnp.sum(qa * qa + qb * qb, axis=-1, keepdims=True)
        inv = lax.rsqrt(ss * (1.0 / QK_DIM) + EPS)
        q_ref[:, hd * QK_PAD:hd * QK_PAD + HD] = (qa * inv * qg[:, :HD]).astype(BF16)
        q_ref[:, hd * QK_PAD + HD:(hd + 1) * QK_PAD] = _rope(qb * inv * qg[:, HD:], c, s).astype(BF16)
        ka = kv[:, hd * HD:(hd + 1) * HD]
        ss = jnp.sum(ka * ka, axis=-1, keepdims=True) + kr_ss
        inv = lax.rsqrt(ss * (1.0 / QK_DIM) + EPS)
        k_ref[:, hd * QK_PAD:hd * QK_PAD + HD] = (ka * inv * kg[:, :HD]).astype(BF16)
        k_ref[:, hd * QK_PAD + HD:(hd + 1) * QK_PAD] = _rope(kr * inv * kg[:, HD:], c, s).astype(BF16)


def _inproj(h, meta, g, win, qag, kvag, wq, wkv, qn, kn, cos_t, sin_t, e, nbatch, lp):
    r = nbatch * lp
    tpb = lp // TM
    ntiles = r // TM
    first_layer = meta is not None
    cur = lambda i: jnp.minimum(i, ntiles - 1)
    prev = lambda i: jnp.maximum(i - 1, 0)
    row = lambda i: (prev(i), 0)
    tab = lambda i: (prev(i) % tpb, 0)
    h_spec = pl.BlockSpec((TM, D_MODEL),
                          (lambda i: (_frame_tile(cur(i), tpb), 0)) if first_layer else (lambda i: (cur(i), 0)))
    lead_specs = [_const_spec((N_META, D_MODEL))] if first_layer else []
    lead_args = [meta] if first_layer else []
    return pl.pallas_call(
        functools.partial(_inproj_body, tpb, ntiles, first_layer),
        grid=(ntiles + 1,),
        in_specs=lead_specs + [
            h_spec,
            _const_spec((1, D_MODEL)),
            _const_spec((D_MODEL, IN_COLS), e),
            _const_spec((1, Q_RANK)),
            _const_spec((1, KV_RANK)),
            _const_spec((Q_RANK, HEADS * QK_PAD), e),
            _const_spec((KV_RANK, 2 * HW), e),
            _const_spec((1, QK_PAD)),
            _const_spec((1, QK_PAD)),
            pl.BlockSpec((TM, HD), tab),
            pl.BlockSpec((TM, HD), tab),
        ],
        out_specs=[
            pl.BlockSpec((TM, 4 * HW), lambda i: (cur(i), 0)),
            pl.BlockSpec((TM, HEADS * QK_PAD), row),
            pl.BlockSpec((TM, HEADS * QK_PAD), row),
            pl.BlockSpec((HW, TM), lambda i: (0, prev(i))),
        ],
        out_shape=[
            jax.ShapeDtypeStruct((r, 4 * HW), F32),
            jax.ShapeDtypeStruct((r, HEADS * QK_PAD), BF16),
            jax.ShapeDtypeStruct((r, HEADS * QK_PAD), BF16),
            jax.ShapeDtypeStruct((HW, r), BF16),
        ],
        scratch_shapes=[pltpu.VMEM((2, TM, IN_COLS - 4 * HW), F32)],
        compiler_params=_params(("arbitrary",)),
        name="inproj",
    )(*lead_args, h, g, win, qag, kvag, wq, wkv, qn, kn, cos_t, sin_t)


def _group_rows(rows):
    return jnp.concatenate([jnp.broadcast_to(r, (SUB, HD)) for r in rows], axis=0)


def _hgrn_stages(zh_ref, lb_ref, one_m_lb_ref, og_ref, tri_ref, st_ref, keep, o_ref):
    tri2 = tri_ref[...]
    lb = lb_ref[...]
    one_m_lb = one_m_lb_ref[...]
    og = og_ref[...]
    tt = lax.broadcasted_iota(jnp.int32, (CHUNK, CHUNK), 0)
    ss_ = lax.broadcasted_iota(jnp.int32, (CHUNK, CHUNK), 1)
    causal = ss_ <= tt
    nsub = CHUNK // SUB
    zero_row = jnp.zeros((1, HD), F32)
    zero_sub = jnp.zeros((SUB, HD), BF16)

    heads = range(HEADS)
    sls = [slice(hd * HD, (hd + 1) * HD) for hd in heads]

    chunk_rows = [slice(c * CHUNK, (c + 1) * CHUNK) for c in range(TM // CHUNK)]

    def gates_and_decay(rows):
        hq = zh_ref[rows, 0:HW]
        hf = zh_ref[rows, HW:2 * HW]
        hi = zh_ref[rows, 2 * HW:3 * HW]
        hg = zh_ref[rows, 3 * HW:4 * HW]
        q = _silu(hq)
        gate = _silu(hg)
        t = jnp.exp(-jnp.abs(hf))
        r = 1.0 / (1.0 + t)
        tr = t * r
        pos = hf >= 0.0
        log2f = jnp.log2(jnp.maximum(lb + one_m_lb * jnp.where(pos, r, tr), TINY))
        k = one_m_lb * jnp.where(pos, tr, r)
        g1 = log2f.astype(BF16)
        g2 = (log2f - g1.astype(F32)).astype(BF16)
        b = jnp.dot(tri2, jnp.concatenate([g1, g2], axis=0), preferred_element_type=F32)
        vt = [hi[:, sls[hd]].T.astype(BF16) for hd in heads]
        return q, k, b, vt, gate

    def intra_chunk(q, k, b, vt, gate):
        att, q_in, k_out, decay = [], [], [], []
        for hd in heads:
            bh = b[:, sls[hd]]
            b_last = bh[CHUNK - 1:CHUNK, :]
            refs = [zero_row] + [bh[i * SUB - 1:i * SUB, :] for i in range(1, nsub)]
            dq = bh - _group_rows(refs)
            qe = q[:, sls[hd]] * jnp.exp2(dq)
            ke = k[:, sls[hd]] * jnp.exp2(jnp.minimum(-dq, EXP2_CLAMP))
            keb = ke.astype(BF16)
            qcat = []
            kcat = []
            for j in range(nsub):
                qcat.append(jnp.concatenate(
                    [zero_sub if i < j else
                     qe[i * SUB:(i + 1) * SUB].astype(BF16) if i == j else
                     (qe[i * SUB:(i + 1) * SUB] * jnp.exp2(refs[i] - refs[j])).astype(BF16)
                     for i in range(nsub)], axis=0))
                kcat.append(jnp.concatenate(
                    [keb[j * SUB:(j + 1) * SUB] if i == j else zero_sub for i in range(nsub)], axis=0))
            q_in.append(qcat[0])
            att.append(lax.dot_general(jnp.concatenate(qcat, axis=1), jnp.concatenate(kcat, axis=1),
                                       (((1,), (1,)), ((), ())), preferred_element_type=F32))
            to_end = [jnp.exp2(b_last - refs[i]) for i in range(nsub)]
            k_out.append((ke * _group_rows(to_end)).astype(BF16))
            decay.append(to_end[0])
        upd = [jnp.dot(vt[hd], k_out[hd], preferred_element_type=F32) for hd in heads]
        return att, q_in, upd, decay, vt, gate

    def stage1():
        return [gates_and_decay(rows) for rows in chunk_rows]

    def stage2(s1):
        return [intra_chunk(*c) for c in s1]

    def stage3(s2):
        st = [st_ref[hd] * keep for hd in heads]
        starts = []
        for att, q_in, upd, decay, vt, gate in s2:
            starts.append([st[hd].astype(BF16) for hd in heads])
            st = [decay[hd] * st[hd] + upd[hd] for hd in heads]
        for hd in heads:
            st_ref[hd] = st[hd]
        return starts

    def stage4(s2, starts):
        for rows, st0, (att, q_in, upd, decay, vt, gate) in zip(chunk_rows, starts, s2):
            for hd in heads:
                a = jnp.where(causal, att[hd], 0.0).astype(BF16)
                oh = lax.dot_general(jnp.concatenate([q_in[hd], a], axis=1),
                                     jnp.concatenate([st0[hd], vt[hd]], axis=1),
                                     (((1,), (1,)), ((), ())), preferred_element_type=F32)
                on = _rms(oh, og) * gate[:, sls[hd]]
                o_ref[rows, sls[hd]] = on.astype(BF16)

    return stage1, stage2, stage3, stage4


META_STEP, FULL_STEP, DIAG_STEP, NO_STEP = range(4)


def _attn_steps(lp):
    qi, fin, kind_a, kj_a, kind_b, kj_b = [], [], [], [], [], []
    for i in range(lp // TQ):
        tiles = [(META_STEP if j == 0 else DIAG_STEP if j == i else FULL_STEP, j) for j in range(i + 1)]
        for s in range(0, len(tiles), 2):
            a = tiles[s]
            b = tiles[s + 1] if s + 1 < len(tiles) else (NO_STEP, a[1])
            qi.append(i)
            fin.append(int(s + 2 >= len(tiles)))
            kind_a.append(a[0])
            kj_a.append(a[1])
            kind_b.append(b[0])
            kj_b.append(b[1])
    return [np.asarray(t, np.int32) for t in (qi, fin, kind_a, kj_a, kind_b, kj_b)]


def _attn_body(fixed_shift, qi_ref, fin_ref, kind_a_ref, kj_a_ref, kind_b_ref, kj_b_ref, bound_ref,
               q_ref, ka_ref, vta_ref, kb_ref, vtb_ref, o_ref, *scratch):
    if fixed_shift:
        l_ref, acc_ref = scratch
    else:
        m_ref, l_ref, acc_ref = scratch
    step = pl.program_id(1)
    qi = qi_ref[step]
    bound = bound_ref[0]

    def update(k_ref, vt_ref, qs, keys, mask, first):
        nk, nq = keys.stop - keys.start, qs.stop - qs.start
        for hd in range(HEADS):
            hq = slice(hd * QK_PAD, (hd + 1) * QK_PAD)
            hv = slice(hd * HD, (hd + 1) * HD)
            st = lax.dot_general(k_ref[keys, hq], q_ref[qs, hq], (((1,), (1,)), ((), ())),
                                 preferred_element_type=F32)
            if fixed_shift:
                p = jnp.exp2(st - bound)
                if mask is not None:
                    p = jnp.where(mask, p, 0.0)
            else:
                if mask is not None:
                    st = jnp.where(mask, st, MASK_VALUE)
                m_prev = m_ref[hd, :, qs]
                m_new = jnp.maximum(m_prev, jnp.max(st, axis=0, keepdims=True))
                alpha = jnp.exp2(m_prev - m_new)
                m_ref[hd, :, qs] = m_new
                p = jnp.exp2(st - m_new[0:1, :])
            part = jnp.sum(p.reshape(nk // SUBLANES, SUBLANES, nq), axis=0)
            pv = jnp.dot(vt_ref[hv, keys], p.astype(BF16), preferred_element_type=F32)
            if first and fixed_shift:
                l_ref[hd, :, qs] = part
                acc_ref[hv, qs] = pv
            elif fixed_shift:
                l_ref[hd, :, qs] += part
                acc_ref[hv, qs] += pv
            else:
                l_ref[hd, :, qs] = alpha * l_ref[hd, :, qs] + part
                acc_ref[hv, qs] = alpha[0:1, :] * acc_ref[hv, qs] + pv

    all_q = slice(0, TQ)
    all_keys = slice(0, TK)

    def key_tile(kind, kj, k_ref, vt_ref):
        @pl.when(kind == META_STEP)
        def _():
            if not fixed_shift:
                m_ref[...] = jnp.full_like(m_ref, MASK_VALUE)
                l_ref[...] = jnp.zeros_like(l_ref)
                acc_ref[...] = jnp.zeros_like(acc_ref)
            key = lax.broadcasted_iota(jnp.int32, (HD, TQ), 0) + (kj * TK + TK - HD)
            update(k_ref, vt_ref, all_q, slice(TK - HD, TK), key >= PAD, True)

        @pl.when(kind == FULL_STEP)
        def _():
            update(k_ref, vt_ref, all_q, all_keys, None, False)

        @pl.when(kind == DIAG_STEP)
        def _():
            half = TQ // 2
            key = lax.broadcasted_iota(jnp.int32, (half, TQ), 0)
            qry = lax.broadcasted_iota(jnp.int32, (half, TQ), 1)
            update(k_ref, vt_ref, all_q, slice(0, half), key // CHUNK <= qry // CHUNK, False)
            update(k_ref, vt_ref, slice(half, TQ), slice(half, TK),
                   (key // CHUNK <= qry // CHUNK)[:, :half], False)

    key_tile(kind_a_ref[step], kj_a_ref[step], ka_ref, vta_ref)
    key_tile(kind_b_ref[step], kj_b_ref[step], kb_ref, vtb_ref)

    @pl.when(fin_ref[step] == 1)
    def _():
        row = lax.broadcasted_iota(jnp.int32, (TQ, HD), 0) + qi * TQ
        valid = row >= PAD
        for hd in range(HEADS):
            hv = slice(hd * HD, (hd + 1) * HD)
            ot = acc_ref[hv, :] / jnp.sum(l_ref[hd], axis=0, keepdims=True)
            o_ref[:, hv] = jnp.where(valid, ot.T, 0.0).astype(BF16)


def _attn(q, k, vt, bound, nbatch, lp, fixed_shift):
    r = q.shape[0]
    nq, nk = lp // TQ, lp // TK
    tables = _attn_steps(lp)
    stats = [pltpu.VMEM((HEADS, SUBLANES, TQ), F32)] * (1 if fixed_shift else 2)
    qmap = lambda b, s, qi, fin, kind_a, kj_a, kind_b, kj_b, bd: (b * nq + qi[s], 0)
    grid_spec = pltpu.PrefetchScalarGridSpec(
        num_scalar_prefetch=7,
        grid=(nbatch, len(tables[0])),
        in_specs=[
            pl.BlockSpec((TQ, HEADS * QK_PAD), qmap),
            pl.BlockSpec((TK, HEADS * QK_PAD),
                         lambda b, s, qi, fin, kind_a, kj_a, kind_b, kj_b, bd: (b * nk + kj_a[s], 0)),
            pl.BlockSpec((HW, TK),
                         lambda b, s, qi, fin, kind_a, kj_a, kind_b, kj_b, bd: (0, b * nk + kj_a[s])),
            pl.BlockSpec((TK, HEADS * QK_PAD),
                         lambda b, s, qi, fin, kind_a, kj_a, kind_b, kj_b, bd: (b * nk + kj_b[s], 0)),
            pl.BlockSpec((HW, TK),
                         lambda b, s, qi, fin, kind_a, kj_a, kind_b, kj_b, bd: (0, b * nk + kj_b[s])),
        ],
        out_specs=pl.BlockSpec((TQ, HW), qmap),
        scratch_shapes=stats + [pltpu.VMEM((HW, TQ), F32)],
    )
    return pl.pallas_call(
        functools.partial(_attn_body, fixed_shift),
        grid_spec=grid_spec,
        out_shape=jax.ShapeDtypeStruct((r, HW), BF16),
        compiler_params=_params(("parallel", "arbitrary")),
        name="attn_fixed_shift" if fixed_shift else "attn_online",
    )(*[jnp.asarray(t) for t in tables], bound, q, k, vt, k, vt)


def _mix_tail_body(tpb, ntiles, first_layer, *refs):
    if first_layer:
        meta_ref, refs = refs[0], refs[1:]
    (zh_ref, h_ref, ob_ref, lb_ref, one_m_lb_ref, og_ref, tri_ref, wo_ref, g_ref, wu_ref, wd_ref,
     out_ref, st_ref, oa_ref) = refs
    i = pl.program_id(0)

    @pl.when(i == 0)
    def _():
        st_ref[...] = jnp.zeros_like(st_ref)

    keep = (jnp.minimum(i, ntiles - 1) % tpb != 0).astype(F32)
    stage1, stage2, stage3, stage4 = _hgrn_stages(
        zh_ref, lb_ref, one_m_lb_ref, og_ref, tri_ref, st_ref, keep, oa_ref.at[i % 2])

    tt = jnp.maximum(i - 1, 0) % tpb

    @pl.when(tt == 0)
    def _():
        s2 = stage2(stage1())
        stage4(s2, stage3(s2))
        out_ref[...] = jnp.zeros_like(out_ref)

    @pl.when(tt != 0)
    def _():
        h = _stream_tile(h_ref, meta_ref, tt) if first_layer else h_ref[...]
        mix = jnp.dot(oa_ref[(i + 1) % 2], wo_ref[:HW, :], preferred_element_type=F32)
        mix = mix + jnp.dot(ob_ref[...], wo_ref[HW:, :], preferred_element_type=F32)
        s1 = stage1()
        h = h + mix
        hn = _rms(h, g_ref[...]).astype(BF16)
        carried = {}

        def run2():
            carried["s2"] = stage2(s1)

        def run3():
            carried["o"] = stage3(carried["s2"])

        def run4():
            stage4(carried["s2"], carried["o"])

        out_ref[...] = h + _mlp_staggered(hn, wu_ref, wd_ref, between=(run2, run3, run4))


def _mix_tail(zh, h, meta, ob, lb, one_m_lb, og, tri2, wo, g, wu, wd, e, layer, nbatch, lp):
    r = nbatch * lp
    tpb = lp // TM
    ntiles = r // TM
    first_layer = meta is not None
    cur = lambda i: jnp.minimum(i, ntiles - 1)
    prev = lambda i: jnp.maximum(i - 1, 0)
    row = lambda i: (prev(i), 0)
    h_spec = pl.BlockSpec((TM, D_MODEL),
                          (lambda i: (_frame_tile(prev(i), tpb), 0)) if first_layer else row)
    lead_specs = [_const_spec((N_META, D_MODEL))] if first_layer else []
    lead_args = [meta] if first_layer else []
    return pl.pallas_call(
        functools.partial(_mix_tail_body, tpb, ntiles, first_layer),
        grid=(ntiles + 1,),
        in_specs=lead_specs + [
            pl.BlockSpec((TM, 4 * HW), lambda i: (cur(i), 0)),
            h_spec,
            pl.BlockSpec((TM, HW), row),
            _const_spec((1, HW)),
            _const_spec((1, HW)),
            _const_spec((1, HD)),
            _const_spec((CHUNK, 2 * CHUNK)),
            _const_spec((2 * HW, D_MODEL), e),
            _const_spec((1, D_MODEL)),
            _const_spec((D_MODEL, D_FF), layer),
            _const_spec((D_FF, D_MODEL), layer),
        ],
        out_specs=pl.BlockSpec((TM, D_MODEL), row),
        out_shape=jax.ShapeDtypeStruct((r, D_MODEL), F32),
        scratch_shapes=[pltpu.VMEM((HEADS, HD, HD), F32), pltpu.VMEM((2, TM, HW), BF16)],
        compiler_params=_params(("arbitrary",)),
        name="mix_tail",
    )(*lead_args, zh, h, ob, lb, one_m_lb, og, tri2, wo, g, wu, wd)


def _pool_mlp_body(tpb, ntiles, h_ref, halo_ref, gm_ref, pw_ref, ps_ref, g_ref, wu_ref, wd_ref,
                   out_ref, u_ref, a_ref, b_ref, hm_ref, hn_ref):
    i = pl.program_id(0)
    tt = jnp.minimum(i, ntiles - 1) % tpb
    n = TM + HALO
    g = POOL_G
    ys = [None] * len(POOL_WINDOWS)

    def normalise():
        gm = gm_ref[...]
        keep = (tt != 0).astype(F32)
        u_ref[0:HALO, :] = _rms(halo_ref[...], gm) * keep
        u_ref[HALO:, :] = _rms(h_ref[...], gm)
        a_ref[8:n, :] = u_ref[8:n, :] + u_ref[7:n - 1, :]

    def group(gi):
        w = POOL_WINDOWS[gi]
        cols = slice(gi * g, (gi + 1) * g)
        win = (a_ref, b_ref, a_ref, b_ref)[gi][HALO:, cols]
        pos = lax.broadcasted_iota(jnp.int32, (TM, g), 0) + (tt * TM - PAD)
        cnt = jnp.minimum(jnp.maximum(pos + 1, 1).astype(F32), float(w))
        d = win / cnt - u_ref[HALO:, cols]
        ys[gi] = jnp.dot(d.astype(BF16), pw_ref[gi], preferred_element_type=F32) * ps_ref[:, cols]

    def first_groups():
        b_ref[16:n, g:] = a_ref[16:n, g:] + a_ref[14:n - 2, g:]
        group(0)
        group(1)

    def last_groups():
        a_ref[24:n, 2 * g:] = b_ref[24:n, 2 * g:] + b_ref[20:n - 4, 2 * g:]
        b_ref[32:n, 3 * g:] = a_ref[32:n, 3 * g:] + a_ref[24:n - 8, 3 * g:]
        group(2)
        group(3)

    def finish():
        hm = h_ref[...] + jnp.concatenate(ys, axis=1)
        hm_ref[i % 2] = hm
        hn_ref[i % 2] = _rms(hm, g_ref[...]).astype(BF16)

    mlp_is_zero = jnp.maximum(i - 1, 0) % tpb == 0

    @pl.when(mlp_is_zero)
    def _():
        normalise()
        first_groups()
        last_groups()
        finish()
        out_ref[...] = jnp.zeros_like(out_ref)

    @pl.when(jnp.logical_not(mlp_is_zero))
    def _():
        acc = _mlp_staggered(hn_ref[(i + 1) % 2], wu_ref, wd_ref,
                             between=(normalise, first_groups, last_groups, finish))
        out_ref[...] = hm_ref[(i + 1) % 2] + acc


def _pool_mlp(h, gm, pw, ps, g, wu, wd, o, layer, nbatch, lp, to_frames):
    r = nbatch * lp
    tpb = lp // TM
    ntiles = r // TM
    cur = lambda i: jnp.minimum(i, ntiles - 1)
    prev = lambda i: jnp.maximum(i - 1, 0)
    out_rows = nbatch * (lp - LEAD) if to_frames else r
    return pl.pallas_call(
        functools.partial(_pool_mlp_body, tpb, ntiles),
        grid=(ntiles + 1,),
        in_specs=[
            pl.BlockSpec((TM, D_MODEL), lambda i: (cur(i), 0)),
            pl.BlockSpec((HALO, D_MODEL), lambda i: (jnp.maximum(cur(i) * (TM // HALO) - 1, 0), 0)),
            _const_spec((1, D_MODEL)),
            _const_spec((len(POOL_WINDOWS), POOL_G, POOL_G), o),
            _const_spec((1, D_MODEL)),
            _const_spec((1, D_MODEL)),
            _const_spec((D_MODEL, D_FF), layer),
            _const_spec((D_FF, D_MODEL), layer),
        ],
        out_specs=pl.BlockSpec((TM, D_MODEL), (lambda i: (_frame_tile(prev(i), tpb), 0)) if to_frames
                               else (lambda i: (prev(i), 0))),
        out_shape=jax.ShapeDtypeStruct((out_rows, D_MODEL), F32),
        scratch_shapes=[pltpu.VMEM((TM + HALO, D_MODEL), F32)] * 3 + [
            pltpu.VMEM((2, TM, D_MODEL), F32), pltpu.VMEM((2, TM, D_MODEL), BF16)],
        compiler_params=_params(("arbitrary",)),
        name="pool_mlp",
    )(h, h, gm, pw, ps, g, wu, wd)


def _rope_cols(w):
    half = ROPE // 2
    z = jnp.zeros(w.shape[:-1] + (half,), w.dtype)
    return jnp.concatenate([w[..., :half], z, w[..., half:], z], axis=-1)


def _qk_cols(w):
    w = w.reshape(w.shape[:-1] + (HEADS, QK_DIM))
    w = jnp.concatenate([w[..., :HD], _rope_cols(w[..., HD:])], axis=-1)
    return w.reshape(w.shape[:-2] + (HEADS * QK_PAD,))


def _rope_tables(lp):
    half = ROPE // 2
    inv = ROPE_THETA ** (-np.arange(half, dtype=np.float64) / half)
    pos = np.maximum(np.arange(lp, dtype=np.float64) - PAD, 0.0)
    ang = pos[:, None] * inv[None, :]
    c = jnp.asarray(np.cos(ang).astype(np.float32))
    s = jnp.asarray(np.sin(ang).astype(np.float32))
    z = jnp.zeros_like(c)
    return (jnp.concatenate([c, z, c, z], axis=1), jnp.concatenate([-s, z, s, z], axis=1))


def kernel(x, meta_tokens, mix_norm, mlp_norm, w_mlp_up, w_mlp_down, w_in, hgrn_lb, hgrn_out_norm, mla_q_a_norm, mla_kv_a_norm, w_q_up, w_kv_up, q_norm, k_norm, w_out, pool_w, pool_scale):
    nbatch, seq, _ = x.shape
    depth = mix_norm.shape[0]
    assert seq % TQ == 0 and depth % 2 == 0
    lp = seq + LEAD

    cos_t, sin_t = _rope_tables(lp)
    lb_cum = jnp.cumsum(jax.nn.softmax(hgrn_lb.astype(F32), axis=0), axis=0)
    lower = lb_cum - lb_cum[0:1]
    tri = jnp.tril(jnp.ones((CHUNK, CHUNK), F32)).astype(BF16)
    tri2 = jnp.concatenate([tri, tri], axis=1)

    w_in_l = w_in.astype(BF16)
    wq_l = _qk_cols(w_q_up).astype(BF16)
    wkv = w_kv_up.reshape(w_kv_up.shape[0], KV_RANK, HEADS, 2 * HD)
    wkv_l = jnp.concatenate([wkv[..., :HD].reshape(-1, KV_RANK, HW),
                             wkv[..., HD:].reshape(-1, KV_RANK, HW)], axis=-1).astype(BF16)
    qn_l = jnp.concatenate([q_norm[:, :HD], _rope_cols(q_norm[:, HD:])], axis=-1) * Q_SCALE
    kn_l = jnp.concatenate([k_norm[:, :HD], _rope_cols(k_norm[:, HD:])], axis=-1)
    wo_l = w_out.astype(BF16)
    wu_l = w_mlp_up.astype(BF16)
    wd_l = w_mlp_down.astype(BF16)
    pw_l = pool_w.astype(BF16)

    h = x.reshape(nbatch * seq, D_MODEL)
    meta = meta_tokens.astype(F32)
    for layer in range(depth):
        if layer % 2 == 0:
            e = layer // 2
            zh, q, k, vt = _inproj(h, meta, mix_norm[layer][None], w_in_l, mla_q_a_norm[e][None],
                                  mla_kv_a_norm[e][None], wq_l, wkv_l, qn_l[e][None],
                                  kn_l[e][None], cos_t, sin_t, e, nbatch, lp)
            bound = (Q_SCALE * QK_DIM * 1.01) * jnp.max(jnp.abs(q_norm[e])) * jnp.max(jnp.abs(k_norm[e]))
            bound = bound.reshape(1).astype(F32)
            ob = lax.cond(bound[0] <= MAX_FIXED_SHIFT,
                          functools.partial(_attn, nbatch=nbatch, lp=lp, fixed_shift=True),
                          functools.partial(_attn, nbatch=nbatch, lp=lp, fixed_shift=False),
                          q, k, vt, bound)
            h = _mix_tail(zh, h, meta, ob, lower[e][None], 1.0 - lower[e][None], hgrn_out_norm[e][None],
                          tri2, wo_l, mlp_norm[layer][None], wu_l, wd_l, e, layer, nbatch, lp)
            meta = None
        else:
            o = layer // 2
            h = _pool_mlp(h, mix_norm[layer][None], pw_l, pool_scale[o][None],
                          mlp_norm[layer][None], wu_l, wd_l, o, layer, nbatch, lp,
                          to_frames=layer == depth - 1)

    return h.reshape(nbatch, seq, D_MODEL)
```

```python
import functools

import numpy as np
import jax
import jax.numpy as jnp
from jax import lax
from jax.experimental import pallas as pl
from jax.experimental.pallas import tpu as pltpu

F32 = jnp.float32
BF16 = jnp.bfloat16

D_MODEL = 1024
D_FF = 4 * D_MODEL
EPS = 1e-6
N_META = 16
CHUNK = 64
HEADS = 4
HD = 128
HW = HEADS * HD
ROPE = 64
QK_DIM = HD + ROPE
QK_PAD = 256
Q_RANK = 256
KV_RANK = 256
ROPE_THETA = 10000.0
POOL_WINDOWS = (2, 4, 8, 16)
POOL_G = D_MODEL // len(POOL_WINDOWS)

LEAD = 1024
PAD = LEAD - N_META
TM = 512
LEAD_TILES = LEAD // TM
TQ = 1024
TK = 1024
SUB = 16
HALO = 32
IN_COLS = 4 * HW + Q_RANK + KV_RANK + ROPE
Q_SCALE = QK_DIM ** -0.5 * float(np.log2(np.e))
MAX_FIXED_SHIFT = 56.0
MASK_VALUE = -1e30
EXP2_CLAMP = 115.0
TINY = 1e-37
SUBLANES = 8
V7X_VMEM_BYTES = 64 * 1024 * 1024
VMEM_LIMIT = V7X_VMEM_BYTES - 8 * 1024 * 1024


def _rms(x, g):
    return x * lax.rsqrt(jnp.mean(x * x, axis=-1, keepdims=True) + EPS) * g


def _silu(x):
    hx = 0.5 * x
    return hx + hx * jnp.tanh(hx)


def _const_spec(shape, layer=None):
    nd = len(shape)
    if layer is None:
        return pl.BlockSpec(shape, lambda *_: (0,) * nd, pipeline_mode=pl.Buffered(1))
    return pl.BlockSpec((None,) + tuple(shape), lambda *_: (layer,) + (0,) * nd, pipeline_mode=pl.Buffered(1))


def _params(sem):
    return pltpu.CompilerParams(dimension_semantics=sem, vmem_limit_bytes=VMEM_LIMIT)


def _frame_tile(i, tpb):
    return (i // tpb) * (tpb - LEAD_TILES) + jnp.maximum(i % tpb - LEAD_TILES, 0)


def _stream_tile(h_ref, meta_ref, tt):
    lead = jnp.concatenate([jnp.zeros((TM - N_META, D_MODEL), F32), meta_ref[...]], axis=0)
    lead = jnp.where(tt == LEAD_TILES - 1, lead, 0.0)
    return jnp.where(tt >= LEAD_TILES, h_ref[...], lead)


MLP_PARTS = 4
MLP_SLAB = D_FF // MLP_PARTS


def _mlp_up(hn, wu_ref, c):
    a = jnp.dot(hn, wu_ref[:, c * MLP_SLAB:(c + 1) * MLP_SLAB], preferred_element_type=F32)
    a = jnp.maximum(a, 0.0)
    return (a * a).astype(BF16)


def _mlp_down(a, wd_ref, c):
    return jnp.dot(a, wd_ref[c * MLP_SLAB:(c + 1) * MLP_SLAB, :], preferred_element_type=F32)


def _mlp_staggered(hn, wu_ref, wd_ref, between=()):
    a = _mlp_up(hn, wu_ref, 0)
    acc = None
    for c in range(MLP_PARTS):
        a_next = _mlp_up(hn, wu_ref, c + 1) if c + 1 < MLP_PARTS else None
        d = _mlp_down(a, wd_ref, c)
        acc = d if acc is None else acc + d
        if c < len(between):
            between[c]()
        a = a_next
    return acc


def _rope(x, c, s):
    return x * c + pltpu.roll(x, HD // 2, axis=1) * s


def _inproj_body(tpb, ntiles, first_layer, *refs):
    if first_layer:
        meta_ref, refs = refs[0], refs[1:]
    (h_ref, g_ref, win_ref, qag_ref, kvag_ref, wq_ref, wkv_ref, qn_ref, kn_ref, cos_ref, sin_ref,
     zh_ref, q_ref, k_ref, vt_ref, qa_ref, kva_ref, kr_ref) = refs
    i = pl.program_id(0)

    @pl.when(i == 0)
    def _():
        qa_ref[1] = jnp.zeros(qa_ref.shape[1:], BF16)
        kva_ref[1] = jnp.zeros(kva_ref.shape[1:], BF16)
        kr_ref[1] = jnp.zeros(kr_ref.shape[1:], F32)

    q = jnp.dot(qa_ref[(i + 1) % 2], wq_ref[...], preferred_element_type=F32)
    kv = jnp.dot(kva_ref[(i + 1) % 2], wkv_ref[...], preferred_element_type=F32)
    kr = _rope_cols(kr_ref[(i + 1) % 2])

    tt = jnp.minimum(i, ntiles - 1) % tpb
    h = _stream_tile(h_ref, meta_ref, tt) if first_layer else h_ref[...]
    u = _rms(h, g_ref[...]).astype(BF16)
    lat = jnp.dot(u, win_ref[:, 4 * HW:], preferred_element_type=F32)
    zh_ref[...] = jnp.dot(u, win_ref[:, :4 * HW], preferred_element_type=F32)
    qa_ref[i % 2] = _rms(lat[:, :Q_RANK], qag_ref[...]).astype(BF16)
    kva_ref[i % 2] = _rms(lat[:, Q_RANK:Q_RANK + KV_RANK], kvag_ref[...]).astype(BF16)
    kr_ref[i % 2] = lat[:, Q_RANK + KV_RANK:]

    vt_ref[...] = kv[:, HW:].T.astype(BF16)
    c = cos_ref[...]
    s = sin_ref[...]
    qg = qn_ref[...]
    kg = kn_ref[...]
    kr_ss = jnp.sum(kr * kr, axis=-1, keepdims=True)
    for hd in range(HEADS):
        qa = q[:, hd * QK_PAD:hd * QK_PAD + HD]
        qb = q[:, hd * QK_PAD + HD:(hd + 1) * QK_PAD]
        ss = jnp.sum(qa * qa + qb * qb, axis=-1, keepdims=True)
        inv = lax.rsqrt(ss * (1.0 / QK_DIM) + EPS)
        q_ref[:, hd * QK_PAD:hd * QK_PAD + HD] = (qa * inv * qg[:, :HD]).astype(BF16)
        q_ref[:, hd * QK_PAD + HD:(hd + 1) * QK_PAD] = _rope(qb * inv * qg[:, HD:], c, s).astype(BF16)
        ka = kv[:, hd * HD:(hd + 1) * HD]
        ss = jnp.sum(ka * ka, axis=-1, keepdims=True) + kr_ss
        inv = lax.rsqrt(ss * (1.0 / QK_DIM) + EPS)
        k_ref[:, hd * QK_PAD:hd * QK_PAD + HD] = (ka * inv * kg[:, :HD]).astype(BF16)
        k_ref[:, hd * QK_PAD + HD:(hd + 1) * QK_PAD] = _rope(kr * inv * kg[:, HD:], c, s).astype(BF16)


def _inproj(h, meta, g, win, qag, kvag, wq, wkv, qn, kn, cos_t, sin_t, e, nbatch, lp):
    r = nbatch * lp
    tpb = lp // TM
    ntiles = r // TM
    first_layer = meta is not None
    cur = lambda i: jnp.minimum(i, ntiles - 1)
    prev = lambda i: jnp.maximum(i - 1, 0)
    row = lambda i: (prev(i), 0)
    tab = lambda i: (prev(i) % tpb, 0)
    h_spec = pl.BlockSpec((TM, D_MODEL),
                          (lambda i: (_frame_tile(cur(i), tpb), 0)) if first_layer else (lambda i: (cur(i), 0)))
    lead_specs = [_const_spec((N_META, D_MODEL))] if first_layer else []
    lead_args = [meta] if first_layer else []
    return pl.pallas_call(
        functools.partial(_inproj_body, tpb, ntiles, first_layer),
        grid=(ntiles + 1,),
        in_specs=lead_specs + [
            h_spec,
            _const_spec((1, D_MODEL)),
            _const_spec((D_MODEL, IN_COLS), e),
            _const_spec((1, Q_RANK)),
            _const_spec((1, KV_RANK)),
            _const_spec((Q_RANK, HEADS * QK_PAD), e),
            _const_spec((KV_RANK, 2 * HW), e),
            _const_spec((1, QK_PAD)),
            _const_spec((1, QK_PAD)),
            pl.BlockSpec((TM, HD), tab),
            pl.BlockSpec((TM, HD), tab),
        ],
        out_specs=[
            pl.BlockSpec((TM, 4 * HW), lambda i: (cur(i), 0)),
            pl.BlockSpec((TM, HEADS * QK_PAD), row),
            pl.BlockSpec((TM, HEADS * QK_PAD), row),
            pl.BlockSpec((HW, TM), lambda i: (0, prev(i))),
        ],
        out_shape=[
            jax.ShapeDtypeStruct((r, 4 * HW), F32),
            jax.ShapeDtypeStruct((r, HEADS * QK_PAD), BF16),
            jax.ShapeDtypeStruct((r, HEADS * QK_PAD), BF16),
            jax.ShapeDtypeStruct((HW, r), BF16),
        ],
        scratch_shapes=[pltpu.VMEM((2, TM, Q_RANK), BF16), pltpu.VMEM((2, TM, KV_RANK), BF16),
                        pltpu.VMEM((2, TM, ROPE), F32)],
        compiler_params=_params(("arbitrary",)),
        name="inproj",
    )(*lead_args, h, g, win, qag, kvag, wq, wkv, qn, kn, cos_t, sin_t)


def _group_rows(rows):
    return jnp.concatenate([jnp.broadcast_to(r, (SUB, HD)) for r in rows], axis=0)


def _hgrn_stages(zh_ref, lb_ref, one_m_lb_ref, og_ref, tri_ref, st_ref, keep, o_ref):
    tri2 = tri_ref[...]
    lb = lb_ref[...]
    one_m_lb = one_m_lb_ref[...]
    og = og_ref[...]
    tt = lax.broadcasted_iota(jnp.int32, (CHUNK, CHUNK), 0)
    ss_ = lax.broadcasted_iota(jnp.int32, (CHUNK, CHUNK), 1)
    causal = ss_ <= tt
    nsub = CHUNK // SUB
    zero_row = jnp.zeros((1, HD), F32)
    zero_sub = jnp.zeros((SUB, HD), BF16)

    heads = range(HEADS)
    sls = [slice(hd * HD, (hd + 1) * HD) for hd in heads]

    chunk_rows = [slice(c * CHUNK, (c + 1) * CHUNK) for c in range(TM // CHUNK)]

    def gates_and_decay(rows):
        hq = zh_ref[rows, 0:HW]
        hf = zh_ref[rows, HW:2 * HW]
        hi = zh_ref[rows, 2 * HW:3 * HW]
        hg = zh_ref[rows, 3 * HW:4 * HW]
        q = _silu(hq)
        gate = _silu(hg)
        t = jnp.exp(-jnp.abs(hf))
        r = 1.0 / (1.0 + t)
        tr = t * r
        pos = hf >= 0.0
        log2f = jnp.log2(jnp.maximum(lb + one_m_lb * jnp.where(pos, r, tr), TINY))
        k = one_m_lb * jnp.where(pos, tr, r)
        g1 = log2f.astype(BF16)
        g2 = (log2f - g1.astype(F32)).astype(BF16)
        b = jnp.dot(tri2, jnp.concatenate([g1, g2], axis=0), preferred_element_type=F32)
        vt = [hi[:, sls[hd]].T.astype(BF16) for hd in heads]
        return q, k, b, vt, gate

    def intra_chunk(q, k, b, vt, gate):
        att, q_in, k_out, decay = [], [], [], []
        for hd in heads:
            bh = b[:, sls[hd]]
            b_last = bh[CHUNK - 1:CHUNK, :]
            refs = [zero_row] + [bh[i * SUB - 1:i * SUB, :] for i in range(1, nsub)]
            dq = bh - _group_rows(refs)
            qe = q[:, sls[hd]] * jnp.exp2(dq)
            ke = k[:, sls[hd]] * jnp.exp2(jnp.minimum(-dq, EXP2_CLAMP))
            keb = ke.astype(BF16)
            qcat = []
            kcat = []
            for j in range(nsub):
                qcat.append(jnp.concatenate(
                    [zero_sub if i < j else
                     qe[i * SUB:(i + 1) * SUB].astype(BF16) if i == j else
                     (qe[i * SUB:(i + 1) * SUB] * jnp.exp2(refs[i] - refs[j])).astype(BF16)
                     for i in range(nsub)], axis=0))
                kcat.append(jnp.concatenate(
                    [keb[j * SUB:(j + 1) * SUB] if i == j else zero_sub for i in range(nsub)], axis=0))
            q_in.append(qcat[0])
            att.append(lax.dot_general(jnp.concatenate(qcat, axis=1), jnp.concatenate(kcat, axis=1),
                                       (((1,), (1,)), ((), ())), preferred_element_type=F32))
            to_end = [jnp.exp2(b_last - refs[i]) for i in range(nsub)]
            k_out.append((ke * _group_rows(to_end)).astype(BF16))
            decay.append(to_end[0])
        upd = [jnp.dot(vt[hd], k_out[hd], preferred_element_type=F32) for hd in heads]
        return att, q_in, upd, decay, vt, gate

    def stage1():
        return [gates_and_decay(rows) for rows in chunk_rows]

    def stage2(s1):
        return [intra_chunk(*c) for c in s1]

    def stage3(s2):
        st = [st_ref[hd] * keep for hd in heads]
        starts = []
        for att, q_in, upd, decay, vt, gate in s2:
            starts.append([st[hd].astype(BF16) for hd in heads])
            st = [decay[hd] * st[hd] + upd[hd] for hd in heads]
        for hd in heads:
            st_ref[hd] = st[hd]
        return starts

    def stage4(s2, starts):
        for rows, st0, (att, q_in, upd, decay, vt, gate) in zip(chunk_rows, starts, s2):
            for hd in heads:
                a = jnp.where(causal, att[hd], 0.0).astype(BF16)
                oh = lax.dot_general(jnp.concatenate([q_in[hd], a], axis=1),
                                     jnp.concatenate([st0[hd], vt[hd]], axis=1),
                                     (((1,), (1,)), ((), ())), preferred_element_type=F32)
                on = _rms(oh, og) * gate[:, sls[hd]]
                o_ref[rows, sls[hd]] = on.astype(BF16)

    return stage1, stage2, stage3, stage4


META_STEP, FULL_STEP, DIAG_STEP, NO_STEP = range(4)


def _attn_steps(lp):
    qi, fin, kind_a, kj_a, kind_b, kj_b = [], [], [], [], [], []
    for i in range(lp // TQ):
        tiles = [(META_STEP if j == 0 else DIAG_STEP if j == i else FULL_STEP, j) for j in range(i + 1)]
        for s in range(0, len(tiles), 2):
            a = tiles[s]
            b = tiles[s + 1] if s + 1 < len(tiles) else (NO_STEP, a[1])
            qi.append(i)
            fin.append(int(s + 2 >= len(tiles)))
            kind_a.append(a[0])
            kj_a.append(a[1])
            kind_b.append(b[0])
            kj_b.append(b[1])
    return [np.asarray(t, np.int32) for t in (qi, fin, kind_a, kj_a, kind_b, kj_b)]


def _attn_body(fixed_shift, qi_ref, fin_ref, kind_a_ref, kj_a_ref, kind_b_ref, kj_b_ref, bound_ref,
               q_ref, ka_ref, vta_ref, kb_ref, vtb_ref, o_ref, *scratch):
    if fixed_shift:
        l_ref, acc_ref = scratch
    else:
        m_ref, l_ref, acc_ref = scratch
    step = pl.program_id(1)
    qi = qi_ref[step]
    bound = bound_ref[0]

    def update(k_ref, vt_ref, qs, keys, mask, first):
        nk, nq = keys.stop - keys.start, qs.stop - qs.start
        for hd in range(HEADS):
            hq = slice(hd * QK_PAD, (hd + 1) * QK_PAD)
            hv = slice(hd * HD, (hd + 1) * HD)
            st = lax.dot_general(k_ref[keys, hq], q_ref[qs, hq], (((1,), (1,)), ((), ())),
                                 preferred_element_type=F32)
            if fixed_shift:
                p = jnp.exp2(st - bound)
                if mask is not None:
                    p = jnp.where(mask, p, 0.0)
            else:
                if mask is not None:
                    st = jnp.where(mask, st, MASK_VALUE)
                m_prev = m_ref[hd, :, qs]
                m_new = jnp.maximum(m_prev, jnp.max(st, axis=0, keepdims=True))
                alpha = jnp.exp2(m_prev - m_new)
                m_ref[hd, :, qs] = m_new
                p = jnp.exp2(st - m_new[0:1, :])
            part = jnp.sum(p.reshape(nk // SUBLANES, SUBLANES, nq), axis=0)
            pv = jnp.dot(vt_ref[hv, keys], p.astype(BF16), preferred_element_type=F32)
            if first and fixed_shift:
                l_ref[hd, :, qs] = part
                acc_ref[hv, qs] = pv
            elif fixed_shift:
                l_ref[hd, :, qs] += part
                acc_ref[hv, qs] += pv
            else:
                l_ref[hd, :, qs] = alpha * l_ref[hd, :, qs] + part
                acc_ref[hv, qs] = alpha[0:1, :] * acc_ref[hv, qs] + pv

    all_q = slice(0, TQ)
    all_keys = slice(0, TK)

    def key_tile(kind, kj, k_ref, vt_ref):
        @pl.when(kind == META_STEP)
        def _():
            if not fixed_shift:
                m_ref[...] = jnp.full_like(m_ref, MASK_VALUE)
                l_ref[...] = jnp.zeros_like(l_ref)
                acc_ref[...] = jnp.zeros_like(acc_ref)
            key = lax.broadcasted_iota(jnp.int32, (HD, TQ), 0) + (kj * TK + TK - HD)
            update(k_ref, vt_ref, all_q, slice(TK - HD, TK), key >= PAD, True)

        @pl.when(kind == FULL_STEP)
        def _():
            update(k_ref, vt_ref, all_q, all_keys, None, False)

        @pl.when(kind == DIAG_STEP)
        def _():
            half = TQ // 2
            key = lax.broadcasted_iota(jnp.int32, (half, TQ), 0)
            qry = lax.broadcasted_iota(jnp.int32, (half, TQ), 1)
            update(k_ref, vt_ref, all_q, slice(0, half), key // CHUNK <= qry // CHUNK, False)
            update(k_ref, vt_ref, slice(half, TQ), slice(half, TK),
                   (key // CHUNK <= qry // CHUNK)[:, :half], False)

    key_tile(kind_a_ref[step], kj_a_ref[step], ka_ref, vta_ref)
    key_tile(kind_b_ref[step], kj_b_ref[step], kb_ref, vtb_ref)

    @pl.when(fin_ref[step] == 1)
    def _():
        row = lax.broadcasted_iota(jnp.int32, (TQ, HD), 0) + qi * TQ
        valid = row >= PAD
        for hd in range(HEADS):
            hv = slice(hd * HD, (hd + 1) * HD)
            ot = acc_ref[hv, :] / jnp.sum(l_ref[hd], axis=0, keepdims=True)
            o_ref[:, hv] = jnp.where(valid, ot.T, 0.0).astype(BF16)


def _attn(q, k, vt, bound, nbatch, lp, fixed_shift):
    r = q.shape[0]
    nq, nk = lp // TQ, lp // TK
    tables = _attn_steps(lp)
    stats = [pltpu.VMEM((HEADS, SUBLANES, TQ), F32)] * (1 if fixed_shift else 2)
    qmap = lambda b, s, qi, fin, kind_a, kj_a, kind_b, kj_b, bd: (b * nq + qi[s], 0)
    grid_spec = pltpu.PrefetchScalarGridSpec(
        num_scalar_prefetch=7,
        grid=(nbatch, len(tables[0])),
        in_specs=[
            pl.BlockSpec((TQ, HEADS * QK_PAD), qmap),
            pl.BlockSpec((TK, HEADS * QK_PAD),
                         lambda b, s, qi, fin, kind_a, kj_a, kind_b, kj_b, bd: (b * nk + kj_a[s], 0)),
            pl.BlockSpec((HW, TK),
                         lambda b, s, qi, fin, kind_a, kj_a, kind_b, kj_b, bd: (0, b * nk + kj_a[s])),
            pl.BlockSpec((TK, HEADS * QK_PAD),
                         lambda b, s, qi, fin, kind_a, kj_a, kind_b, kj_b, bd: (b * nk + kj_b[s], 0)),
            pl.BlockSpec((HW, TK),
                         lambda b, s, qi, fin, kind_a, kj_a, kind_b, kj_b, bd: (0, b * nk + kj_b[s])),
        ],
        out_specs=pl.BlockSpec((TQ, HW), qmap),
        scratch_shapes=stats + [pltpu.VMEM((HW, TQ), F32)],
    )
    return pl.pallas_call(
        functools.partial(_attn_body, fixed_shift),
        grid_spec=grid_spec,
        out_shape=jax.ShapeDtypeStruct((r, HW), BF16),
        compiler_params=_params(("parallel", "arbitrary")),
        name="attn_fixed_shift" if fixed_shift else "attn_online",
    )(*[jnp.asarray(t) for t in tables], bound, q, k, vt, k, vt)


def _mix_tail_body(tpb, ntiles, first_layer, *refs):
    if first_layer:
        meta_ref, refs = refs[0], refs[1:]
    (zh_ref, h_ref, ob_ref, lb_ref, one_m_lb_ref, og_ref, tri_ref, wo_ref, g_ref, wu_ref, wd_ref,
     out_ref, st_ref, oa_ref) = refs
    i = pl.program_id(0)

    @pl.when(i == 0)
    def _():
        st_ref[...] = jnp.zeros_like(st_ref)

    keep = (jnp.minimum(i, ntiles - 1) % tpb != 0).astype(F32)
    stage1, stage2, stage3, stage4 = _hgrn_stages(
        zh_ref, lb_ref, one_m_lb_ref, og_ref, tri_ref, st_ref, keep, oa_ref.at[i % 2])

    tt = jnp.maximum(i - 1, 0) % tpb

    @pl.when(tt == 0)
    def _():
        s2 = stage2(stage1())
        stage4(s2, stage3(s2))
        out_ref[...] = jnp.zeros_like(out_ref)

    @pl.when(tt != 0)
    def _():
        h = _stream_tile(h_ref, meta_ref, tt) if first_layer else h_ref[...]
        mix = jnp.dot(oa_ref[(i + 1) % 2], wo_ref[:HW, :], preferred_element_type=F32)
        mix = mix + jnp.dot(ob_ref[...], wo_ref[HW:, :], preferred_element_type=F32)
        s1 = stage1()
        h = h + mix
        hn = _rms(h, g_ref[...]).astype(BF16)
        carried = {}

        def run2():
            carried["s2"] = stage2(s1)

        def run3():
            carried["o"] = stage3(carried["s2"])

        def run4():
            stage4(carried["s2"], carried["o"])

        out_ref[...] = h + _mlp_staggered(hn, wu_ref, wd_ref, between=(run2, run3, run4))


def _mix_tail(zh, h, meta, ob, lb, one_m_lb, og, tri2, wo, g, wu, wd, e, layer, nbatch, lp):
    r = nbatch * lp
    tpb = lp // TM
    ntiles = r // TM
    first_layer = meta is not None
    cur = lambda i: jnp.minimum(i, ntiles - 1)
    prev = lambda i: jnp.maximum(i - 1, 0)
    row = lambda i: (prev(i), 0)
    h_spec = pl.BlockSpec((TM, D_MODEL),
                          (lambda i: (_frame_tile(prev(i), tpb), 0)) if first_layer else row)
    lead_specs = [_const_spec((N_META, D_MODEL))] if first_layer else []
    lead_args = [meta] if first_layer else []
    return pl.pallas_call(
        functools.partial(_mix_tail_body, tpb, ntiles, first_layer),
        grid=(ntiles + 1,),
        in_specs=lead_specs + [
            pl.BlockSpec((TM, 4 * HW), lambda i: (cur(i), 0)),
            h_spec,
            pl.BlockSpec((TM, HW), row),
            _const_spec((1, HW)),
            _const_spec((1, HW)),
            _const_spec((1, HD)),
            _const_spec((CHUNK, 2 * CHUNK)),
            _const_spec((2 * HW, D_MODEL), e),
            _const_spec((1, D_MODEL)),
            _const_spec((D_MODEL, D_FF), layer),
            _const_spec((D_FF, D_MODEL), layer),
        ],
        out_specs=pl.BlockSpec((TM, D_MODEL), row),
        out_shape=jax.ShapeDtypeStruct((r, D_MODEL), F32),
        scratch_shapes=[pltpu.VMEM((HEADS, HD, HD), F32), pltpu.VMEM((2, TM, HW), BF16)],
        compiler_params=_params(("arbitrary",)),
        name="mix_tail",
    )(*lead_args, zh, h, ob, lb, one_m_lb, og, tri2, wo, g, wu, wd)


def _pool_mlp_body(tpb, ntiles, h_ref, halo_ref, gm_ref, pw_ref, ps_ref, g_ref, wu_ref, wd_ref,
                   out_ref, u_ref, a_ref, b_ref, hm_ref, hn_ref):
    i = pl.program_id(0)
    tt = jnp.minimum(i, ntiles - 1) % tpb
    n = TM + HALO
    g = POOL_G
    ys = [None] * len(POOL_WINDOWS)

    def normalise():
        gm = gm_ref[...]
        keep = (tt != 0).astype(F32)
        u_ref[0:HALO, :] = _rms(halo_ref[...], gm) * keep
        u_ref[HALO:, :] = _rms(h_ref[...], gm)
        a_ref[8:n, :] = u_ref[8:n, :] + u_ref[7:n - 1, :]

    def group(gi):
        w = POOL_WINDOWS[gi]
        cols = slice(gi * g, (gi + 1) * g)
        win = (a_ref, b_ref, a_ref, b_ref)[gi][HALO:, cols]
        pos = lax.broadcasted_iota(jnp.int32, (TM, g), 0) + (tt * TM - PAD)
        cnt = jnp.minimum(jnp.maximum(pos + 1, 1).astype(F32), float(w))
        d = win / cnt - u_ref[HALO:, cols]
        ys[gi] = jnp.dot(d.astype(BF16), pw_ref[gi], preferred_element_type=F32) * ps_ref[:, cols]

    def first_groups():
        b_ref[16:n, g:] = a_ref[16:n, g:] + a_ref[14:n - 2, g:]
        group(0)
        group(1)

    def last_groups():
        a_ref[24:n, 2 * g:] = b_ref[24:n, 2 * g:] + b_ref[20:n - 4, 2 * g:]
        b_ref[32:n, 3 * g:] = a_ref[32:n, 3 * g:] + a_ref[24:n - 8, 3 * g:]
        group(2)
        group(3)

    def finish():
        hm = h_ref[...] + jnp.concatenate(ys, axis=1)
        hm_ref[i % 2] = hm
        hn_ref[i % 2] = _rms(hm, g_ref[...]).astype(BF16)

    mlp_is_zero = jnp.maximum(i - 1, 0) % tpb == 0

    @pl.when(mlp_is_zero)
    def _():
        normalise()
        first_groups()
        last_groups()
        finish()
        out_ref[...] = jnp.zeros_like(out_ref)

    @pl.when(jnp.logical_not(mlp_is_zero))
    def _():
        acc = _mlp_staggered(hn_ref[(i + 1) % 2], wu_ref, wd_ref,
                             between=(normalise, first_groups, last_groups, finish))
        out_ref[...] = hm_ref[(i + 1) % 2] + acc


def _pool_mlp(h, gm, pw, ps, g, wu, wd, o, layer, nbatch, lp, to_frames):
    r = nbatch * lp
    tpb = lp // TM
    ntiles = r // TM
    cur = lambda i: jnp.minimum(i, ntiles - 1)
    prev = lambda i: jnp.maximum(i - 1, 0)
    out_rows = nbatch * (lp - LEAD) if to_frames else r
    return pl.pallas_call(
        functools.partial(_pool_mlp_body, tpb, ntiles),
        grid=(ntiles + 1,),
        in_specs=[
            pl.BlockSpec((TM, D_MODEL), lambda i: (cur(i), 0)),
            pl.BlockSpec((HALO, D_MODEL), lambda i: (jnp.maximum(cur(i) * (TM // HALO) - 1, 0), 0)),
            _const_spec((1, D_MODEL)),
            _const_spec((len(POOL_WINDOWS), POOL_G, POOL_G), o),
            _const_spec((1, D_MODEL)),
            _const_spec((1, D_MODEL)),
            _const_spec((D_MODEL, D_FF), layer),
            _const_spec((D_FF, D_MODEL), layer),
        ],
        out_specs=pl.BlockSpec((TM, D_MODEL), (lambda i: (_frame_tile(prev(i), tpb), 0)) if to_frames
                               else (lambda i: (prev(i), 0))),
        out_shape=jax.ShapeDtypeStruct((out_rows, D_MODEL), F32),
        scratch_shapes=[pltpu.VMEM((TM + HALO, D_MODEL), F32)] * 3 + [
            pltpu.VMEM((2, TM, D_MODEL), F32), pltpu.VMEM((2, TM, D_MODEL), BF16)],
        compiler_params=_params(("arbitrary",)),
        name="pool_mlp",
    )(h, h, gm, pw, ps, g, wu, wd)


def _rope_cols(w):
    half = ROPE // 2
    z = jnp.zeros(w.shape[:-1] + (half,), w.dtype)
    return jnp.concatenate([w[..., :half], z, w[..., half:], z], axis=-1)


def _qk_cols(w):
    w = w.reshape(w.shape[:-1] + (HEADS, QK_DIM))
    w = jnp.concatenate([w[..., :HD], _rope_cols(w[..., HD:])], axis=-1)
    return w.reshape(w.shape[:-2] + (HEADS * QK_PAD,))


def _rope_tables(lp):
    half = ROPE // 2
    inv = ROPE_THETA ** (-np.arange(half, dtype=np.float64) / half)
    pos = np.maximum(np.arange(lp, dtype=np.float64) - PAD, 0.0)
    ang = pos[:, None] * inv[None, :]
    c = jnp.asarray(np.cos(ang).astype(np.float32))
    s = jnp.asarray(np.sin(ang).astype(np.float32))
    z = jnp.zeros_like(c)
    return (jnp.concatenate([c, z, c, z], axis=1), jnp.concatenate([-s, z, s, z], axis=1))


def kernel(x, meta_tokens, mix_norm, mlp_norm, w_mlp_up, w_mlp_down, w_in, hgrn_lb, hgrn_out_norm, mla_q_a_norm, mla_kv_a_norm, w_q_up, w_kv_up, q_norm, k_norm, w_out, pool_w, pool_scale):
    nbatch, seq, _ = x.shape
    depth = mix_norm.shape[0]
    assert seq % TQ == 0 and depth % 2 == 0
    lp = seq + LEAD

    cos_t, sin_t = _rope_tables(lp)
    lb_cum = jnp.cumsum(jax.nn.softmax(hgrn_lb.astype(F32), axis=0), axis=0)
    lower = lb_cum - lb_cum[0:1]
    tri = jnp.tril(jnp.ones((CHUNK, CHUNK), F32)).astype(BF16)
    tri2 = jnp.concatenate([tri, tri], axis=1)

    w_in_l = w_in.astype(BF16)
    wq_l = _qk_cols(w_q_up).astype(BF16)
    wkv = w_kv_up.reshape(w_kv_up.shape[0], KV_RANK, HEADS, 2 * HD)
    wkv_l = jnp.concatenate([wkv[..., :HD].reshape(-1, KV_RANK, HW),
                             wkv[..., HD:].reshape(-1, KV_RANK, HW)], axis=-1).astype(BF16)
    qn_l = jnp.concatenate([q_norm[:, :HD], _rope_cols(q_norm[:, HD:])], axis=-1) * Q_SCALE
    kn_l = jnp.concatenate([k_norm[:, :HD], _rope_cols(k_norm[:, HD:])], axis=-1)
    wo_l = w_out.astype(BF16)
    wu_l = w_mlp_up.astype(BF16)
    wd_l = w_mlp_down.astype(BF16)
    pw_l = pool_w.astype(BF16)

    h = x.reshape(nbatch * seq, D_MODEL)
    meta = meta_tokens.astype(F32)
    for layer in range(depth):
        if layer % 2 == 0:
            e = layer // 2
            zh, q, k, vt = _inproj(h, meta, mix_norm[layer][None], w_in_l, mla_q_a_norm[e][None],
                                  mla_kv_a_norm[e][None], wq_l, wkv_l, qn_l[e][None],
                                  kn_l[e][None], cos_t, sin_t, e, nbatch, lp)
            bound = (Q_SCALE * QK_DIM * 1.01) * jnp.max(jnp.abs(q_norm[e])) * jnp.max(jnp.abs(k_norm[e]))
            bound = bound.reshape(1).astype(F32)
            ob = lax.cond(bound[0] <= MAX_FIXED_SHIFT,
                          functools.partial(_attn, nbatch=nbatch, lp=lp, fixed_shift=True),
                          functools.partial(_attn, nbatch=nbatch, lp=lp, fixed_shift=False),
                          q, k, vt, bound)
            h = _mix_tail(zh, h, meta, ob, lower[e][None], 1.0 - lower[e][None], hgrn_out_norm[e][None],
                          tri2, wo_l, mlp_norm[layer][None], wu_l, wd_l, e, layer, nbatch, lp)
            meta = None
        else:
            o = layer // 2
            h = _pool_mlp(h, mix_norm[layer][None], pw_l, pool_scale[o][None],
                          mlp_norm[layer][None], wu_l, wd_l, o, layer, nbatch, lp,
                          to_frames=layer == depth - 1)

    return h.reshape(nbatch, seq, D_MODEL)
```

```python
import functools

import numpy as np
import jax
import jax.numpy as jnp
from jax import lax
from jax.experimental import pallas as pl
from jax.experimental.pallas import tpu as pltpu

F32 = jnp.float32
BF16 = jnp.bfloat16

D_MODEL = 1024
D_FF = 4 * D_MODEL
EPS = 1e-6
N_META = 16
CHUNK = 64
HEADS = 4
HD = 128
HW = HEADS * HD
ROPE = 64
QK_DIM = HD + ROPE
QK_PAD = 256
Q_RANK = 256
KV_RANK = 256
ROPE_THETA = 10000.0
POOL_WINDOWS = (2, 4, 8, 16)
POOL_G = D_MODEL // len(POOL_WINDOWS)

LEAD = 1024
PAD = LEAD - N_META
TM = 512
LEAD_TILES = LEAD // TM
TQ = 1024
TK = 1024
SUB = 16
HALO = 32
IN_COLS = 4 * HW + Q_RANK + KV_RANK + ROPE
Q_SCALE = QK_DIM ** -0.5 * float(np.log2(np.e))
MAX_FIXED_SHIFT = 56.0
MASK_VALUE = -1e30
EXP2_CLAMP = 115.0
TINY = 1e-37
SUBLANES = 8
V7X_VMEM_BYTES = 64 * 1024 * 1024
VMEM_LIMIT = V7X_VMEM_BYTES - 8 * 1024 * 1024


def _rms(x, g):
    return x * lax.rsqrt(jnp.mean(x * x, axis=-1, keepdims=True) + EPS) * g


def _silu(x):
    hx = 0.5 * x
    return hx + hx * jnp.tanh(hx)


def _const_spec(shape, layer=None):
    nd = len(shape)
    if layer is None:
        return pl.BlockSpec(shape, lambda *_: (0,) * nd, pipeline_mode=pl.Buffered(1))
    return pl.BlockSpec((None,) + tuple(shape), lambda *_: (layer,) + (0,) * nd, pipeline_mode=pl.Buffered(1))


def _params(sem):
    return pltpu.CompilerParams(dimension_semantics=sem, vmem_limit_bytes=VMEM_LIMIT)


def _frame_tile(i, tpb):
    return (i // tpb) * (tpb - LEAD_TILES) + jnp.maximum(i % tpb - LEAD_TILES, 0)


def _stream_tile(h_ref, meta_ref, tt):
    lead = jnp.concatenate([jnp.zeros((TM - N_META, D_MODEL), F32), meta_ref[...]], axis=0)
    lead = jnp.where(tt == LEAD_TILES - 1, lead, 0.0)
    return jnp.where(tt >= LEAD_TILES, h_ref[...], lead)


MLP_PARTS = 4
MLP_SLAB = D_FF // MLP_PARTS


def _mlp_up(hn, wu_ref, c):
    a = jnp.dot(hn, wu_ref[:, c * MLP_SLAB:(c + 1) * MLP_SLAB], preferred_element_type=F32)
    a = jnp.maximum(a, 0.0)
    return (a * a).astype(BF16)


def _mlp_down(a, wd_ref, c):
    return jnp.dot(a, wd_ref[c * MLP_SLAB:(c + 1) * MLP_SLAB, :], preferred_element_type=F32)


def _mlp_staggered(hn, wu_ref, wd_ref, between=()):
    a = _mlp_up(hn, wu_ref, 0)
    acc = None
    for c in range(MLP_PARTS):
        a_next = _mlp_up(hn, wu_ref, c + 1) if c + 1 < MLP_PARTS else None
        d = _mlp_down(a, wd_ref, c)
        acc = d if acc is None else acc + d
        if c < len(between):
            between[c]()
        a = a_next
    return acc


def _rope(x, c, s):
    return x * c + pltpu.roll(x, HD // 2, axis=1) * s


def _inproj_body(tpb, ntiles, first_layer, *refs):
    if first_layer:
        meta_ref, refs = refs[0], refs[1:]
    (h_ref, g_ref, win_ref, qag_ref, kvag_ref, wq_ref, wkv_ref, qn_ref, kn_ref, cos_ref, sin_ref,
     zh_ref, q_ref, k_ref, vt_ref, qa_ref, kva_ref, kr_ref) = refs
    i = pl.program_id(0)

    @pl.when(i == 0)
    def _():
        qa_ref[1] = jnp.zeros(qa_ref.shape[1:], BF16)
        kva_ref[1] = jnp.zeros(kva_ref.shape[1:], BF16)
        kr_ref[1] = jnp.zeros(kr_ref.shape[1:], F32)

    q = jnp.dot(qa_ref[(i + 1) % 2], wq_ref[...], preferred_element_type=F32)
    kv = jnp.dot(kva_ref[(i + 1) % 2], wkv_ref[...], preferred_element_type=F32)
    kr = _rope_cols(kr_ref[(i + 1) % 2])

    tt = jnp.minimum(i, ntiles - 1) % tpb
    h = _stream_tile(h_ref, meta_ref, tt) if first_layer else h_ref[...]
    u = _rms(h, g_ref[...]).astype(BF16)
    lat = jnp.dot(u, win_ref[:, 4 * HW:], preferred_element_type=F32)
    zh_ref[...] = jnp.dot(u, win_ref[:, :4 * HW], preferred_element_type=F32)
    qa_ref[i % 2] = _rms(lat[:, :Q_RANK], qag_ref[...]).astype(BF16)
    kva_ref[i % 2] = _rms(lat[:, Q_RANK:Q_RANK + KV_RANK], kvag_ref[...]).astype(BF16)
    kr_ref[i % 2] = lat[:, Q_RANK + KV_RANK:]

    vt_ref[...] = kv[:, HW:].T.astype(BF16)
    c = cos_ref[...]
    s = sin_ref[...]
    qg = qn_ref[...]
    kg = kn_ref[...]
    kr_ss = jnp.sum(kr * kr, axis=-1, keepdims=True)
    for hd in range(HEADS):
        qa = q[:, hd * QK_PAD:hd * QK_PAD + HD]
        qb = q[:, hd * QK_PAD + HD:(hd + 1) * QK_PAD]
        ss = jnp.sum(qa * qa + qb * qb, axis=-1, keepdims=True)
        inv = lax.rsqrt(ss * (1.0 / QK_DIM) + EPS)
        q_ref[:, hd * QK_PAD:hd * QK_PAD + HD] = (qa * inv * qg[:, :HD]).astype(BF16)
        q_ref[:, hd * QK_PAD + HD:(hd + 1) * QK_PAD] = _rope(qb * inv * qg[:, HD:], c, s).astype(BF16)
        ka = kv[:, hd * HD:(hd + 1) * HD]
        ss = jnp.sum(ka * ka, axis=-1, keepdims=True) + kr_ss
        inv = lax.rsqrt(ss * (1.0 / QK_DIM) + EPS)
        k_ref[:, hd * QK_PAD:hd * QK_PAD + HD] = (ka * inv * kg[:, :HD]).astype(BF16)
        k_ref[:, hd * QK_PAD + HD:(hd + 1) * QK_PAD] = _rope(kr * inv * kg[:, HD:], c, s).astype(BF16)


def _inproj(h, meta, g, win, qag, kvag, wq, wkv, qn, kn, cos_t, sin_t, e, nbatch, lp):
    r = nbatch * lp
    tpb = lp // TM
    ntiles = r // TM
    first_layer = meta is not None
    cur = lambda i: jnp.minimum(i, ntiles - 1)
    prev = lambda i: jnp.maximum(i - 1, 0)
    row = lambda i: (prev(i), 0)
    tab = lambda i: (prev(i) % tpb, 0)
    h_spec = pl.BlockSpec((TM, D_MODEL),
                          (lambda i: (_frame_tile(cur(i), tpb), 0)) if first_layer else (lambda i: (cur(i), 0)))
    lead_specs = [_const_spec((N_META, D_MODEL))] if first_layer else []
    lead_args = [meta] if first_layer else []
    return pl.pallas_call(
        functools.partial(_inproj_body, tpb, ntiles, first_layer),
        grid=(ntiles + 1,),
        in_specs=lead_specs + [
            h_spec,
            _const_spec((1, D_MODEL)),
            _const_spec((D_MODEL, IN_COLS), e),
            _const_spec((1, Q_RANK)),
            _const_spec((1, KV_RANK)),
            _const_spec((Q_RANK, HEADS * QK_PAD), e),
            _const_spec((KV_RANK, 2 * HW), e),
            _const_spec((1, QK_PAD)),
            _const_spec((1, QK_PAD)),
            pl.BlockSpec((TM, HD), tab),
            pl.BlockSpec((TM, HD), tab),
        ],
        out_specs=[
            pl.BlockSpec((TM, 4 * HW), lambda i: (cur(i), 0)),
            pl.BlockSpec((TM, HEADS * QK_PAD), row),
            pl.BlockSpec((TM, HEADS * QK_PAD), row),
            pl.BlockSpec((HW, TM), lambda i: (0, prev(i))),
        ],
        out_shape=[
            jax.ShapeDtypeStruct((r, 4 * HW), F32),
            jax.ShapeDtypeStruct((r, HEADS * QK_PAD), BF16),
            jax.ShapeDtypeStruct((r, HEADS * QK_PAD), BF16),
            jax.ShapeDtypeStruct((HW, r), BF16),
        ],
        scratch_shapes=[pltpu.VMEM((2, TM, Q_RANK), BF16), pltpu.VMEM((2, TM, KV_RANK), BF16),
                        pltpu.VMEM((2, TM, ROPE), F32)],
        compiler_params=_params(("arbitrary",)),
        name="inproj",
    )(*lead_args, h, g, win, qag, kvag, wq, wkv, qn, kn, cos_t, sin_t)


def _group_rows(rows):
    return jnp.concatenate([jnp.broadcast_to(r, (SUB, HD)) for r in rows], axis=0)


def _hgrn_stages(zh_ref, lb_ref, one_m_lb_ref, og_ref, tri_ref, st_ref, keep, o_ref):
    tri2 = tri_ref[...]
    lb = lb_ref[...]
    one_m_lb = one_m_lb_ref[...]
    og = og_ref[...]
    tt = lax.broadcasted_iota(jnp.int32, (CHUNK, CHUNK), 0)
    ss_ = lax.broadcasted_iota(jnp.int32, (CHUNK, CHUNK), 1)
    causal = ss_ <= tt
    nsub = CHUNK // SUB
    zero_row = jnp.zeros((1, HD), F32)
    zero_sub = jnp.zeros((SUB, HD), BF16)

    heads = range(HEADS)
    sls = [slice(hd * HD, (hd + 1) * HD) for hd in heads]

    chunk_rows = [slice(c * CHUNK, (c + 1) * CHUNK) for c in range(TM // CHUNK)]

    def gates_and_decay(rows):
        hq = zh_ref[rows, 0:HW]
        hf = zh_ref[rows, HW:2 * HW]
        hi = zh_ref[rows, 2 * HW:3 * HW]
        hg = zh_ref[rows, 3 * HW:4 * HW]
        q = _silu(hq)
        gate = _silu(hg)
        t = jnp.exp(-jnp.abs(hf))
        r = 1.0 / (1.0 + t)
        tr = t * r
        pos = hf >= 0.0
        log2f = jnp.log2(jnp.maximum(lb + one_m_lb * jnp.where(pos, r, tr), TINY))
        k = one_m_lb * jnp.where(pos, tr, r)
        g1 = log2f.astype(BF16)
        g2 = (log2f - g1.astype(F32)).astype(BF16)
        b = jnp.dot(tri2, jnp.concatenate([g1, g2], axis=0), preferred_element_type=F32)
        vt = [hi[:, sls[hd]].T.astype(BF16) for hd in heads]
        return q, k, b, vt, gate

    def intra_chunk(q, k, b, vt, gate):
        att, q_in, k_out, decay = [], [], [], []
        for hd in heads:
            bh = b[:, sls[hd]]
            b_last = bh[CHUNK - 1:CHUNK, :]
            refs = [zero_row] + [bh[i * SUB - 1:i * SUB, :] for i in range(1, nsub)]
            dq = bh - _group_rows(refs)
            qe = q[:, sls[hd]] * jnp.exp2(dq)
            ke = k[:, sls[hd]] * jnp.exp2(jnp.minimum(-dq, EXP2_CLAMP))
            keb = ke.astype(BF16)
            qcat = []
            kcat = []
            for j in range(nsub):
                qcat.append(jnp.concatenate(
                    [zero_sub if i < j else
                     qe[i * SUB:(i + 1) * SUB].astype(BF16) if i == j else
                     (qe[i * SUB:(i + 1) * SUB] * jnp.exp2(refs[i] - refs[j])).astype(BF16)
                     for i in range(nsub)], axis=0))
                kcat.append(jnp.concatenate(
                    [keb[j * SUB:(j + 1) * SUB] if i == j else zero_sub for i in range(nsub)], axis=0))
            q_in.append(qcat[0])
            att.append(lax.dot_general(jnp.concatenate(qcat, axis=1), jnp.concatenate(kcat, axis=1),
                                       (((1,), (1,)), ((), ())), preferred_element_type=F32))
            to_end = [jnp.exp2(b_last - refs[i]) for i in range(nsub)]
            k_out.append((ke * _group_rows(to_end)).astype(BF16))
            decay.append(to_end[0])
        upd = [jnp.dot(vt[hd], k_out[hd], preferred_element_type=F32) for hd in heads]
        return att, q_in, upd, decay, vt, gate

    def stage1():
        return [gates_and_decay(rows) for rows in chunk_rows]

    def stage2(s1):
        return [intra_chunk(*c) for c in s1]

    def stage3(s2):
        st = [st_ref[hd] * keep for hd in heads]
        starts = []
        for att, q_in, upd, decay, vt, gate in s2:
            starts.append([st[hd].astype(BF16) for hd in heads])
            st = [decay[hd] * st[hd] + upd[hd] for hd in heads]
        for hd in heads:
            st_ref[hd] = st[hd]
        return starts

    def stage4(s2, starts):
        for rows, st0, (att, q_in, upd, decay, vt, gate) in zip(chunk_rows, starts, s2):
            for hd in heads:
                a = jnp.where(causal, att[hd], 0.0).astype(BF16)
                oh = lax.dot_general(jnp.concatenate([q_in[hd], a], axis=1),
                                     jnp.concatenate([st0[hd], vt[hd]], axis=1),
                                     (((1,), (1,)), ((), ())), preferred_element_type=F32)
                on = _rms(oh, og) * gate[:, sls[hd]]
                o_ref[rows, sls[hd]] = on.astype(BF16)

    return stage1, stage2, stage3, stage4


META_STEP, FULL_STEP, DIAG_STEP, NO_STEP = range(4)


def _attn_steps(lp):
    qi, fin, kind_a, kj_a, kind_b, kj_b = [], [], [], [], [], []
    for i in range(lp // TQ):
        tiles = [(META_STEP if j == 0 else DIAG_STEP if j == i else FULL_STEP, j) for j in range(i + 1)]
        for s in range(0, len(tiles), 2):
            a = tiles[s]
            b = tiles[s + 1] if s + 1 < len(tiles) else (NO_STEP, a[1])
            qi.append(i)
            fin.append(int(s + 2 >= len(tiles)))
            kind_a.append(a[0])
            kj_a.append(a[1])
            kind_b.append(b[0])
            kj_b.append(b[1])
    return [np.asarray(t, np.int32) for t in (qi, fin, kind_a, kj_a, kind_b, kj_b)]


def _attn_body(fixed_shift, qi_ref, fin_ref, kind_a_ref, kj_a_ref, kind_b_ref, kj_b_ref, bound_ref,
               q_ref, ka_ref, vta_ref, kb_ref, vtb_ref, o_ref, *scratch):
    if fixed_shift:
        l_ref, acc_ref = scratch
    else:
        m_ref, l_ref, acc_ref = scratch
    step = pl.program_id(1)
    qi = qi_ref[step]
    bound = bound_ref[0]

    def update(k_ref, vt_ref, qs, keys, mask, first):
        nk, nq = keys.stop - keys.start, qs.stop - qs.start
        for hd in range(HEADS):
            hq = slice(hd * QK_PAD, (hd + 1) * QK_PAD)
            hv = slice(hd * HD, (hd + 1) * HD)
            st = lax.dot_general(k_ref[keys, hq], q_ref[qs, hq], (((1,), (1,)), ((), ())),
                                 preferred_element_type=F32)
            if fixed_shift:
                p = jnp.exp2(st - bound)
                if mask is not None:
                    p = jnp.where(mask, p, 0.0)
            else:
                if mask is not None:
                    st = jnp.where(mask, st, MASK_VALUE)
                m_prev = m_ref[hd, :, qs]
                m_new = jnp.maximum(m_prev, jnp.max(st, axis=0, keepdims=True))
                alpha = jnp.exp2(m_prev - m_new)
                m_ref[hd, :, qs] = m_new
                p = jnp.exp2(st - m_new[0:1, :])
            part = jnp.sum(p.reshape(nk // SUBLANES, SUBLANES, nq), axis=0)
            pv = jnp.dot(vt_ref[hv, keys], p.astype(BF16), preferred_element_type=F32)
            if first and fixed_shift:
                l_ref[hd, :, qs] = part
                acc_ref[hv, qs] = pv
            elif fixed_shift:
                l_ref[hd, :, qs] += part
                acc_ref[hv, qs] += pv
            else:
                l_ref[hd, :, qs] = alpha * l_ref[hd, :, qs] + part
                acc_ref[hv, qs] = alpha[0:1, :] * acc_ref[hv, qs] + pv

    all_q = slice(0, TQ)
    all_keys = slice(0, TK)

    def key_tile(kind, k_ref, vt_ref, may_be_first):
        if may_be_first:
            @pl.when(kind == META_STEP)
            def _():
                if not fixed_shift:
                    m_ref[...] = jnp.full_like(m_ref, MASK_VALUE)
                    l_ref[...] = jnp.zeros_like(l_ref)
                    acc_ref[...] = jnp.zeros_like(acc_ref)
                update(k_ref, vt_ref, all_q, slice(TK - N_META, TK), None, True)

        @pl.when(kind == FULL_STEP)
        def _():
            update(k_ref, vt_ref, all_q, all_keys, None, False)

        @pl.when(kind == DIAG_STEP)
        def _():
            half = TQ // 2
            key = lax.broadcasted_iota(jnp.int32, (half, TQ), 0)
            qry = lax.broadcasted_iota(jnp.int32, (half, TQ), 1)
            update(k_ref, vt_ref, all_q, slice(0, half), key // CHUNK <= qry // CHUNK, False)
            update(k_ref, vt_ref, slice(half, TQ), slice(half, TK),
                   (key // CHUNK <= qry // CHUNK)[:, :half], False)

    key_tile(kind_a_ref[step], ka_ref, vta_ref, True)
    key_tile(kind_b_ref[step], kb_ref, vtb_ref, False)

    @pl.when(fin_ref[step] == 1)
    def _():
        row = lax.broadcasted_iota(jnp.int32, (TQ, HD), 0) + qi * TQ
        valid = row >= PAD
        for hd in range(HEADS):
            hv = slice(hd * HD, (hd + 1) * HD)
            ot = acc_ref[hv, :] / jnp.sum(l_ref[hd], axis=0, keepdims=True)
            o_ref[:, hv] = jnp.where(valid, ot.T, 0.0).astype(BF16)


def _attn(q, k, vt, bound, nbatch, lp, fixed_shift):
    r = q.shape[0]
    nq, nk = lp // TQ, lp // TK
    tables = _attn_steps(lp)
    stats = [pltpu.VMEM((HEADS, SUBLANES, TQ), F32)] * (1 if fixed_shift else 2)
    qmap = lambda b, s, qi, fin, kind_a, kj_a, kind_b, kj_b, bd: (b * nq + qi[s], 0)
    grid_spec = pltpu.PrefetchScalarGridSpec(
        num_scalar_prefetch=7,
        grid=(nbatch, len(tables[0])),
        in_specs=[
            pl.BlockSpec((TQ, HEADS * QK_PAD), qmap),
            pl.BlockSpec((TK, HEADS * QK_PAD),
                         lambda b, s, qi, fin, kind_a, kj_a, kind_b, kj_b, bd: (b * nk + kj_a[s], 0)),
            pl.BlockSpec((HW, TK),
                         lambda b, s, qi, fin, kind_a, kj_a, kind_b, kj_b, bd: (0, b * nk + kj_a[s])),
            pl.BlockSpec((TK, HEADS * QK_PAD),
                         lambda b, s, qi, fin, kind_a, kj_a, kind_b, kj_b, bd: (b * nk + kj_b[s], 0)),
            pl.BlockSpec((HW, TK),
                         lambda b, s, qi, fin, kind_a, kj_a, kind_b, kj_b, bd: (0, b * nk + kj_b[s])),
        ],
        out_specs=pl.BlockSpec((TQ, HW), qmap),
        scratch_shapes=stats + [pltpu.VMEM((HW, TQ), F32)],
    )
    return pl.pallas_call(
        functools.partial(_attn_body, fixed_shift),
        grid_spec=grid_spec,
        out_shape=jax.ShapeDtypeStruct((r, HW), BF16),
        compiler_params=_params(("parallel", "arbitrary")),
        name="attn_fixed_shift" if fixed_shift else "attn_online",
    )(*[jnp.asarray(t) for t in tables], bound, q, k, vt, k, vt)


def _mix_tail_body(tpb, ntiles, first_layer, *refs):
    if first_layer:
        meta_ref, refs = refs[0], refs[1:]
    (zh_ref, h_ref, ob_ref, lb_ref, one_m_lb_ref, og_ref, tri_ref, wo_ref, g_ref, wu_ref, wd_ref,
     out_ref, st_ref, oa_ref) = refs
    i = pl.program_id(0)

    @pl.when(i == 0)
    def _():
        st_ref[...] = jnp.zeros_like(st_ref)

    keep = (jnp.minimum(i, ntiles - 1) % tpb != 0).astype(F32)
    stage1, stage2, stage3, stage4 = _hgrn_stages(
        zh_ref, lb_ref, one_m_lb_ref, og_ref, tri_ref, st_ref, keep, oa_ref.at[i % 2])

    tt = jnp.maximum(i - 1, 0) % tpb

    @pl.when(tt == 0)
    def _():
        s2 = stage2(stage1())
        stage4(s2, stage3(s2))
        out_ref[...] = jnp.zeros_like(out_ref)

    @pl.when(tt != 0)
    def _():
        h = _stream_tile(h_ref, meta_ref, tt) if first_layer else h_ref[...]
        mix = jnp.dot(oa_ref[(i + 1) % 2], wo_ref[:HW, :], preferred_element_type=F32)
        mix = mix + jnp.dot(ob_ref[...], wo_ref[HW:, :], preferred_element_type=F32)
        s1 = stage1()
        h = h + mix
        hn = _rms(h, g_ref[...]).astype(BF16)
        carried = {}

        def run2():
            carried["s2"] = stage2(s1)

        def run3():
            carried["o"] = stage3(carried["s2"])

        def run4():
            stage4(carried["s2"], carried["o"])

        out_ref[...] = h + _mlp_staggered(hn, wu_ref, wd_ref, between=(run2, run3, run4))


def _mix_tail(zh, h, meta, ob, lb, one_m_lb, og, tri2, wo, g, wu, wd, e, layer, nbatch, lp):
    r = nbatch * lp
    tpb = lp // TM
    ntiles = r // TM
    first_layer = meta is not None
    cur = lambda i: jnp.minimum(i, ntiles - 1)
    prev = lambda i: jnp.maximum(i - 1, 0)
    row = lambda i: (prev(i), 0)
    h_spec = pl.BlockSpec((TM, D_MODEL),
                          (lambda i: (_frame_tile(prev(i), tpb), 0)) if first_layer else row)
    lead_specs = [_const_spec((N_META, D_MODEL))] if first_layer else []
    lead_args = [meta] if first_layer else []
    return pl.pallas_call(
        functools.partial(_mix_tail_body, tpb, ntiles, first_layer),
        grid=(ntiles + 1,),
        in_specs=lead_specs + [
            pl.BlockSpec((TM, 4 * HW), lambda i: (cur(i), 0)),
            h_spec,
            pl.BlockSpec((TM, HW), row),
            _const_spec((1, HW)),
            _const_spec((1, HW)),
            _const_spec((1, HD)),
            _const_spec((CHUNK, 2 * CHUNK)),
            _const_spec((2 * HW, D_MODEL), e),
            _const_spec((1, D_MODEL)),
            _const_spec((D_MODEL, D_FF), layer),
            _const_spec((D_FF, D_MODEL), layer),
        ],
        out_specs=pl.BlockSpec((TM, D_MODEL), row),
        out_shape=jax.ShapeDtypeStruct((r, D_MODEL), F32),
        scratch_shapes=[pltpu.VMEM((HEADS, HD, HD), F32), pltpu.VMEM((2, TM, HW), BF16)],
        compiler_params=_params(("arbitrary",)),
        name="mix_tail",
    )(*lead_args, zh, h, ob, lb, one_m_lb, og, tri2, wo, g, wu, wd)


def _pool_mlp_body(tpb, ntiles, h_ref, halo_ref, gm_ref, pw_ref, ps_ref, g_ref, wu_ref, wd_ref,
                   out_ref, u_ref, a_ref, b_ref, hm_ref, hn_ref):
    i = pl.program_id(0)
    tt = jnp.minimum(i, ntiles - 1) % tpb
    n = TM + HALO
    g = POOL_G
    ys = [None] * len(POOL_WINDOWS)

    def normalise():
        gm = gm_ref[...]
        keep = (tt != 0).astype(F32)
        u_ref[0:HALO, :] = _rms(halo_ref[...], gm) * keep
        u_ref[HALO:, :] = _rms(h_ref[...], gm)
        a_ref[8:n, :] = u_ref[8:n, :] + u_ref[7:n - 1, :]

    def group(gi):
        w = POOL_WINDOWS[gi]
        cols = slice(gi * g, (gi + 1) * g)
        win = (a_ref, b_ref, a_ref, b_ref)[gi][HALO:, cols]
        pos = lax.broadcasted_iota(jnp.int32, (TM, g), 0) + (tt * TM - PAD)
        cnt = jnp.minimum(jnp.maximum(pos + 1, 1).astype(F32), float(w))
        d = win / cnt - u_ref[HALO:, cols]
        ys[gi] = jnp.dot(d.astype(BF16), pw_ref[gi], preferred_element_type=F32) * ps_ref[:, cols]

    def first_groups():
        b_ref[16:n, g:] = a_ref[16:n, g:] + a_ref[14:n - 2, g:]
        group(0)
        group(1)

    def last_groups():
        a_ref[24:n, 2 * g:] = b_ref[24:n, 2 * g:] + b_ref[20:n - 4, 2 * g:]
        b_ref[32:n, 3 * g:] = a_ref[32:n, 3 * g:] + a_ref[24:n - 8, 3 * g:]
        group(2)
        group(3)

    def finish():
        hm = h_ref[...] + jnp.concatenate(ys, axis=1)
        hm_ref[i % 2] = hm
        hn_ref[i % 2] = _rms(hm, g_ref[...]).astype(BF16)

    mlp_is_zero = jnp.maximum(i - 1, 0) % tpb == 0

    @pl.when(mlp_is_zero)
    def _():
        normalise()
        first_groups()
        last_groups()
        finish()
        out_ref[...] = jnp.zeros_like(out_ref)

    @pl.when(jnp.logical_not(mlp_is_zero))
    def _():
        acc = _mlp_staggered(hn_ref[(i + 1) % 2], wu_ref, wd_ref,
                             between=(normalise, first_groups, last_groups, finish))
        out_ref[...] = hm_ref[(i + 1) % 2] + acc


def _pool_mlp(h, gm, pw, ps, g, wu, wd, o, layer, nbatch, lp, to_frames):
    r = nbatch * lp
    tpb = lp // TM
    ntiles = r // TM
    cur = lambda i: jnp.minimum(i, ntiles - 1)
    prev = lambda i: jnp.maximum(i - 1, 0)
    out_rows = nbatch * (lp - LEAD) if to_frames else r
    return pl.pallas_call(
        functools.partial(_pool_mlp_body, tpb, ntiles),
        grid=(ntiles + 1,),
        in_specs=[
            pl.BlockSpec((TM, D_MODEL), lambda i: (cur(i), 0)),
            pl.BlockSpec((HALO, D_MODEL), lambda i: (jnp.maximum(cur(i) * (TM // HALO) - 1, 0), 0)),
            _const_spec((1, D_MODEL)),
            _const_spec((len(POOL_WINDOWS), POOL_G, POOL_G), o),
            _const_spec((1, D_MODEL)),
            _const_spec((1, D_MODEL)),
            _const_spec((D_MODEL, D_FF), layer),
            _const_spec((D_FF, D_MODEL), layer),
        ],
        out_specs=pl.BlockSpec((TM, D_MODEL), (lambda i: (_frame_tile(prev(i), tpb), 0)) if to_frames
                               else (lambda i: (prev(i), 0))),
        out_shape=jax.ShapeDtypeStruct((out_rows, D_MODEL), F32),
        scratch_shapes=[pltpu.VMEM((TM + HALO, D_MODEL), F32)] * 3 + [
            pltpu.VMEM((2, TM, D_MODEL), F32), pltpu.VMEM((2, TM, D_MODEL), BF16)],
        compiler_params=_params(("arbitrary",)),
        name="pool_mlp",
    )(h, h, gm, pw, ps, g, wu, wd)


def _rope_cols(w):
    half = ROPE // 2
    z = jnp.zeros(w.shape[:-1] + (half,), w.dtype)
    return jnp.concatenate([w[..., :half], z, w[..., half:], z], axis=-1)


def _qk_cols(w):
    w = w.reshape(w.shape[:-1] + (HEADS, QK_DIM))
    w = jnp.concatenate([w[..., :HD], _rope_cols(w[..., HD:])], axis=-1)
    return w.reshape(w.shape[:-2] + (HEADS * QK_PAD,))


def _rope_tables(lp):
    half = ROPE // 2
    inv = ROPE_THETA ** (-np.arange(half, dtype=np.float64) / half)
    pos = np.maximum(np.arange(lp, dtype=np.float64) - PAD, 0.0)
    ang = pos[:, None] * inv[None, :]
    c = jnp.asarray(np.cos(ang).astype(np.float32))
    s = jnp.asarray(np.sin(ang).astype(np.float32))
    z = jnp.zeros_like(c)
    return (jnp.concatenate([c, z, c, z], axis=1), jnp.concatenate([-s, z, s, z], axis=1))


def kernel(x, meta_tokens, mix_norm, mlp_norm, w_mlp_up, w_mlp_down, w_in, hgrn_lb, hgrn_out_norm, mla_q_a_norm, mla_kv_a_norm, w_q_up, w_kv_up, q_norm, k_norm, w_out, pool_w, pool_scale):
    nbatch, seq, _ = x.shape
    depth = mix_norm.shape[0]
    assert seq % TQ == 0 and depth % 2 == 0
    lp = seq + LEAD

    cos_t, sin_t = _rope_tables(lp)
    lb_cum = jnp.cumsum(jax.nn.softmax(hgrn_lb.astype(F32), axis=0), axis=0)
    lower = lb_cum - lb_cum[0:1]
    tri = jnp.tril(jnp.ones((CHUNK, CHUNK), F32)).astype(BF16)
    tri2 = jnp.concatenate([tri, tri], axis=1)

    w_in_l = w_in.astype(BF16)
    wq_l = _qk_cols(w_q_up).astype(BF16)
    wkv = w_kv_up.reshape(w_kv_up.shape[0], KV_RANK, HEADS, 2 * HD)
    wkv_l = jnp.concatenate([wkv[..., :HD].reshape(-1, KV_RANK, HW),
                             wkv[..., HD:].reshape(-1, KV_RANK, HW)], axis=-1).astype(BF16)
    qn_l = jnp.concatenate([q_norm[:, :HD], _rope_cols(q_norm[:, HD:])], axis=-1) * Q_SCALE
    kn_l = jnp.concatenate([k_norm[:, :HD], _rope_cols(k_norm[:, HD:])], axis=-1)
    wo_l = w_out.astype(BF16)
    wu_l = w_mlp_up.astype(BF16)
    wd_l = w_mlp_down.astype(BF16)
    pw_l = pool_w.astype(BF16)

    h = x.reshape(nbatch * seq, D_MODEL)
    meta = meta_tokens.astype(F32)
    for layer in range(depth):
        if layer % 2 == 0:
            e = layer // 2
            zh, q, k, vt = _inproj(h, meta, mix_norm[layer][None], w_in_l, mla_q_a_norm[e][None],
                                  mla_kv_a_norm[e][None], wq_l, wkv_l, qn_l[e][None],
                                  kn_l[e][None], cos_t, sin_t, e, nbatch, lp)
            bound = (Q_SCALE * QK_DIM * 1.01) * jnp.max(jnp.abs(q_norm[e])) * jnp.max(jnp.abs(k_norm[e]))
            bound = bound.reshape(1).astype(F32)
            ob = lax.cond(bound[0] <= MAX_FIXED_SHIFT,
                          functools.partial(_attn, nbatch=nbatch, lp=lp, fixed_shift=True),
                          functools.partial(_attn, nbatch=nbatch, lp=lp, fixed_shift=False),
                          q, k, vt, bound)
            h = _mix_tail(zh, h, meta, ob, lower[e][None], 1.0 - lower[e][None], hgrn_out_norm[e][None],
                          tri2, wo_l, mlp_norm[layer][None], wu_l, wd_l, e, layer, nbatch, lp)
            meta = None
        else:
            o = layer // 2
            h = _pool_mlp(h, mix_norm[layer][None], pw_l, pool_scale[o][None],
                          mlp_norm[layer][None], wu_l, wd_l, o, layer, nbatch, lp,
                          to_frames=layer == depth - 1)

    return h.reshape(nbatch, seq, D_MODEL)
```

```python
import functools

import numpy as np
import jax
import jax.numpy as jnp
from jax import lax
from jax.experimental import pallas as pl
from jax.experimental.pallas import tpu as pltpu

F32 = jnp.float32
BF16 = jnp.bfloat16

D_MODEL = 1024
D_FF = 4 * D_MODEL
EPS = 1e-6
N_META = 16
CHUNK = 64
HEADS = 4
HD = 128
HW = HEADS * HD
ROPE = 64
QK_DIM = HD + ROPE
QK_PAD = 256
Q_RANK = 256
KV_RANK = 256
ROPE_THETA = 10000.0
POOL_WINDOWS = (2, 4, 8, 16)
POOL_G = D_MODEL // len(POOL_WINDOWS)

LEAD = 1024
PAD = LEAD - N_META
TM = 512
LEAD_TILES = LEAD // TM
TQ = 1024
TK = 1024
SUB = 16
HALO = 32
IN_COLS = 4 * HW + Q_RANK + KV_RANK + ROPE
Q_SCALE = QK_DIM ** -0.5 * float(np.log2(np.e))
MAX_FIXED_SHIFT = 56.0
MASK_VALUE = -1e30
EXP2_CLAMP = 115.0
TINY = 1e-37
SUBLANES = 8
V7X_VMEM_BYTES = 64 * 1024 * 1024
VMEM_LIMIT = V7X_VMEM_BYTES - 8 * 1024 * 1024

assert LEAD % TM == 0 and LEAD == TK == TQ and TQ % (2 * CHUNK) == 0 and TM % CHUNK == 0 and CHUNK % SUB == 0


def _rms(x, g):
    return x * lax.rsqrt(jnp.mean(x * x, axis=-1, keepdims=True) + EPS) * g


def _silu(x):
    hx = 0.5 * x
    return hx + hx * jnp.tanh(hx)


def _const_spec(shape, layer=None):
    nd = len(shape)
    if layer is None:
        return pl.BlockSpec(shape, lambda *_: (0,) * nd, pipeline_mode=pl.Buffered(1))
    return pl.BlockSpec((None,) + tuple(shape), lambda *_: (layer,) + (0,) * nd, pipeline_mode=pl.Buffered(1))


def _params(sem):
    return pltpu.CompilerParams(dimension_semantics=sem, vmem_limit_bytes=VMEM_LIMIT)


def _frame_tile(i, tpb):
    return (i // tpb) * (tpb - LEAD_TILES) + jnp.maximum(i % tpb - LEAD_TILES, 0)


def _stream_tile(h_ref, meta_ref, tt):
    lead = jnp.concatenate([jnp.zeros((TM - N_META, D_MODEL), F32), meta_ref[...]], axis=0)
    lead = jnp.where(tt == LEAD_TILES - 1, lead, 0.0)
    return jnp.where(tt >= LEAD_TILES, h_ref[...], lead)


MLP_PARTS = 4
MLP_SLAB = D_FF // MLP_PARTS


def _mlp_up(hn, wu_ref, c):
    a = jnp.dot(hn, wu_ref[:, c * MLP_SLAB:(c + 1) * MLP_SLAB], preferred_element_type=F32)
    a = jnp.maximum(a, 0.0)
    return (a * a).astype(BF16)


def _mlp_down(a, wd_ref, c):
    return jnp.dot(a, wd_ref[c * MLP_SLAB:(c + 1) * MLP_SLAB, :], preferred_element_type=F32)


def _mlp_staggered(hn, wu_ref, wd_ref, between=()):
    a = _mlp_up(hn, wu_ref, 0)
    acc = None
    for c in range(MLP_PARTS):
        a_next = _mlp_up(hn, wu_ref, c + 1) if c + 1 < MLP_PARTS else None
        d = _mlp_down(a, wd_ref, c)
        acc = d if acc is None else acc + d
        if c < len(between):
            between[c]()
        a = a_next
    return acc


def _rope(x, c, s):
    return x * c + pltpu.roll(x, HD // 2, axis=1) * s


def _inproj_body(tpb, ntiles, first_layer, *refs):
    if first_layer:
        meta_ref, refs = refs[0], refs[1:]
    (h_ref, g_ref, win_ref, qag_ref, kvag_ref, wq_ref, wkv_ref, qn_ref, kn_ref, cos_ref, sin_ref,
     zh_ref, q_ref, k_ref, vt_ref, qa_ref, kva_ref, kr_ref) = refs
    i = pl.program_id(0)

    @pl.when(i == 0)
    def _():
        qa_ref[1] = jnp.zeros(qa_ref.shape[1:], BF16)
        kva_ref[1] = jnp.zeros(kva_ref.shape[1:], BF16)
        kr_ref[1] = jnp.zeros(kr_ref.shape[1:], F32)

    q = jnp.dot(qa_ref[(i + 1) % 2], wq_ref[...], preferred_element_type=F32)
    kv = jnp.dot(kva_ref[(i + 1) % 2], wkv_ref[...], preferred_element_type=F32)
    kr = _rope_cols(kr_ref[(i + 1) % 2])

    tt = jnp.minimum(i, ntiles - 1) % tpb
    h = _stream_tile(h_ref, meta_ref, tt) if first_layer else h_ref[...]
    u = _rms(h, g_ref[...]).astype(BF16)
    lat = jnp.dot(u, win_ref[:, 4 * HW:], preferred_element_type=F32)
    zh_ref[...] = jnp.dot(u, win_ref[:, :4 * HW], preferred_element_type=F32)
    qa_ref[i % 2] = _rms(lat[:, :Q_RANK], qag_ref[...]).astype(BF16)
    kva_ref[i % 2] = _rms(lat[:, Q_RANK:Q_RANK + KV_RANK], kvag_ref[...]).astype(BF16)
    kr_ref[i % 2] = lat[:, Q_RANK + KV_RANK:]

    vt_ref[...] = kv[:, HW:].T.astype(BF16)
    c = cos_ref[...]
    s = sin_ref[...]
    qg = qn_ref[...]
    kg = kn_ref[...]
    kr_ss = jnp.sum(kr * kr, axis=-1, keepdims=True)
    for hd in range(HEADS):
        qa = q[:, hd * QK_PAD:hd * QK_PAD + HD]
        qb = q[:, hd * QK_PAD + HD:(hd + 1) * QK_PAD]
        ss = jnp.sum(qa * qa + qb * qb, axis=-1, keepdims=True)
        inv = lax.rsqrt(ss * (1.0 / QK_DIM) + EPS)
        q_ref[:, hd * QK_PAD:hd * QK_PAD + HD] = (qa * inv * qg[:, :HD]).astype(BF16)
        q_ref[:, hd * QK_PAD + HD:(hd + 1) * QK_PAD] = _rope(qb * inv * qg[:, HD:], c, s).astype(BF16)
        ka = kv[:, hd * HD:(hd + 1) * HD]
        ss = jnp.sum(ka * ka, axis=-1, keepdims=True) + kr_ss
        inv = lax.rsqrt(ss * (1.0 / QK_DIM) + EPS)
        k_ref[:, hd * QK_PAD:hd * QK_PAD + HD] = (ka * inv * kg[:, :HD]).astype(BF16)
        k_ref[:, hd * QK_PAD + HD:(hd + 1) * QK_PAD] = _rope(kr * inv * kg[:, HD:], c, s).astype(BF16)


def _inproj(h, meta, g, win, qag, kvag, wq, wkv, qn, kn, cos_t, sin_t, e, nbatch, lp):
    r = nbatch * lp
    tpb = lp // TM
    ntiles = r // TM
    first_layer = meta is not None
    cur = lambda i: jnp.minimum(i, ntiles - 1)
    prev = lambda i: jnp.maximum(i - 1, 0)
    row = lambda i: (prev(i), 0)
    tab = lambda i: (prev(i) % tpb, 0)
    h_spec = pl.BlockSpec((TM, D_MODEL),
                          (lambda i: (_frame_tile(cur(i), tpb), 0)) if first_layer else (lambda i: (cur(i), 0)))
    lead_specs = [_const_spec((N_META, D_MODEL))] if first_layer else []
    lead_args = [meta] if first_layer else []
    return pl.pallas_call(
        functools.partial(_inproj_body, tpb, ntiles, first_layer),
        grid=(ntiles + 1,),
        in_specs=lead_specs + [
            h_spec,
            _const_spec((1, D_MODEL)),
            _const_spec((D_MODEL, IN_COLS), e),
            _const_spec((1, Q_RANK)),
            _const_spec((1, KV_RANK)),
            _const_spec((Q_RANK, HEADS * QK_PAD), e),
            _const_spec((KV_RANK, 2 * HW), e),
            _const_spec((1, QK_PAD)),
            _const_spec((1, QK_PAD)),
            pl.BlockSpec((TM, HD), tab),
            pl.BlockSpec((TM, HD), tab),
        ],
        out_specs=[
            pl.BlockSpec((TM, 4 * HW), lambda i: (cur(i), 0)),
            pl.BlockSpec((TM, HEADS * QK_PAD), row),
            pl.BlockSpec((TM, HEADS * QK_PAD), row),
            pl.BlockSpec((HW, TM), lambda i: (0, prev(i))),
        ],
        out_shape=[
            jax.ShapeDtypeStruct((r, 4 * HW), F32),
            jax.ShapeDtypeStruct((r, HEADS * QK_PAD), BF16),
            jax.ShapeDtypeStruct((r, HEADS * QK_PAD), BF16),
            jax.ShapeDtypeStruct((HW, r), BF16),
        ],
        scratch_shapes=[pltpu.VMEM((2, TM, Q_RANK), BF16), pltpu.VMEM((2, TM, KV_RANK), BF16),
                        pltpu.VMEM((2, TM, ROPE), F32)],
        compiler_params=_params(("arbitrary",)),
        name="inproj",
    )(*lead_args, h, g, win, qag, kvag, wq, wkv, qn, kn, cos_t, sin_t)


def _group_rows(rows):
    return jnp.concatenate([jnp.broadcast_to(r, (SUB, HD)) for r in rows], axis=0)


def _hgrn_stages(zh_ref, lb_ref, one_m_lb_ref, og_ref, tri_ref, st_ref, keep, o_ref):
    tri2 = tri_ref[...]
    lb = lb_ref[...]
    one_m_lb = one_m_lb_ref[...]
    og = og_ref[...]
    tt = lax.broadcasted_iota(jnp.int32, (CHUNK, CHUNK), 0)
    ss_ = lax.broadcasted_iota(jnp.int32, (CHUNK, CHUNK), 1)
    causal = ss_ <= tt
    nsub = CHUNK // SUB
    zero_row = jnp.zeros((1, HD), F32)
    zero_sub = jnp.zeros((SUB, HD), BF16)

    heads = range(HEADS)
    sls = [slice(hd * HD, (hd + 1) * HD) for hd in heads]

    chunk_rows = [slice(c * CHUNK, (c + 1) * CHUNK) for c in range(TM // CHUNK)]

    def gates_and_decay(rows):
        hq = zh_ref[rows, 0:HW]
        hf = zh_ref[rows, HW:2 * HW]
        hi = zh_ref[rows, 2 * HW:3 * HW]
        hg = zh_ref[rows, 3 * HW:4 * HW]
        q = _silu(hq)
        gate = _silu(hg)
        t = jnp.exp(-jnp.abs(hf))
        r = 1.0 / (1.0 + t)
        tr = t * r
        pos = hf >= 0.0
        log2f = jnp.log2(jnp.maximum(lb + one_m_lb * jnp.where(pos, r, tr), TINY))
        k = one_m_lb * jnp.where(pos, tr, r)
        g1 = log2f.astype(BF16)
        g2 = (log2f - g1.astype(F32)).astype(BF16)
        b = jnp.dot(tri2, jnp.concatenate([g1, g2], axis=0), preferred_element_type=F32)
        vt = [hi[:, sls[hd]].T.astype(BF16) for hd in heads]
        return q, k, b, vt, gate

    def intra_chunk(q, k, b, vt, gate):
        att, q_in, k_out, decay = [], [], [], []
        for hd in heads:
            bh = b[:, sls[hd]]
            b_last = bh[CHUNK - 1:CHUNK, :]
            refs = [zero_row] + [bh[i * SUB - 1:i * SUB, :] for i in range(1, nsub)]
            dq = bh - _group_rows(refs)
            qe = q[:, sls[hd]] * jnp.exp2(dq)
            ke = k[:, sls[hd]] * jnp.exp2(jnp.minimum(-dq, EXP2_CLAMP))
            keb = ke.astype(BF16)
            qcat = []
            kcat = []
            for j in range(nsub):
                qcat.append(jnp.concatenate(
                    [zero_sub if i < j else
                     qe[i * SUB:(i + 1) * SUB].astype(BF16) if i == j else
                     (qe[i * SUB:(i + 1) * SUB] * jnp.exp2(refs[i] - refs[j])).astype(BF16)
                     for i in range(nsub)], axis=0))
                kcat.append(jnp.concatenate(
                    [keb[j * SUB:(j + 1) * SUB] if i == j else zero_sub for i in range(nsub)], axis=0))
            q_in.append(qcat[0])
            att.append(lax.dot_general(jnp.concatenate(qcat, axis=1), jnp.concatenate(kcat, axis=1),
                                       (((1,), (1,)), ((), ())), preferred_element_type=F32))
            to_end = [jnp.exp2(b_last - refs[i]) for i in range(nsub)]
            k_out.append((ke * _group_rows(to_end)).astype(BF16))
            decay.append(to_end[0])
        upd = [jnp.dot(vt[hd], k_out[hd], preferred_element_type=F32) for hd in heads]
        return att, q_in, upd, decay, vt, gate

    def stage1():
        return [gates_and_decay(rows) for rows in chunk_rows]

    def stage2(s1):
        return [intra_chunk(*c) for c in s1]

    def stage3(s2):
        st = [st_ref[hd] * keep for hd in heads]
        starts = []
        for att, q_in, upd, decay, vt, gate in s2:
            starts.append([st[hd].astype(BF16) for hd in heads])
            st = [decay[hd] * st[hd] + upd[hd] for hd in heads]
        for hd in heads:
            st_ref[hd] = st[hd]
        return starts

    def stage4(s2, starts):
        for rows, st0, (att, q_in, upd, decay, vt, gate) in zip(chunk_rows, starts, s2):
            for hd in heads:
                a = jnp.where(causal, att[hd], 0.0).astype(BF16)
                oh = lax.dot_general(jnp.concatenate([q_in[hd], a], axis=1),
                                     jnp.concatenate([st0[hd], vt[hd]], axis=1),
                                     (((1,), (1,)), ((), ())), preferred_element_type=F32)
                on = _rms(oh, og) * gate[:, sls[hd]]
                o_ref[rows, sls[hd]] = on.astype(BF16)

    return stage1, stage2, stage3, stage4


META_STEP, FULL_STEP, DIAG_STEP, NO_STEP = range(4)


def _attn_steps(lp):
    qi, fin, kind_a, kj_a, kind_b, kj_b = [], [], [], [], [], []
    for i in range(lp // TQ):
        tiles = [(META_STEP if j == 0 else DIAG_STEP if j == i else FULL_STEP, j) for j in range(i + 1)]
        for s in range(0, len(tiles), 2):
            a = tiles[s]
            b = tiles[s + 1] if s + 1 < len(tiles) else (NO_STEP, a[1])
            qi.append(i)
            fin.append(int(s + 2 >= len(tiles)))
            kind_a.append(a[0])
            kj_a.append(a[1])
            kind_b.append(b[0])
            kj_b.append(b[1])
    return [np.asarray(t, np.int32) for t in (qi, fin, kind_a, kj_a, kind_b, kj_b)]


def _attn_body(fixed_shift, qi_ref, fin_ref, kind_a_ref, kj_a_ref, kind_b_ref, kj_b_ref, bound_ref,
               q_ref, ka_ref, vta_ref, kb_ref, vtb_ref, o_ref, *scratch):
    if fixed_shift:
        l_ref, acc_ref = scratch
    else:
        m_ref, l_ref, acc_ref = scratch
    step = pl.program_id(1)
    qi = qi_ref[step]
    bound = bound_ref[0]

    def update(k_ref, vt_ref, qs, keys, mask, first):
        nk, nq = keys.stop - keys.start, qs.stop - qs.start
        for hd in range(HEADS):
            hq = slice(hd * QK_PAD, (hd + 1) * QK_PAD)
            hv = slice(hd * HD, (hd + 1) * HD)
            st = lax.dot_general(k_ref[keys, hq], q_ref[qs, hq], (((1,), (1,)), ((), ())),
                                 preferred_element_type=F32)
            if fixed_shift:
                p = jnp.exp2(st - bound)
                if mask is not None:
                    p = jnp.where(mask, p, 0.0)
            else:
                if mask is not None:
                    st = jnp.where(mask, st, MASK_VALUE)
                m_prev = m_ref[hd, :, qs]
                m_new = jnp.maximum(m_prev, jnp.max(st, axis=0, keepdims=True))
                alpha = jnp.exp2(m_prev - m_new)
                m_ref[hd, :, qs] = m_new
                p = jnp.exp2(st - m_new[0:1, :])
            part = jnp.sum(p.reshape(nk // SUBLANES, SUBLANES, nq), axis=0)
            pv = jnp.dot(vt_ref[hv, keys], p.astype(BF16), preferred_element_type=F32)
            if first and fixed_shift:
                l_ref[hd, :, qs] = part
                acc_ref[hv, qs] = pv
            elif fixed_shift:
                l_ref[hd, :, qs] += part
                acc_ref[hv, qs] += pv
            else:
                l_ref[hd, :, qs] = alpha * l_ref[hd, :, qs] + part
                acc_ref[hv, qs] = alpha[0:1, :] * acc_ref[hv, qs] + pv

    all_q = slice(0, TQ)
    all_keys = slice(0, TK)

    def key_tile(kind, k_ref, vt_ref, may_be_first):
        if may_be_first:
            @pl.when(kind == META_STEP)
            def _():
                if not fixed_shift:
                    m_ref[...] = jnp.full_like(m_ref, MASK_VALUE)
                    l_ref[...] = jnp.zeros_like(l_ref)
                    acc_ref[...] = jnp.zeros_like(acc_ref)
                update(k_ref, vt_ref, all_q, slice(TK - N_META, TK), None, True)

        @pl.when(kind == FULL_STEP)
        def _():
            update(k_ref, vt_ref, all_q, all_keys, None, False)

        @pl.when(kind == DIAG_STEP)
        def _():
            half = TQ // 2
            key = lax.broadcasted_iota(jnp.int32, (half, TQ), 0)
            qry = lax.broadcasted_iota(jnp.int32, (half, TQ), 1)
            update(k_ref, vt_ref, all_q, slice(0, half), key // CHUNK <= qry // CHUNK, False)
            update(k_ref, vt_ref, slice(half, TQ), slice(half, TK),
                   (key // CHUNK <= qry // CHUNK)[:, :half], False)

    key_tile(kind_a_ref[step], ka_ref, vta_ref, True)
    key_tile(kind_b_ref[step], kb_ref, vtb_ref, False)

    @pl.when(fin_ref[step] == 1)
    def _():
        row = lax.broadcasted_iota(jnp.int32, (TQ, HD), 0) + qi * TQ
        valid = row >= PAD
        for hd in range(HEADS):
            hv = slice(hd * HD, (hd + 1) * HD)
            ot = acc_ref[hv, :] / jnp.sum(l_ref[hd], axis=0, keepdims=True)
            o_ref[:, hv] = jnp.where(valid, ot.T, 0.0).astype(BF16)


def _attn(q, k, vt, bound, nbatch, lp, fixed_shift):
    r = q.shape[0]
    nq, nk = lp // TQ, lp // TK
    tables = _attn_steps(lp)
    stats = [pltpu.VMEM((HEADS, SUBLANES, TQ), F32)] * (1 if fixed_shift else 2)
    qmap = lambda b, s, qi, fin, kind_a, kj_a, kind_b, kj_b, bd: (b * nq + qi[s], 0)
    grid_spec = pltpu.PrefetchScalarGridSpec(
        num_scalar_prefetch=7,
        grid=(nbatch, len(tables[0])),
        in_specs=[
            pl.BlockSpec((TQ, HEADS * QK_PAD), qmap),
            pl.BlockSpec((TK, HEADS * QK_PAD),
                         lambda b, s, qi, fin, kind_a, kj_a, kind_b, kj_b, bd: (b * nk + kj_a[s], 0)),
            pl.BlockSpec((HW, TK),
                         lambda b, s, qi, fin, kind_a, kj_a, kind_b, kj_b, bd: (0, b * nk + kj_a[s])),
            pl.BlockSpec((TK, HEADS * QK_PAD),
                         lambda b, s, qi, fin, kind_a, kj_a, kind_b, kj_b, bd: (b * nk + kj_b[s], 0)),
            pl.BlockSpec((HW, TK),
                         lambda b, s, qi, fin, kind_a, kj_a, kind_b, kj_b, bd: (0, b * nk + kj_b[s])),
        ],
        out_specs=pl.BlockSpec((TQ, HW), qmap),
        scratch_shapes=stats + [pltpu.VMEM((HW, TQ), F32)],
    )
    return pl.pallas_call(
        functools.partial(_attn_body, fixed_shift),
        grid_spec=grid_spec,
        out_shape=jax.ShapeDtypeStruct((r, HW), BF16),
        compiler_params=_params(("parallel", "arbitrary")),
        name="attn_fixed_shift" if fixed_shift else "attn_online",
    )(*[jnp.asarray(t) for t in tables], bound, q, k, vt, k, vt)


def _mix_tail_body(tpb, ntiles, first_layer, *refs):
    if first_layer:
        meta_ref, refs = refs[0], refs[1:]
    (zh_ref, h_ref, ob_ref, lb_ref, one_m_lb_ref, og_ref, tri_ref, wo_ref, g_ref, wu_ref, wd_ref,
     out_ref, st_ref, oa_ref) = refs
    i = pl.program_id(0)

    @pl.when(i == 0)
    def _():
        st_ref[...] = jnp.zeros_like(st_ref)

    keep = (jnp.minimum(i, ntiles - 1) % tpb != 0).astype(F32)
    stage1, stage2, stage3, stage4 = _hgrn_stages(
        zh_ref, lb_ref, one_m_lb_ref, og_ref, tri_ref, st_ref, keep, oa_ref.at[i % 2])

    tt = jnp.maximum(i - 1, 0) % tpb

    @pl.when(tt == 0)
    def _():
        s2 = stage2(stage1())
        stage4(s2, stage3(s2))
        out_ref[...] = jnp.zeros_like(out_ref)

    @pl.when(tt != 0)
    def _():
        h = _stream_tile(h_ref, meta_ref, tt) if first_layer else h_ref[...]
        mix = jnp.dot(oa_ref[(i + 1) % 2], wo_ref[:HW, :], preferred_element_type=F32)
        mix = mix + jnp.dot(ob_ref[...], wo_ref[HW:, :], preferred_element_type=F32)
        s1 = stage1()
        h = h + mix
        hn = _rms(h, g_ref[...]).astype(BF16)
        carried = {}

        def run2():
            carried["s2"] = stage2(s1)

        def run3():
            carried["o"] = stage3(carried["s2"])

        def run4():
            stage4(carried["s2"], carried["o"])

        out_ref[...] = h + _mlp_staggered(hn, wu_ref, wd_ref, between=(run2, run3, run4))


def _mix_tail(zh, h, meta, ob, lb, one_m_lb, og, tri2, wo, g, wu, wd, e, layer, nbatch, lp):
    r = nbatch * lp
    tpb = lp // TM
    ntiles = r // TM
    first_layer = meta is not None
    cur = lambda i: jnp.minimum(i, ntiles - 1)
    prev = lambda i: jnp.maximum(i - 1, 0)
    row = lambda i: (prev(i), 0)
    h_spec = pl.BlockSpec((TM, D_MODEL),
                          (lambda i: (_frame_tile(prev(i), tpb), 0)) if first_layer else row)
    lead_specs = [_const_spec((N_META, D_MODEL))] if first_layer else []
    lead_args = [meta] if first_layer else []
    return pl.pallas_call(
        functools.partial(_mix_tail_body, tpb, ntiles, first_layer),
        grid=(ntiles + 1,),
        in_specs=lead_specs + [
            pl.BlockSpec((TM, 4 * HW), lambda i: (cur(i), 0)),
            h_spec,
            pl.BlockSpec((TM, HW), row),
            _const_spec((1, HW)),
            _const_spec((1, HW)),
            _const_spec((1, HD)),
            _const_spec((CHUNK, 2 * CHUNK)),
            _const_spec((2 * HW, D_MODEL), e),
            _const_spec((1, D_MODEL)),
            _const_spec((D_MODEL, D_FF), layer),
            _const_spec((D_FF, D_MODEL), layer),
        ],
        out_specs=pl.BlockSpec((TM, D_MODEL), row),
        out_shape=jax.ShapeDtypeStruct((r, D_MODEL), F32),
        scratch_shapes=[pltpu.VMEM((HEADS, HD, HD), F32), pltpu.VMEM((2, TM, HW), BF16)],
        compiler_params=_params(("arbitrary",)),
        name="mix_tail",
    )(*lead_args, zh, h, ob, lb, one_m_lb, og, tri2, wo, g, wu, wd)


def _pool_mlp_body(tpb, ntiles, h_ref, halo_ref, gm_ref, pw_ref, ps_ref, g_ref, wu_ref, wd_ref,
                   out_ref, u_ref, a_ref, b_ref, hm_ref, hn_ref):
    i = pl.program_id(0)
    tt = jnp.minimum(i, ntiles - 1) % tpb
    n = TM + HALO
    g = POOL_G
    ys = [None] * len(POOL_WINDOWS)

    def normalise():
        gm = gm_ref[...]
        keep = (tt != 0).astype(F32)
        u_ref[0:HALO, :] = _rms(halo_ref[...], gm) * keep
        u_ref[HALO:, :] = _rms(h_ref[...], gm)
        a_ref[8:n, :] = u_ref[8:n, :] + u_ref[7:n - 1, :]

    def group(gi):
        w = POOL_WINDOWS[gi]
        cols = slice(gi * g, (gi + 1) * g)
        win = (a_ref, b_ref, a_ref, b_ref)[gi][HALO:, cols]
        pos = lax.broadcasted_iota(jnp.int32, (TM, g), 0) + (tt * TM - PAD)
        cnt = jnp.minimum(jnp.maximum(pos + 1, 1).astype(F32), float(w))
        d = win / cnt - u_ref[HALO:, cols]
        ys[gi] = jnp.dot(d.astype(BF16), pw_ref[gi], preferred_element_type=F32) * ps_ref[:, cols]

    def first_groups():
        b_ref[16:n, g:] = a_ref[16:n, g:] + a_ref[14:n - 2, g:]
        group(0)
        group(1)

    def last_groups():
        a_ref[24:n, 2 * g:] = b_ref[24:n, 2 * g:] + b_ref[20:n - 4, 2 * g:]
        b_ref[32:n, 3 * g:] = a_ref[32:n, 3 * g:] + a_ref[24:n - 8, 3 * g:]
        group(2)
        group(3)

    def finish():
        hm = h_ref[...] + jnp.concatenate(ys, axis=1)
        hm_ref[i % 2] = hm
        hn_ref[i % 2] = _rms(hm, g_ref[...]).astype(BF16)

    mlp_is_zero = jnp.maximum(i - 1, 0) % tpb == 0

    @pl.when(mlp_is_zero)
    def _():
        normalise()
        first_groups()
        last_groups()
        finish()
        out_ref[...] = jnp.zeros_like(out_ref)

    @pl.when(jnp.logical_not(mlp_is_zero))
    def _():
        acc = _mlp_staggered(hn_ref[(i + 1) % 2], wu_ref, wd_ref,
                             between=(normalise, first_groups, last_groups, finish))
        out_ref[...] = hm_ref[(i + 1) % 2] + acc


def _pool_mlp(h, gm, pw, ps, g, wu, wd, o, layer, nbatch, lp, to_frames):
    r = nbatch * lp
    tpb = lp // TM
    ntiles = r // TM
    cur = lambda i: jnp.minimum(i, ntiles - 1)
    prev = lambda i: jnp.maximum(i - 1, 0)
    out_rows = nbatch * (lp - LEAD) if to_frames else r
    return pl.pallas_call(
        functools.partial(_pool_mlp_body, tpb, ntiles),
        grid=(ntiles + 1,),
        in_specs=[
            pl.BlockSpec((TM, D_MODEL), lambda i: (cur(i), 0)),
            pl.BlockSpec((HALO, D_MODEL), lambda i: (jnp.maximum(cur(i) * (TM // HALO) - 1, 0), 0)),
            _const_spec((1, D_MODEL)),
            _const_spec((len(POOL_WINDOWS), POOL_G, POOL_G), o),
            _const_spec((1, D_MODEL)),
            _const_spec((1, D_MODEL)),
            _const_spec((D_MODEL, D_FF), layer),
            _const_spec((D_FF, D_MODEL), layer),
        ],
        out_specs=pl.BlockSpec((TM, D_MODEL), (lambda i: (_frame_tile(prev(i), tpb), 0)) if to_frames
                               else (lambda i: (prev(i), 0))),
        out_shape=jax.ShapeDtypeStruct((out_rows, D_MODEL), F32),
        scratch_shapes=[pltpu.VMEM((TM + HALO, D_MODEL), F32)] * 3 + [
            pltpu.VMEM((2, TM, D_MODEL), F32), pltpu.VMEM((2, TM, D_MODEL), BF16)],
        compiler_params=_params(("arbitrary",)),
        name="pool_mlp",
    )(h, h, gm, pw, ps, g, wu, wd)


def _rope_cols(w):
    half = ROPE // 2
    z = jnp.zeros(w.shape[:-1] + (half,), w.dtype)
    return jnp.concatenate([w[..., :half], z, w[..., half:], z], axis=-1)


def _qk_cols(w):
    w = w.reshape(w.shape[:-1] + (HEADS, QK_DIM))
    w = jnp.concatenate([w[..., :HD], _rope_cols(w[..., HD:])], axis=-1)
    return w.reshape(w.shape[:-2] + (HEADS * QK_PAD,))


def _rope_tables(lp):
    half = ROPE // 2
    inv = ROPE_THETA ** (-np.arange(half, dtype=np.float64) / half)
    pos = np.maximum(np.arange(lp, dtype=np.float64) - PAD, 0.0)
    ang = pos[:, None] * inv[None, :]
    c = jnp.asarray(np.cos(ang).astype(np.float32))
    s = jnp.asarray(np.sin(ang).astype(np.float32))
    z = jnp.zeros_like(c)
    return (jnp.concatenate([c, z, c, z], axis=1), jnp.concatenate([-s, z, s, z], axis=1))


def kernel(x, meta_tokens, mix_norm, mlp_norm, w_mlp_up, w_mlp_down, w_in, hgrn_lb, hgrn_out_norm, mla_q_a_norm, mla_kv_a_norm, w_q_up, w_kv_up, q_norm, k_norm, w_out, pool_w, pool_scale):
    nbatch, seq, _ = x.shape
    depth = mix_norm.shape[0]
    assert seq % TQ == 0 and depth % 2 == 0
    lp = seq + LEAD

    cos_t, sin_t = _rope_tables(lp)
    lb_cum = jnp.cumsum(jax.nn.softmax(hgrn_lb.astype(F32), axis=0), axis=0)
    lower = lb_cum - lb_cum[0:1]
    tri = jnp.tril(jnp.ones((CHUNK, CHUNK), F32)).astype(BF16)
    tri2 = jnp.concatenate([tri, tri], axis=1)

    w_in_l = w_in.astype(BF16)
    wq_l = _qk_cols(w_q_up).astype(BF16)
    wkv = w_kv_up.reshape(w_kv_up.shape[0], KV_RANK, HEADS, 2 * HD)
    wkv_l = jnp.concatenate([wkv[..., :HD].reshape(-1, KV_RANK, HW),
                             wkv[..., HD:].reshape(-1, KV_RANK, HW)], axis=-1).astype(BF16)
    qn_l = jnp.concatenate([q_norm[:, :HD], _rope_cols(q_norm[:, HD:])], axis=-1) * Q_SCALE
    kn_l = jnp.concatenate([k_norm[:, :HD], _rope_cols(k_norm[:, HD:])], axis=-1)
    wo_l = w_out.astype(BF16)
    wu_l = w_mlp_up.astype(BF16)
    wd_l = w_mlp_down.astype(BF16)
    pw_l = pool_w.astype(BF16)

    h = x.reshape(nbatch * seq, D_MODEL)
    meta = meta_tokens.astype(F32)
    for layer in range(depth):
        if layer % 2 == 0:
            e = layer // 2
            zh, q, k, vt = _inproj(h, meta, mix_norm[layer][None], w_in_l, mla_q_a_norm[e][None],
                                  mla_kv_a_norm[e][None], wq_l, wkv_l, qn_l[e][None],
                                  kn_l[e][None], cos_t, sin_t, e, nbatch, lp)
            bound = (Q_SCALE * QK_DIM * 1.01) * jnp.max(jnp.abs(q_norm[e])) * jnp.max(jnp.abs(k_norm[e]))
            bound = bound.reshape(1).astype(F32)
            ob = lax.cond(bound[0] <= MAX_FIXED_SHIFT,
                          functools.partial(_attn, nbatch=nbatch, lp=lp, fixed_shift=True),
                          functools.partial(_attn, nbatch=nbatch, lp=lp, fixed_shift=False),
                          q, k, vt, bound)
            h = _mix_tail(zh, h, meta, ob, lower[e][None], 1.0 - lower[e][None], hgrn_out_norm[e][None],
                          tri2, wo_l, mlp_norm[layer][None], wu_l, wd_l, e, layer, nbatch, lp)
            meta = None
        else:
            o = layer // 2
            h = _pool_mlp(h, mix_norm[layer][None], pw_l, pool_scale[o][None],
                          mlp_norm[layer][None], wu_l, wd_l, o, layer, nbatch, lp,
                          to_frames=layer == depth - 1)

    return h.reshape(nbatch, seq, D_MODEL)
```

```python
import functools

import numpy as np
import jax
import jax.numpy as jnp
from jax import lax
from jax.experimental import pallas as pl
from jax.experimental.pallas import tpu as pltpu

F32 = jnp.float32
BF16 = jnp.bfloat16

D_MODEL = 1024
D_FF = 4 * D_MODEL
EPS = 1e-6
N_META = 16
CHUNK = 64
HEADS = 4
HD = 128
HW = HEADS * HD
ROPE = 64
QK_DIM = HD + ROPE
QK_PAD = 256
Q_RANK = 256
KV_RANK = 256
ROPE_THETA = 10000.0
POOL_WINDOWS = (2, 4, 8, 16)
POOL_G = D_MODEL // len(POOL_WINDOWS)

LEAD = 1024
PAD = LEAD - N_META
TM = 512
LEAD_TILES = LEAD // TM
TQ = 1024
TK = 1024
KEY_BUFFERS = 3
SUB = 16
HALO = 32
IN_COLS = 4 * HW + Q_RANK + KV_RANK + ROPE
Q_SCALE = QK_DIM ** -0.5 * float(np.log2(np.e))
MAX_FIXED_SHIFT = 56.0
MASK_VALUE = -1e30
EXP2_CLAMP = 115.0
TINY = 1e-37
SUBLANES = 8
V7X_VMEM_BYTES = 64 * 1024 * 1024
VMEM_LIMIT = V7X_VMEM_BYTES - 8 * 1024 * 1024


def _rms(x, g):
    return x * lax.rsqrt(jnp.mean(x * x, axis=-1, keepdims=True) + EPS) * g


def _silu(x):
    hx = 0.5 * x
    return hx + hx * jnp.tanh(hx)


def _const_spec(shape, layer=None):
    nd = len(shape)
    if layer is None:
        return pl.BlockSpec(shape, lambda *_: (0,) * nd, pipeline_mode=pl.Buffered(1))
    return pl.BlockSpec((None,) + tuple(shape), lambda *_: (layer,) + (0,) * nd, pipeline_mode=pl.Buffered(1))


def _params(sem):
    return pltpu.CompilerParams(dimension_semantics=sem, vmem_limit_bytes=VMEM_LIMIT)


def _frame_tile(i, tpb):
    return (i // tpb) * (tpb - LEAD_TILES) + jnp.maximum(i % tpb - LEAD_TILES, 0)


def _stream_tile(h_ref, meta_ref, tt):
    lead = jnp.concatenate([jnp.zeros((TM - N_META, D_MODEL), F32), meta_ref[...]], axis=0)
    lead = jnp.where(tt == LEAD_TILES - 1, lead, 0.0)
    return jnp.where(tt >= LEAD_TILES, h_ref[...], lead)


MLP_PARTS = 4
MLP_SLAB = D_FF // MLP_PARTS


def _mlp_up(hn, wu_ref, c):
    a = jnp.dot(hn, wu_ref[:, c * MLP_SLAB:(c + 1) * MLP_SLAB], preferred_element_type=F32)
    a = jnp.maximum(a, 0.0)
    return (a * a).astype(BF16)


def _mlp_down(a, wd_ref, c):
    return jnp.dot(a, wd_ref[c * MLP_SLAB:(c + 1) * MLP_SLAB, :], preferred_element_type=F32)


def _mlp_staggered(hn, wu_ref, wd_ref, between=()):
    a = _mlp_up(hn, wu_ref, 0)
    acc = None
    for c in range(MLP_PARTS):
        a_next = _mlp_up(hn, wu_ref, c + 1) if c + 1 < MLP_PARTS else None
        d = _mlp_down(a, wd_ref, c)
        acc = d if acc is None else acc + d
        if c < len(between):
            between[c]()
        a = a_next
    return acc


def _rope(x, c, s):
    return x * c + pltpu.roll(x, HD // 2, axis=1) * s


def _inproj_body(tpb, ntiles, first_layer, *refs):
    if first_layer:
        meta_ref, refs = refs[0], refs[1:]
    (h_ref, g_ref, win_ref, qag_ref, kvag_ref, wq_ref, wkv_ref, qn_ref, kn_ref, cos_ref, sin_ref,
     zh_ref, q_ref, k_ref, vt_ref, qa_ref, kva_ref, kr_ref) = refs
    i = pl.program_id(0)

    @pl.when(i == 0)
    def _():
        qa_ref[1] = jnp.zeros(qa_ref.shape[1:], BF16)
        kva_ref[1] = jnp.zeros(kva_ref.shape[1:], BF16)
        kr_ref[1] = jnp.zeros(kr_ref.shape[1:], F32)

    q = jnp.dot(qa_ref[(i + 1) % 2], wq_ref[...], preferred_element_type=F32)
    kv = jnp.dot(kva_ref[(i + 1) % 2], wkv_ref[...], preferred_element_type=F32)
    kr = _rope_cols(kr_ref[(i + 1) % 2])

    tt = jnp.minimum(i, ntiles - 1) % tpb
    h = _stream_tile(h_ref, meta_ref, tt) if first_layer else h_ref[...]
    u = _rms(h, g_ref[...]).astype(BF16)
    lat = jnp.dot(u, win_ref[:, 4 * HW:], preferred_element_type=F32)
    zh_ref[...] = jnp.dot(u, win_ref[:, :4 * HW], preferred_element_type=F32)
    qa_ref[i % 2] = _rms(lat[:, :Q_RANK], qag_ref[...]).astype(BF16)
    kva_ref[i % 2] = _rms(lat[:, Q_RANK:Q_RANK + KV_RANK], kvag_ref[...]).astype(BF16)
    kr_ref[i % 2] = lat[:, Q_RANK + KV_RANK:]

    vt_ref[...] = kv[:, HW:].T.astype(BF16)
    c = cos_ref[...]
    s = sin_ref[...]
    qg = qn_ref[...]
    kg = kn_ref[...]
    kr_ss = jnp.sum(kr * kr, axis=-1, keepdims=True)
    for hd in range(HEADS):
        qa = q[:, hd * QK_PAD:hd * QK_PAD + HD]
        qb = q[:, hd * QK_PAD + HD:(hd + 1) * QK_PAD]
        ss = jnp.sum(qa * qa + qb * qb, axis=-1, keepdims=True)
        inv = lax.rsqrt(ss * (1.0 / QK_DIM) + EPS)
        q_ref[:, hd * QK_PAD:hd * QK_PAD + HD] = (qa * inv * qg[:, :HD]).astype(BF16)
        q_ref[:, hd * QK_PAD + HD:(hd + 1) * QK_PAD] = _rope(qb * inv * qg[:, HD:], c, s).astype(BF16)
        ka = kv[:, hd * HD:(hd + 1) * HD]
        ss = jnp.sum(ka * ka, axis=-1, keepdims=True) + kr_ss
        inv = lax.rsqrt(ss * (1.0 / QK_DIM) + EPS)
        k_ref[:, hd * QK_PAD:hd * QK_PAD + HD] = (ka * inv * kg[:, :HD]).astype(BF16)
        k_ref[:, hd * QK_PAD + HD:(hd + 1) * QK_PAD] = _rope(kr * inv * kg[:, HD:], c, s).astype(BF16)


def _inproj(h, meta, g, win, qag, kvag, wq, wkv, qn, kn, cos_t, sin_t, e, nbatch, lp):
    r = nbatch * lp
    tpb = lp // TM
    ntiles = r // TM
    first_layer = meta is not None
    cur = lambda i: jnp.minimum(i, ntiles - 1)
    prev = lambda i: jnp.maximum(i - 1, 0)
    row = lambda i: (prev(i), 0)
    tab = lambda i: (prev(i) % tpb, 0)
    h_spec = pl.BlockSpec((TM, D_MODEL),
                          (lambda i: (_frame_tile(cur(i), tpb), 0)) if first_layer else (lambda i: (cur(i), 0)))
    lead_specs = [_const_spec((N_META, D_MODEL))] if first_layer else []
    lead_args = [meta] if first_layer else []
    return pl.pallas_call(
        functools.partial(_inproj_body, tpb, ntiles, first_layer),
        grid=(ntiles + 1,),
        in_specs=lead_specs + [
            h_spec,
            _const_spec((1, D_MODEL)),
            _const_spec((D_MODEL, IN_COLS), e),
            _const_spec((1, Q_RANK)),
            _const_spec((1, KV_RANK)),
            _const_spec((Q_RANK, HEADS * QK_PAD), e),
            _const_spec((KV_RANK, 2 * HW), e),
            _const_spec((1, QK_PAD)),
            _const_spec((1, QK_PAD)),
            pl.BlockSpec((TM, HD), tab),
            pl.BlockSpec((TM, HD), tab),
        ],
        out_specs=[
            pl.BlockSpec((TM, 4 * HW), lambda i: (cur(i), 0)),
            pl.BlockSpec((TM, HEADS * QK_PAD), row),
            pl.BlockSpec((TM, HEADS * QK_PAD), row),
            pl.BlockSpec((HW, TM), lambda i: (0, prev(i))),
        ],
        out_shape=[
            jax.ShapeDtypeStruct((r, 4 * HW), F32),
            jax.ShapeDtypeStruct((r, HEADS * QK_PAD), BF16),
            jax.ShapeDtypeStruct((r, HEADS * QK_PAD), BF16),
            jax.ShapeDtypeStruct((HW, r), BF16),
        ],
        scratch_shapes=[pltpu.VMEM((2, TM, Q_RANK), BF16), pltpu.VMEM((2, TM, KV_RANK), BF16),
                        pltpu.VMEM((2, TM, ROPE), F32)],
        compiler_params=_params(("arbitrary",)),
        name="inproj",
    )(*lead_args, h, g, win, qag, kvag, wq, wkv, qn, kn, cos_t, sin_t)


def _group_rows(rows):
    return jnp.concatenate([jnp.broadcast_to(r, (SUB, HD)) for r in rows], axis=0)


def _hgrn_stages(zh_ref, lb_ref, one_m_lb_ref, og_ref, tri_ref, st_ref, keep, o_ref):
    tri2 = tri_ref[...]
    lb = lb_ref[...]
    one_m_lb = one_m_lb_ref[...]
    og = og_ref[...]
    tt = lax.broadcasted_iota(jnp.int32, (CHUNK, CHUNK), 0)
    ss_ = lax.broadcasted_iota(jnp.int32, (CHUNK, CHUNK), 1)
    causal = ss_ <= tt
    nsub = CHUNK // SUB
    zero_row = jnp.zeros((1, HD), F32)
    zero_sub = jnp.zeros((SUB, HD), BF16)

    heads = range(HEADS)
    sls = [slice(hd * HD, (hd + 1) * HD) for hd in heads]

    chunk_rows = [slice(c * CHUNK, (c + 1) * CHUNK) for c in range(TM // CHUNK)]

    def gates_and_decay(rows):
        hq = zh_ref[rows, 0:HW]
        hf = zh_ref[rows, HW:2 * HW]
        hi = zh_ref[rows, 2 * HW:3 * HW]
        hg = zh_ref[rows, 3 * HW:4 * HW]
        q = _silu(hq)
        gate = _silu(hg)
        t = jnp.exp(-jnp.abs(hf))
        r = 1.0 / (1.0 + t)
        tr = t * r
        pos = hf >= 0.0
        log2f = jnp.log2(jnp.maximum(lb + one_m_lb * jnp.where(pos, r, tr), TINY))
        k = one_m_lb * jnp.where(pos, tr, r)
        g1 = log2f.astype(BF16)
        g2 = (log2f - g1.astype(F32)).astype(BF16)
        b = jnp.dot(tri2, jnp.concatenate([g1, g2], axis=0), preferred_element_type=F32)
        vt = [hi[:, sls[hd]].T.astype(BF16) for hd in heads]
        return q, k, b, vt, gate

    def intra_chunk(q, k, b, vt, gate):
        att, q_in, k_out, decay = [], [], [], []
        for hd in heads:
            bh = b[:, sls[hd]]
            b_last = bh[CHUNK - 1:CHUNK, :]
            refs = [zero_row] + [bh[i * SUB - 1:i * SUB, :] for i in range(1, nsub)]
            dq = bh - _group_rows(refs)
            qe = q[:, sls[hd]] * jnp.exp2(dq)
            ke = k[:, sls[hd]] * jnp.exp2(jnp.minimum(-dq, EXP2_CLAMP))
            keb = ke.astype(BF16)
            qcat = []
            kcat = []
            for j in range(nsub):
                qcat.append(jnp.concatenate(
                    [zero_sub if i < j else
                     qe[i * SUB:(i + 1) * SUB].astype(BF16) if i == j else
                     (qe[i * SUB:(i + 1) * SUB] * jnp.exp2(refs[i] - refs[j])).astype(BF16)
                     for i in range(nsub)], axis=0))
                kcat.append(jnp.concatenate(
                    [keb[j * SUB:(j + 1) * SUB] if i == j else zero_sub for i in range(nsub)], axis=0))
            q_in.append(qcat[0])
            att.append(lax.dot_general(jnp.concatenate(qcat, axis=1), jnp.concatenate(kcat, axis=1),
                                       (((1,), (1,)), ((), ())), preferred_element_type=F32))
            to_end = [jnp.exp2(b_last - refs[i]) for i in range(nsub)]
            k_out.append((ke * _group_rows(to_end)).astype(BF16))
            decay.append(to_end[0])
        upd = [jnp.dot(vt[hd], k_out[hd], preferred_element_type=F32) for hd in heads]
        return att, q_in, upd, decay, vt, gate

    def stage1():
        return [gates_and_decay(rows) for rows in chunk_rows]

    def stage2(s1):
        return [intra_chunk(*c) for c in s1]

    def stage3(s2):
        st = [st_ref[hd] * keep for hd in heads]
        starts = []
        for att, q_in, upd, decay, vt, gate in s2:
            starts.append([st[hd].astype(BF16) for hd in heads])
            st = [decay[hd] * st[hd] + upd[hd] for hd in heads]
        for hd in heads:
            st_ref[hd] = st[hd]
        return starts

    def stage4(s2, starts):
        for rows, st0, (att, q_in, upd, decay, vt, gate) in zip(chunk_rows, starts, s2):
            for hd in heads:
                a = jnp.where(causal, att[hd], 0.0).astype(BF16)
                oh = lax.dot_general(jnp.concatenate([q_in[hd], a], axis=1),
                                     jnp.concatenate([st0[hd], vt[hd]], axis=1),
                                     (((1,), (1,)), ((), ())), preferred_element_type=F32)
                on = _rms(oh, og) * gate[:, sls[hd]]
                o_ref[rows, sls[hd]] = on.astype(BF16)

    return stage1, stage2, stage3, stage4


META_STEP, FULL_STEP, DIAG_STEP, NO_STEP = range(4)


def _attn_steps(lp):
    qi, fin, kind_a, kj_a, kind_b, kj_b = [], [], [], [], [], []
    for i in range(lp // TQ):
        tiles = [(META_STEP if j == 0 else DIAG_STEP if j == i else FULL_STEP, j) for j in range(i + 1)]
        for s in range(0, len(tiles), 2):
            a = tiles[s]
            b = tiles[s + 1] if s + 1 < len(tiles) else (NO_STEP, a[1])
            qi.append(i)
            fin.append(int(s + 2 >= len(tiles)))
            kind_a.append(a[0])
            kj_a.append(a[1])
            kind_b.append(b[0])
            kj_b.append(b[1])
    return [np.asarray(t, np.int32) for t in (qi, fin, kind_a, kj_a, kind_b, kj_b)]


def _key_copies(k_hbm, vt_hbm, kbuf, vbuf, sem, kj_a_ref, kj_b_ref, s, slot):
    copies = []
    for half, kj_ref in enumerate((kj_a_ref, kj_b_ref)):
        start = pl.multiple_of(kj_ref[s] * TK, TK)
        copies.append(pltpu.make_async_copy(k_hbm.at[pl.ds(start, TK), :], kbuf.at[slot, half], sem.at[slot, 2 * half]))
        copies.append(pltpu.make_async_copy(vt_hbm.at[:, pl.ds(start, TK)], vbuf.at[slot, half],
                                            sem.at[slot, 2 * half + 1]))
    return copies


def _attn_body(fixed_shift, nsteps, qi_ref, fin_ref, kind_a_ref, kj_a_ref, kind_b_ref, kj_b_ref, qg_ref,
               bound_ref, q_ref, k_hbm, vt_hbm, o_ref, *scratch):
    kbuf, vbuf, sem = scratch[-3:]
    if fixed_shift:
        l_ref, acc_ref = scratch[:-3]
    else:
        m_ref, l_ref, acc_ref = scratch[:-3]
    step = pl.program_id(0)
    qi = qi_ref[step]
    bound = bound_ref[0]

    ahead = KEY_BUFFERS - 1

    @pl.when(step == 0)
    def _():
        for s0 in range(min(ahead, nsteps)):
            for cp in _key_copies(k_hbm, vt_hbm, kbuf, vbuf, sem, kj_a_ref, kj_b_ref, s0, s0 % KEY_BUFFERS):
                cp.start()

    @pl.when(step + ahead < nsteps)
    def _():
        for cp in _key_copies(k_hbm, vt_hbm, kbuf, vbuf, sem, kj_a_ref, kj_b_ref, step + ahead,
                              (step + ahead) % KEY_BUFFERS):
            cp.start()

    slot = step % KEY_BUFFERS
    for cp in _key_copies(k_hbm, vt_hbm, kbuf, vbuf, sem, kj_a_ref, kj_b_ref, step, slot):
        cp.wait()
    ka_ref, kb_ref = kbuf.at[slot, 0], kbuf.at[slot, 1]
    vta_ref, vtb_ref = vbuf.at[slot, 0], vbuf.at[slot, 1]

    def update(k_ref, vt_ref, qs, keys, mask, first):
        nk, nq = keys.stop - keys.start, qs.stop - qs.start
        for hd in range(HEADS):
            hq = slice(hd * QK_PAD, (hd + 1) * QK_PAD)
            hv = slice(hd * HD, (hd + 1) * HD)
            st = lax.dot_general(k_ref[keys, hq], q_ref[qs, hq], (((1,), (1,)), ((), ())),
                                 preferred_element_type=F32)
            if fixed_shift:
                p = jnp.exp2(st - bound)
                if mask is not None:
                    p = jnp.where(mask, p, 0.0)
            else:
                if mask is not None:
                    st = jnp.where(mask, st, MASK_VALUE)
                m_prev = m_ref[hd, :, qs]
                m_new = jnp.maximum(m_prev, jnp.max(st, axis=0, keepdims=True))
                alpha = jnp.exp2(m_prev - m_new)
                m_ref[hd, :, qs] = m_new
                p = jnp.exp2(st - m_new[0:1, :])
            part = jnp.sum(p.reshape(nk // SUBLANES, SUBLANES, nq), axis=0)
            pv = jnp.dot(vt_ref[hv, keys], p.astype(BF16), preferred_element_type=F32)
            if first and fixed_shift:
                l_ref[hd, :, qs] = part
                acc_ref[hv, qs] = pv
            elif fixed_shift:
                l_ref[hd, :, qs] += part
                acc_ref[hv, qs] += pv
            else:
                l_ref[hd, :, qs] = alpha * l_ref[hd, :, qs] + part
                acc_ref[hv, qs] = alpha[0:1, :] * acc_ref[hv, qs] + pv

    all_q = slice(0, TQ)
    all_keys = slice(0, TK)

    def key_tile(kind, k_ref, vt_ref, may_be_first):
        if may_be_first:
            @pl.when(kind == META_STEP)
            def _():
                if not fixed_shift:
                    m_ref[...] = jnp.full_like(m_ref, MASK_VALUE)
                    l_ref[...] = jnp.zeros_like(l_ref)
                    acc_ref[...] = jnp.zeros_like(acc_ref)
                update(k_ref, vt_ref, all_q, slice(TK - N_META, TK), None, True)

        @pl.when(kind == FULL_STEP)
        def _():
            update(k_ref, vt_ref, all_q, all_keys, None, False)

        @pl.when(kind == DIAG_STEP)
        def _():
            half = TQ // 2
            key = lax.broadcasted_iota(jnp.int32, (half, TQ), 0)
            qry = lax.broadcasted_iota(jnp.int32, (half, TQ), 1)
            update(k_ref, vt_ref, all_q, slice(0, half), key // CHUNK <= qry // CHUNK, False)
            update(k_ref, vt_ref, slice(half, TQ), slice(half, TK),
                   (key // CHUNK <= qry // CHUNK)[:, :half], False)

    key_tile(kind_a_ref[step], ka_ref, vta_ref, True)
    key_tile(kind_b_ref[step], kb_ref, vtb_ref, False)

    @pl.when(fin_ref[step] == 1)
    def _():
        row = lax.broadcasted_iota(jnp.int32, (TQ, HD), 0) + qi * TQ
        valid = row >= PAD
        for hd in range(HEADS):
            hv = slice(hd * HD, (hd + 1) * HD)
            ot = acc_ref[hv, :] / jnp.sum(l_ref[hd], axis=0, keepdims=True)
            o_ref[:, hv] = jnp.where(valid, ot.T, 0.0).astype(BF16)


def _attn(q, k, vt, bound, nbatch, lp, fixed_shift):
    r = q.shape[0]
    nq, nk = lp // TQ, lp // TK
    qi, fin, kind_a, kj_a, kind_b, kj_b = _attn_steps(lp)
    reps = lambda t: np.tile(t, nbatch)
    offs = lambda t, n: np.concatenate([t + b * n for b in range(nbatch)]).astype(np.int32)
    tables = [reps(qi), reps(fin), reps(kind_a), offs(kj_a, nk), reps(kind_b), offs(kj_b, nk), offs(qi, nq)]
    nsteps = len(tables[0])
    stats = [pltpu.VMEM((HEADS, SUBLANES, TQ), F32)] * (1 if fixed_shift else 2)
    qmap = lambda s, qi, fin, kind_a, kj_a, kind_b, kj_b, qg, bd: (qg[s], 0)
    grid_spec = pltpu.PrefetchScalarGridSpec(
        num_scalar_prefetch=8,
        grid=(nsteps,),
        in_specs=[
            pl.BlockSpec((TQ, HEADS * QK_PAD), qmap),
            pl.BlockSpec(memory_space=pl.ANY),
            pl.BlockSpec(memory_space=pl.ANY),
        ],
        out_specs=pl.BlockSpec((TQ, HW), qmap),
        scratch_shapes=stats + [
            pltpu.VMEM((HW, TQ), F32),
            pltpu.VMEM((KEY_BUFFERS, 2, TK, HEADS * QK_PAD), BF16),
            pltpu.VMEM((KEY_BUFFERS, 2, HW, TK), BF16),
            pltpu.SemaphoreType.DMA((KEY_BUFFERS, 4)),
        ],
    )
    return pl.pallas_call(
        functools.partial(_attn_body, fixed_shift, nsteps),
        grid_spec=grid_spec,
        out_shape=jax.ShapeDtypeStruct((r, HW), BF16),
        compiler_params=_params(("arbitrary",)),
        name="attn_fixed_shift" if fixed_shift else "attn_online",
    )(*[jnp.asarray(t) for t in tables], bound, q, k, vt)


def _mix_tail_body(tpb, ntiles, first_layer, *refs):
    if first_layer:
        meta_ref, refs = refs[0], refs[1:]
    (zh_ref, h_ref, ob_ref, lb_ref, one_m_lb_ref, og_ref, tri_ref, wo_ref, g_ref, wu_ref, wd_ref,
     out_ref, st_ref, oa_ref) = refs
    i = pl.program_id(0)

    @pl.when(i == 0)
    def _():
        st_ref[...] = jnp.zeros_like(st_ref)

    keep = (jnp.minimum(i, ntiles - 1) % tpb != 0).astype(F32)
    stage1, stage2, stage3, stage4 = _hgrn_stages(
        zh_ref, lb_ref, one_m_lb_ref, og_ref, tri_ref, st_ref, keep, oa_ref.at[i % 2])

    tt = jnp.maximum(i - 1, 0) % tpb

    @pl.when(tt == 0)
    def _():
        s2 = stage2(stage1())
        stage4(s2, stage3(s2))
        out_ref[...] = jnp.zeros_like(out_ref)

    @pl.when(tt != 0)
    def _():
        h = _stream_tile(h_ref, meta_ref, tt) if first_layer else h_ref[...]
        mix = jnp.dot(oa_ref[(i + 1) % 2], wo_ref[:HW, :], preferred_element_type=F32)
        mix = mix + jnp.dot(ob_ref[...], wo_ref[HW:, :], preferred_element_type=F32)
        s1 = stage1()
        h = h + mix
        hn = _rms(h, g_ref[...]).astype(BF16)
        carried = {}

        def run2():
            carried["s2"] = stage2(s1)

        def run3():
            carried["o"] = stage3(carried["s2"])

        def run4():
            stage4(carried["s2"], carried["o"])

        out_ref[...] = h + _mlp_staggered(hn, wu_ref, wd_ref, between=(run2, run3, run4))


def _mix_tail(zh, h, meta, ob, lb, one_m_lb, og, tri2, wo, g, wu, wd, e, layer, nbatch, lp):
    r = nbatch * lp
    tpb = lp // TM
    ntiles = r // TM
    first_layer = meta is not None
    cur = lambda i: jnp.minimum(i, ntiles - 1)
    prev = lambda i: jnp.maximum(i - 1, 0)
    row = lambda i: (prev(i), 0)
    h_spec = pl.BlockSpec((TM, D_MODEL),
                          (lambda i: (_frame_tile(prev(i), tpb), 0)) if first_layer else row)
    lead_specs = [_const_spec((N_META, D_MODEL))] if first_layer else []
    lead_args = [meta] if first_layer else []
    return pl.pallas_call(
        functools.partial(_mix_tail_body, tpb, ntiles, first_layer),
        grid=(ntiles + 1,),
        in_specs=lead_specs + [
            pl.BlockSpec((TM, 4 * HW), lambda i: (cur(i), 0)),
            h_spec,
            pl.BlockSpec((TM, HW), row),
            _const_spec((1, HW)),
            _const_spec((1, HW)),
            _const_spec((1, HD)),
            _const_spec((CHUNK, 2 * CHUNK)),
            _const_spec((2 * HW, D_MODEL), e),
            _const_spec((1, D_MODEL)),
            _const_spec((D_MODEL, D_FF), layer),
            _const_spec((D_FF, D_MODEL), layer),
        ],
        out_specs=pl.BlockSpec((TM, D_MODEL), row),
        out_shape=jax.ShapeDtypeStruct((r, D_MODEL), F32),
        scratch_shapes=[pltpu.VMEM((HEADS, HD, HD), F32), pltpu.VMEM((2, TM, HW), BF16)],
        compiler_params=_params(("arbitrary",)),
        name="mix_tail",
    )(*lead_args, zh, h, ob, lb, one_m_lb, og, tri2, wo, g, wu, wd)


def _pool_mlp_body(tpb, ntiles, h_ref, halo_ref, gm_ref, pw_ref, ps_ref, g_ref, wu_ref, wd_ref,
                   out_ref, u_ref, a_ref, b_ref, hm_ref, hn_ref):
    i = pl.program_id(0)
    tt = jnp.minimum(i, ntiles - 1) % tpb
    n = TM + HALO
    g = POOL_G
    ys = [None] * len(POOL_WINDOWS)

    def normalise():
        gm = gm_ref[...]
        keep = (tt != 0).astype(F32)
        u_ref[0:HALO, :] = _rms(halo_ref[...], gm) * keep
        u_ref[HALO:, :] = _rms(h_ref[...], gm)
        a_ref[8:n, :] = u_ref[8:n, :] + u_ref[7:n - 1, :]

    def group(gi):
        w = POOL_WINDOWS[gi]
        cols = slice(gi * g, (gi + 1) * g)
        win = (a_ref, b_ref, a_ref, b_ref)[gi][HALO:, cols]
        pos = lax.broadcasted_iota(jnp.int32, (TM, g), 0) + (tt * TM - PAD)
        cnt = jnp.minimum(jnp.maximum(pos + 1, 1).astype(F32), float(w))
        d = win / cnt - u_ref[HALO:, cols]
        ys[gi] = jnp.dot(d.astype(BF16), pw_ref[gi], preferred_element_type=F32) * ps_ref[:, cols]

    def first_groups():
        b_ref[16:n, g:] = a_ref[16:n, g:] + a_ref[14:n - 2, g:]
        group(0)
        group(1)

    def last_groups():
        a_ref[24:n, 2 * g:] = b_ref[24:n, 2 * g:] + b_ref[20:n - 4, 2 * g:]
        b_ref[32:n, 3 * g:] = a_ref[32:n, 3 * g:] + a_ref[24:n - 8, 3 * g:]
        group(2)
        group(3)

    def finish():
        hm = h_ref[...] + jnp.concatenate(ys, axis=1)
        hm_ref[i % 2] = hm
        hn_ref[i % 2] = _rms(hm, g_ref[...]).astype(BF16)

    mlp_is_zero = jnp.maximum(i - 1, 0) % tpb == 0

    @pl.when(mlp_is_zero)
    def _():
        normalise()
        first_groups()
        last_groups()
        finish()
        out_ref[...] = jnp.zeros_like(out_ref)

    @pl.when(jnp.logical_not(mlp_is_zero))
    def _():
        acc = _mlp_staggered(hn_ref[(i + 1) % 2], wu_ref, wd_ref,
                             between=(normalise, first_groups, last_groups, finish))
        out_ref[...] = hm_ref[(i + 1) % 2] + acc


def _pool_mlp(h, gm, pw, ps, g, wu, wd, o, layer, nbatch, lp, to_frames):
    r = nbatch * lp
    tpb = lp // TM
    ntiles = r // TM
    cur = lambda i: jnp.minimum(i, ntiles - 1)
    prev = lambda i: jnp.maximum(i - 1, 0)
    out_rows = nbatch * (lp - LEAD) if to_frames else r
    return pl.pallas_call(
        functools.partial(_pool_mlp_body, tpb, ntiles),
        grid=(ntiles + 1,),
        in_specs=[
            pl.BlockSpec((TM, D_MODEL), lambda i: (cur(i), 0)),
            pl.BlockSpec((HALO, D_MODEL), lambda i: (jnp.maximum(cur(i) * (TM // HALO) - 1, 0), 0)),
            _const_spec((1, D_MODEL)),
            _const_spec((len(POOL_WINDOWS), POOL_G, POOL_G), o),
            _const_spec((1, D_MODEL)),
            _const_spec((1, D_MODEL)),
            _const_spec((D_MODEL, D_FF), layer),
            _const_spec((D_FF, D_MODEL), layer),
        ],
        out_specs=pl.BlockSpec((TM, D_MODEL), (lambda i: (_frame_tile(prev(i), tpb), 0)) if to_frames
                               else (lambda i: (prev(i), 0))),
        out_shape=jax.ShapeDtypeStruct((out_rows, D_MODEL), F32),
        scratch_shapes=[pltpu.VMEM((TM + HALO, D_MODEL), F32)] * 3 + [
            pltpu.VMEM((2, TM, D_MODEL), F32), pltpu.VMEM((2, TM, D_MODEL), BF16)],
        compiler_params=_params(("arbitrary",)),
        name="pool_mlp",
    )(h, h, gm, pw, ps, g, wu, wd)


def _rope_cols(w):
    half = ROPE // 2
    z = jnp.zeros(w.shape[:-1] + (half,), w.dtype)
    return jnp.concatenate([w[..., :half], z, w[..., half:], z], axis=-1)


def _qk_cols(w):
    w = w.reshape(w.shape[:-1] + (HEADS, QK_DIM))
    w = jnp.concatenate([w[..., :HD], _rope_cols(w[..., HD:])], axis=-1)
    return w.reshape(w.shape[:-2] + (HEADS * QK_PAD,))


def _rope_tables(lp):
    half = ROPE // 2
    inv = ROPE_THETA ** (-np.arange(half, dtype=np.float64) / half)
    pos = np.maximum(np.arange(lp, dtype=np.float64) - PAD, 0.0)
    ang = pos[:, None] * inv[None, :]
    c = jnp.asarray(np.cos(ang).astype(np.float32))
    s = jnp.asarray(np.sin(ang).astype(np.float32))
    z = jnp.zeros_like(c)
    return (jnp.concatenate([c, z, c, z], axis=1), jnp.concatenate([-s, z, s, z], axis=1))


def kernel(x, meta_tokens, mix_norm, mlp_norm, w_mlp_up, w_mlp_down, w_in, hgrn_lb, hgrn_out_norm, mla_q_a_norm, mla_kv_a_norm, w_q_up, w_kv_up, q_norm, k_norm, w_out, pool_w, pool_scale):
    nbatch, seq, _ = x.shape
    depth = mix_norm.shape[0]
    assert seq % TQ == 0 and depth % 2 == 0
    lp = seq + LEAD

    cos_t, sin_t = _rope_tables(lp)
    lb_cum = jnp.cumsum(jax.nn.softmax(hgrn_lb.astype(F32), axis=0), axis=0)
    lower = lb_cum - lb_cum[0:1]
    tri = jnp.tril(jnp.ones((CHUNK, CHUNK), F32)).astype(BF16)
    tri2 = jnp.concatenate([tri, tri], axis=1)

    w_in_l = w_in.astype(BF16)
    wq_l = _qk_cols(w_q_up).astype(BF16)
    wkv = w_kv_up.reshape(w_kv_up.shape[0], KV_RANK, HEADS, 2 * HD)
    wkv_l = jnp.concatenate([wkv[..., :HD].reshape(-1, KV_RANK, HW),
                             wkv[..., HD:].reshape(-1, KV_RANK, HW)], axis=-1).astype(BF16)
    qn_l = jnp.concatenate([q_norm[:, :HD], _rope_cols(q_norm[:, HD:])], axis=-1) * Q_SCALE
    kn_l = jnp.concatenate([k_norm[:, :HD], _rope_cols(k_norm[:, HD:])], axis=-1)
    wo_l = w_out.astype(BF16)
    wu_l = w_mlp_up.astype(BF16)
    wd_l = w_mlp_down.astype(BF16)
    pw_l = pool_w.astype(BF16)

    h = x.reshape(nbatch * seq, D_MODEL)
    meta = meta_tokens.astype(F32)
    for layer in range(depth):
        if layer % 2 == 0:
            e = layer // 2
            zh, q, k, vt = _inproj(h, meta, mix_norm[layer][None], w_in_l, mla_q_a_norm[e][None],
                                  mla_kv_a_norm[e][None], wq_l, wkv_l, qn_l[e][None],
                                  kn_l[e][None], cos_t, sin_t, e, nbatch, lp)
            bound = (Q_SCALE * QK_DIM * 1.01) * jnp.max(jnp.abs(q_norm[e])) * jnp.max(jnp.abs(k_norm[e]))
            bound = bound.reshape(1).astype(F32)
            ob = lax.cond(bound[0] <= MAX_FIXED_SHIFT,
                          functools.partial(_attn, nbatch=nbatch, lp=lp, fixed_shift=True),
                          functools.partial(_attn, nbatch=nbatch, lp=lp, fixed_shift=False),
                          q, k, vt, bound)
            h = _mix_tail(zh, h, meta, ob, lower[e][None], 1.0 - lower[e][None], hgrn_out_norm[e][None],
                          tri2, wo_l, mlp_norm[layer][None], wu_l, wd_l, e, layer, nbatch, lp)
            meta = None
        else:
            o = layer // 2
            h = _pool_mlp(h, mix_norm[layer][None], pw_l, pool_scale[o][None],
                          mlp_norm[layer][None], wu_l, wd_l, o, layer, nbatch, lp,
                          to_frames=layer == depth - 1)

    return h.reshape(nbatch, seq, D_MODEL)
```
